```python
import math
import jax, jax.numpy as jnp
from jax import lax
import numpy as np

D_MODEL = 1024
BATCH = 8
SEQ = 2048
DEPTH = 4
DEC_BATCH = 128
DEC_SEQ = 1
PAST_LEN = 16384
PAGE_SIZE = 128

D_MIX = D_MODEL
W_A = D_MIX // 2
H_A = 4
DH_A = W_A // H_A
W_B = D_MIX - W_A
G_B = 4
DG_B = W_B // G_B
CHUNK_B = 128
MLSTM_CHUNK = 128
CONV_W = 4
QK_W = 2 * W_A
D_IN = QK_W + 2 * W_A + 2 * H_A + 2 * W_B
SPLIT_POINTS = (QK_W, QK_W + W_A, QK_W + 2 * W_A, QK_W + 2 * W_A + H_A, QK_W + 2 * W_A + 2 * H_A, QK_W + 2 * W_A + 2 * H_A + W_B)
N_GROUPS = 4
EXPERTS_PER_GROUP = 8
N_EXPERTS = N_GROUPS * EXPERTS_PER_GROUP
TOP_K_IN_GROUP = 2
D_FF_EXPERT = D_MODEL // 2
MOE_BLOCK = 128
PLE_DIM = 256
EPS = 1e-6

kernel_name = 'hymba_mlstm_chunkmlp_hmoe_step'


def rmsnorm(x, g):
    xf = x.astype(jnp.float32)
    y = xf * lax.rsqrt(jnp.mean(xf * xf, axis=-1, keepdims=True) + EPS)
    return (y * g.astype(jnp.float32)).astype(x.dtype)


def _mlstm_chunk_step(carry, chunk):
    C, n, m = carry
    q, k, v, ig, lf = chunk
    L = q.shape[2]
    b = jnp.cumsum(lf, axis=-1)
    causal = jnp.tril(jnp.ones((L, L), dtype=bool))
    d_log = jnp.where(causal, b[..., :, None] - b[..., None, :] + ig[..., None, :], -jnp.inf)
    inter = b + m[..., None]
    m_t = jnp.maximum(inter, jnp.max(d_log, axis=-1))
    s = jnp.einsum('bhtd,bhsd->bhts', q, k) * jnp.exp(d_log - m_t[..., None])
    w_inter = jnp.exp(inter - m_t)
    num = w_inter[..., None] * jnp.einsum('bhvd,bhtd->bhtv', C, q) + jnp.einsum('bhts,bhsv->bhtv', s, v)
    nq = w_inter * jnp.einsum('bhd,bhtd->bht', n, q) + jnp.sum(s, axis=-1)
    h = num / jnp.maximum(jnp.abs(nq), jnp.exp(-m_t))[..., None]
    m_new = m_t[..., -1]
    w_state = jnp.exp(b[..., -1:] - b + ig - m_new[..., None])
    decay = jnp.exp(b[..., -1] + m - m_new)
    C_new = decay[..., None, None] * C + jnp.einsum('bhs,bhsv,bhsd->bhvd', w_state, v, k)
    n_new = decay[..., None] * n + jnp.einsum('bhs,bhsd->bhd', w_state, k)
    return (C_new, n_new, m_new), h


def mlstm(q, k, v, ig, lf, C0, n0, m0):
    B, H, L, d = q.shape
    Lc = min(L, MLSTM_CHUNK)
    nC = L // Lc

    def to_chunks(a):
        return jnp.moveaxis(a.reshape((B, H, nC, Lc) + a.shape[3:]), 2, 0)

    (C, n, m), hs = lax.scan(_mlstm_chunk_step, (C0, n0, m0),
                             (to_chunks(q), to_chunks(k), to_chunks(v), to_chunks(ig), to_chunks(lf)))
    h = jnp.moveaxis(hs, 0, 2).reshape(B, H, L, d)
    return h, (C, n, m)


def chunk_mlp(u, v, w_s_l, b_s_l, vn_g):
    B, L, _ = v.shape
    Lc = min(L, CHUNK_B)
    nC = L // Lc
    vn = rmsnorm(jax.nn.gelu(v).reshape(B, L, G_B, DG_B), vn_g.reshape(G_B, DG_B))
    w = w_s_l[:, :Lc, :Lc] * jnp.tril(jnp.ones((Lc, Lc), dtype=w_s_l.dtype))
    z = jnp.einsum('gts,bnsgc->bntgc', w, vn.reshape(B, nC, Lc, G_B, DG_B)) + b_s_l[:, :Lc].T[:, :, None]
    out = jax.nn.gelu(u) * z.reshape(B, L, W_B)
    return out, vn.reshape(B, L, W_B)


def parallel_mixer(xn, conv_buf, C0, n0, m0, w_in_l, conv_w_l, conv_b_l, b_i, b_f, hn_g, vn_g, w_s_l, b_s_l, w_out_l):
    B, L, _ = xn.shape
    qk_pre, v_a, o_a, ig, fg, u_b, v_b = jnp.split(xn @ w_in_l, SPLIT_POINTS, axis=-1)
    xp = jnp.concatenate([conv_buf.astype(qk_pre.dtype), qk_pre], axis=1)
    qk = conv_b_l
    for j in range(CONV_W):
        qk = qk + xp[:, j:j + L] * conv_w_l[j]
    qk = jax.nn.silu(qk)
    new_conv = xp[:, L:]
    q, k = jnp.split(qk, 2, axis=-1)

    def heads(a):
        return a.reshape(B, L, H_A, DH_A).transpose(0, 2, 1, 3).astype(jnp.float32)

    ig_h = (ig + b_i).astype(jnp.float32).transpose(0, 2, 1)
    lf_h = jax.nn.log_sigmoid((fg + b_f).astype(jnp.float32)).transpose(0, 2, 1)
    h_a, (C, n, m) = mlstm(heads(q), heads(k) * (DH_A ** -0.5), heads(v_a), ig_h, lf_h,
                           C0.astype(jnp.float32), n0.astype(jnp.float32), m0.astype(jnp.float32))
    h_a = rmsnorm(h_a, hn_g.reshape(H_A, 1, DH_A)).transpose(0, 2, 1, 3).reshape(B, L, W_A)
    h_a = h_a.astype(xn.dtype) * jax.nn.sigmoid(o_a)
    h_b, v_rows = chunk_mlp(u_b, v_b, w_s_l, b_s_l, vn_g)
    y = jnp.concatenate([h_a, h_b], axis=-1) @ w_out_l
    return y, (new_conv, C.astype(C0.dtype), n.astype(n0.dtype), m.astype(m0.dtype), v_rows)


def grouped_ffn(xf, expert_idx, weights, w1_l, w3_l, w2_l):
    T, D = xf.shape
    A = T * TOP_K_IN_GROUP
    e_flat = expert_idx.reshape(A)
    tok = jnp.repeat(jnp.arange(T, dtype=jnp.int32), TOP_K_IN_GROUP)
    w_flat = weights.reshape(A)
    order = jnp.argsort(e_flat)
    e_sorted = e_flat[order]
    counts = jnp.zeros((N_EXPERTS,), jnp.int32).at[e_flat].add(1)
    padded = (counts + MOE_BLOCK - 1) // MOE_BLOCK * MOE_BLOCK
    start = jnp.cumsum(counts) - counts
    pstart = jnp.cumsum(padded) - padded
    dest = pstart[e_sorted] + (jnp.arange(A, dtype=jnp.int32) - start[e_sorted])
    n_blocks = -(-A // MOE_BLOCK) + N_EXPERTS
    n_rows = n_blocks * MOE_BLOCK
    row_tok = jnp.full((n_rows,), T, jnp.int32).at[dest].set(tok[order])
    row_w = jnp.zeros((n_rows,), w_flat.dtype).at[dest].set(w_flat[order])
    block_expert = jnp.minimum(jnp.searchsorted(jnp.cumsum(padded), jnp.arange(n_blocks, dtype=jnp.int32) * MOE_BLOCK, side='right'), N_EXPERTS - 1)
    x_pad = jnp.concatenate([xf, jnp.zeros((1, D), xf.dtype)], axis=0)
    x_blocks = x_pad[row_tok].reshape(n_blocks, MOE_BLOCK, D)

    def expert_block(args):
        xb, e = args
        return (jax.nn.silu(xb @ w1_l[e]) * (xb @ w3_l[e])) @ w2_l[e]

    y = lax.map(expert_block, (x_blocks, block_expert)).reshape(n_rows, D)
    return jax.ops.segment_sum(y * row_w[:, None], row_tok, num_segments=T + 1)[:T]


def hier_moe(x, rg_w, rg_b, re_w, re_b, w1_l, w3_l, w2_l):
    B, L, D = x.shape
    T = B * L
    xf = x.reshape(T, D)
    g_prob = jax.nn.softmax((xf @ rg_w + rg_b).astype(jnp.float32), axis=-1)
    g_sel = jnp.argmax(g_prob, axis=-1)
    g_w = jnp.take_along_axis(g_prob, g_sel[:, None], axis=-1)
    e_logits = (xf @ re_w + re_b).astype(jnp.float32).reshape(T, N_GROUPS, EXPERTS_PER_GROUP)
    e_logits = jnp.take_along_axis(e_logits, g_sel[:, None, None], axis=1)[:, 0]
    top_v, top_i = lax.top_k(e_logits, TOP_K_IN_GROUP)
    weights = g_w * jax.nn.softmax(top_v, axis=-1)
    expert_idx = (g_sel[:, None] * EXPERTS_PER_GROUP + top_i).astype(jnp.int32)
    y = grouped_ffn(xf, expert_idx, weights.astype(x.dtype), w1_l, w3_l, w2_l)
    return y.reshape(B, L, D)


def trunk_layer(x, p, conv_buf, C0, n0, m0, lp):
    (g_mix, w_in_l, conv_w_l, conv_b_l, b_i, b_f, hn_g, vn_g, w_s_l, b_s_l, w_out_l,
     g_ffn, rg_w, rg_b, re_w, re_b, w1_l, w3_l, w2_l, w_ple_l, ple_g, w_pg) = lp
    mix, new_state = parallel_mixer(rmsnorm(x, g_mix), conv_buf, C0, n0, m0, w_in_l, conv_w_l, conv_b_l,
                                    b_i, b_f, hn_g, vn_g, w_s_l, b_s_l, w_out_l)
    h = x + mix
    h = h + hier_moe(rmsnorm(h, g_ffn), rg_w, rg_b, re_w, re_b, w1_l, w3_l, w2_l)
    h = h + jax.nn.sigmoid(h @ w_pg) * rmsnorm(p @ w_ple_l, ple_g)
    return h, new_state


def setup_inputs(seed: int = 0) -> dict:
    key = jax.random.key(seed)
    ks = iter(jax.random.split(key, 48))

    def nrm(shape, scale):
        return scale * jax.random.normal(next(ks), shape, jnp.float32)

    def gain(shape):
        return 1.0 + nrm(shape, 0.05)

    f_bias = jnp.linspace(3.0, 6.0, H_A, dtype=jnp.float32)[None, :] + nrm((DEPTH, H_A), 0.1)
    return {
        'x_prompt': nrm((BATCH, SEQ, D_MODEL), 1.0),
        'x_sample': nrm((DEC_BATCH, DEC_SEQ, D_MODEL), 1.0),
        'state_C': nrm((DEPTH, DEC_BATCH, H_A, DH_A, DH_A), 0.3),
        'state_n': nrm((DEPTH, DEC_BATCH, H_A, DH_A), 0.3),
        'state_m': jax.random.uniform(next(ks), (DEPTH, DEC_BATCH, H_A), jnp.float32, -1.0, 2.0),
        'state_conv': nrm((DEPTH, DEC_BATCH, CONV_W - 1, QK_W), 1.0),
        'p_prompt': nrm((DEPTH, BATCH, SEQ, PLE_DIM), 1.0),
        'p_sample': nrm((DEPTH, DEC_BATCH, DEC_SEQ, PLE_DIM), 1.0),
        'norm_mix_g': gain((DEPTH, D_MODEL)),
        'w_in': nrm((DEPTH, D_MODEL, D_IN), D_MODEL ** -0.5),
        'conv_w': nrm((DEPTH, CONV_W, QK_W), CONV_W ** -0.5),
        'conv_b': nrm((DEPTH, QK_W), 0.02),
        'b_igate': nrm((DEPTH, H_A), 0.1),
        'b_fgate': f_bias,
        'hnorm_g': gain((DEPTH, W_A)),
        'vnorm_g': gain((DEPTH, W_B)),
        'w_s': nrm((DEPTH, G_B, CHUNK_B, CHUNK_B), CHUNK_B ** -0.5),
        'b_s': 1.0 + nrm((DEPTH, G_B, CHUNK_B), 0.1),
        'w_out': nrm((DEPTH, D_MIX, D_MODEL), D_MIX ** -0.5),
        'norm_ffn_g': gain((DEPTH, D_MODEL)),
        'router_g_w': nrm((DEPTH, D_MODEL, N_GROUPS), D_MODEL ** -0.5),
        'router_g_b': nrm((DEPTH, N_GROUPS), 0.01),
        'router_e_w': nrm((DEPTH, D_MODEL, N_EXPERTS), D_MODEL ** -0.5),
        'router_e_b': nrm((DEPTH, N_EXPERTS), 0.01),
        'w1': nrm((DEPTH, N_EXPERTS, D_MODEL, D_FF_EXPERT), D_MODEL ** -0.5),
        'w3': nrm((DEPTH, N_EXPERTS, D_MODEL, D_FF_EXPERT), D_MODEL ** -0.5),
        'w2': nrm((DEPTH, N_EXPERTS, D_FF_EXPERT, D_MODEL), D_FF_EXPERT ** -0.5),
        'w_ple': nrm((DEPTH, PLE_DIM, D_MODEL), PLE_DIM ** -0.5),
        'ple_norm_g': gain((DEPTH, D_MODEL)),
        'w_ple_gate': nrm((DEPTH, D_MODEL, D_MODEL), D_MODEL ** -0.5),
        'final_norm_g': gain((D_MODEL,)),
    }


def reference(x_prompt, x_sample, state_C, state_n, state_m, state_conv, p_prompt, p_sample,
              norm_mix_g, w_in, conv_w, conv_b, b_igate, b_fgate, hnorm_g, vnorm_g, w_s, b_s, w_out,
              norm_ffn_g, router_g_w, router_g_b, router_e_w, router_e_b, w1, w3, w2,
              w_ple, ple_norm_g, w_ple_gate, final_norm_g):
    B = x_prompt.shape[0]
    f32 = jnp.float32
    hp = x_prompt
    hs = x_sample
    Cp, np_, mp, cbp = [], [], [], []
    Cs, ns, ms, cbs, vs = [], [], [], [], []
    for l in range(DEPTH):
        lp = (norm_mix_g[l], w_in[l], conv_w[l], conv_b[l], b_igate[l], b_fgate[l], hnorm_g[l], vnorm_g[l],
              w_s[l], b_s[l], w_out[l], norm_ffn_g[l], router_g_w[l], router_g_b[l], router_e_w[l], router_e_b[l],
              w1[l], w3[l], w2[l], w_ple[l], ple_norm_g[l], w_ple_gate[l])
        hp, (cb_p, C_p, n_p, m_p, _) = trunk_layer(
            hp, p_prompt[l], jnp.zeros((B, CONV_W - 1, QK_W), x_prompt.dtype),
            jnp.zeros((B, H_A, DH_A, DH_A), f32), jnp.zeros((B, H_A, DH_A), f32), jnp.zeros((B, H_A), f32), lp)
        hs, (cb_s, C_s, n_s, m_s, v_s) = trunk_layer(
            hs, p_sample[l], state_conv[l], state_C[l], state_n[l], state_m[l], lp)
        Cp.append(C_p); np_.append(n_p); mp.append(m_p); cbp.append(cb_p)
        Cs.append(C_s); ns.append(n_s); ms.append(m_s); cbs.append(cb_s); vs.append(v_s)
    y_prompt = rmsnorm(hp, final_norm_g)
    y_sample = rmsnorm(hs, final_norm_g)
    return (y_prompt, y_sample,
            jnp.stack(Cp), jnp.stack(np_), jnp.stack(mp), jnp.stack(cbp),
            jnp.stack(Cs), jnp.stack(ns), jnp.stack(ms), jnp.stack(cbs), jnp.stack(vs))
```

```python
import functools

import jax
import jax.numpy as jnp
from jax import lax
from jax.experimental import pallas as pl
from jax.experimental.pallas import tpu as pltpu

F32 = jnp.float32
BF16 = jnp.bfloat16

D_MODEL = 1024
W_A = 512
H_A = 4
DH_A = 128
W_B = 512
G_B = 4
DG_B = 128
CHUNK = 128
CONV_W = 4
QK_W = 2 * W_A
N_GROUPS = 4
EXPERTS_PER_GROUP = 8
N_EXPERTS = N_GROUPS * EXPERTS_PER_GROUP
TOP_K = 2
D_FF = 512
PLE_DIM = 256
EPS = 1e-6

LANES = 128
SUBLANES = 8
VMEM_LIMIT_BYTES = 56 * 1024 * 1024

P_QK = 0
P_V = QK_W
P_O = P_V + W_A
P_U = P_O + W_A
P_VB = P_U + W_B
P_G = P_VB + W_B
P_W = P_G + LANES

TL_MIX = 512
BT_SAMPLE = 8
TM_TOK = 384
FFN_BLK = 256


def _cparams(*sem):
    return pltpu.CompilerParams(dimension_semantics=sem, vmem_limit_bytes=VMEM_LIMIT_BYTES)


def _rms(x, g):
    return x * lax.rsqrt(jnp.mean(x * x, axis=-1, keepdims=True) + EPS) * g


def _log_sigmoid(x):
    return -(jnp.maximum(-x, 0.0) + jnp.log1p(jnp.exp(-jnp.abs(x))))


def _split_dot(a, b_bf16):
    hi = a.astype(BF16)
    lo = (a - hi.astype(F32)).astype(BF16)
    return (jnp.dot(hi, b_bf16, preferred_element_type=F32)
            + jnp.dot(lo, b_bf16, preferred_element_type=F32))


def _dot_nt(a, b):
    return lax.dot_general(a, b, (((1,), (1,)), ((), ())), preferred_element_type=F32)


def _mix_prompt_kernel(x_ref, gmix_ref, win_ref, cw_ref, cb_ref, gb_ref, hng_ref, vng_ref,
                       ws_ref, bs_ref, wout_ref,
                       out_ref, c_out, n_out, m_out, conv_out,
                       proj, xpad, hcat, c_s, n_s, m_s, *, tl):
    j = pl.program_id(1)
    nj = pl.num_programs(1)
    n_chunks = tl // CHUNK

    @pl.when(j == 0)
    def _():
        c_s[...] = jnp.zeros_like(c_s)
        n_s[...] = jnp.zeros_like(n_s)
        m_s[...] = jnp.zeros_like(m_s)
        xpad[0:SUBLANES, :] = jnp.zeros((SUBLANES, QK_W), F32)

    x = x_ref[...]
    xn = _rms(x, gmix_ref[...]).astype(BF16)
    proj[...] = jnp.dot(xn, win_ref[...], preferred_element_type=F32)

    pre = proj[:, P_QK:P_QK + QK_W]
    xpad[SUBLANES:SUBLANES + tl, :] = pre
    qk = cb_ref[...] + cw_ref[CONV_W - 1:CONV_W, :] * pre
    for jj in range(1, CONV_W):
        qk = qk + cw_ref[CONV_W - 1 - jj:CONV_W - jj, :] * xpad[SUBLANES - jj:SUBLANES - jj + tl, :]
    new_rows = pre[tl - (CONV_W - 1):tl, :]
    xpad[SUBLANES - (CONV_W - 1):SUBLANES, :] = new_rows

    @pl.when(j == nj - 1)
    def _():
        conv_out[0] = new_rows

    qk = qk * jax.nn.sigmoid(qk)
    proj[:, 0:W_A] = qk[:, 0:W_A]
    proj[:, W_A:QK_W] = qk[:, W_A:QK_W] * (DH_A ** -0.5)

    row = lax.broadcasted_iota(jnp.int32, (CHUNK, CHUNK), 0)
    col = lax.broadcasted_iota(jnp.int32, (CHUNK, CHUNK), 1)
    causal = row >= col
    tril = jnp.where(causal, 1.0, 0.0).astype(BF16)
    triu = jnp.where(row <= col, 1.0, 0.0).astype(BF16)
    lane = lax.broadcasted_iota(jnp.int32, (CHUNK, LANES), 1)
    wm = [jnp.where(causal, ws_ref[g], 0.0).astype(BF16) for g in range(G_B)]

    def chunk_body(c, carry):
        r0 = pl.multiple_of(c * CHUNK, CHUNK)
        rows = pl.ds(r0, CHUNK)
        gl = proj[rows, P_G:P_G + LANES] + gb_ref[...]
        gl = jnp.where(lane >= H_A, _log_sigmoid(gl), gl)
        glt = gl.T
        b_col_all = _split_dot_left(tril, gl)
        b_row_all = _split_dot(glt[0:2 * SUBLANES, :], triu)

        for h in range(H_A):
            q32 = proj[rows, h * DH_A:(h + 1) * DH_A]
            k32 = proj[rows, W_A + h * DH_A:W_A + (h + 1) * DH_A]
            v32 = proj[rows, P_V + h * DH_A:P_V + (h + 1) * DH_A]
            q = q32.astype(BF16)
            k = k32.astype(BF16)
            ig_c = gl[:, h:h + 1]
            ig_r = glt[h:h + 1, :]
            b_c = b_col_all[:, H_A + h:H_A + h + 1]
            b_r = b_row_all[H_A + h:H_A + h + 1, :]
            c_old = c_s[h]
            n_old = n_s[h:h + 1, :]
            m_prev = m_s[h:h + 1, 0:1]

            d_log = jnp.where(causal, b_c - b_r + ig_r, -jnp.inf)
            inter = b_c + m_prev
            m_t = jnp.maximum(inter, jnp.max(d_log, axis=-1, keepdims=True))
            s = _dot_nt(q, k) * jnp.exp(d_log - m_t)
            w_inter = jnp.exp(inter - m_t)
            num = (w_inter * _dot_nt(q, c_old.astype(BF16))
                   + jnp.dot(s.astype(BF16), v32.astype(BF16), preferred_element_type=F32))
            nq = (w_inter * jnp.sum(q32 * n_old, axis=-1, keepdims=True)
                  + jnp.sum(s, axis=-1, keepdims=True))
            hh = num / jnp.maximum(jnp.abs(nq), jnp.exp(-m_t))

            m_new = m_t[CHUNK - 1:CHUNK, :]
            b_last = b_c[CHUNK - 1:CHUNK, :]
            w_state = jnp.exp(b_last - b_c + ig_c - m_new)
            decay = jnp.exp(b_last + m_prev - m_new)
            vw = (v32 * w_state).astype(BF16)
            c_s[h] = decay * c_old + lax.dot_general(
                vw, k, (((0,), (0,)), ((), ())), preferred_element_type=F32)
            n_s[h:h + 1, :] = decay * n_old + jnp.sum(w_state * k32, axis=0, keepdims=True)
            m_s[h:h + 1, :] = jnp.broadcast_to(m_new, (1, LANES))

            ha = _rms(hh, hng_ref[:, h * DH_A:(h + 1) * DH_A])
            o = proj[rows, P_O + h * DH_A:P_O + (h + 1) * DH_A]
            hcat[rows, h * DH_A:(h + 1) * DH_A] = (ha * jax.nn.sigmoid(o)).astype(BF16)

        for g in range(G_B):
            vb = proj[rows, P_VB + g * DG_B:P_VB + (g + 1) * DG_B]
            u = proj[rows, P_U + g * DG_B:P_U + (g + 1) * DG_B]
            vn = _rms(jax.nn.gelu(vb), vng_ref[:, g * DG_B:(g + 1) * DG_B])
            z = jnp.dot(wm[g], vn.astype(BF16), preferred_element_type=F32) + bs_ref[:, g:g + 1]
            hcat[rows, W_A + g * DG_B:W_A + (g + 1) * DG_B] = (jax.nn.gelu(u) * z).astype(BF16)
        return carry

    lax.fori_loop(0, n_chunks, chunk_body, 0)

    out_ref[...] = x + jnp.dot(hcat[...], wout_ref[...], preferred_element_type=F32)

    @pl.when(j == nj - 1)
    def _():
        c_out[0] = c_s[...]
        n_out[0] = n_s[0:H_A, :]
        m_out[0] = m_s[0:H_A, :]


def _split_dot_left(a_bf16, b):
    hi = b.astype(BF16)
    lo = (b - hi.astype(F32)).astype(BF16)
    return (jnp.dot(a_bf16, hi, preferred_element_type=F32)
            + jnp.dot(a_bf16, lo, preferred_element_type=F32))


def _mix_prompt(hbuf, n_batch, seq, lw):
    tl = max(t for t in range(CHUNK, TL_MIX + 1, CHUNK) if seq % t == 0)
    nj = seq // tl
    const = lambda b, j: (0, 0)
    kern = functools.partial(_mix_prompt_kernel, tl=tl)
    return pl.pallas_call(
        kern,
        name="mix_prompt",
        grid=(n_batch, nj),
        in_specs=[
            pl.BlockSpec((tl, D_MODEL), lambda b, j: (b * nj + j, 0)),
            pl.BlockSpec((1, D_MODEL), const),
            pl.BlockSpec((D_MODEL, P_W), const),
            pl.BlockSpec((CONV_W, QK_W), const),
            pl.BlockSpec((1, QK_W), const),
            pl.BlockSpec((1, LANES), const),
            pl.BlockSpec((1, W_A), const),
            pl.BlockSpec((1, W_B), const),
            pl.BlockSpec((G_B, CHUNK, CHUNK), lambda b, j: (0, 0, 0)),
            pl.BlockSpec((CHUNK, LANES), const),
            pl.BlockSpec((D_MODEL, D_MODEL), const),
        ],
        out_specs=[
            pl.BlockSpec((tl, D_MODEL), lambda b, j: (b * nj + j, 0)),
            pl.BlockSpec((1, H_A, DH_A, DH_A), lambda b, j: (b, 0, 0, 0)),
            pl.BlockSpec((1, H_A, DH_A), lambda b, j: (b, 0, 0)),
            pl.BlockSpec((1, H_A, LANES), lambda b, j: (b, 0, 0)),
            pl.BlockSpec((1, CONV_W - 1, QK_W), lambda b, j: (b, 0, 0)),
        ],
        out_shape=[
            jax.ShapeDtypeStruct(hbuf.shape, F32),
            jax.ShapeDtypeStruct((n_batch, H_A, DH_A, DH_A), F32),
            jax.ShapeDtypeStruct((n_batch, H_A, DH_A), F32),
            jax.ShapeDtypeStruct((n_batch, H_A, LANES), F32),
            jax.ShapeDtypeStruct((n_batch, CONV_W - 1, QK_W), F32),
        ],
        scratch_shapes=[
            pltpu.VMEM((tl, P_W), F32),
            pltpu.VMEM((SUBLANES + tl, QK_W), F32),
            pltpu.VMEM((tl, D_MODEL), BF16),
            pltpu.VMEM((H_A, DH_A, DH_A), F32),
            pltpu.VMEM((SUBLANES, LANES), F32),
            pltpu.VMEM((SUBLANES, LANES), F32),
        ],
        input_output_aliases={0: 0},
        compiler_params=_cparams("arbitrary", "arbitrary"),
    )(hbuf, lw["g_mix"], lw["w_in"], lw["conv_w"], lw["conv_b"], lw["gate_b"], lw["hn_g"], lw["vn_g"],
      lw["w_s"], lw["b_s_col"], lw["w_out"])


def _mix_sample_kernel(x_ref, gmix_ref, win_ref, cw_ref, cb_ref, gb_ref, hng_ref, vng_ref,
                       w00_ref, bs0_ref, wout_ref, sconv_ref, c_in, n_in, m_in,
                       out_ref, c_out, n_out, m_out, conv_out, vrow_out,
                       proj, hcat, *, bt):
    i = pl.program_id(0)
    ni = pl.num_programs(0)

    @pl.when(i == 0)
    def _():
        x = x_ref[...]
        xn = _rms(x, gmix_ref[...]).astype(BF16)
        proj[...] = jnp.dot(xn, win_ref[...], preferred_element_type=F32)
        pre = proj[:, P_QK:P_QK + QK_W]
        qk = cb_ref[...] + cw_ref[CONV_W - 1:CONV_W, :] * pre
        for jj in range(CONV_W - 1):
            qk = qk + cw_ref[jj:jj + 1, :] * sconv_ref[jj]
        for jj in range(CONV_W - 2):
            conv_out[jj] = sconv_ref[jj + 1]
        conv_out[CONV_W - 2] = pre
        qk = qk * jax.nn.sigmoid(qk)
        proj[:, 0:W_A] = qk[:, 0:W_A]
        proj[:, W_A:QK_W] = qk[:, W_A:QK_W] * (DH_A ** -0.5)
        for g in range(G_B):
            sl = slice(g * DG_B, (g + 1) * DG_B)
            vb = proj[:, P_VB + g * DG_B:P_VB + (g + 1) * DG_B]
            u = proj[:, P_U + g * DG_B:P_U + (g + 1) * DG_B]
            vn = _rms(jax.nn.gelu(vb), vng_ref[:, sl])
            vrow_out[:, sl] = vn
            z = w00_ref[:, sl] * vn + bs0_ref[:, sl]
            hcat[:, W_A + g * DG_B:W_A + (g + 1) * DG_B] = jax.nn.gelu(u) * z

    rows = pl.ds(pl.multiple_of(i * bt, bt), bt)
    gates = proj[rows, P_G:P_G + LANES] + gb_ref[...]
    lane = lax.broadcasted_iota(jnp.int32, (bt, LANES), 1)
    sub = lax.broadcasted_iota(jnp.int32, (bt, DH_A), 0)
    m_new_all = jnp.zeros((bt, LANES), F32)
    for h in range(H_A):
        ig = gates[:, h:h + 1]
        lf = _log_sigmoid(gates[:, H_A + h:H_A + h + 1])
        m_prev = m_in[rows, h:h + 1]
        inter = lf + m_prev
        m_t = jnp.maximum(inter, ig)
        w_inter = jnp.exp(inter - m_t)
        e_d = jnp.exp(ig - m_t)
        q = proj[rows, h * DH_A:(h + 1) * DH_A]
        k = proj[rows, W_A + h * DH_A:W_A + (h + 1) * DH_A]
        v = proj[rows, P_V + h * DH_A:P_V + (h + 1) * DH_A]
        n_old = n_in[h]
        s = jnp.sum(q * k, axis=-1, keepdims=True) * e_d
        vw = v * e_d
        vw_t = jnp.concatenate([vw, jnp.zeros((DH_A - bt, DH_A), F32)], axis=0).T
        qb = q.astype(BF16)
        qc = jnp.zeros((bt, DH_A), F32)
        for t in range(bt):
            c_old = c_in[t, h]
            r = _dot_nt(qb, c_old.astype(BF16))
            qc = jnp.where(sub == t, r, qc)
            c_out[t, h] = w_inter[t:t + 1, :] * c_old + vw_t[:, t:t + 1] * k[t:t + 1, :]
        num = w_inter * qc + s * v
        nq = w_inter * jnp.sum(q * n_old, axis=-1, keepdims=True) + s
        hh = num / jnp.maximum(jnp.abs(nq), jnp.exp(-m_t))
        n_out[h] = w_inter * n_old + e_d * k
        m_new_all = jnp.where(lane == h, m_t, m_new_all)
        ha = _rms(hh, hng_ref[:, h * DH_A:(h + 1) * DH_A])
        o = proj[rows, P_O + h * DH_A:P_O + (h + 1) * DH_A]
        hcat[rows, h * DH_A:(h + 1) * DH_A] = ha * jax.nn.sigmoid(o)
    m_out[rows, :] = m_new_all

    @pl.when(i == ni - 1)
    def _():
        out_ref[...] = x_ref[...] + jnp.dot(hcat[...].astype(BF16), wout_ref[...],
                                            preferred_element_type=F32)


def _mix_sample(hbuf, lw, sconv_t, c0, n0_t, m0_pad, n_s):
    bt = BT_SAMPLE
    ni = n_s // bt
    const = lambda i: (0, 0)
    out_blk = hbuf.shape[0] // n_s - 1
    kern = functools.partial(_mix_sample_kernel, bt=bt)
    return pl.pallas_call(
        kern,
        name="mix_sample",
        grid=(ni,),
        in_specs=[
            pl.BlockSpec((n_s, D_MODEL), lambda i: (out_blk, 0)),
            pl.BlockSpec((1, D_MODEL), const),
            pl.BlockSpec((D_MODEL, P_W), const),
            pl.BlockSpec((CONV_W, QK_W), const),
            pl.BlockSpec((1, QK_W), const),
            pl.BlockSpec((1, LANES), const),
            pl.BlockSpec((1, W_A), const),
            pl.BlockSpec((1, W_B), const),
            pl.BlockSpec((1, W_B), const),
            pl.BlockSpec((1, W_B), const),
            pl.BlockSpec((D_MODEL, D_MODEL), const),
            pl.BlockSpec((CONV_W - 1, n_s, QK_W), lambda i: (0, 0, 0)),
            pl.BlockSpec((bt, H_A, DH_A, DH_A), lambda i: (i, 0, 0, 0)),
            pl.BlockSpec((H_A, bt, DH_A), lambda i: (0, i, 0)),
            pl.BlockSpec((n_s, LANES), const),
        ],
        out_specs=[
            pl.BlockSpec((n_s, D_MODEL), lambda i: (out_blk, 0)),
            pl.BlockSpec((bt, H_A, DH_A, DH_A), lambda i: (i, 0, 0, 0)),
            pl.BlockSpec((H_A, bt, DH_A), lambda i: (0, i, 0)),
            pl.BlockSpec((n_s, LANES), const),
            pl.BlockSpec((CONV_W - 1, n_s, QK_W), lambda i: (0, 0, 0)),
            pl.BlockSpec((n_s, W_B), const),
        ],
        out_shape=[
            jax.ShapeDtypeStruct(hbuf.shape, F32),
            jax.ShapeDtypeStruct(c0.shape, F32),
            jax.ShapeDtypeStruct(n0_t.shape, F32),
            jax.ShapeDtypeStruct((n_s, LANES), F32),
            jax.ShapeDtypeStruct((CONV_W - 1, n_s, QK_W), F32),
            jax.ShapeDtypeStruct((n_s, W_B), F32),
        ],
        scratch_shapes=[
            pltpu.VMEM((n_s, P_W), F32),
            pltpu.VMEM((n_s, D_MODEL), F32),
        ],
        input_output_aliases={0: 0},
        compiler_params=_cparams("arbitrary"),
    )(hbuf, lw["g_mix"], lw["w_in"], lw["conv_w"], lw["conv_b"], lw["gate_b"], lw["hn_g"], lw["vn_g"],
      lw["w00_row"], lw["bs0_row"], lw["w_out"], sconv_t, c0, n0_t, m0_pad)


def _router_kernel(h_ref, g_ref, whi_ref, wlo_ref, b_ref, route_ref, cnt_ref, cnt_s, *, tm):
    i = pl.program_id(0)

    @pl.when(i == 0)
    def _():
        cnt_s[...] = jnp.zeros_like(cnt_s)

    hn = _rms(h_ref[...], g_ref[...])
    hi = hn.astype(BF16)
    lo = (hn - hi.astype(F32)).astype(BF16)
    logits = (jnp.dot(hi, whi_ref[...], preferred_element_type=F32)
              + jnp.dot(lo, whi_ref[...], preferred_element_type=F32)
              + jnp.dot(hi, wlo_ref[...], preferred_element_type=F32)) + b_ref[...]

    lane_i = lax.broadcasted_iota(jnp.int32, (tm, LANES), 1)
    lane = lane_i.astype(F32)
    neg = -jnp.inf
    is_g = lane_i < N_GROUPS
    gl = jnp.where(is_g, logits, neg)
    gmax = jnp.max(gl, axis=-1, keepdims=True)
    g_sel = jnp.min(jnp.where(gl == gmax, lane, float(LANES)), axis=-1, keepdims=True)
    g_w = 1.0 / jnp.sum(jnp.where(is_g, jnp.exp(logits - gmax), 0.0), axis=-1, keepdims=True)

    e_lane = lane_i - N_GROUPS
    lane_grp = (e_lane >> 3).astype(F32)
    in_grp = (e_lane >= 0) & (e_lane < N_EXPERTS) & (lane_grp == g_sel)
    el = jnp.where(in_grp, logits, neg)
    v1 = jnp.max(el, axis=-1, keepdims=True)
    i1 = jnp.min(jnp.where(el == v1, lane, float(LANES)), axis=-1, keepdims=True)
    el2 = jnp.where(lane == i1, neg, el)
    v2 = jnp.max(el2, axis=-1, keepdims=True)
    i2 = jnp.min(jnp.where(el2 == v2, lane, float(LANES)), axis=-1, keepdims=True)
    e21 = jnp.exp(v2 - v1)
    den = 1.0 + e21
    w1 = g_w * (1.0 / den)
    w2 = g_w * (e21 / den)

    onehot = jnp.where((lane == i1) | (lane == i2), 1.0, 0.0)
    r_i = lax.broadcasted_iota(jnp.int32, (tm, tm), 0)
    c_i = lax.broadcasted_iota(jnp.int32, (tm, tm), 1)
    strict = jnp.where(r_i > c_i, 1.0, 0.0).astype(BF16)
    before = jnp.dot(strict, onehot.astype(BF16), preferred_element_type=F32) + cnt_s[0:1, :]
    rank1 = jnp.sum(jnp.where(lane == i1, before, 0.0), axis=-1, keepdims=True)
    rank2 = jnp.sum(jnp.where(lane == i2, before, 0.0), axis=-1, keepdims=True)
    cnt_new = cnt_s[0:1, :] + jnp.sum(onehot, axis=0, keepdims=True)
    cnt_s[...] = jnp.broadcast_to(cnt_new, cnt_s.shape)
    cnt_ref[...] = jnp.broadcast_to(cnt_new, cnt_ref.shape)

    e1 = i1 - float(N_GROUPS)
    e2 = i2 - float(N_GROUPS)
    out = jnp.zeros((tm, LANES), F32)
    for idx, val in enumerate((e1, e2, rank1, rank2, w1, w2)):
        out = jnp.where(lane_i == idx, val, out)
    route_ref[...] = out


def _router(h1, lw):
    t_total = h1.shape[0]
    tm = _token_tile(t_total)
    const = lambda i: (0, 0)
    return pl.pallas_call(
        functools.partial(_router_kernel, tm=tm),
        name="router",
        grid=(t_total // tm,),
        in_specs=[
            pl.BlockSpec((tm, D_MODEL), lambda i: (i, 0)),
            pl.BlockSpec((1, D_MODEL), const),
            pl.BlockSpec((D_MODEL, LANES), const),
            pl.BlockSpec((D_MODEL, LANES), const),
            pl.BlockSpec((1, LANES), const),
        ],
        out_specs=[
            pl.BlockSpec((tm, LANES), lambda i: (i, 0)),
            pl.BlockSpec((SUBLANES, LANES), const),
        ],
        out_shape=[
            jax.ShapeDtypeStruct((t_total, LANES), F32),
            jax.ShapeDtypeStruct((SUBLANES, LANES), F32),
        ],
        scratch_shapes=[pltpu.VMEM((SUBLANES, LANES), F32)],
        compiler_params=_cparams("arbitrary"),
    )(h1, lw["g_ffn"], lw["wr_hi"], lw["wr_lo"], lw["br"])


def _token_tile(t_total):
    tm = TM_TOK
    while t_total % tm:
        tm //= 2
    return tm


def _dispatch_kernel(dest_ref, h_ref, g_ref, xs_in, xs_out, xn_s, sem, *, tm):
    del xs_in
    i = pl.program_id(0)
    xn_s[...] = _rms(h_ref[...], g_ref[...])
    base = i * (tm * TOP_K)

    def issue(r, carry):
        for kk in range(TOP_K):
            d = dest_ref[base + r * TOP_K + kk]
            pltpu.make_async_copy(xn_s.at[pl.ds(r, 1)], xs_out.at[pl.ds(d, 1)], sem).start()
        return carry

    lax.fori_loop(0, tm, issue, 0)
    for kk in range(TOP_K):
        pltpu.make_async_copy(xn_s, xs_out.at[pl.ds(0, tm)], sem).wait()


def _dispatch(dest_flat, h1, lw, xs):
    t_total = h1.shape[0]
    tm = _token_tile(t_total)
    return pl.pallas_call(
        functools.partial(_dispatch_kernel, tm=tm),
        name="dispatch",
        grid_spec=pltpu.PrefetchScalarGridSpec(
            num_scalar_prefetch=1,
            grid=(t_total // tm,),
            in_specs=[
                pl.BlockSpec((tm, D_MODEL), lambda i, d: (i, 0)),
                pl.BlockSpec((1, D_MODEL), lambda i, d: (0, 0)),
                pl.BlockSpec(memory_space=pl.ANY),
            ],
            out_specs=pl.BlockSpec(memory_space=pl.ANY),
            scratch_shapes=[pltpu.VMEM((tm, D_MODEL), F32), pltpu.SemaphoreType.DMA(())],
        ),
        out_shape=jax.ShapeDtypeStruct(xs.shape, F32),
        input_output_aliases={3: 0},
        compiler_params=_cparams("arbitrary"),
    )(dest_flat, h1, lw["g_ffn"], xs)


def _ffn_kernel(be_ref, nu_ref, x_ref, w1_ref, w3_ref, w2_ref, y_ref, w1b, w3b, w2b):
    i = pl.program_id(0)
    prev = be_ref[jnp.maximum(i - 1, 0)]
    fresh = (i == 0) | (be_ref[i] != prev)

    @pl.when(fresh)
    def _():
        w1b[...] = w1_ref[...].astype(BF16)
        w3b[...] = w3_ref[...].astype(BF16)
        w2b[...] = w2_ref[...].astype(BF16)

    @pl.when(i < nu_ref[0])
    def _():
        x = x_ref[...].astype(BF16)
        a = jnp.dot(x, w1b[...], preferred_element_type=F32)
        b = jnp.dot(x, w3b[...], preferred_element_type=F32)
        hmid = (a * jax.nn.sigmoid(a) * b).astype(BF16)
        y_ref[...] = jnp.dot(hmid, w2b[...], preferred_element_type=F32)


def _ffn(block_expert, n_used, xs, w1, w3, w2, layer):
    n_rows = xs.shape[0]
    nb = n_rows // FFN_BLK

    def row_map(i, be, nu):
        return (jnp.minimum(i, nu[0] - 1), 0)

    def w_map(i, be, nu):
        return (layer, be[i], 0, 0)

    return pl.pallas_call(
        _ffn_kernel,
        name="expert_ffn",
        grid_spec=pltpu.PrefetchScalarGridSpec(
            num_scalar_prefetch=2,
            grid=(nb,),
            in_specs=[
                pl.BlockSpec((FFN_BLK, D_MODEL), row_map),
                pl.BlockSpec((None, None, D_MODEL, D_FF), w_map),
                pl.BlockSpec((None, None, D_MODEL, D_FF), w_map),
                pl.BlockSpec((None, None, D_FF, D_MODEL), w_map),
            ],
            out_specs=pl.BlockSpec((FFN_BLK, D_MODEL), row_map),
            scratch_shapes=[
                pltpu.VMEM((D_MODEL, D_FF), BF16),
                pltpu.VMEM((D_MODEL, D_FF), BF16),
                pltpu.VMEM((D_FF, D_MODEL), BF16),
            ],
        ),
        out_shape=jax.ShapeDtypeStruct((n_rows, D_MODEL), F32),
        input_output_aliases={2: 0},
        compiler_params=_cparams("arbitrary"),
    )(block_expert, n_used, xs, w1, w3, w2)


def _ple_kernel(dest_ref, h_ref, route_ref, p_ref, wpg_ref, wple_ref, pg_ref, fg_ref, ys_ref,
                *rest, tm, n_s, final):
    if final:
        yp_ref, ysm_ref, ybuf, sem = rest
    else:
        out_ref, ybuf, sem = rest
    i = pl.program_id(0)
    base = i * (tm * TOP_K)

    def issue(r, carry):
        for kk in range(TOP_K):
            d = dest_ref[base + r * TOP_K + kk]
            pltpu.make_async_copy(ys_ref.at[pl.ds(d, 1)], ybuf.at[kk, pl.ds(r, 1)], sem).start()
        return carry

    lax.fori_loop(0, tm, issue, 0)
    pe = _rms(jnp.dot(p_ref[...].astype(BF16), wple_ref[...], preferred_element_type=F32), pg_ref[...])
    for kk in range(TOP_K):
        pltpu.make_async_copy(ys_ref.at[pl.ds(0, tm)], ybuf.at[kk], sem).wait()
    route = route_ref[...]
    moe = route[:, 4:5] * ybuf[0] + route[:, 5:6] * ybuf[1]
    h2 = h_ref[...] + moe
    gate = jax.nn.sigmoid(jnp.dot(h2.astype(BF16), wpg_ref[...], preferred_element_type=F32))
    out = h2 + gate * pe
    if final:
        fin = _rms(out, fg_ref[...])
        yp_ref[...] = fin

        @pl.when(i == pl.num_programs(0) - 1)
        def _():
            ysm_ref[...] = fin[tm - n_s:tm, :]
    else:
        out_ref[...] = out


def _ple(dest_flat, h1, route, p_l, lw, final_g, ys, n_s, final):
    t_total = h1.shape[0]
    tm = _token_tile(t_total)
    assert n_s <= tm
    const = lambda i, d: (0, 0)
    tok = lambda i, d: (i, 0)
    if final:
        out_specs = [pl.BlockSpec((tm, D_MODEL), tok), pl.BlockSpec((n_s, D_MODEL), const)]
        out_shape = [jax.ShapeDtypeStruct((t_total - n_s, D_MODEL), F32),
                     jax.ShapeDtypeStruct((n_s, D_MODEL), F32)]
        aliases = {}
    else:
        out_specs = [pl.BlockSpec((tm, D_MODEL), tok)]
        out_shape = [jax.ShapeDtypeStruct((t_total, D_MODEL), F32)]
        aliases = {1: 0}
    return pl.pallas_call(
        functools.partial(_ple_kernel, tm=tm, n_s=n_s, final=final),
        name="combine_ple",
        grid_spec=pltpu.PrefetchScalarGridSpec(
            num_scalar_prefetch=1,
            grid=(t_total // tm,),
            in_specs=[
                pl.BlockSpec((tm, D_MODEL), tok),
                pl.BlockSpec((tm, LANES), tok),
                pl.BlockSpec((tm, PLE_DIM), tok),
                pl.BlockSpec((D_MODEL, D_MODEL), const),
                pl.BlockSpec((PLE_DIM, D_MODEL), const),
                pl.BlockSpec((1, D_MODEL), const),
                pl.BlockSpec((1, D_MODEL), const),
                pl.BlockSpec(memory_space=pl.ANY),
            ],
            out_specs=out_specs,
            scratch_shapes=[pltpu.VMEM((TOP_K, tm, D_MODEL), F32), pltpu.SemaphoreType.DMA(())],
        ),
        out_shape=out_shape,
        input_output_aliases=aliases,
        compiler_params=_cparams("arbitrary"),
    )(dest_flat, h1, route, p_l, lw["w_pg"], lw["w_ple"], lw["ple_g"], final_g, ys)


def _layer_weights(l, norm_mix_g, w_in, conv_w, conv_b, b_igate, b_fgate, hnorm_g, vnorm_g, w_s, b_s, w_out,
                   norm_ffn_g, router_g_w, router_g_b, router_e_w, router_e_b, w_ple, ple_norm_g, w_ple_gate):
    wi = w_in[l]
    a0 = QK_W + 2 * W_A
    gates_w = jnp.pad(wi[:, a0:a0 + 2 * H_A], ((0, 0), (0, LANES - 2 * H_A)))
    w_packed = jnp.concatenate(
        [wi[:, :a0], wi[:, a0 + 2 * H_A:], gates_w], axis=1).astype(BF16)
    gate_b = jnp.pad(jnp.concatenate([b_igate[l], b_fgate[l]]), (0, LANES - 2 * H_A))[None, :]
    wr = jnp.pad(jnp.concatenate([router_g_w[l], router_e_w[l]], axis=1),
                 ((0, 0), (0, LANES - N_GROUPS - N_EXPERTS)))
    wr_hi = wr.astype(BF16)
    wr_lo = (wr - wr_hi.astype(F32)).astype(BF16)
    br = jnp.pad(jnp.concatenate([router_g_b[l], router_e_b[l]]), (0, LANES - N_GROUPS - N_EXPERTS))[None, :]
    return {
        "g_mix": norm_mix_g[l][None, :],
        "w_in": w_packed,
        "conv_w": conv_w[l],
        "conv_b": conv_b[l][None, :],
        "gate_b": gate_b,
        "hn_g": hnorm_g[l][None, :],
        "vn_g": vnorm_g[l][None, :],
        "w_s": w_s[l],
        "b_s_col": jnp.pad(b_s[l].T, ((0, 0), (0, LANES - G_B))),
        "w00_row": jnp.repeat(w_s[l][:, 0, 0], DG_B)[None, :],
        "bs0_row": jnp.repeat(b_s[l][:, 0], DG_B)[None, :],
        "w_out": w_out[l].astype(BF16),
        "g_ffn": norm_ffn_g[l][None, :],
        "wr_hi": wr_hi,
        "wr_lo": wr_lo,
        "br": br,
        "w_pg": w_ple_gate[l].astype(BF16),
        "w_ple": w_ple[l].astype(BF16),
        "ple_g": ple_norm_g[l][None, :],
    }


def _routing_tables(route, counts_row, n_blocks):
    counts = counts_row[N_GROUPS:N_GROUPS + N_EXPERTS].astype(jnp.int32)
    blocks = (counts + FFN_BLK - 1) // FFN_BLK
    blk_end = jnp.cumsum(blocks)
    pstart = (blk_end - blocks) * FFN_BLK
    e = route[:, 0:TOP_K].astype(jnp.int32)
    rank = route[:, TOP_K:2 * TOP_K].astype(jnp.int32)
    dest = (pstart[e] + rank).reshape(-1)
    n_used = blk_end[-1:]
    blk = jnp.arange(n_blocks, dtype=jnp.int32)
    block_expert = jnp.minimum(jnp.searchsorted(blk_end, jnp.minimum(blk, n_used[0] - 1), side="right"),
                               N_EXPERTS - 1).astype(jnp.int32)
    return dest, block_expert, n_used.astype(jnp.int32)


def kernel(x_prompt, x_sample, state_C, state_n, state_m, state_conv, p_prompt, p_sample, norm_mix_g, w_in, conv_w, conv_b, b_igate, b_fgate, hnorm_g, vnorm_g, w_s, b_s, w_out, norm_ffn_g, router_g_w, router_g_b, router_e_w, router_e_b, w1, w3, w2, w_ple, ple_norm_g, w_ple_gate, final_norm_g):
    n_batch, seq, _ = x_prompt.shape
    n_s = x_sample.shape[0]
    depth = w_in.shape[0]
    t_p = n_batch * seq
    t_total = t_p + n_s
    assert seq % CHUNK == 0 and t_p % n_s == 0 and n_s % BT_SAMPLE == 0

    n_assign = t_total * TOP_K
    n_blocks = (n_assign + N_EXPERTS * (FFN_BLK - 1)) // FFN_BLK
    xs = jnp.zeros((n_blocks * FFN_BLK, D_MODEL), F32)

    p_all = jnp.concatenate([p_prompt.reshape(depth, t_p, PLE_DIM), p_sample.reshape(depth, n_s, PLE_DIM)], axis=1)
    sconv_t = jnp.transpose(state_conv, (0, 2, 1, 3))
    n0_t = jnp.transpose(state_n, (0, 2, 1, 3))
    m0_pad = jnp.pad(state_m, ((0, 0), (0, 0), (0, LANES - H_A)))
    final_g = final_norm_g[None, :]

    hbuf = jnp.concatenate([x_prompt.reshape(t_p, D_MODEL), x_sample.reshape(n_s, D_MODEL)], axis=0)
    outs = {k: [] for k in ("Cp", "np", "mp", "cbp", "Cs", "ns", "ms", "cbs", "vs")}
    res = None
    for l in range(depth):
        lw = _layer_weights(l, norm_mix_g, w_in, conv_w, conv_b, b_igate, b_fgate, hnorm_g, vnorm_g, w_s, b_s,
                            w_out, norm_ffn_g, router_g_w, router_g_b, router_e_w, router_e_b, w_ple,
                            ple_norm_g, w_ple_gate)
        hbuf, c_p, n_p, m_p, cb_p = _mix_prompt(hbuf, n_batch, seq, lw)
        hbuf, c_s, n_s_t, m_s, cb_s, v_s = _mix_sample(hbuf, lw, sconv_t[l], state_C[l], n0_t[l], m0_pad[l], n_s)
        route, counts = _router(hbuf, lw)
        dest, block_expert, n_used = _routing_tables(route, counts[0], n_blocks)
        xs = _dispatch(dest, hbuf, lw, xs)
        xs = _ffn(block_expert, n_used, xs, w1, w3, w2, l)
        final = l == depth - 1
        res = _ple(dest, hbuf, route, p_all[l], lw, final_g, xs, n_s, final)
        hbuf = res[0]
        outs["Cp"].append(c_p)
        outs["np"].append(n_p)
        outs["mp"].append(m_p[:, :, 0])
        outs["cbp"].append(cb_p)
        outs["Cs"].append(c_s)
        outs["ns"].append(jnp.transpose(n_s_t, (1, 0, 2)))
        outs["ms"].append(m_s[:, 0:H_A])
        outs["cbs"].append(jnp.transpose(cb_s, (1, 0, 2)))
        outs["vs"].append(v_s[:, None, :])

    y_prompt = res[0].reshape(n_batch, seq, D_MODEL)
    y_sample = res[1].reshape(n_s, 1, D_MODEL)
    st = lambda k: jnp.stack(outs[k])
    return (y_prompt, y_sample, st("Cp"), st("np"), st("mp"), st("cbp"),
            st("Cs"), st("ns"), st("ms"), st("cbs"), st("vs"))
```

```python
import functools

import jax
import jax.numpy as jnp
from jax import lax
from jax.experimental import pallas as pl
from jax.experimental.pallas import tpu as pltpu

F32 = jnp.float32
BF16 = jnp.bfloat16

D_MODEL = 1024
W_A = 512
H_A = 4
DH_A = 128
W_B = 512
G_B = 4
DG_B = 128
CHUNK = 128
CONV_W = 4
QK_W = 2 * W_A
N_GROUPS = 4
EXPERTS_PER_GROUP = 8
N_EXPERTS = N_GROUPS * EXPERTS_PER_GROUP
TOP_K = 2
D_FF = 512
PLE_DIM = 256
EPS = 1e-6

LANES = 128
SUBLANES = 8
VMEM_LIMIT_BYTES = 56 * 1024 * 1024

P_QK = 0
P_V = QK_W
P_O = P_V + W_A
P_U = P_O + W_A
P_VB = P_U + W_B
P_G = P_VB + W_B
P_W = P_G + LANES

TL_MIX = 512
BT_SAMPLE = 8
TM_TOK = 384
FFN_BLK = 256


def _cparams(*sem):
    return pltpu.CompilerParams(dimension_semantics=sem, vmem_limit_bytes=VMEM_LIMIT_BYTES)


def _rms(x, g):
    return x * lax.rsqrt(jnp.mean(x * x, axis=-1, keepdims=True) + EPS) * g


def _log_sigmoid(x):
    return -(jnp.maximum(-x, 0.0) + jnp.log1p(jnp.exp(-jnp.abs(x))))


def _split_dot(a, b_bf16):
    hi = a.astype(BF16)
    lo = (a - hi.astype(F32)).astype(BF16)
    return (jnp.dot(hi, b_bf16, preferred_element_type=F32)
            + jnp.dot(lo, b_bf16, preferred_element_type=F32))


def _dot_nt(a, b):
    return lax.dot_general(a, b, (((1,), (1,)), ((), ())), preferred_element_type=F32)


def _mix_prompt_kernel(x_ref, gmix_ref, win_ref, cw_ref, cb_ref, gb_ref, hng_ref, vng_ref,
                       ws_ref, bs_ref, wout_ref,
                       out_ref, c_out, n_out, m_out, conv_out,
                       proj, xpad, hcat, c_s, n_s, m_s, *, tl):
    j = pl.program_id(1)
    nj = pl.num_programs(1)
    n_chunks = tl // CHUNK

    @pl.when(j == 0)
    def _():
        c_s[...] = jnp.zeros_like(c_s)
        n_s[...] = jnp.zeros_like(n_s)
        m_s[...] = jnp.zeros_like(m_s)
        xpad[0:SUBLANES, :] = jnp.zeros((SUBLANES, QK_W), F32)

    x = x_ref[...]
    xn = _rms(x, gmix_ref[...]).astype(BF16)
    proj[...] = jnp.dot(xn, win_ref[...], preferred_element_type=F32)

    pre = proj[:, P_QK:P_QK + QK_W]
    xpad[SUBLANES:SUBLANES + tl, :] = pre
    qk = cb_ref[...] + cw_ref[CONV_W - 1:CONV_W, :] * pre
    for jj in range(1, CONV_W):
        qk = qk + cw_ref[CONV_W - 1 - jj:CONV_W - jj, :] * xpad[SUBLANES - jj:SUBLANES - jj + tl, :]
    new_rows = pre[tl - (CONV_W - 1):tl, :]
    xpad[SUBLANES - (CONV_W - 1):SUBLANES, :] = new_rows

    @pl.when(j == nj - 1)
    def _():
        conv_out[0] = new_rows

    qk = qk * jax.nn.sigmoid(qk)
    proj[:, 0:W_A] = qk[:, 0:W_A]
    proj[:, W_A:QK_W] = qk[:, W_A:QK_W] * (DH_A ** -0.5)

    row = lax.broadcasted_iota(jnp.int32, (CHUNK, CHUNK), 0)
    col = lax.broadcasted_iota(jnp.int32, (CHUNK, CHUNK), 1)
    causal = row >= col
    tril = jnp.where(causal, 1.0, 0.0).astype(BF16)
    triu = jnp.where(row <= col, 1.0, 0.0).astype(BF16)
    lane = lax.broadcasted_iota(jnp.int32, (CHUNK, LANES), 1)
    wm = [jnp.where(causal, ws_ref[g], 0.0).astype(BF16) for g in range(G_B)]

    def chunk_body(c, carry):
        r0 = pl.multiple_of(c * CHUNK, CHUNK)
        rows = pl.ds(r0, CHUNK)
        gl = proj[rows, P_G:P_G + LANES] + gb_ref[...]
        gl = jnp.where(lane >= H_A, _log_sigmoid(gl), gl)
        glt = gl.T
        b_col_all = _split_dot_left(tril, gl)
        b_row_all = _split_dot(glt[0:2 * SUBLANES, :], triu)

        for h in range(H_A):
            q32 = proj[rows, h * DH_A:(h + 1) * DH_A]
            k32 = proj[rows, W_A + h * DH_A:W_A + (h + 1) * DH_A]
            v32 = proj[rows, P_V + h * DH_A:P_V + (h + 1) * DH_A]
            q = q32.astype(BF16)
            k = k32.astype(BF16)
            ig_c = gl[:, h:h + 1]
            ig_r = glt[h:h + 1, :]
            b_c = b_col_all[:, H_A + h:H_A + h + 1]
            b_r = b_row_all[H_A + h:H_A + h + 1, :]
            c_old = c_s[h]
            n_old = n_s[h:h + 1, :]
            m_prev = m_s[h:h + 1, 0:1]

            d_log = jnp.where(causal, b_c - b_r + ig_r, -jnp.inf)
            inter = b_c + m_prev
            m_t = jnp.maximum(inter, jnp.max(d_log, axis=-1, keepdims=True))
            s = _dot_nt(q, k) * jnp.exp(d_log - m_t)
            w_inter = jnp.exp(inter - m_t)
            num = (w_inter * _dot_nt(q, c_old.astype(BF16))
                   + jnp.dot(s.astype(BF16), v32.astype(BF16), preferred_element_type=F32))
            nq = (w_inter * jnp.sum(q32 * n_old, axis=-1, keepdims=True)
                  + jnp.sum(s, axis=-1, keepdims=True))
            hh = num / jnp.maximum(jnp.abs(nq), jnp.exp(-m_t))

            m_new = m_t[CHUNK - 1:CHUNK, :]
            b_last = b_c[CHUNK - 1:CHUNK, :]
            w_state = jnp.exp(b_last - b_c + ig_c - m_new)
            decay = jnp.exp(b_last + m_prev - m_new)
            vw = (v32 * w_state).astype(BF16)
            c_s[h] = decay * c_old + lax.dot_general(
                vw, k, (((0,), (0,)), ((), ())), preferred_element_type=F32)
            n_s[h:h + 1, :] = decay * n_old + jnp.sum(w_state * k32, axis=0, keepdims=True)
            m_s[h:h + 1, :] = jnp.broadcast_to(m_new, (1, LANES))

            ha = _rms(hh, hng_ref[:, h * DH_A:(h + 1) * DH_A])
            o = proj[rows, P_O + h * DH_A:P_O + (h + 1) * DH_A]
            hcat[rows, h * DH_A:(h + 1) * DH_A] = (ha * jax.nn.sigmoid(o)).astype(BF16)

        for g in range(G_B):
            vb = proj[rows, P_VB + g * DG_B:P_VB + (g + 1) * DG_B]
            u = proj[rows, P_U + g * DG_B:P_U + (g + 1) * DG_B]
            vn = _rms(jax.nn.gelu(vb), vng_ref[:, g * DG_B:(g + 1) * DG_B])
            z = jnp.dot(wm[g], vn.astype(BF16), preferred_element_type=F32) + bs_ref[:, g:g + 1]
            hcat[rows, W_A + g * DG_B:W_A + (g + 1) * DG_B] = (jax.nn.gelu(u) * z).astype(BF16)
        return carry

    lax.fori_loop(0, n_chunks, chunk_body, 0)

    out_ref[...] = x + jnp.dot(hcat[...], wout_ref[...], preferred_element_type=F32)

    @pl.when(j == nj - 1)
    def _():
        c_out[0] = c_s[...]
        n_out[0] = n_s[0:H_A, :]
        m_out[0] = m_s[0:H_A, :]


def _split_dot_left(a_bf16, b):
    hi = b.astype(BF16)
    lo = (b - hi.astype(F32)).astype(BF16)
    return (jnp.dot(a_bf16, hi, preferred_element_type=F32)
            + jnp.dot(a_bf16, lo, preferred_element_type=F32))


def _mix_prompt(hbuf, n_batch, seq, lw):
    tl = max(t for t in range(CHUNK, TL_MIX + 1, CHUNK) if seq % t == 0)
    nj = seq // tl
    const = lambda b, j: (0, 0)
    kern = functools.partial(_mix_prompt_kernel, tl=tl)
    return pl.pallas_call(
        kern,
        name="mix_prompt",
        grid=(n_batch, nj),
        in_specs=[
            pl.BlockSpec((tl, D_MODEL), lambda b, j: (b * nj + j, 0)),
            pl.BlockSpec((1, D_MODEL), const),
            pl.BlockSpec((D_MODEL, P_W), const),
            pl.BlockSpec((CONV_W, QK_W), const),
            pl.BlockSpec((1, QK_W), const),
            pl.BlockSpec((1, LANES), const),
            pl.BlockSpec((1, W_A), const),
            pl.BlockSpec((1, W_B), const),
            pl.BlockSpec((G_B, CHUNK, CHUNK), lambda b, j: (0, 0, 0)),
            pl.BlockSpec((CHUNK, LANES), const),
            pl.BlockSpec((D_MODEL, D_MODEL), const),
        ],
        out_specs=[
            pl.BlockSpec((tl, D_MODEL), lambda b, j: (b * nj + j, 0)),
            pl.BlockSpec((1, H_A, DH_A, DH_A), lambda b, j: (b, 0, 0, 0)),
            pl.BlockSpec((1, H_A, DH_A), lambda b, j: (b, 0, 0)),
            pl.BlockSpec((1, H_A, LANES), lambda b, j: (b, 0, 0)),
            pl.BlockSpec((1, CONV_W - 1, QK_W), lambda b, j: (b, 0, 0)),
        ],
        out_shape=[
            jax.ShapeDtypeStruct(hbuf.shape, F32),
            jax.ShapeDtypeStruct((n_batch, H_A, DH_A, DH_A), F32),
            jax.ShapeDtypeStruct((n_batch, H_A, DH_A), F32),
            jax.ShapeDtypeStruct((n_batch, H_A, LANES), F32),
            jax.ShapeDtypeStruct((n_batch, CONV_W - 1, QK_W), F32),
        ],
        scratch_shapes=[
            pltpu.VMEM((tl, P_W), F32),
            pltpu.VMEM((SUBLANES + tl, QK_W), F32),
            pltpu.VMEM((tl, D_MODEL), BF16),
            pltpu.VMEM((H_A, DH_A, DH_A), F32),
            pltpu.VMEM((SUBLANES, LANES), F32),
            pltpu.VMEM((SUBLANES, LANES), F32),
        ],
        input_output_aliases={0: 0},
        compiler_params=_cparams("arbitrary", "arbitrary"),
    )(hbuf, lw["g_mix"], lw["w_in"], lw["conv_w"], lw["conv_b"], lw["gate_b"], lw["hn_g"], lw["vn_g"],
      lw["w_s"], lw["b_s_col"], lw["w_out"])


def _mix_sample_kernel(x_ref, gmix_ref, win_ref, cw_ref, cb_ref, gb_ref, hng_ref, vng_ref,
                       w00_ref, bs0_ref, wout_ref, sconv_ref, c_in, n_in, m_in,
                       out_ref, c_out, n_out, m_out, conv_out, vrow_out,
                       proj, hcat, *, bt):
    i = pl.program_id(0)
    ni = pl.num_programs(0)

    @pl.when(i == 0)
    def _():
        x = x_ref[...]
        xn = _rms(x, gmix_ref[...]).astype(BF16)
        proj[...] = jnp.dot(xn, win_ref[...], preferred_element_type=F32)
        pre = proj[:, P_QK:P_QK + QK_W]
        qk = cb_ref[...] + cw_ref[CONV_W - 1:CONV_W, :] * pre
        for jj in range(CONV_W - 1):
            qk = qk + cw_ref[jj:jj + 1, :] * sconv_ref[jj]
        for jj in range(CONV_W - 2):
            conv_out[jj] = sconv_ref[jj + 1]
        conv_out[CONV_W - 2] = pre
        qk = qk * jax.nn.sigmoid(qk)
        proj[:, 0:W_A] = qk[:, 0:W_A]
        proj[:, W_A:QK_W] = qk[:, W_A:QK_W] * (DH_A ** -0.5)
        for g in range(G_B):
            sl = slice(g * DG_B, (g + 1) * DG_B)
            vb = proj[:, P_VB + g * DG_B:P_VB + (g + 1) * DG_B]
            u = proj[:, P_U + g * DG_B:P_U + (g + 1) * DG_B]
            vn = _rms(jax.nn.gelu(vb), vng_ref[:, sl])
            vrow_out[:, sl] = vn
            z = w00_ref[:, sl] * vn + bs0_ref[:, sl]
            hcat[:, W_A + g * DG_B:W_A + (g + 1) * DG_B] = jax.nn.gelu(u) * z

    rows = pl.ds(pl.multiple_of(i * bt, bt), bt)
    gates = proj[rows, P_G:P_G + LANES] + gb_ref[...]
    lane = lax.broadcasted_iota(jnp.int32, (bt, LANES), 1)
    sub = lax.broadcasted_iota(jnp.int32, (bt, DH_A), 0)
    m_new_all = jnp.zeros((bt, LANES), F32)
    for h in range(H_A):
        ig = gates[:, h:h + 1]
        lf = _log_sigmoid(gates[:, H_A + h:H_A + h + 1])
        m_prev = m_in[rows, h:h + 1]
        inter = lf + m_prev
        m_t = jnp.maximum(inter, ig)
        w_inter = jnp.exp(inter - m_t)
        e_d = jnp.exp(ig - m_t)
        q = proj[rows, h * DH_A:(h + 1) * DH_A]
        k = proj[rows, W_A + h * DH_A:W_A + (h + 1) * DH_A]
        v = proj[rows, P_V + h * DH_A:P_V + (h + 1) * DH_A]
        n_old = n_in[h]
        s = jnp.sum(q * k, axis=-1, keepdims=True) * e_d
        vw = v * e_d
        vw_t = jnp.concatenate([vw, jnp.zeros((DH_A - bt, DH_A), F32)], axis=0).T
        qb = q.astype(BF16)
        qc = jnp.zeros((bt, DH_A), F32)
        for t in range(bt):
            c_old = c_in[t, h]
            r = _dot_nt(qb, c_old.astype(BF16))
            qc = jnp.where(sub == t, r, qc)
            c_out[t, h] = w_inter[t:t + 1, :] * c_old + vw_t[:, t:t + 1] * k[t:t + 1, :]
        num = w_inter * qc + s * v
        nq = w_inter * jnp.sum(q * n_old, axis=-1, keepdims=True) + s
        hh = num / jnp.maximum(jnp.abs(nq), jnp.exp(-m_t))
        n_out[h] = w_inter * n_old + e_d * k
        m_new_all = jnp.where(lane == h, m_t, m_new_all)
        ha = _rms(hh, hng_ref[:, h * DH_A:(h + 1) * DH_A])
        o = proj[rows, P_O + h * DH_A:P_O + (h + 1) * DH_A]
        hcat[rows, h * DH_A:(h + 1) * DH_A] = ha * jax.nn.sigmoid(o)
    m_out[rows, :] = m_new_all

    @pl.when(i == ni - 1)
    def _():
        out_ref[...] = x_ref[...] + jnp.dot(hcat[...].astype(BF16), wout_ref[...],
                                            preferred_element_type=F32)


def _mix_sample(hbuf, lw, sconv_t, c0, n0_t, m0_pad, n_s, layer):
    bt = BT_SAMPLE
    ni = n_s // bt
    const = lambda i: (0, 0)
    out_blk = hbuf.shape[0] // n_s - 1
    kern = functools.partial(_mix_sample_kernel, bt=bt)
    return pl.pallas_call(
        kern,
        name="mix_sample",
        grid=(ni,),
        in_specs=[
            pl.BlockSpec((n_s, D_MODEL), lambda i: (out_blk, 0)),
            pl.BlockSpec((1, D_MODEL), const),
            pl.BlockSpec((D_MODEL, P_W), const),
            pl.BlockSpec((CONV_W, QK_W), const),
            pl.BlockSpec((1, QK_W), const),
            pl.BlockSpec((1, LANES), const),
            pl.BlockSpec((1, W_A), const),
            pl.BlockSpec((1, W_B), const),
            pl.BlockSpec((1, W_B), const),
            pl.BlockSpec((1, W_B), const),
            pl.BlockSpec((D_MODEL, D_MODEL), const),
            pl.BlockSpec((None, CONV_W - 1, n_s, QK_W), lambda i: (layer, 0, 0, 0)),
            pl.BlockSpec((None, bt, H_A, DH_A, DH_A), lambda i: (layer, i, 0, 0, 0)),
            pl.BlockSpec((None, H_A, bt, DH_A), lambda i: (layer, 0, i, 0)),
            pl.BlockSpec((None, n_s, LANES), lambda i: (layer, 0, 0)),
        ],
        out_specs=[
            pl.BlockSpec((n_s, D_MODEL), lambda i: (out_blk, 0)),
            pl.BlockSpec((bt, H_A, DH_A, DH_A), lambda i: (i, 0, 0, 0)),
            pl.BlockSpec((H_A, bt, DH_A), lambda i: (0, i, 0)),
            pl.BlockSpec((n_s, LANES), const),
            pl.BlockSpec((CONV_W - 1, n_s, QK_W), lambda i: (0, 0, 0)),
            pl.BlockSpec((n_s, W_B), const),
        ],
        out_shape=[
            jax.ShapeDtypeStruct(hbuf.shape, F32),
            jax.ShapeDtypeStruct(c0.shape[1:], F32),
            jax.ShapeDtypeStruct(n0_t.shape[1:], F32),
            jax.ShapeDtypeStruct((n_s, LANES), F32),
            jax.ShapeDtypeStruct((CONV_W - 1, n_s, QK_W), F32),
            jax.ShapeDtypeStruct((n_s, W_B), F32),
        ],
        scratch_shapes=[
            pltpu.VMEM((n_s, P_W), F32),
            pltpu.VMEM((n_s, D_MODEL), F32),
        ],
        input_output_aliases={0: 0},
        compiler_params=_cparams("arbitrary"),
    )(hbuf, lw["g_mix"], lw["w_in"], lw["conv_w"], lw["conv_b"], lw["gate_b"], lw["hn_g"], lw["vn_g"],
      lw["w00_row"], lw["bs0_row"], lw["w_out"], sconv_t, c0, n0_t, m0_pad)


def _router_kernel(h_ref, g_ref, whi_ref, wlo_ref, b_ref, route_ref, cnt_ref, cnt_s, *, tm):
    i = pl.program_id(0)

    @pl.when(i == 0)
    def _():
        cnt_s[...] = jnp.zeros_like(cnt_s)

    hn = _rms(h_ref[...], g_ref[...])
    hi = hn.astype(BF16)
    lo = (hn - hi.astype(F32)).astype(BF16)
    logits = (jnp.dot(hi, whi_ref[...], preferred_element_type=F32)
              + jnp.dot(lo, whi_ref[...], preferred_element_type=F32)
              + jnp.dot(hi, wlo_ref[...], preferred_element_type=F32)) + b_ref[...]

    lane_i = lax.broadcasted_iota(jnp.int32, (tm, LANES), 1)
    lane = lane_i.astype(F32)
    neg = -jnp.inf
    is_g = lane_i < N_GROUPS
    gl = jnp.where(is_g, logits, neg)
    gmax = jnp.max(gl, axis=-1, keepdims=True)
    g_sel = jnp.min(jnp.where(gl == gmax, lane, float(LANES)), axis=-1, keepdims=True)
    g_w = 1.0 / jnp.sum(jnp.where(is_g, jnp.exp(logits - gmax), 0.0), axis=-1, keepdims=True)

    e_lane = lane_i - N_GROUPS
    lane_grp = (e_lane >> 3).astype(F32)
    in_grp = (e_lane >= 0) & (e_lane < N_EXPERTS) & (lane_grp == g_sel)
    el = jnp.where(in_grp, logits, neg)
    v1 = jnp.max(el, axis=-1, keepdims=True)
    i1 = jnp.min(jnp.where(el == v1, lane, float(LANES)), axis=-1, keepdims=True)
    el2 = jnp.where(lane == i1, neg, el)
    v2 = jnp.max(el2, axis=-1, keepdims=True)
    i2 = jnp.min(jnp.where(el2 == v2, lane, float(LANES)), axis=-1, keepdims=True)
    e21 = jnp.exp(v2 - v1)
    den = 1.0 + e21
    w1 = g_w * (1.0 / den)
    w2 = g_w * (e21 / den)

    onehot = jnp.where((lane == i1) | (lane == i2), 1.0, 0.0)
    r_i = lax.broadcasted_iota(jnp.int32, (tm, tm), 0)
    c_i = lax.broadcasted_iota(jnp.int32, (tm, tm), 1)
    strict = jnp.where(r_i > c_i, 1.0, 0.0).astype(BF16)
    before = jnp.dot(strict, onehot.astype(BF16), preferred_element_type=F32) + cnt_s[0:1, :]
    rank1 = jnp.sum(jnp.where(lane == i1, before, 0.0), axis=-1, keepdims=True)
    rank2 = jnp.sum(jnp.where(lane == i2, before, 0.0), axis=-1, keepdims=True)
    cnt_new = cnt_s[0:1, :] + jnp.sum(onehot, axis=0, keepdims=True)
    cnt_s[...] = jnp.broadcast_to(cnt_new, cnt_s.shape)
    cnt_ref[...] = jnp.broadcast_to(cnt_new, cnt_ref.shape)

    e1 = i1 - float(N_GROUPS)
    e2 = i2 - float(N_GROUPS)
    out = jnp.zeros((tm, LANES), F32)
    for idx, val in enumerate((e1, e2, rank1, rank2, w1, w2)):
        out = jnp.where(lane_i == idx, val, out)
    route_ref[...] = out


def _router(h1, lw):
    t_total = h1.shape[0]
    tm = _token_tile(t_total)
    const = lambda i: (0, 0)
    return pl.pallas_call(
        functools.partial(_router_kernel, tm=tm),
        name="router",
        grid=(t_total // tm,),
        in_specs=[
            pl.BlockSpec((tm, D_MODEL), lambda i: (i, 0)),
            pl.BlockSpec((1, D_MODEL), const),
            pl.BlockSpec((D_MODEL, LANES), const),
            pl.BlockSpec((D_MODEL, LANES), const),
            pl.BlockSpec((1, LANES), const),
        ],
        out_specs=[
            pl.BlockSpec((tm, LANES), lambda i: (i, 0)),
            pl.BlockSpec((SUBLANES, LANES), const),
        ],
        out_shape=[
            jax.ShapeDtypeStruct((t_total, LANES), F32),
            jax.ShapeDtypeStruct((SUBLANES, LANES), F32),
        ],
        scratch_shapes=[pltpu.VMEM((SUBLANES, LANES), F32)],
        compiler_params=_cparams("arbitrary"),
    )(h1, lw["g_ffn"], lw["wr_hi"], lw["wr_lo"], lw["br"])


def _token_tile(t_total):
    tm = TM_TOK
    while t_total % tm:
        tm //= 2
    return tm


def _dispatch_kernel(dest_ref, h_ref, g_ref, xs_in, xs_out, xn_s, sem, *, tm):
    del xs_in
    i = pl.program_id(0)
    xn_s[...] = _rms(h_ref[...], g_ref[...])
    base = i * (tm * TOP_K)

    def issue(r, carry):
        for kk in range(TOP_K):
            d = dest_ref[base + r * TOP_K + kk]
            pltpu.make_async_copy(xn_s.at[pl.ds(r, 1)], xs_out.at[pl.ds(d, 1)], sem).start()
        return carry

    lax.fori_loop(0, tm, issue, 0)
    for kk in range(TOP_K):
        pltpu.make_async_copy(xn_s, xs_out.at[pl.ds(0, tm)], sem).wait()


def _dispatch(dest_flat, h1, lw, xs):
    t_total = h1.shape[0]
    tm = _token_tile(t_total)
    return pl.pallas_call(
        functools.partial(_dispatch_kernel, tm=tm),
        name="dispatch",
        grid_spec=pltpu.PrefetchScalarGridSpec(
            num_scalar_prefetch=1,
            grid=(t_total // tm,),
            in_specs=[
                pl.BlockSpec((tm, D_MODEL), lambda i, d: (i, 0)),
                pl.BlockSpec((1, D_MODEL), lambda i, d: (0, 0)),
                pl.BlockSpec(memory_space=pl.ANY),
            ],
            out_specs=pl.BlockSpec(memory_space=pl.ANY),
            scratch_shapes=[pltpu.VMEM((tm, D_MODEL), F32), pltpu.SemaphoreType.DMA(())],
        ),
        out_shape=jax.ShapeDtypeStruct(xs.shape, F32),
        input_output_aliases={3: 0},
        compiler_params=_cparams("arbitrary"),
    )(dest_flat, h1, lw["g_ffn"], xs)


def _ffn_kernel(be_ref, nu_ref, x_ref, w1_ref, w3_ref, w2_ref, y_ref, w1b, w3b, w2b):
    i = pl.program_id(0)
    prev = be_ref[jnp.maximum(i - 1, 0)]
    fresh = (i == 0) | (be_ref[i] != prev)

    @pl.when(fresh)
    def _():
        w1b[...] = w1_ref[...].astype(BF16)
        w3b[...] = w3_ref[...].astype(BF16)
        w2b[...] = w2_ref[...].astype(BF16)

    @pl.when(i < nu_ref[0])
    def _():
        x = x_ref[...].astype(BF16)
        a = jnp.dot(x, w1b[...], preferred_element_type=F32)
        b = jnp.dot(x, w3b[...], preferred_element_type=F32)
        hmid = (a * jax.nn.sigmoid(a) * b).astype(BF16)
        y_ref[...] = jnp.dot(hmid, w2b[...], preferred_element_type=F32)


def _ffn(block_expert, n_used, xs, w1, w3, w2, layer):
    n_rows = xs.shape[0]
    nb = n_rows // FFN_BLK

    def row_map(i, be, nu):
        return (jnp.minimum(i, nu[0] - 1), 0)

    def w_map(i, be, nu):
        return (layer, be[i], 0, 0)

    return pl.pallas_call(
        _ffn_kernel,
        name="expert_ffn",
        grid_spec=pltpu.PrefetchScalarGridSpec(
            num_scalar_prefetch=2,
            grid=(nb,),
            in_specs=[
                pl.BlockSpec((FFN_BLK, D_MODEL), row_map),
                pl.BlockSpec((None, None, D_MODEL, D_FF), w_map),
                pl.BlockSpec((None, None, D_MODEL, D_FF), w_map),
                pl.BlockSpec((None, None, D_FF, D_MODEL), w_map),
            ],
            out_specs=pl.BlockSpec((FFN_BLK, D_MODEL), row_map),
            scratch_shapes=[
                pltpu.VMEM((D_MODEL, D_FF), BF16),
                pltpu.VMEM((D_MODEL, D_FF), BF16),
                pltpu.VMEM((D_FF, D_MODEL), BF16),
            ],
        ),
        out_shape=jax.ShapeDtypeStruct((n_rows, D_MODEL), F32),
        input_output_aliases={2: 0},
        compiler_params=_cparams("arbitrary"),
    )(block_expert, n_used, xs, w1, w3, w2)


def _ple_kernel(dest_ref, h_ref, route_ref, p_ref, wpg_ref, wple_ref, pg_ref, fg_ref, ys_ref,
                *rest, tm, n_s, final):
    if final:
        yp_ref, ysm_ref, ybuf, sem = rest
    else:
        out_ref, ybuf, sem = rest
    i = pl.program_id(0)
    base = i * (tm * TOP_K)

    def issue(r, carry):
        for kk in range(TOP_K):
            d = dest_ref[base + r * TOP_K + kk]
            pltpu.make_async_copy(ys_ref.at[pl.ds(d, 1)], ybuf.at[kk, pl.ds(r, 1)], sem).start()
        return carry

    lax.fori_loop(0, tm, issue, 0)
    pe = _rms(jnp.dot(p_ref[...].astype(BF16), wple_ref[...], preferred_element_type=F32), pg_ref[...])
    for kk in range(TOP_K):
        pltpu.make_async_copy(ys_ref.at[pl.ds(0, tm)], ybuf.at[kk], sem).wait()
    route = route_ref[...]
    moe = route[:, 4:5] * ybuf[0] + route[:, 5:6] * ybuf[1]
    h2 = h_ref[...] + moe
    gate = jax.nn.sigmoid(jnp.dot(h2.astype(BF16), wpg_ref[...], preferred_element_type=F32))
    out = h2 + gate * pe
    if final:
        fin = _rms(out, fg_ref[...])
        yp_ref[...] = fin

        @pl.when(i == pl.num_programs(0) - 1)
        def _():
            ysm_ref[...] = fin[tm - n_s:tm, :]
    else:
        out_ref[...] = out


def _ple(dest_flat, h1, route, p_all, lw, final_g, ys, n_s, layer, final):
    t_total = h1.shape[0]
    tm = _token_tile(t_total)
    assert n_s <= tm
    const = lambda i, d: (0, 0)
    tok = lambda i, d: (i, 0)
    if final:
        out_specs = [pl.BlockSpec((tm, D_MODEL), tok), pl.BlockSpec((n_s, D_MODEL), const)]
        out_shape = [jax.ShapeDtypeStruct((t_total - n_s, D_MODEL), F32),
                     jax.ShapeDtypeStruct((n_s, D_MODEL), F32)]
        aliases = {}
    else:
        out_specs = [pl.BlockSpec((tm, D_MODEL), tok)]
        out_shape = [jax.ShapeDtypeStruct((t_total, D_MODEL), F32)]
        aliases = {1: 0}
    return pl.pallas_call(
        functools.partial(_ple_kernel, tm=tm, n_s=n_s, final=final),
        name="combine_ple",
        grid_spec=pltpu.PrefetchScalarGridSpec(
            num_scalar_prefetch=1,
            grid=(t_total // tm,),
            in_specs=[
                pl.BlockSpec((tm, D_MODEL), tok),
                pl.BlockSpec((tm, LANES), tok),
                pl.BlockSpec((None, tm, PLE_DIM), lambda i, d: (layer, i, 0)),
                pl.BlockSpec((D_MODEL, D_MODEL), const),
                pl.BlockSpec((PLE_DIM, D_MODEL), const),
                pl.BlockSpec((1, D_MODEL), const),
                pl.BlockSpec((1, D_MODEL), const),
                pl.BlockSpec(memory_space=pl.ANY),
            ],
            out_specs=out_specs,
            scratch_shapes=[pltpu.VMEM((TOP_K, tm, D_MODEL), F32), pltpu.SemaphoreType.DMA(())],
        ),
        out_shape=out_shape,
        input_output_aliases=aliases,
        compiler_params=_cparams("arbitrary"),
    )(dest_flat, h1, route, p_all, lw["w_pg"], lw["w_ple"], lw["ple_g"], final_g, ys)


def _layer_weights(l, norm_mix_g, w_in, conv_w, conv_b, b_igate, b_fgate, hnorm_g, vnorm_g, w_s, b_s, w_out,
                   norm_ffn_g, router_g_w, router_g_b, router_e_w, router_e_b, w_ple, ple_norm_g, w_ple_gate):
    wi = w_in[l]
    a0 = QK_W + 2 * W_A
    gates_w = jnp.pad(wi[:, a0:a0 + 2 * H_A], ((0, 0), (0, LANES - 2 * H_A)))
    w_packed = jnp.concatenate(
        [wi[:, :a0], wi[:, a0 + 2 * H_A:], gates_w], axis=1).astype(BF16)
    gate_b = jnp.pad(jnp.concatenate([b_igate[l], b_fgate[l]]), (0, LANES - 2 * H_A))[None, :]
    wr = jnp.pad(jnp.concatenate([router_g_w[l], router_e_w[l]], axis=1),
                 ((0, 0), (0, LANES - N_GROUPS - N_EXPERTS)))
    wr_hi = wr.astype(BF16)
    wr_lo = (wr - wr_hi.astype(F32)).astype(BF16)
    br = jnp.pad(jnp.concatenate([router_g_b[l], router_e_b[l]]), (0, LANES - N_GROUPS - N_EXPERTS))[None, :]
    return {
        "g_mix": norm_mix_g[l][None, :],
        "w_in": w_packed,
        "conv_w": conv_w[l],
        "conv_b": conv_b[l][None, :],
        "gate_b": gate_b,
        "hn_g": hnorm_g[l][None, :],
        "vn_g": vnorm_g[l][None, :],
        "w_s": w_s[l],
        "b_s_col": jnp.pad(b_s[l].T, ((0, 0), (0, LANES - G_B))),
        "w00_row": jnp.repeat(w_s[l][:, 0, 0], DG_B)[None, :],
        "bs0_row": jnp.repeat(b_s[l][:, 0], DG_B)[None, :],
        "w_out": w_out[l].astype(BF16),
        "g_ffn": norm_ffn_g[l][None, :],
        "wr_hi": wr_hi,
        "wr_lo": wr_lo,
        "br": br,
        "w_pg": w_ple_gate[l].astype(BF16),
        "w_ple": w_ple[l].astype(BF16),
        "ple_g": ple_norm_g[l][None, :],
    }


def _routing_tables(route, counts_row, n_blocks):
    counts = counts_row[N_GROUPS:N_GROUPS + N_EXPERTS].astype(jnp.int32)
    blocks = (counts + FFN_BLK - 1) // FFN_BLK
    blk_end = jnp.cumsum(blocks)
    pstart = (blk_end - blocks) * FFN_BLK
    e = route[:, 0:TOP_K].astype(jnp.int32)
    rank = route[:, TOP_K:2 * TOP_K].astype(jnp.int32)
    dest = (pstart[e] + rank).reshape(-1)
    n_used = blk_end[-1:]
    blk = jnp.minimum(jnp.arange(n_blocks, dtype=jnp.int32), n_used[0] - 1)
    block_expert = jnp.minimum(jnp.sum((blk_end[None, :] <= blk[:, None]).astype(jnp.int32), axis=1),
                               N_EXPERTS - 1)
    return dest, block_expert, n_used.astype(jnp.int32)


def kernel(x_prompt, x_sample, state_C, state_n, state_m, state_conv, p_prompt, p_sample, norm_mix_g, w_in, conv_w, conv_b, b_igate, b_fgate, hnorm_g, vnorm_g, w_s, b_s, w_out, norm_ffn_g, router_g_w, router_g_b, router_e_w, router_e_b, w1, w3, w2, w_ple, ple_norm_g, w_ple_gate, final_norm_g):
    n_batch, seq, _ = x_prompt.shape
    n_s = x_sample.shape[0]
    depth = w_in.shape[0]
    t_p = n_batch * seq
    t_total = t_p + n_s
    assert seq % CHUNK == 0 and t_p % n_s == 0 and n_s % BT_SAMPLE == 0

    n_assign = t_total * TOP_K
    n_blocks = (n_assign + N_EXPERTS * (FFN_BLK - 1)) // FFN_BLK
    xs = jnp.zeros((n_blocks * FFN_BLK, D_MODEL), F32)

    p_all = jnp.concatenate([p_prompt.reshape(depth, t_p, PLE_DIM), p_sample.reshape(depth, n_s, PLE_DIM)], axis=1)
    sconv_t = jnp.transpose(state_conv, (0, 2, 1, 3))
    n0_t = jnp.transpose(state_n, (0, 2, 1, 3))
    m0_pad = jnp.pad(state_m, ((0, 0), (0, 0), (0, LANES - H_A)))
    final_g = final_norm_g[None, :]

    hbuf = jnp.concatenate([x_prompt.reshape(t_p, D_MODEL), x_sample.reshape(n_s, D_MODEL)], axis=0)
    outs = {k: [] for k in ("Cp", "np", "mp", "cbp", "Cs", "ns", "ms", "cbs", "vs")}
    res = None
    for l in range(depth):
        lw = _layer_weights(l, norm_mix_g, w_in, conv_w, conv_b, b_igate, b_fgate, hnorm_g, vnorm_g, w_s, b_s,
                            w_out, norm_ffn_g, router_g_w, router_g_b, router_e_w, router_e_b, w_ple,
                            ple_norm_g, w_ple_gate)
        hbuf, c_p, n_p, m_p, cb_p = _mix_prompt(hbuf, n_batch, seq, lw)
        hbuf, c_s, n_s_t, m_s, cb_s, v_s = _mix_sample(hbuf, lw, sconv_t, state_C, n0_t, m0_pad, n_s, l)
        route, counts = _router(hbuf, lw)
        dest, block_expert, n_used = _routing_tables(route, counts[0], n_blocks)
        xs = _dispatch(dest, hbuf, lw, xs)
        xs = _ffn(block_expert, n_used, xs, w1, w3, w2, l)
        final = l == depth - 1
        res = _ple(dest, hbuf, route, p_all, lw, final_g, xs, n_s, l, final)
        hbuf = res[0]
        outs["Cp"].append(c_p)
        outs["np"].append(n_p)
        outs["mp"].append(m_p[:, :, 0])
        outs["cbp"].append(cb_p)
        outs["Cs"].append(c_s)
        outs["ns"].append(jnp.transpose(n_s_t, (1, 0, 2)))
        outs["ms"].append(m_s[:, 0:H_A])
        outs["cbs"].append(jnp.transpose(cb_s, (1, 0, 2)))
        outs["vs"].append(v_s[:, None, :])

    y_prompt = res[0].reshape(n_batch, seq, D_MODEL)
    y_sample = res[1].reshape(n_s, 1, D_MODEL)
    st = lambda k: jnp.stack(outs[k])
    return (y_prompt, y_sample, st("Cp"), st("np"), st("mp"), st("cbp"),
            st("Cs"), st("ns"), st("ms"), st("cbs"), st("vs"))
```

```python
import functools

import jax
import jax.numpy as jnp
from jax import lax
from jax.experimental import pallas as pl
from jax.experimental.pallas import tpu as pltpu

F32 = jnp.float32
BF16 = jnp.bfloat16

D_MODEL = 1024
W_A = 512
H_A = 4
DH_A = 128
W_B = 512
G_B = 4
DG_B = 128
CHUNK = 128
CONV_W = 4
QK_W = 2 * W_A
N_GROUPS = 4
EXPERTS_PER_GROUP = 8
N_EXPERTS = N_GROUPS * EXPERTS_PER_GROUP
TOP_K = 2
D_FF = 512
PLE_DIM = 256
EPS = 1e-6

LANES = 128
SUBLANES = 8
VMEM_LIMIT_BYTES = 56 * 1024 * 1024

P_QK = 0
P_V = QK_W
P_O = P_V + W_A
P_U = P_O + W_A
P_VB = P_U + W_B
P_G = P_VB + W_B
P_W = P_G + LANES

TL_MIX = 512
BT_SAMPLE = 8
TM_TOK = 384
FFN_BLK = 256


def _cparams(*sem):
    return pltpu.CompilerParams(dimension_semantics=sem, vmem_limit_bytes=VMEM_LIMIT_BYTES)


def _rms(x, g):
    return x * lax.rsqrt(jnp.mean(x * x, axis=-1, keepdims=True) + EPS) * g


def _log_sigmoid(x):
    return -(jnp.maximum(-x, 0.0) + jnp.log1p(jnp.exp(-jnp.abs(x))))


def _split_dot(a, b_bf16):
    hi = a.astype(BF16)
    lo = (a - hi.astype(F32)).astype(BF16)
    return (jnp.dot(hi, b_bf16, preferred_element_type=F32)
            + jnp.dot(lo, b_bf16, preferred_element_type=F32))


def _dot_nt(a, b):
    return lax.dot_general(a, b, (((1,), (1,)), ((), ())), preferred_element_type=F32)


def _mix_prompt_kernel(x_ref, gmix_ref, win_ref, cw_ref, cb_ref, gb_ref, hng_ref, vng_ref,
                       ws_ref, bs_ref, wout_ref,
                       out_ref, c_out, n_out, m_out, conv_out,
                       proj, xpad, hcat, c_s, n_s, m_s, *, tl):
    j = pl.program_id(1)
    nj = pl.num_programs(1)
    n_chunks = tl // CHUNK

    @pl.when(j == 0)
    def _():
        c_s[...] = jnp.zeros_like(c_s)
        n_s[...] = jnp.zeros_like(n_s)
        m_s[...] = jnp.zeros_like(m_s)
        xpad[0:SUBLANES, :] = jnp.zeros((SUBLANES, QK_W), F32)

    x = x_ref[...]
    xn = _rms(x, gmix_ref[...]).astype(BF16)
    proj[...] = jnp.dot(xn, win_ref[...], preferred_element_type=F32)

    pre = proj[:, P_QK:P_QK + QK_W]
    xpad[SUBLANES:SUBLANES + tl, :] = pre
    qk = cb_ref[...] + cw_ref[CONV_W - 1:CONV_W, :] * pre
    for jj in range(1, CONV_W):
        qk = qk + cw_ref[CONV_W - 1 - jj:CONV_W - jj, :] * xpad[SUBLANES - jj:SUBLANES - jj + tl, :]
    new_rows = pre[tl - (CONV_W - 1):tl, :]
    xpad[SUBLANES - (CONV_W - 1):SUBLANES, :] = new_rows

    @pl.when(j == nj - 1)
    def _():
        conv_out[0] = new_rows

    qk = qk * jax.nn.sigmoid(qk)
    proj[:, 0:W_A] = qk[:, 0:W_A]
    proj[:, W_A:QK_W] = qk[:, W_A:QK_W] * (DH_A ** -0.5)

    row = lax.broadcasted_iota(jnp.int32, (CHUNK, CHUNK), 0)
    col = lax.broadcasted_iota(jnp.int32, (CHUNK, CHUNK), 1)
    causal = row >= col
    tril = jnp.where(causal, 1.0, 0.0).astype(BF16)
    triu = jnp.where(row <= col, 1.0, 0.0).astype(BF16)
    lane = lax.broadcasted_iota(jnp.int32, (CHUNK, LANES), 1)
    wm = [jnp.where(causal, ws_ref[g], 0.0).astype(BF16) for g in range(G_B)]

    def chunk_body(c, carry):
        r0 = pl.multiple_of(c * CHUNK, CHUNK)
        rows = pl.ds(r0, CHUNK)
        gl = proj[rows, P_G:P_G + LANES] + gb_ref[...]
        gl = jnp.where(lane >= H_A, _log_sigmoid(gl), gl)
        glt = gl.T
        b_col_all = _split_dot_left(tril, gl)
        b_row_all = _split_dot(glt[0:2 * SUBLANES, :], triu)

        for h in range(H_A):
            q32 = proj[rows, h * DH_A:(h + 1) * DH_A]
            k32 = proj[rows, W_A + h * DH_A:W_A + (h + 1) * DH_A]
            v32 = proj[rows, P_V + h * DH_A:P_V + (h + 1) * DH_A]
            q = q32.astype(BF16)
            k = k32.astype(BF16)
            ig_c = gl[:, h:h + 1]
            ig_r = glt[h:h + 1, :]
            b_c = b_col_all[:, H_A + h:H_A + h + 1]
            b_r = b_row_all[H_A + h:H_A + h + 1, :]
            c_old = c_s[h]
            n_old = n_s[h:h + 1, :]
            m_prev = m_s[h:h + 1, 0:1]

            d_log = jnp.where(causal, b_c - b_r + ig_r, -jnp.inf)
            inter = b_c + m_prev
            m_t = jnp.maximum(inter, jnp.max(d_log, axis=-1, keepdims=True))
            s = _dot_nt(q, k) * jnp.exp(d_log - m_t)
            w_inter = jnp.exp(inter - m_t)
            num = (w_inter * _dot_nt(q, c_old.astype(BF16))
                   + jnp.dot(s.astype(BF16), v32.astype(BF16), preferred_element_type=F32))
            nq = (w_inter * jnp.sum(q32 * n_old, axis=-1, keepdims=True)
                  + jnp.sum(s, axis=-1, keepdims=True))
            hh = num / jnp.maximum(jnp.abs(nq), jnp.exp(-m_t))

            m_new = m_t[CHUNK - 1:CHUNK, :]
            b_last = b_c[CHUNK - 1:CHUNK, :]
            w_state = jnp.exp(b_last - b_c + ig_c - m_new)
            decay = jnp.exp(b_last + m_prev - m_new)
            vw = (v32 * w_state).astype(BF16)
            c_s[h] = decay * c_old + lax.dot_general(
                vw, k, (((0,), (0,)), ((), ())), preferred_element_type=F32)
            n_s[h:h + 1, :] = decay * n_old + jnp.sum(w_state * k32, axis=0, keepdims=True)
            m_s[h:h + 1, :] = jnp.broadcast_to(m_new, (1, LANES))

            ha = _rms(hh, hng_ref[:, h * DH_A:(h + 1) * DH_A])
            o = proj[rows, P_O + h * DH_A:P_O + (h + 1) * DH_A]
            hcat[rows, h * DH_A:(h + 1) * DH_A] = (ha * jax.nn.sigmoid(o)).astype(BF16)

        for g in range(G_B):
            vb = proj[rows, P_VB + g * DG_B:P_VB + (g + 1) * DG_B]
            u = proj[rows, P_U + g * DG_B:P_U + (g + 1) * DG_B]
            vn = _rms(jax.nn.gelu(vb), vng_ref[:, g * DG_B:(g + 1) * DG_B])
            z = jnp.dot(wm[g], vn.astype(BF16), preferred_element_type=F32) + bs_ref[:, g:g + 1]
            hcat[rows, W_A + g * DG_B:W_A + (g + 1) * DG_B] = (jax.nn.gelu(u) * z).astype(BF16)
        return carry

    lax.fori_loop(0, n_chunks, chunk_body, 0)

    out_ref[...] = x + jnp.dot(hcat[...], wout_ref[...], preferred_element_type=F32)

    @pl.when(j == nj - 1)
    def _():
        c_out[0] = c_s[...]
        n_out[0] = n_s[0:H_A, :]
        m_out[0] = m_s[0:H_A, :]


def _split_dot_left(a_bf16, b):
    hi = b.astype(BF16)
    lo = (b - hi.astype(F32)).astype(BF16)
    return (jnp.dot(a_bf16, hi, preferred_element_type=F32)
            + jnp.dot(a_bf16, lo, preferred_element_type=F32))


def _mix_prompt(hbuf, n_batch, seq, lw):
    tl = max(t for t in range(CHUNK, TL_MIX + 1, CHUNK) if seq % t == 0)
    nj = seq // tl
    const = lambda b, j: (0, 0)
    kern = functools.partial(_mix_prompt_kernel, tl=tl)
    return pl.pallas_call(
        kern,
        name="mix_prompt",
        grid=(n_batch, nj),
        in_specs=[
            pl.BlockSpec((tl, D_MODEL), lambda b, j: (b * nj + j, 0)),
            pl.BlockSpec((1, D_MODEL), const),
            pl.BlockSpec((D_MODEL, P_W), const),
            pl.BlockSpec((CONV_W, QK_W), const),
            pl.BlockSpec((1, QK_W), const),
            pl.BlockSpec((1, LANES), const),
            pl.BlockSpec((1, W_A), const),
            pl.BlockSpec((1, W_B), const),
            pl.BlockSpec((G_B, CHUNK, CHUNK), lambda b, j: (0, 0, 0)),
            pl.BlockSpec((CHUNK, LANES), const),
            pl.BlockSpec((D_MODEL, D_MODEL), const),
        ],
        out_specs=[
            pl.BlockSpec((tl, D_MODEL), lambda b, j: (b * nj + j, 0)),
            pl.BlockSpec((1, H_A, DH_A, DH_A), lambda b, j: (b, 0, 0, 0)),
            pl.BlockSpec((1, H_A, DH_A), lambda b, j: (b, 0, 0)),
            pl.BlockSpec((1, H_A, LANES), lambda b, j: (b, 0, 0)),
            pl.BlockSpec((1, CONV_W - 1, QK_W), lambda b, j: (b, 0, 0)),
        ],
        out_shape=[
            jax.ShapeDtypeStruct(hbuf.shape, F32),
            jax.ShapeDtypeStruct((n_batch, H_A, DH_A, DH_A), F32),
            jax.ShapeDtypeStruct((n_batch, H_A, DH_A), F32),
            jax.ShapeDtypeStruct((n_batch, H_A, LANES), F32),
            jax.ShapeDtypeStruct((n_batch, CONV_W - 1, QK_W), F32),
        ],
        scratch_shapes=[
            pltpu.VMEM((tl, P_W), F32),
            pltpu.VMEM((SUBLANES + tl, QK_W), F32),
            pltpu.VMEM((tl, D_MODEL), BF16),
            pltpu.VMEM((H_A, DH_A, DH_A), F32),
            pltpu.VMEM((SUBLANES, LANES), F32),
            pltpu.VMEM((SUBLANES, LANES), F32),
        ],
        input_output_aliases={0: 0},
        compiler_params=_cparams("arbitrary", "arbitrary"),
    )(hbuf, lw["g_mix"], lw["w_in"], lw["conv_w"], lw["conv_b"], lw["gate_b"], lw["hn_g"], lw["vn_g"],
      lw["w_s"], lw["b_s_col"], lw["w_out"])


def _mix_sample_kernel(x_ref, gmix_ref, win_ref, cw_ref, cb_ref, gb_ref, hng_ref, vng_ref,
                       w00_ref, bs0_ref, wout_ref, sconv_ref, c_in, n_in, m_in,
                       out_ref, c_out, n_out, m_out, conv_out, vrow_out,
                       proj, hcat, *, bt):
    i = pl.program_id(0)
    ni = pl.num_programs(0)

    @pl.when(i == 0)
    def _():
        x = x_ref[...]
        xn = _rms(x, gmix_ref[...]).astype(BF16)
        proj[...] = jnp.dot(xn, win_ref[...], preferred_element_type=F32)
        pre = proj[:, P_QK:P_QK + QK_W]
        qk = cb_ref[...] + cw_ref[CONV_W - 1:CONV_W, :] * pre
        for jj in range(CONV_W - 1):
            qk = qk + cw_ref[jj:jj + 1, :] * sconv_ref[jj]
        for jj in range(CONV_W - 2):
            conv_out[jj] = sconv_ref[jj + 1]
        conv_out[CONV_W - 2] = pre
        qk = qk * jax.nn.sigmoid(qk)
        proj[:, 0:W_A] = qk[:, 0:W_A]
        proj[:, W_A:QK_W] = qk[:, W_A:QK_W] * (DH_A ** -0.5)
        for g in range(G_B):
            sl = slice(g * DG_B, (g + 1) * DG_B)
            vb = proj[:, P_VB + g * DG_B:P_VB + (g + 1) * DG_B]
            u = proj[:, P_U + g * DG_B:P_U + (g + 1) * DG_B]
            vn = _rms(jax.nn.gelu(vb), vng_ref[:, sl])
            vrow_out[:, sl] = vn
            z = w00_ref[:, sl] * vn + bs0_ref[:, sl]
            hcat[:, W_A + g * DG_B:W_A + (g + 1) * DG_B] = jax.nn.gelu(u) * z

    rows = pl.ds(pl.multiple_of(i * bt, bt), bt)
    gates = proj[rows, P_G:P_G + LANES] + gb_ref[...]
    lane = lax.broadcasted_iota(jnp.int32, (bt, LANES), 1)
    sub = lax.broadcasted_iota(jnp.int32, (bt, DH_A), 0)
    m_new_all = jnp.zeros((bt, LANES), F32)
    for h in range(H_A):
        ig = gates[:, h:h + 1]
        lf = _log_sigmoid(gates[:, H_A + h:H_A + h + 1])
        m_prev = m_in[rows, h:h + 1]
        inter = lf + m_prev
        m_t = jnp.maximum(inter, ig)
        w_inter = jnp.exp(inter - m_t)
        e_d = jnp.exp(ig - m_t)
        q = proj[rows, h * DH_A:(h + 1) * DH_A]
        k = proj[rows, W_A + h * DH_A:W_A + (h + 1) * DH_A]
        v = proj[rows, P_V + h * DH_A:P_V + (h + 1) * DH_A]
        n_old = n_in[h]
        s = jnp.sum(q * k, axis=-1, keepdims=True) * e_d
        vw = v * e_d
        vw_t = jnp.concatenate([vw, jnp.zeros((DH_A - bt, DH_A), F32)], axis=0).T
        qb = q.astype(BF16)
        qc = jnp.zeros((bt, DH_A), F32)
        for t in range(bt):
            c_old = c_in[t, h]
            r = _dot_nt(qb, c_old.astype(BF16))
            qc = jnp.where(sub == t, r, qc)
            c_out[t, h] = w_inter[t:t + 1, :] * c_old + vw_t[:, t:t + 1] * k[t:t + 1, :]
        num = w_inter * qc + s * v
        nq = w_inter * jnp.sum(q * n_old, axis=-1, keepdims=True) + s
        hh = num / jnp.maximum(jnp.abs(nq), jnp.exp(-m_t))
        n_out[h] = w_inter * n_old + e_d * k
        m_new_all = jnp.where(lane == h, m_t, m_new_all)
        ha = _rms(hh, hng_ref[:, h * DH_A:(h + 1) * DH_A])
        o = proj[rows, P_O + h * DH_A:P_O + (h + 1) * DH_A]
        hcat[rows, h * DH_A:(h + 1) * DH_A] = ha * jax.nn.sigmoid(o)
    m_out[rows, :] = m_new_all

    @pl.when(i == ni - 1)
    def _():
        out_ref[...] = x_ref[...] + jnp.dot(hcat[...].astype(BF16), wout_ref[...],
                                            preferred_element_type=F32)


def _mix_sample(hbuf, lw, sconv_t, c0, n0_t, m0_pad, n_s, layer):
    bt = BT_SAMPLE
    ni = n_s // bt
    const = lambda i: (0, 0)
    out_blk = hbuf.shape[0] // n_s - 1
    kern = functools.partial(_mix_sample_kernel, bt=bt)
    return pl.pallas_call(
        kern,
        name="mix_sample",
        grid=(ni,),
        in_specs=[
            pl.BlockSpec((n_s, D_MODEL), lambda i: (out_blk, 0)),
            pl.BlockSpec((1, D_MODEL), const),
            pl.BlockSpec((D_MODEL, P_W), const),
            pl.BlockSpec((CONV_W, QK_W), const),
            pl.BlockSpec((1, QK_W), const),
            pl.BlockSpec((1, LANES), const),
            pl.BlockSpec((1, W_A), const),
            pl.BlockSpec((1, W_B), const),
            pl.BlockSpec((1, W_B), const),
            pl.BlockSpec((1, W_B), const),
            pl.BlockSpec((D_MODEL, D_MODEL), const),
            pl.BlockSpec((None, CONV_W - 1, n_s, QK_W), lambda i: (layer, 0, 0, 0)),
            pl.BlockSpec((None, bt, H_A, DH_A, DH_A), lambda i: (layer, i, 0, 0, 0)),
            pl.BlockSpec((None, H_A, bt, DH_A), lambda i: (layer, 0, i, 0)),
            pl.BlockSpec((None, n_s, LANES), lambda i: (layer, 0, 0)),
        ],
        out_specs=[
            pl.BlockSpec((n_s, D_MODEL), lambda i: (out_blk, 0)),
            pl.BlockSpec((bt, H_A, DH_A, DH_A), lambda i: (i, 0, 0, 0)),
            pl.BlockSpec((H_A, bt, DH_A), lambda i: (0, i, 0)),
            pl.BlockSpec((n_s, LANES), const),
            pl.BlockSpec((CONV_W - 1, n_s, QK_W), lambda i: (0, 0, 0)),
            pl.BlockSpec((n_s, W_B), const),
        ],
        out_shape=[
            jax.ShapeDtypeStruct(hbuf.shape, F32),
            jax.ShapeDtypeStruct(c0.shape[1:], F32),
            jax.ShapeDtypeStruct(n0_t.shape[1:], F32),
            jax.ShapeDtypeStruct((n_s, LANES), F32),
            jax.ShapeDtypeStruct((CONV_W - 1, n_s, QK_W), F32),
            jax.ShapeDtypeStruct((n_s, W_B), F32),
        ],
        scratch_shapes=[
            pltpu.VMEM((n_s, P_W), F32),
            pltpu.VMEM((n_s, D_MODEL), F32),
        ],
        input_output_aliases={0: 0},
        compiler_params=_cparams("arbitrary"),
    )(hbuf, lw["g_mix"], lw["w_in"], lw["conv_w"], lw["conv_b"], lw["gate_b"], lw["hn_g"], lw["vn_g"],
      lw["w00_row"], lw["bs0_row"], lw["w_out"], sconv_t, c0, n0_t, m0_pad)


def _router_kernel(h_ref, g_ref, whi_ref, wlo_ref, b_ref, route_ref, route_t_ref, tcnt_ref, *, tm):
    hn = _rms(h_ref[...], g_ref[...])
    hi = hn.astype(BF16)
    lo = (hn - hi.astype(F32)).astype(BF16)
    logits = (jnp.dot(hi, whi_ref[...], preferred_element_type=F32)
              + jnp.dot(lo, whi_ref[...], preferred_element_type=F32)
              + jnp.dot(hi, wlo_ref[...], preferred_element_type=F32)) + b_ref[...]

    lane_i = lax.broadcasted_iota(jnp.int32, (tm, LANES), 1)
    lane = lane_i.astype(F32)
    neg = -jnp.inf
    is_g = lane_i < N_GROUPS
    gl = jnp.where(is_g, logits, neg)
    gmax = jnp.max(gl, axis=-1, keepdims=True)
    g_sel = jnp.min(jnp.where(gl == gmax, lane, float(LANES)), axis=-1, keepdims=True)
    g_w = 1.0 / jnp.sum(jnp.where(is_g, jnp.exp(logits - gmax), 0.0), axis=-1, keepdims=True)

    e_lane = lane_i - N_GROUPS
    lane_grp = (e_lane >> 3).astype(F32)
    in_grp = (e_lane >= 0) & (e_lane < N_EXPERTS) & (lane_grp == g_sel)
    el = jnp.where(in_grp, logits, neg)
    v1 = jnp.max(el, axis=-1, keepdims=True)
    i1 = jnp.min(jnp.where(el == v1, lane, float(LANES)), axis=-1, keepdims=True)
    el2 = jnp.where(lane == i1, neg, el)
    v2 = jnp.max(el2, axis=-1, keepdims=True)
    i2 = jnp.min(jnp.where(el2 == v2, lane, float(LANES)), axis=-1, keepdims=True)
    e21 = jnp.exp(v2 - v1)
    den = 1.0 + e21
    w1 = g_w * (1.0 / den)
    w2 = g_w * (e21 / den)

    onehot = jnp.where((lane == i1) | (lane == i2), 1.0, 0.0)
    r_i = lax.broadcasted_iota(jnp.int32, (tm, tm), 0)
    c_i = lax.broadcasted_iota(jnp.int32, (tm, tm), 1)
    strict = jnp.where(r_i > c_i, 1.0, 0.0).astype(BF16)
    before = jnp.dot(strict, onehot.astype(BF16), preferred_element_type=F32)
    cnt = jnp.sum(onehot, axis=0, keepdims=True)
    cnt8 = jnp.floor((cnt + (SUBLANES - 1.0)) * (1.0 / SUBLANES)) * SUBLANES
    l_r = lax.broadcasted_iota(jnp.int32, (LANES, LANES), 0)
    l_c = lax.broadcasted_iota(jnp.int32, (LANES, LANES), 1)
    lanes_before = jnp.where(l_r < l_c, 1.0, 0.0).astype(BF16)
    start = _split_dot(jnp.broadcast_to(cnt8, (2 * SUBLANES, LANES)), lanes_before)[0:1, :]
    slot = start + before
    pos1 = jnp.sum(jnp.where(lane == i1, slot, 0.0), axis=-1, keepdims=True)
    pos2 = jnp.sum(jnp.where(lane == i2, slot, 0.0), axis=-1, keepdims=True)
    tcnt_ref[...] = jnp.broadcast_to(cnt8, tcnt_ref.shape)

    out = jnp.zeros((tm, LANES), F32)
    for idx, val in enumerate((pos1, pos2, w1, w2)):
        out = jnp.where(lane_i == idx, val, out)
    route_ref[...] = out
    route_t_ref[...] = out.T[0:SUBLANES, :]


def _router(h1, lw):
    t_total = h1.shape[0]
    tm = _token_tile(t_total)
    n_tiles = t_total // tm
    const = lambda i: (0, 0)
    return pl.pallas_call(
        functools.partial(_router_kernel, tm=tm),
        name="router",
        grid=(n_tiles,),
        in_specs=[
            pl.BlockSpec((tm, D_MODEL), lambda i: (i, 0)),
            pl.BlockSpec((1, D_MODEL), const),
            pl.BlockSpec((D_MODEL, LANES), const),
            pl.BlockSpec((D_MODEL, LANES), const),
            pl.BlockSpec((1, LANES), const),
        ],
        out_specs=[
            pl.BlockSpec((tm, LANES), lambda i: (i, 0)),
            pl.BlockSpec((SUBLANES, tm), lambda i: (0, i)),
            pl.BlockSpec((None, SUBLANES, LANES), lambda i: (i, 0, 0)),
        ],
        out_shape=[
            jax.ShapeDtypeStruct((t_total, LANES), F32),
            jax.ShapeDtypeStruct((SUBLANES, t_total), F32),
            jax.ShapeDtypeStruct((n_tiles, SUBLANES, LANES), F32),
        ],
        compiler_params=_cparams("arbitrary"),
    )(h1, lw["g_ffn"], lw["wr_hi"], lw["wr_lo"], lw["br"])


def _tile_slots(tm):
    raw = tm * TOP_K + N_EXPERTS * (SUBLANES - 1)
    return -(-raw // LANES) * LANES


def _token_tile(t_total):
    tm = TM_TOK
    while t_total % tm:
        tm //= 2
    return tm


def _run_copies(tile, rdst_ref, rsrc_ref, runits_ref, make_copy):
    def per_expert(e, carry):
        k = tile * N_EXPERTS + e
        s0 = rsrc_ref[k]
        d0 = rdst_ref[k]

        def per_unit(u, c2):
            off = u * SUBLANES
            make_copy(pl.multiple_of(s0 + off, SUBLANES), pl.multiple_of(d0 + off, SUBLANES)).start()
            return c2

        lax.fori_loop(0, runits_ref[k], per_unit, 0)
        return carry

    lax.fori_loop(0, N_EXPERTS, per_expert, 0)


def _drain_copies(n_units, make_copy):
    def wait_one(u, carry):
        make_copy(0, 0).wait()
        return carry

    lax.fori_loop(0, n_units, wait_one, 0)


def _dispatch_kernel(rdst_ref, rsrc_ref, runits_ref, tunits_ref, h_ref, g_ref, rt_ref, xs_in, xs_out,
                     srt, sem, *, tm, slots):
    del xs_in
    i = pl.program_id(0)
    xn = _rms(h_ref[...], g_ref[...]).astype(BF16)
    slot_id = lax.broadcasted_iota(jnp.int32, (slots, tm), 0).astype(F32)
    sel = (slot_id == rt_ref[0:1, :]) | (slot_id == rt_ref[1:2, :])
    srt[...] = jnp.dot(jnp.where(sel, 1.0, 0.0).astype(BF16), xn, preferred_element_type=F32)

    def make_copy(tile_row, sorted_row):
        return pltpu.make_async_copy(srt.at[pl.ds(tile_row, SUBLANES)],
                                     xs_out.at[pl.ds(sorted_row, SUBLANES)], sem)

    _run_copies(i, rdst_ref, rsrc_ref, runits_ref, make_copy)
    _drain_copies(tunits_ref[i], make_copy)


def _dispatch(tables, h1, route_t, lw, xs):
    t_total = h1.shape[0]
    tm = _token_tile(t_total)
    slots = _tile_slots(tm)
    return pl.pallas_call(
        functools.partial(_dispatch_kernel, tm=tm, slots=slots),
        name="dispatch",
        grid_spec=pltpu.PrefetchScalarGridSpec(
            num_scalar_prefetch=4,
            grid=(t_total // tm,),
            in_specs=[
                pl.BlockSpec((tm, D_MODEL), lambda i, *_: (i, 0)),
                pl.BlockSpec((1, D_MODEL), lambda i, *_: (0, 0)),
                pl.BlockSpec((SUBLANES, tm), lambda i, *_: (0, i)),
                pl.BlockSpec(memory_space=pl.ANY),
            ],
            out_specs=pl.BlockSpec(memory_space=pl.ANY),
            scratch_shapes=[pltpu.VMEM((slots, D_MODEL), F32), pltpu.SemaphoreType.DMA(())],
        ),
        out_shape=jax.ShapeDtypeStruct(xs.shape, F32),
        input_output_aliases={7: 0},
        compiler_params=_cparams("arbitrary"),
    )(*tables, h1, lw["g_ffn"], route_t, xs)


def _ffn_kernel(be_ref, nu_ref, x_ref, w1_ref, w3_ref, w2_ref, y_ref, w1b, w3b, w2b):
    i = pl.program_id(0)
    prev = be_ref[jnp.maximum(i - 1, 0)]
    fresh = (i == 0) | (be_ref[i] != prev)

    @pl.when(fresh)
    def _():
        w1b[...] = w1_ref[...].astype(BF16)
        w3b[...] = w3_ref[...].astype(BF16)
        w2b[...] = w2_ref[...].astype(BF16)

    @pl.when(i < nu_ref[0])
    def _():
        x = x_ref[...].astype(BF16)
        a = jnp.dot(x, w1b[...], preferred_element_type=F32)
        b = jnp.dot(x, w3b[...], preferred_element_type=F32)
        hmid = (a * jax.nn.sigmoid(a) * b).astype(BF16)
        y_ref[...] = jnp.dot(hmid, w2b[...], preferred_element_type=F32)


def _ffn(block_expert, n_used, xs, w1, w3, w2, layer):
    n_rows = xs.shape[0]
    nb = n_rows // FFN_BLK

    def row_map(i, be, nu):
        return (jnp.minimum(i, nu[0] - 1), 0)

    def w_map(i, be, nu):
        return (layer, be[i], 0, 0)

    return pl.pallas_call(
        _ffn_kernel,
        name="expert_ffn",
        grid_spec=pltpu.PrefetchScalarGridSpec(
            num_scalar_prefetch=2,
            grid=(nb,),
            in_specs=[
                pl.BlockSpec((FFN_BLK, D_MODEL), row_map),
                pl.BlockSpec((None, None, D_MODEL, D_FF), w_map),
                pl.BlockSpec((None, None, D_MODEL, D_FF), w_map),
                pl.BlockSpec((None, None, D_FF, D_MODEL), w_map),
            ],
            out_specs=pl.BlockSpec((FFN_BLK, D_MODEL), row_map),
            scratch_shapes=[
                pltpu.VMEM((D_MODEL, D_FF), BF16),
                pltpu.VMEM((D_MODEL, D_FF), BF16),
                pltpu.VMEM((D_FF, D_MODEL), BF16),
            ],
        ),
        out_shape=jax.ShapeDtypeStruct((n_rows, D_MODEL), F32),
        input_output_aliases={2: 0},
        compiler_params=_cparams("arbitrary"),
    )(block_expert, n_used, xs, w1, w3, w2)


def _ple_kernel(rdst_ref, rsrc_ref, runits_ref, tunits_ref, h_ref, route_ref, p_ref, wpg_ref, wple_ref,
                pg_ref, fg_ref, ys_ref, *rest, tm, slots, n_s, final):
    if final:
        yp_ref, ysm_ref, ysrt, sem = rest
    else:
        out_ref, ysrt, sem = rest
    i = pl.program_id(0)

    @pl.when(i == 0)
    def _():
        ysrt[...] = jnp.zeros_like(ysrt)

    def make_copy(tile_row, sorted_row):
        return pltpu.make_async_copy(ys_ref.at[pl.ds(sorted_row, SUBLANES)],
                                     ysrt.at[pl.ds(tile_row, SUBLANES)], sem)

    _run_copies(i, rdst_ref, rsrc_ref, runits_ref, make_copy)
    pe = _rms(jnp.dot(p_ref[...].astype(BF16), wple_ref[...], preferred_element_type=F32), pg_ref[...])
    _drain_copies(tunits_ref[i], make_copy)

    route = route_ref[...]
    yb = ysrt[...].astype(BF16)
    slot_id = lax.broadcasted_iota(jnp.int32, (tm, slots), 1).astype(F32)
    moe = None
    for kk in range(TOP_K):
        pick = jnp.where(slot_id == route[:, kk:kk + 1], 1.0, 0.0).astype(BF16)
        term = route[:, TOP_K + kk:TOP_K + kk + 1] * jnp.dot(pick, yb, preferred_element_type=F32)
        moe = term if moe is None else moe + term
    h2 = h_ref[...] + moe
    gate = jax.nn.sigmoid(jnp.dot(h2.astype(BF16), wpg_ref[...], preferred_element_type=F32))
    out = h2 + gate * pe
    if final:
        fin = _rms(out, fg_ref[...])
        yp_ref[...] = fin

        @pl.when(i == pl.num_programs(0) - 1)
        def _():
            ysm_ref[...] = fin[tm - n_s:tm, :]
    else:
        out_ref[...] = out


def _ple(tables, h1, route, p_all, lw, final_g, ys, n_s, layer, final):
    t_total = h1.shape[0]
    tm = _token_tile(t_total)
    slots = _tile_slots(tm)
    assert n_s <= tm
    const = lambda i, *_: (0, 0)
    tok = lambda i, *_: (i, 0)
    if final:
        out_specs = [pl.BlockSpec((tm, D_MODEL), tok), pl.BlockSpec((n_s, D_MODEL), const)]
        out_shape = [jax.ShapeDtypeStruct((t_total - n_s, D_MODEL), F32),
                     jax.ShapeDtypeStruct((n_s, D_MODEL), F32)]
        aliases = {}
    else:
        out_specs = [pl.BlockSpec((tm, D_MODEL), tok)]
        out_shape = [jax.ShapeDtypeStruct((t_total, D_MODEL), F32)]
        aliases = {4: 0}
    return pl.pallas_call(
        functools.partial(_ple_kernel, tm=tm, slots=slots, n_s=n_s, final=final),
        name="combine_ple",
        grid_spec=pltpu.PrefetchScalarGridSpec(
            num_scalar_prefetch=4,
            grid=(t_total // tm,),
            in_specs=[
                pl.BlockSpec((tm, D_MODEL), tok),
                pl.BlockSpec((tm, LANES), tok),
                pl.BlockSpec((None, tm, PLE_DIM), lambda i, *_: (layer, i, 0)),
                pl.BlockSpec((D_MODEL, D_MODEL), const),
                pl.BlockSpec((PLE_DIM, D_MODEL), const),
                pl.BlockSpec((1, D_MODEL), const),
                pl.BlockSpec((1, D_MODEL), const),
                pl.BlockSpec(memory_space=pl.ANY),
            ],
            out_specs=out_specs,
            scratch_shapes=[pltpu.VMEM((slots, D_MODEL), F32), pltpu.SemaphoreType.DMA(())],
        ),
        out_shape=out_shape,
        input_output_aliases=aliases,
        compiler_params=_cparams("arbitrary"),
    )(*tables, h1, route, p_all, lw["w_pg"], lw["w_ple"], lw["ple_g"], final_g, ys)


def _layer_weights(l, norm_mix_g, w_in, conv_w, conv_b, b_igate, b_fgate, hnorm_g, vnorm_g, w_s, b_s, w_out,
                   norm_ffn_g, router_g_w, router_g_b, router_e_w, router_e_b, w_ple, ple_norm_g, w_ple_gate):
    wi = w_in[l]
    a0 = QK_W + 2 * W_A
    gates_w = jnp.pad(wi[:, a0:a0 + 2 * H_A], ((0, 0), (0, LANES - 2 * H_A)))
    w_packed = jnp.concatenate(
        [wi[:, :a0], wi[:, a0 + 2 * H_A:], gates_w], axis=1).astype(BF16)
    gate_b = jnp.pad(jnp.concatenate([b_igate[l], b_fgate[l]]), (0, LANES - 2 * H_A))[None, :]
    wr = jnp.pad(jnp.concatenate([router_g_w[l], router_e_w[l]], axis=1),
                 ((0, 0), (0, LANES - N_GROUPS - N_EXPERTS)))
    wr_hi = wr.astype(BF16)
    wr_lo = (wr - wr_hi.astype(F32)).astype(BF16)
    br = jnp.pad(jnp.concatenate([router_g_b[l], router_e_b[l]]), (0, LANES - N_GROUPS - N_EXPERTS))[None, :]
    return {
        "g_mix": norm_mix_g[l][None, :],
        "w_in": w_packed,
        "conv_w": conv_w[l],
        "conv_b": conv_b[l][None, :],
        "gate_b": gate_b,
        "hn_g": hnorm_g[l][None, :],
        "vn_g": vnorm_g[l][None, :],
        "w_s": w_s[l],
        "b_s_col": jnp.pad(b_s[l].T, ((0, 0), (0, LANES - G_B))),
        "w00_row": jnp.repeat(w_s[l][:, 0, 0], DG_B)[None, :],
        "bs0_row": jnp.repeat(b_s[l][:, 0], DG_B)[None, :],
        "w_out": w_out[l].astype(BF16),
        "g_ffn": norm_ffn_g[l][None, :],
        "wr_hi": wr_hi,
        "wr_lo": wr_lo,
        "br": br,
        "w_pg": w_ple_gate[l].astype(BF16),
        "w_ple": w_ple[l].astype(BF16),
        "ple_g": ple_norm_g[l][None, :],
    }


def _routing_tables(tcnt, n_blocks):
    c8 = tcnt[:, 0, N_GROUPS:N_GROUPS + N_EXPERTS].astype(jnp.int32)
    blocks = (jnp.sum(c8, axis=0) + FFN_BLK - 1) // FFN_BLK
    blk_end = jnp.cumsum(blocks)
    pstart = (blk_end - blocks) * FFN_BLK
    run_dst = pstart[None, :] + jnp.cumsum(c8, axis=0) - c8
    run_src = jnp.cumsum(c8, axis=1) - c8
    units = c8 // SUBLANES
    n_used = blk_end[-1:]
    blk = jnp.minimum(jnp.arange(n_blocks, dtype=jnp.int32), n_used[0] - 1)
    block_expert = jnp.minimum(jnp.sum((blk_end[None, :] <= blk[:, None]).astype(jnp.int32), axis=1),
                               N_EXPERTS - 1)
    tables = (run_dst.reshape(-1), run_src.reshape(-1), units.reshape(-1), jnp.sum(units, axis=1))
    return tables, block_expert, n_used


def kernel(x_prompt, x_sample, state_C, state_n, state_m, state_conv, p_prompt, p_sample, norm_mix_g, w_in, conv_w, conv_b, b_igate, b_fgate, hnorm_g, vnorm_g, w_s, b_s, w_out, norm_ffn_g, router_g_w, router_g_b, router_e_w, router_e_b, w1, w3, w2, w_ple, ple_norm_g, w_ple_gate, final_norm_g):
    n_batch, seq, _ = x_prompt.shape
    n_s = x_sample.shape[0]
    depth = w_in.shape[0]
    t_p = n_batch * seq
    t_total = t_p + n_s
    assert seq % CHUNK == 0 and t_p % n_s == 0 and n_s % BT_SAMPLE == 0

    n_tiles = t_total // _token_tile(t_total)
    max_rows = t_total * TOP_K + N_EXPERTS * ((SUBLANES - 1) * n_tiles + FFN_BLK - 1)
    n_blocks = max_rows // FFN_BLK
    xs = jnp.zeros((n_blocks * FFN_BLK, D_MODEL), F32)

    p_all = jnp.concatenate([p_prompt.reshape(depth, t_p, PLE_DIM), p_sample.reshape(depth, n_s, PLE_DIM)], axis=1)
    sconv_t = jnp.transpose(state_conv, (0, 2, 1, 3))
    n0_t = jnp.transpose(state_n, (0, 2, 1, 3))
    m0_pad = jnp.pad(state_m, ((0, 0), (0, 0), (0, LANES - H_A)))
    final_g = final_norm_g[None, :]

    hbuf = jnp.concatenate([x_prompt.reshape(t_p, D_MODEL), x_sample.reshape(n_s, D_MODEL)], axis=0)
    outs = {k: [] for k in ("Cp", "np", "mp", "cbp", "Cs", "ns", "ms", "cbs", "vs")}
    res = None
    for l in range(depth):
        lw = _layer_weights(l, norm_mix_g, w_in, conv_w, conv_b, b_igate, b_fgate, hnorm_g, vnorm_g, w_s, b_s,
                            w_out, norm_ffn_g, router_g_w, router_g_b, router_e_w, router_e_b, w_ple,
                            ple_norm_g, w_ple_gate)
        hbuf, c_p, n_p, m_p, cb_p = _mix_prompt(hbuf, n_batch, seq, lw)
        hbuf, c_s, n_s_t, m_s, cb_s, v_s = _mix_sample(hbuf, lw, sconv_t, state_C, n0_t, m0_pad, n_s, l)
        route, route_t, tcnt = _router(hbuf, lw)
        tables, block_expert, n_used = _routing_tables(tcnt, n_blocks)
        xs = _dispatch(tables, hbuf, route_t, lw, xs)
        xs = _ffn(block_expert, n_used, xs, w1, w3, w2, l)
        final = l == depth - 1
        res = _ple(tables, hbuf, route, p_all, lw, final_g, xs, n_s, l, final)
        hbuf = res[0]
        outs["Cp"].append(c_p)
        outs["np"].append(n_p)
        outs["mp"].append(m_p[:, :, 0])
        outs["cbp"].append(cb_p)
        outs["Cs"].append(c_s)
        outs["ns"].append(jnp.transpose(n_s_t, (1, 0, 2)))
        outs["ms"].append(m_s[:, 0:H_A])
        outs["cbs"].append(jnp.transpose(cb_s, (1, 0, 2)))
        outs["vs"].append(v_s[:, None, :])

    y_prompt = res[0].reshape(n_batch, seq, D_MODEL)
    y_sample = res[1].reshape(n_s, 1, D_MODEL)
    st = lambda k: jnp.stack(outs[k])
    return (y_prompt, y_sample, st("Cp"), st("np"), st("mp"), st("cbp"),
            st("Cs"), st("ns"), st("ms"), st("cbs"), st("vs"))
```

```python
import functools

import jax
import jax.numpy as jnp
from jax import lax
from jax.experimental import pallas as pl
from jax.experimental.pallas import tpu as pltpu

F32 = jnp.float32
BF16 = jnp.bfloat16
U32 = jnp.uint32

D_MODEL = 1024
W_A = 512
H_A = 4
DH_A = 128
W_B = 512
G_B = 4
DG_B = 128
CHUNK = 128
CONV_W = 4
QK_W = 2 * W_A
N_GROUPS = 4
EXPERTS_PER_GROUP = 8
N_EXPERTS = N_GROUPS * EXPERTS_PER_GROUP
TOP_K = 2
D_FF = 512
PLE_DIM = 256
EPS = 1e-6
HALF_D = D_MODEL // 2

LANES = 128
SUBLANES = 8
VMEM_LIMIT_BYTES = 56 * 1024 * 1024

P_QK = 0
P_V = QK_W
P_O = P_V + W_A
P_U = P_O + W_A
P_VB = P_U + W_B
P_G = P_VB + W_B
P_W = P_G + LANES

TL_MIX = 512
BT_SAMPLE = 8
TM_TOK = 384
FFN_BLK = 256


def _cparams(*sem):
    return pltpu.CompilerParams(dimension_semantics=sem, vmem_limit_bytes=VMEM_LIMIT_BYTES)


def _rms(x, g):
    return x * lax.rsqrt(jnp.mean(x * x, axis=-1, keepdims=True) + EPS) * g


def _log_sigmoid(x):
    return -(jnp.maximum(-x, 0.0) + jnp.log1p(jnp.exp(-jnp.abs(x))))


def _split_dot(a, b_bf16):
    hi = a.astype(BF16)
    lo = (a - hi.astype(F32)).astype(BF16)
    return (jnp.dot(hi, b_bf16, preferred_element_type=F32)
            + jnp.dot(lo, b_bf16, preferred_element_type=F32))


def _dot_nt(a, b):
    return lax.dot_general(a, b, (((1,), (1,)), ((), ())), preferred_element_type=F32)


def _pack_rows(x):
    lo = lax.bitcast_convert_type(x[:, :HALF_D], U32) >> 16
    hi = lax.bitcast_convert_type(x[:, HALF_D:], U32) & jnp.uint32(0xFFFF0000)
    return lo | hi


def _unpack_rows(u):
    lo = lax.bitcast_convert_type(u << 16, F32)
    hi = lax.bitcast_convert_type(u & jnp.uint32(0xFFFF0000), F32)
    return lo.astype(BF16), hi.astype(BF16)


def _mix_prompt_kernel(x_ref, gmix_ref, win_ref, cw_ref, cb_ref, gb_ref, hng_ref, vng_ref,
                       ws_ref, bs_ref, wout_ref,
                       out_ref, c_out, n_out, m_out, conv_out,
                       proj, xpad, hcat, c_s, n_s, m_s, *, tl):
    j = pl.program_id(1)
    nj = pl.num_programs(1)
    n_chunks = tl // CHUNK

    @pl.when(j == 0)
    def _():
        c_s[...] = jnp.zeros_like(c_s)
        n_s[...] = jnp.zeros_like(n_s)
        m_s[...] = jnp.zeros_like(m_s)
        xpad[0:SUBLANES, :] = jnp.zeros((SUBLANES, QK_W), F32)

    x = x_ref[...]
    xn = _rms(x, gmix_ref[...]).astype(BF16)
    proj[...] = jnp.dot(xn, win_ref[...], preferred_element_type=F32)

    pre = proj[:, P_QK:P_QK + QK_W]
    xpad[SUBLANES:SUBLANES + tl, :] = pre
    qk = cb_ref[...] + cw_ref[CONV_W - 1:CONV_W, :] * pre
    for jj in range(1, CONV_W):
        qk = qk + cw_ref[CONV_W - 1 - jj:CONV_W - jj, :] * xpad[SUBLANES - jj:SUBLANES - jj + tl, :]
    new_rows = pre[tl - (CONV_W - 1):tl, :]
    xpad[SUBLANES - (CONV_W - 1):SUBLANES, :] = new_rows

    @pl.when(j == nj - 1)
    def _():
        conv_out[0] = new_rows

    qk = qk * jax.nn.sigmoid(qk)
    proj[:, 0:W_A] = qk[:, 0:W_A]
    proj[:, W_A:QK_W] = qk[:, W_A:QK_W] * (DH_A ** -0.5)

    row = lax.broadcasted_iota(jnp.int32, (CHUNK, CHUNK), 0)
    col = lax.broadcasted_iota(jnp.int32, (CHUNK, CHUNK), 1)
    causal = row >= col
    tril = jnp.where(causal, 1.0, 0.0).astype(BF16)
    triu = jnp.where(row <= col, 1.0, 0.0).astype(BF16)
    lane = lax.broadcasted_iota(jnp.int32, (CHUNK, LANES), 1)
    wm = [jnp.where(causal, ws_ref[g], 0.0).astype(BF16) for g in range(G_B)]

    def chunk_body(c, carry):
        r0 = pl.multiple_of(c * CHUNK, CHUNK)
        rows = pl.ds(r0, CHUNK)
        gl = proj[rows, P_G:P_G + LANES] + gb_ref[...]
        gl = jnp.where(lane >= H_A, _log_sigmoid(gl), gl)
        glt = gl.T
        b_col_all = _split_dot_left(tril, gl)
        b_row_all = _split_dot(glt[0:2 * SUBLANES, :], triu)

        for h in range(H_A):
            q32 = proj[rows, h * DH_A:(h + 1) * DH_A]
            k32 = proj[rows, W_A + h * DH_A:W_A + (h + 1) * DH_A]
            v32 = proj[rows, P_V + h * DH_A:P_V + (h + 1) * DH_A]
            q = q32.astype(BF16)
            k = k32.astype(BF16)
            ig_c = gl[:, h:h + 1]
            ig_r = glt[h:h + 1, :]
            b_c = b_col_all[:, H_A + h:H_A + h + 1]
            b_r = b_row_all[H_A + h:H_A + h + 1, :]
            c_old = c_s[h]
            n_old = n_s[h:h + 1, :]
            m_prev = m_s[h:h + 1, 0:1]

            d_log = jnp.where(causal, b_c - b_r + ig_r, -jnp.inf)
            inter = b_c + m_prev
            m_t = jnp.maximum(inter, jnp.max(d_log, axis=-1, keepdims=True))
            s = _dot_nt(q, k) * jnp.exp(d_log - m_t)
            w_inter = jnp.exp(inter - m_t)
            num = (w_inter * _dot_nt(q, c_old.astype(BF16))
                   + jnp.dot(s.astype(BF16), v32.astype(BF16), preferred_element_type=F32))
            nq = (w_inter * jnp.sum(q32 * n_old, axis=-1, keepdims=True)
                  + jnp.sum(s, axis=-1, keepdims=True))
            hh = num / jnp.maximum(jnp.abs(nq), jnp.exp(-m_t))

            m_new = m_t[CHUNK - 1:CHUNK, :]
            b_last = b_c[CHUNK - 1:CHUNK, :]
            w_state = jnp.exp(b_last - b_c + ig_c - m_new)
            decay = jnp.exp(b_last + m_prev - m_new)
            vw = (v32 * w_state).astype(BF16)
            c_s[h] = decay * c_old + lax.dot_general(
                vw, k, (((0,), (0,)), ((), ())), preferred_element_type=F32)
            n_s[h:h + 1, :] = decay * n_old + jnp.sum(w_state * k32, axis=0, keepdims=True)
            m_s[h:h + 1, :] = jnp.broadcast_to(m_new, (1, LANES))

            ha = _rms(hh, hng_ref[:, h * DH_A:(h + 1) * DH_A])
            o = proj[rows, P_O + h * DH_A:P_O + (h + 1) * DH_A]
            hcat[rows, h * DH_A:(h + 1) * DH_A] = (ha * jax.nn.sigmoid(o)).astype(BF16)

        for g in range(G_B):
            vb = proj[rows, P_VB + g * DG_B:P_VB + (g + 1) * DG_B]
            u = proj[rows, P_U + g * DG_B:P_U + (g + 1) * DG_B]
            vn = _rms(jax.nn.gelu(vb), vng_ref[:, g * DG_B:(g + 1) * DG_B])
            z = jnp.dot(wm[g], vn.astype(BF16), preferred_element_type=F32) + bs_ref[:, g:g + 1]
            hcat[rows, W_A + g * DG_B:W_A + (g + 1) * DG_B] = (jax.nn.gelu(u) * z).astype(BF16)
        return carry

    lax.fori_loop(0, n_chunks, chunk_body, 0)

    out_ref[...] = x + jnp.dot(hcat[...], wout_ref[...], preferred_element_type=F32)

    @pl.when(j == nj - 1)
    def _():
        c_out[0] = c_s[...]
        n_out[0] = n_s[0:H_A, :]
        m_out[0] = m_s[0:H_A, :]


def _split_dot_left(a_bf16, b):
    hi = b.astype(BF16)
    lo = (b - hi.astype(F32)).astype(BF16)
    return (jnp.dot(a_bf16, hi, preferred_element_type=F32)
            + jnp.dot(a_bf16, lo, preferred_element_type=F32))


def _mix_prompt(hbuf, n_batch, seq, lw):
    tl = max(t for t in range(CHUNK, TL_MIX + 1, CHUNK) if seq % t == 0)
    nj = seq // tl
    const = lambda b, j: (0, 0)
    kern = functools.partial(_mix_prompt_kernel, tl=tl)
    return pl.pallas_call(
        kern,
        name="mix_prompt",
        grid=(n_batch, nj),
        in_specs=[
            pl.BlockSpec((tl, D_MODEL), lambda b, j: (b * nj + j, 0)),
            pl.BlockSpec((1, D_MODEL), const),
            pl.BlockSpec((D_MODEL, P_W), const),
            pl.BlockSpec((CONV_W, QK_W), const),
            pl.BlockSpec((1, QK_W), const),
            pl.BlockSpec((1, LANES), const),
            pl.BlockSpec((1, W_A), const),
            pl.BlockSpec((1, W_B), const),
            pl.BlockSpec((G_B, CHUNK, CHUNK), lambda b, j: (0, 0, 0)),
            pl.BlockSpec((CHUNK, LANES), const),
            pl.BlockSpec((D_MODEL, D_MODEL), const),
        ],
        out_specs=[
            pl.BlockSpec((tl, D_MODEL), lambda b, j: (b * nj + j, 0)),
            pl.BlockSpec((1, H_A, DH_A, DH_A), lambda b, j: (b, 0, 0, 0)),
            pl.BlockSpec((1, H_A, DH_A), lambda b, j: (b, 0, 0)),
            pl.BlockSpec((1, H_A, LANES), lambda b, j: (b, 0, 0)),
            pl.BlockSpec((1, CONV_W - 1, QK_W), lambda b, j: (b, 0, 0)),
        ],
        out_shape=[
            jax.ShapeDtypeStruct(hbuf.shape, F32),
            jax.ShapeDtypeStruct((n_batch, H_A, DH_A, DH_A), F32),
            jax.ShapeDtypeStruct((n_batch, H_A, DH_A), F32),
            jax.ShapeDtypeStruct((n_batch, H_A, LANES), F32),
            jax.ShapeDtypeStruct((n_batch, CONV_W - 1, QK_W), F32),
        ],
        scratch_shapes=[
            pltpu.VMEM((tl, P_W), F32),
            pltpu.VMEM((SUBLANES + tl, QK_W), F32),
            pltpu.VMEM((tl, D_MODEL), BF16),
            pltpu.VMEM((H_A, DH_A, DH_A), F32),
            pltpu.VMEM((SUBLANES, LANES), F32),
            pltpu.VMEM((SUBLANES, LANES), F32),
        ],
        input_output_aliases={0: 0},
        compiler_params=_cparams("arbitrary", "arbitrary"),
    )(hbuf, lw["g_mix"], lw["w_in"], lw["conv_w"], lw["conv_b"], lw["gate_b"], lw["hn_g"], lw["vn_g"],
      lw["w_s"], lw["b_s_col"], lw["w_out"])


def _mix_sample_kernel(x_ref, gmix_ref, win_ref, cw_ref, cb_ref, gb_ref, hng_ref, vng_ref,
                       w00_ref, bs0_ref, wout_ref, sconv_ref, c_in, n_in, m_in,
                       out_ref, c_out, n_out, m_out, conv_out, vrow_out,
                       proj, hcat, *, bt):
    i = pl.program_id(0)
    ni = pl.num_programs(0)

    @pl.when(i == 0)
    def _():
        x = x_ref[...]
        xn = _rms(x, gmix_ref[...]).astype(BF16)
        proj[...] = jnp.dot(xn, win_ref[...], preferred_element_type=F32)
        pre = proj[:, P_QK:P_QK + QK_W]
        qk = cb_ref[...] + cw_ref[CONV_W - 1:CONV_W, :] * pre
        for jj in range(CONV_W - 1):
            qk = qk + cw_ref[jj:jj + 1, :] * sconv_ref[jj]
        for jj in range(CONV_W - 2):
            conv_out[jj] = sconv_ref[jj + 1]
        conv_out[CONV_W - 2] = pre
        qk = qk * jax.nn.sigmoid(qk)
        proj[:, 0:W_A] = qk[:, 0:W_A]
        proj[:, W_A:QK_W] = qk[:, W_A:QK_W] * (DH_A ** -0.5)
        for g in range(G_B):
            sl = slice(g * DG_B, (g + 1) * DG_B)
            vb = proj[:, P_VB + g * DG_B:P_VB + (g + 1) * DG_B]
            u = proj[:, P_U + g * DG_B:P_U + (g + 1) * DG_B]
            vn = _rms(jax.nn.gelu(vb), vng_ref[:, sl])
            vrow_out[:, sl] = vn
            z = w00_ref[:, sl] * vn + bs0_ref[:, sl]
            hcat[:, W_A + g * DG_B:W_A + (g + 1) * DG_B] = jax.nn.gelu(u) * z

    rows = pl.ds(pl.multiple_of(i * bt, bt), bt)
    gates = proj[rows, P_G:P_G + LANES] + gb_ref[...]
    lane = lax.broadcasted_iota(jnp.int32, (bt, LANES), 1)
    sub = lax.broadcasted_iota(jnp.int32, (bt, DH_A), 0)
    m_new_all = jnp.zeros((bt, LANES), F32)
    for h in range(H_A):
        ig = gates[:, h:h + 1]
        lf = _log_sigmoid(gates[:, H_A + h:H_A + h + 1])
        m_prev = m_in[rows, h:h + 1]
        inter = lf + m_prev
        m_t = jnp.maximum(inter, ig)
        w_inter = jnp.exp(inter - m_t)
        e_d = jnp.exp(ig - m_t)
        q = proj[rows, h * DH_A:(h + 1) * DH_A]
        k = proj[rows, W_A + h * DH_A:W_A + (h + 1) * DH_A]
        v = proj[rows, P_V + h * DH_A:P_V + (h + 1) * DH_A]
        n_old = n_in[h]
        s = jnp.sum(q * k, axis=-1, keepdims=True) * e_d
        vw = v * e_d
        vw_t = jnp.concatenate([vw, jnp.zeros((DH_A - bt, DH_A), F32)], axis=0).T
        qb = q.astype(BF16)
        qc = jnp.zeros((bt, DH_A), F32)
        for t in range(bt):
            c_old = c_in[t, h]
            r = _dot_nt(qb, c_old.astype(BF16))
            qc = jnp.where(sub == t, r, qc)
            c_out[t, h] = w_inter[t:t + 1, :] * c_old + vw_t[:, t:t + 1] * k[t:t + 1, :]
        num = w_inter * qc + s * v
        nq = w_inter * jnp.sum(q * n_old, axis=-1, keepdims=True) + s
        hh = num / jnp.maximum(jnp.abs(nq), jnp.exp(-m_t))
        n_out[h] = w_inter * n_old + e_d * k
        m_new_all = jnp.where(lane == h, m_t, m_new_all)
        ha = _rms(hh, hng_ref[:, h * DH_A:(h + 1) * DH_A])
        o = proj[rows, P_O + h * DH_A:P_O + (h + 1) * DH_A]
        hcat[rows, h * DH_A:(h + 1) * DH_A] = ha * jax.nn.sigmoid(o)
    m_out[rows, :] = m_new_all

    @pl.when(i == ni - 1)
    def _():
        out_ref[...] = x_ref[...] + jnp.dot(hcat[...].astype(BF16), wout_ref[...],
                                            preferred_element_type=F32)


def _mix_sample(hbuf, lw, sconv_t, c0, n0_t, m0_pad, n_s, layer):
    bt = BT_SAMPLE
    ni = n_s // bt
    const = lambda i: (0, 0)
    out_blk = hbuf.shape[0] // n_s - 1
    kern = functools.partial(_mix_sample_kernel, bt=bt)
    return pl.pallas_call(
        kern,
        name="mix_sample",
        grid=(ni,),
        in_specs=[
            pl.BlockSpec((n_s, D_MODEL), lambda i: (out_blk, 0)),
            pl.BlockSpec((1, D_MODEL), const),
            pl.BlockSpec((D_MODEL, P_W), const),
            pl.BlockSpec((CONV_W, QK_W), const),
            pl.BlockSpec((1, QK_W), const),
            pl.BlockSpec((1, LANES), const),
            pl.BlockSpec((1, W_A), const),
            pl.BlockSpec((1, W_B), const),
            pl.BlockSpec((1, W_B), const),
            pl.BlockSpec((1, W_B), const),
            pl.BlockSpec((D_MODEL, D_MODEL), const),
            pl.BlockSpec((None, CONV_W - 1, n_s, QK_W), lambda i: (layer, 0, 0, 0)),
            pl.BlockSpec((None, bt, H_A, DH_A, DH_A), lambda i: (layer, i, 0, 0, 0)),
            pl.BlockSpec((None, H_A, bt, DH_A), lambda i: (layer, 0, i, 0)),
            pl.BlockSpec((None, n_s, LANES), lambda i: (layer, 0, 0)),
        ],
        out_specs=[
            pl.BlockSpec((n_s, D_MODEL), lambda i: (out_blk, 0)),
            pl.BlockSpec((bt, H_A, DH_A, DH_A), lambda i: (i, 0, 0, 0)),
            pl.BlockSpec((H_A, bt, DH_A), lambda i: (0, i, 0)),
            pl.BlockSpec((n_s, LANES), const),
            pl.BlockSpec((CONV_W - 1, n_s, QK_W), lambda i: (0, 0, 0)),
            pl.BlockSpec((n_s, W_B), const),
        ],
        out_shape=[
            jax.ShapeDtypeStruct(hbuf.shape, F32),
            jax.ShapeDtypeStruct(c0.shape[1:], F32),
            jax.ShapeDtypeStruct(n0_t.shape[1:], F32),
            jax.ShapeDtypeStruct((n_s, LANES), F32),
            jax.ShapeDtypeStruct((CONV_W - 1, n_s, QK_W), F32),
            jax.ShapeDtypeStruct((n_s, W_B), F32),
        ],
        scratch_shapes=[
            pltpu.VMEM((n_s, P_W), F32),
            pltpu.VMEM((n_s, D_MODEL), F32),
        ],
        input_output_aliases={0: 0},
        compiler_params=_cparams("arbitrary"),
    )(hbuf, lw["g_mix"], lw["w_in"], lw["conv_w"], lw["conv_b"], lw["gate_b"], lw["hn_g"], lw["vn_g"],
      lw["w00_row"], lw["bs0_row"], lw["w_out"], sconv_t, c0, n0_t, m0_pad)


def _router_kernel(h_ref, g_ref, whi_ref, wlo_ref, b_ref, route_ref, route_t_ref, tcnt_ref, *, tm):
    hn = _rms(h_ref[...], g_ref[...])
    hi = hn.astype(BF16)
    lo = (hn - hi.astype(F32)).astype(BF16)
    logits = (jnp.dot(hi, whi_ref[...], preferred_element_type=F32)
              + jnp.dot(lo, whi_ref[...], preferred_element_type=F32)
              + jnp.dot(hi, wlo_ref[...], preferred_element_type=F32)) + b_ref[...]

    lane_i = lax.broadcasted_iota(jnp.int32, (tm, LANES), 1)
    lane = lane_i.astype(F32)
    neg = -jnp.inf
    is_g = lane_i < N_GROUPS
    gl = jnp.where(is_g, logits, neg)
    gmax = jnp.max(gl, axis=-1, keepdims=True)
    g_sel = jnp.min(jnp.where(gl == gmax, lane, float(LANES)), axis=-1, keepdims=True)
    g_w = 1.0 / jnp.sum(jnp.where(is_g, jnp.exp(logits - gmax), 0.0), axis=-1, keepdims=True)

    e_lane = lane_i - N_GROUPS
    lane_grp = (e_lane >> 3).astype(F32)
    in_grp = (e_lane >= 0) & (e_lane < N_EXPERTS) & (lane_grp == g_sel)
    el = jnp.where(in_grp, logits, neg)
    v1 = jnp.max(el, axis=-1, keepdims=True)
    i1 = jnp.min(jnp.where(el == v1, lane, float(LANES)), axis=-1, keepdims=True)
    el2 = jnp.where(lane == i1, neg, el)
    v2 = jnp.max(el2, axis=-1, keepdims=True)
    i2 = jnp.min(jnp.where(el2 == v2, lane, float(LANES)), axis=-1, keepdims=True)
    e21 = jnp.exp(v2 - v1)
    den = 1.0 + e21
    w1 = g_w * (1.0 / den)
    w2 = g_w * (e21 / den)

    onehot = jnp.where((lane == i1) | (lane == i2), 1.0, 0.0)
    r_i = lax.broadcasted_iota(jnp.int32, (tm, tm), 0)
    c_i = lax.broadcasted_iota(jnp.int32, (tm, tm), 1)
    strict = jnp.where(r_i > c_i, 1.0, 0.0).astype(BF16)
    before = jnp.dot(strict, onehot.astype(BF16), preferred_element_type=F32)
    cnt = jnp.sum(onehot, axis=0, keepdims=True)
    cnt8 = jnp.floor((cnt + (SUBLANES - 1.0)) * (1.0 / SUBLANES)) * SUBLANES
    l_r = lax.broadcasted_iota(jnp.int32, (LANES, LANES), 0)
    l_c = lax.broadcasted_iota(jnp.int32, (LANES, LANES), 1)
    lanes_before = jnp.where(l_r < l_c, 1.0, 0.0).astype(BF16)
    start = _split_dot(jnp.broadcast_to(cnt8, (2 * SUBLANES, LANES)), lanes_before)[0:1, :]
    slot = start + before
    pos1 = jnp.sum(jnp.where(lane == i1, slot, 0.0), axis=-1, keepdims=True)
    pos2 = jnp.sum(jnp.where(lane == i2, slot, 0.0), axis=-1, keepdims=True)
    tcnt_ref[...] = jnp.broadcast_to(cnt8, tcnt_ref.shape)

    out = jnp.zeros((tm, LANES), F32)
    for idx, val in enumerate((pos1, pos2, w1, w2)):
        out = jnp.where(lane_i == idx, val, out)
    route_ref[...] = out
    route_t_ref[...] = out.T[0:SUBLANES, :]


def _router(h1, lw):
    t_total = h1.shape[0]
    tm = _token_tile(t_total)
    n_tiles = t_total // tm
    const = lambda i: (0, 0)
    return pl.pallas_call(
        functools.partial(_router_kernel, tm=tm),
        name="router",
        grid=(n_tiles,),
        in_specs=[
            pl.BlockSpec((tm, D_MODEL), lambda i: (i, 0)),
            pl.BlockSpec((1, D_MODEL), const),
            pl.BlockSpec((D_MODEL, LANES), const),
            pl.BlockSpec((D_MODEL, LANES), const),
            pl.BlockSpec((1, LANES), const),
        ],
        out_specs=[
            pl.BlockSpec((tm, LANES), lambda i: (i, 0)),
            pl.BlockSpec((SUBLANES, tm), lambda i: (0, i)),
            pl.BlockSpec((None, SUBLANES, LANES), lambda i: (i, 0, 0)),
        ],
        out_shape=[
            jax.ShapeDtypeStruct((t_total, LANES), F32),
            jax.ShapeDtypeStruct((SUBLANES, t_total), F32),
            jax.ShapeDtypeStruct((n_tiles, SUBLANES, LANES), F32),
        ],
        compiler_params=_cparams("arbitrary"),
    )(h1, lw["g_ffn"], lw["wr_hi"], lw["wr_lo"], lw["br"])


def _tile_slots(tm):
    raw = tm * TOP_K + N_EXPERTS * (SUBLANES - 1)
    return -(-raw // LANES) * LANES


def _token_tile(t_total):
    tm = TM_TOK
    while t_total % tm:
        tm //= 2
    return tm


def _run_copies(tile, rdst_ref, rsrc_ref, runits_ref, make_copy):
    def per_expert(e, carry):
        k = tile * N_EXPERTS + e
        s0 = rsrc_ref[k]
        d0 = rdst_ref[k]

        def per_unit(u, c2):
            off = u * SUBLANES
            make_copy(pl.multiple_of(s0 + off, SUBLANES), pl.multiple_of(d0 + off, SUBLANES)).start()
            return c2

        lax.fori_loop(0, runits_ref[k], per_unit, 0)
        return carry

    lax.fori_loop(0, N_EXPERTS, per_expert, 0)


def _drain_copies(n_units, make_copy):
    def wait_one(u, carry):
        make_copy(0, 0).wait()
        return carry

    lax.fori_loop(0, n_units, wait_one, 0)


def _dispatch_kernel(rdst_ref, rsrc_ref, runits_ref, tunits_ref, h_ref, g_ref, rt_ref, xs_in, xs_out,
                     srt, sem, *, tm, slots):
    del xs_in
    i = pl.program_id(0)
    xn = _rms(h_ref[...], g_ref[...]).astype(BF16)
    slot_id = lax.broadcasted_iota(jnp.int32, (slots, tm), 0).astype(F32)
    sel = (slot_id == rt_ref[0:1, :]) | (slot_id == rt_ref[1:2, :])
    srt[...] = _pack_rows(jnp.dot(jnp.where(sel, 1.0, 0.0).astype(BF16), xn, preferred_element_type=F32))

    def make_copy(tile_row, sorted_row):
        return pltpu.make_async_copy(srt.at[pl.ds(tile_row, SUBLANES)],
                                     xs_out.at[pl.ds(sorted_row, SUBLANES)], sem)

    _run_copies(i, rdst_ref, rsrc_ref, runits_ref, make_copy)
    _drain_copies(tunits_ref[i], make_copy)


def _dispatch(tables, h1, route_t, lw, xs):
    t_total = h1.shape[0]
    tm = _token_tile(t_total)
    slots = _tile_slots(tm)
    return pl.pallas_call(
        functools.partial(_dispatch_kernel, tm=tm, slots=slots),
        name="dispatch",
        grid_spec=pltpu.PrefetchScalarGridSpec(
            num_scalar_prefetch=4,
            grid=(t_total // tm,),
            in_specs=[
                pl.BlockSpec((tm, D_MODEL), lambda i, *_: (i, 0)),
                pl.BlockSpec((1, D_MODEL), lambda i, *_: (0, 0)),
                pl.BlockSpec((SUBLANES, tm), lambda i, *_: (0, i)),
                pl.BlockSpec(memory_space=pl.ANY),
            ],
            out_specs=pl.BlockSpec(memory_space=pl.ANY),
            scratch_shapes=[pltpu.VMEM((slots, HALF_D), U32), pltpu.SemaphoreType.DMA(())],
        ),
        out_shape=jax.ShapeDtypeStruct(xs.shape, U32),
        input_output_aliases={7: 0},
        compiler_params=_cparams("arbitrary"),
    )(*tables, h1, lw["g_ffn"], route_t, xs)


def _ffn_kernel(be_ref, nu_ref, x_ref, w1_ref, w3_ref, w2_ref, y_ref, w1b, w3b, w2b):
    i = pl.program_id(0)
    prev = be_ref[jnp.maximum(i - 1, 0)]
    fresh = (i == 0) | (be_ref[i] != prev)

    @pl.when(fresh)
    def _():
        w1b[...] = w1_ref[...].astype(BF16)
        w3b[...] = w3_ref[...].astype(BF16)
        w2b[...] = w2_ref[...].astype(BF16)

    @pl.when(i < nu_ref[0])
    def _():
        x_lo, x_hi = _unpack_rows(x_ref[...])
        a = (jnp.dot(x_lo, w1b[0:HALF_D, :], preferred_element_type=F32)
             + jnp.dot(x_hi, w1b[HALF_D:, :], preferred_element_type=F32))
        b = (jnp.dot(x_lo, w3b[0:HALF_D, :], preferred_element_type=F32)
             + jnp.dot(x_hi, w3b[HALF_D:, :], preferred_element_type=F32))
        hmid = (a * jax.nn.sigmoid(a) * b).astype(BF16)
        y = jnp.dot(hmid, w2b[...], preferred_element_type=F32)
        y_ref[...] = _pack_rows(y.astype(BF16).astype(F32))


def _ffn(block_expert, n_used, xs, w1, w3, w2, layer):
    n_rows = xs.shape[0]
    nb = n_rows // FFN_BLK

    def row_map(i, be, nu):
        return (jnp.minimum(i, nu[0] - 1), 0)

    def w_map(i, be, nu):
        return (layer, be[i], 0, 0)

    return pl.pallas_call(
        _ffn_kernel,
        name="expert_ffn",
        grid_spec=pltpu.PrefetchScalarGridSpec(
            num_scalar_prefetch=2,
            grid=(nb,),
            in_specs=[
                pl.BlockSpec((FFN_BLK, HALF_D), row_map),
                pl.BlockSpec((None, None, D_MODEL, D_FF), w_map),
                pl.BlockSpec((None, None, D_MODEL, D_FF), w_map),
                pl.BlockSpec((None, None, D_FF, D_MODEL), w_map),
            ],
            out_specs=pl.BlockSpec((FFN_BLK, HALF_D), row_map),
            scratch_shapes=[
                pltpu.VMEM((D_MODEL, D_FF), BF16),
                pltpu.VMEM((D_MODEL, D_FF), BF16),
                pltpu.VMEM((D_FF, D_MODEL), BF16),
            ],
        ),
        out_shape=jax.ShapeDtypeStruct((n_rows, HALF_D), U32),
        input_output_aliases={2: 0},
        compiler_params=_cparams("arbitrary"),
    )(block_expert, n_used, xs, w1, w3, w2)


def _ple_kernel(rdst_ref, rsrc_ref, runits_ref, tunits_ref, h_ref, route_ref, p_ref, wpg_ref, wple_ref,
                pg_ref, fg_ref, ys_ref, *rest, tm, slots, n_s, final):
    if final:
        yp_ref, ysm_ref, ysrt, sem = rest
    else:
        out_ref, ysrt, sem = rest
    i = pl.program_id(0)

    @pl.when(i == 0)
    def _():
        ysrt[...] = jnp.zeros_like(ysrt)

    def make_copy(tile_row, sorted_row):
        return pltpu.make_async_copy(ys_ref.at[pl.ds(sorted_row, SUBLANES)],
                                     ysrt.at[pl.ds(tile_row, SUBLANES)], sem)

    _run_copies(i, rdst_ref, rsrc_ref, runits_ref, make_copy)
    pe = _rms(jnp.dot(p_ref[...].astype(BF16), wple_ref[...], preferred_element_type=F32), pg_ref[...])
    _drain_copies(tunits_ref[i], make_copy)

    route = route_ref[...]
    y_lo, y_hi = _unpack_rows(ysrt[...])
    slot_id = lax.broadcasted_iota(jnp.int32, (tm, slots), 1).astype(F32)
    moe = None
    for kk in range(TOP_K):
        pick = jnp.where(slot_id == route[:, kk:kk + 1], 1.0, 0.0).astype(BF16)
        rows = jnp.concatenate([jnp.dot(pick, y_lo, preferred_element_type=F32),
                                jnp.dot(pick, y_hi, preferred_element_type=F32)], axis=1)
        term = route[:, TOP_K + kk:TOP_K + kk + 1] * rows
        moe = term if moe is None else moe + term
    h2 = h_ref[...] + moe
    gate = jax.nn.sigmoid(jnp.dot(h2.astype(BF16), wpg_ref[...], preferred_element_type=F32))
    out = h2 + gate * pe
    if final:
        fin = _rms(out, fg_ref[...])
        yp_ref[...] = fin

        @pl.when(i == pl.num_programs(0) - 1)
        def _():
            ysm_ref[...] = fin[tm - n_s:tm, :]
    else:
        out_ref[...] = out


def _ple(tables, h1, route, p_all, lw, final_g, ys, n_s, layer, final):
    t_total = h1.shape[0]
    tm = _token_tile(t_total)
    slots = _tile_slots(tm)
    assert n_s <= tm
    const = lambda i, *_: (0, 0)
    tok = lambda i, *_: (i, 0)
    if final:
        out_specs = [pl.BlockSpec((tm, D_MODEL), tok), pl.BlockSpec((n_s, D_MODEL), const)]
        out_shape = [jax.ShapeDtypeStruct((t_total - n_s, D_MODEL), F32),
                     jax.ShapeDtypeStruct((n_s, D_MODEL), F32)]
        aliases = {}
    else:
        out_specs = [pl.BlockSpec((tm, D_MODEL), tok)]
        out_shape = [jax.ShapeDtypeStruct((t_total, D_MODEL), F32)]
        aliases = {4: 0}
    return pl.pallas_call(
        functools.partial(_ple_kernel, tm=tm, slots=slots, n_s=n_s, final=final),
        name="combine_ple",
        grid_spec=pltpu.PrefetchScalarGridSpec(
            num_scalar_prefetch=4,
            grid=(t_total // tm,),
            in_specs=[
                pl.BlockSpec((tm, D_MODEL), tok),
                pl.BlockSpec((tm, LANES), tok),
                pl.BlockSpec((None, tm, PLE_DIM), lambda i, *_: (layer, i, 0)),
                pl.BlockSpec((D_MODEL, D_MODEL), const),
                pl.BlockSpec((PLE_DIM, D_MODEL), const),
                pl.BlockSpec((1, D_MODEL), const),
                pl.BlockSpec((1, D_MODEL), const),
                pl.BlockSpec(memory_space=pl.ANY),
            ],
            out_specs=out_specs,
            scratch_shapes=[pltpu.VMEM((slots, HALF_D), U32), pltpu.SemaphoreType.DMA(())],
        ),
        out_shape=out_shape,
        input_output_aliases=aliases,
        compiler_params=_cparams("arbitrary"),
    )(*tables, h1, route, p_all, lw["w_pg"], lw["w_ple"], lw["ple_g"], final_g, ys)


def _layer_weights(l, norm_mix_g, w_in, conv_w, conv_b, b_igate, b_fgate, hnorm_g, vnorm_g, w_s, b_s, w_out,
                   norm_ffn_g, router_g_w, router_g_b, router_e_w, router_e_b, w_ple, ple_norm_g, w_ple_gate):
    wi = w_in[l]
    a0 = QK_W + 2 * W_A
    gates_w = jnp.pad(wi[:, a0:a0 + 2 * H_A], ((0, 0), (0, LANES - 2 * H_A)))
    w_packed = jnp.concatenate(
        [wi[:, :a0], wi[:, a0 + 2 * H_A:], gates_w], axis=1).astype(BF16)
    gate_b = jnp.pad(jnp.concatenate([b_igate[l], b_fgate[l]]), (0, LANES - 2 * H_A))[None, :]
    wr = jnp.pad(jnp.concatenate([router_g_w[l], router_e_w[l]], axis=1),
                 ((0, 0), (0, LANES - N_GROUPS - N_EXPERTS)))
    wr_hi = wr.astype(BF16)
    wr_lo = (wr - wr_hi.astype(F32)).astype(BF16)
    br = jnp.pad(jnp.concatenate([router_g_b[l], router_e_b[l]]), (0, LANES - N_GROUPS - N_EXPERTS))[None, :]
    return {
        "g_mix": norm_mix_g[l][None, :],
        "w_in": w_packed,
        "conv_w": conv_w[l],
        "conv_b": conv_b[l][None, :],
        "gate_b": gate_b,
        "hn_g": hnorm_g[l][None, :],
        "vn_g": vnorm_g[l][None, :],
        "w_s": w_s[l],
        "b_s_col": jnp.pad(b_s[l].T, ((0, 0), (0, LANES - G_B))),
        "w00_row": jnp.repeat(w_s[l][:, 0, 0], DG_B)[None, :],
        "bs0_row": jnp.repeat(b_s[l][:, 0], DG_B)[None, :],
        "w_out": w_out[l].astype(BF16),
        "g_ffn": norm_ffn_g[l][None, :],
        "wr_hi": wr_hi,
        "wr_lo": wr_lo,
        "br": br,
        "w_pg": w_ple_gate[l].astype(BF16),
        "w_ple": w_ple[l].astype(BF16),
        "ple_g": ple_norm_g[l][None, :],
    }


def _routing_tables(tcnt, n_blocks):
    c8 = tcnt[:, 0, N_GROUPS:N_GROUPS + N_EXPERTS].astype(jnp.int32)
    blocks = (jnp.sum(c8, axis=0) + FFN_BLK - 1) // FFN_BLK
    blk_end = jnp.cumsum(blocks)
    pstart = (blk_end - blocks) * FFN_BLK
    run_dst = pstart[None, :] + jnp.cumsum(c8, axis=0) - c8
    run_src = jnp.cumsum(c8, axis=1) - c8
    units = c8 // SUBLANES
    n_used = blk_end[-1:]
    blk = jnp.minimum(jnp.arange(n_blocks, dtype=jnp.int32), n_used[0] - 1)
    block_expert = jnp.minimum(jnp.sum((blk_end[None, :] <= blk[:, None]).astype(jnp.int32), axis=1),
                               N_EXPERTS - 1)
    tables = (run_dst.reshape(-1), run_src.reshape(-1), units.reshape(-1), jnp.sum(units, axis=1))
    return tables, block_expert, n_used


def kernel(x_prompt, x_sample, state_C, state_n, state_m, state_conv, p_prompt, p_sample, norm_mix_g, w_in, conv_w, conv_b, b_igate, b_fgate, hnorm_g, vnorm_g, w_s, b_s, w_out, norm_ffn_g, router_g_w, router_g_b, router_e_w, router_e_b, w1, w3, w2, w_ple, ple_norm_g, w_ple_gate, final_norm_g):
    n_batch, seq, _ = x_prompt.shape
    n_s = x_sample.shape[0]
    depth = w_in.shape[0]
    t_p = n_batch * seq
    t_total = t_p + n_s
    assert seq % CHUNK == 0 and t_p % n_s == 0 and n_s % BT_SAMPLE == 0

    n_tiles = t_total // _token_tile(t_total)
    max_rows = t_total * TOP_K + N_EXPERTS * ((SUBLANES - 1) * n_tiles + FFN_BLK - 1)
    n_blocks = max_rows // FFN_BLK
    xs = jnp.zeros((n_blocks * FFN_BLK, HALF_D), U32)

    p_all = jnp.concatenate([p_prompt.reshape(depth, t_p, PLE_DIM), p_sample.reshape(depth, n_s, PLE_DIM)], axis=1)
    sconv_t = jnp.transpose(state_conv, (0, 2, 1, 3))
    n0_t = jnp.transpose(state_n, (0, 2, 1, 3))
    m0_pad = jnp.pad(state_m, ((0, 0), (0, 0), (0, LANES - H_A)))
    final_g = final_norm_g[None, :]

    hbuf = jnp.concatenate([x_prompt.reshape(t_p, D_MODEL), x_sample.reshape(n_s, D_MODEL)], axis=0)
    outs = {k: [] for k in ("Cp", "np", "mp", "cbp", "Cs", "ns", "ms", "cbs", "vs")}
    res = None
    for l in range(depth):
        lw = _layer_weights(l, norm_mix_g, w_in, conv_w, conv_b, b_igate, b_fgate, hnorm_g, vnorm_g, w_s, b_s,
                            w_out, norm_ffn_g, router_g_w, router_g_b, router_e_w, router_e_b, w_ple,
                            ple_norm_g, w_ple_gate)
        hbuf, c_p, n_p, m_p, cb_p = _mix_prompt(hbuf, n_batch, seq, lw)
        hbuf, c_s, n_s_t, m_s, cb_s, v_s = _mix_sample(hbuf, lw, sconv_t, state_C, n0_t, m0_pad, n_s, l)
        route, route_t, tcnt = _router(hbuf, lw)
        tables, block_expert, n_used = _routing_tables(tcnt, n_blocks)
        xs = _dispatch(tables, hbuf, route_t, lw, xs)
        xs = _ffn(block_expert, n_used, xs, w1, w3, w2, l)
        final = l == depth - 1
        res = _ple(tables, hbuf, route, p_all, lw, final_g, xs, n_s, l, final)
        hbuf = res[0]
        outs["Cp"].append(c_p)
        outs["np"].append(n_p)
        outs["mp"].append(m_p[:, :, 0])
        outs["cbp"].append(cb_p)
        outs["Cs"].append(c_s)
        outs["ns"].append(jnp.transpose(n_s_t, (1, 0, 2)))
        outs["ms"].append(m_s[:, 0:H_A])
        outs["cbs"].append(jnp.transpose(cb_s, (1, 0, 2)))
        outs["vs"].append(v_s[:, None, :])

    y_prompt = res[0].reshape(n_batch, seq, D_MODEL)
    y_sample = res[1].reshape(n_s, 1, D_MODEL)
    st = lambda k: jnp.stack(outs[k])
    return (y_prompt, y_sample, st("Cp"), st("np"), st("mp"), st("cbp"),
            st("Cs"), st("ns"), st("ms"), st("cbs"), st("vs"))
```

```python
import functools

import jax
import jax.numpy as jnp
from jax import lax
from jax.experimental import pallas as pl
from jax.experimental.pallas import tpu as pltpu

F32 = jnp.float32
BF16 = jnp.bfloat16
U32 = jnp.uint32

D_MODEL = 1024
W_A = 512
H_A = 4
DH_A = 128
W_B = 512
G_B = 4
DG_B = 128
CHUNK = 128
CONV_W = 4
QK_W = 2 * W_A
N_GROUPS = 4
EXPERTS_PER_GROUP = 8
N_EXPERTS = N_GROUPS * EXPERTS_PER_GROUP
TOP_K = 2
D_FF = 512
PLE_DIM = 256
EPS = 1e-6
HALF_D = D_MODEL // 2

LANES = 128
SUBLANES = 8
VMEM_LIMIT_BYTES = 56 * 1024 * 1024

P_QK = 0
P_V = QK_W
P_O = P_V + W_A
P_U = P_O + W_A
P_VB = P_U + W_B
P_G = P_VB + W_B
P_W = P_G + LANES

TL_MIX = 512
BT_SAMPLE = 8
TM_TOK = 384
FFN_BLK = 256


def _cparams(*sem):
    return pltpu.CompilerParams(dimension_semantics=sem, vmem_limit_bytes=VMEM_LIMIT_BYTES)


def _rms(x, g):
    return x * lax.rsqrt(jnp.mean(x * x, axis=-1, keepdims=True) + EPS) * g


def _log_sigmoid(x):
    return -(jnp.maximum(-x, 0.0) + jnp.log1p(jnp.exp(-jnp.abs(x))))


def _split_dot(a, b_bf16):
    hi = a.astype(BF16)
    lo = (a - hi.astype(F32)).astype(BF16)
    return (jnp.dot(hi, b_bf16, preferred_element_type=F32)
            + jnp.dot(lo, b_bf16, preferred_element_type=F32))


def _dot_nt(a, b):
    return lax.dot_general(a, b, (((1,), (1,)), ((), ())), preferred_element_type=F32)


def _pack_rows(x):
    lo = lax.bitcast_convert_type(x[:, :HALF_D], U32) >> 16
    hi = lax.bitcast_convert_type(x[:, HALF_D:], U32) & jnp.uint32(0xFFFF0000)
    return lo | hi


def _unpack_rows(u):
    lo = lax.bitcast_convert_type(u << 16, F32)
    hi = lax.bitcast_convert_type(u & jnp.uint32(0xFFFF0000), F32)
    return lo.astype(BF16), hi.astype(BF16)


def _mix_prompt_kernel(x_ref, xnext_ref, hprev_ref, gmix_ref, win_ref, cw_ref, cb_ref, gb_ref, hng_ref,
                       vng_ref, ws_ref, bs_ref, wout_ref,
                       out_ref, c_out, n_out, m_out, conv_out,
                       proj_a, proj_b, xpad, hcat, c_s, n_s, m_s, *, tl):
    del hprev_ref
    b = pl.program_id(0)
    j = pl.program_id(1)
    nj = pl.num_programs(1)
    n_chunks = tl // CHUNK
    bounds = [(P_W // LANES * c // n_chunks) * LANES for c in range(n_chunks + 1)]
    col_groups = [slice(bounds[c], bounds[c + 1]) for c in range(n_chunks)]

    @pl.when(j == 0)
    def _():
        c_s[...] = jnp.zeros_like(c_s)
        n_s[...] = jnp.zeros_like(n_s)
        m_s[...] = jnp.zeros_like(m_s)
        xpad[0:SUBLANES, :] = jnp.zeros((SUBLANES, QK_W), F32)

    @pl.when((b == 0) & (j == 0))
    def _():
        xn0 = _rms(x_ref[0:tl, :], gmix_ref[...]).astype(BF16)
        proj_a[...] = jnp.dot(xn0, win_ref[...], preferred_element_type=F32)

    row = lax.broadcasted_iota(jnp.int32, (CHUNK, CHUNK), 0)
    col = lax.broadcasted_iota(jnp.int32, (CHUNK, CHUNK), 1)
    causal = row >= col
    tril = jnp.where(causal, 1.0, 0.0).astype(BF16)
    triu = jnp.where(row <= col, 1.0, 0.0).astype(BF16)
    lane = lax.broadcasted_iota(jnp.int32, (CHUNK, LANES), 1)
    wm = [jnp.where(causal, ws_ref[g], 0.0).astype(BF16) for g in range(G_B)]

    def conv_step(proj):
        pre = proj[:, P_QK:P_QK + QK_W]
        xpad[SUBLANES:SUBLANES + tl, :] = pre
        qk = cb_ref[...] + cw_ref[CONV_W - 1:CONV_W, :] * pre
        for jj in range(1, CONV_W):
            qk = qk + cw_ref[CONV_W - 1 - jj:CONV_W - jj, :] * xpad[SUBLANES - jj:SUBLANES - jj + tl, :]
        xpad[SUBLANES - (CONV_W - 1):SUBLANES, :] = pre[tl - (CONV_W - 1):tl, :]
        qk = qk * jax.nn.sigmoid(qk)
        proj[:, 0:W_A] = qk[:, 0:W_A]
        proj[:, W_A:QK_W] = qk[:, W_A:QK_W] * (DH_A ** -0.5)

    def chunk_step(proj, c):
        rows = slice(c * CHUNK, (c + 1) * CHUNK)
        gl = proj[rows, P_G:P_G + LANES] + gb_ref[...]
        gl = jnp.where(lane >= H_A, _log_sigmoid(gl), gl)
        glt = gl.T
        b_col_all = _split_dot_left(tril, gl)
        b_row_all = _split_dot(glt[0:2 * SUBLANES, :], triu)

        for h in range(H_A):
            q32 = proj[rows, h * DH_A:(h + 1) * DH_A]
            k32 = proj[rows, W_A + h * DH_A:W_A + (h + 1) * DH_A]
            v32 = proj[rows, P_V + h * DH_A:P_V + (h + 1) * DH_A]
            q = q32.astype(BF16)
            k = k32.astype(BF16)
            ig_c = gl[:, h:h + 1]
            ig_r = glt[h:h + 1, :]
            b_c = b_col_all[:, H_A + h:H_A + h + 1]
            b_r = b_row_all[H_A + h:H_A + h + 1, :]
            c_old = c_s[h]
            n_old = n_s[h:h + 1, :]
            m_prev = m_s[h:h + 1, 0:1]

            d_log = jnp.where(causal, b_c - b_r + ig_r, -jnp.inf)
            inter = b_c + m_prev
            m_t = jnp.maximum(inter, jnp.max(d_log, axis=-1, keepdims=True))
            s = _dot_nt(q, k) * jnp.exp(d_log - m_t)
            w_inter = jnp.exp(inter - m_t)
            num = (w_inter * _dot_nt(q, c_old.astype(BF16))
                   + jnp.dot(s.astype(BF16), v32.astype(BF16), preferred_element_type=F32))
            nq = (w_inter * jnp.sum(q32 * n_old, axis=-1, keepdims=True)
                  + jnp.sum(s, axis=-1, keepdims=True))
            hh = num / jnp.maximum(jnp.abs(nq), jnp.exp(-m_t))

            m_new = m_t[CHUNK - 1:CHUNK, :]
            b_last = b_c[CHUNK - 1:CHUNK, :]
            w_state = jnp.exp(b_last - b_c + ig_c - m_new)
            decay = jnp.exp(b_last + m_prev - m_new)
            vw = (v32 * w_state).astype(BF16)
            c_s[h] = decay * c_old + lax.dot_general(
                vw, k, (((0,), (0,)), ((), ())), preferred_element_type=F32)
            n_s[h:h + 1, :] = decay * n_old + jnp.sum(w_state * k32, axis=0, keepdims=True)
            m_s[h:h + 1, :] = jnp.broadcast_to(m_new, (1, LANES))

            ha = _rms(hh, hng_ref[:, h * DH_A:(h + 1) * DH_A])
            o = proj[rows, P_O + h * DH_A:P_O + (h + 1) * DH_A]
            hcat[rows, h * DH_A:(h + 1) * DH_A] = (ha * jax.nn.sigmoid(o)).astype(BF16)

        for g in range(G_B):
            vb = proj[rows, P_VB + g * DG_B:P_VB + (g + 1) * DG_B]
            u = proj[rows, P_U + g * DG_B:P_U + (g + 1) * DG_B]
            vn = _rms(jax.nn.gelu(vb), vng_ref[:, g * DG_B:(g + 1) * DG_B])
            z = jnp.dot(wm[g], vn.astype(BF16), preferred_element_type=F32) + bs_ref[:, g:g + 1]
            hcat[rows, W_A + g * DG_B:W_A + (g + 1) * DG_B] = (jax.nn.gelu(u) * z).astype(BF16)

    def tile_pass(row0, proj_cur, proj_nxt, xn_next):
        conv_step(proj_cur)
        for c in range(n_chunks):
            chunk_step(proj_cur, c)
            cs = col_groups[c]
            proj_nxt[:, cs] = jnp.dot(xn_next, win_ref[:, cs], preferred_element_type=F32)
        out_ref[row0:row0 + tl, :] = x_ref[row0:row0 + tl, :] + jnp.dot(
            hcat[...], wout_ref[...], preferred_element_type=F32)

    tile_pass(0, proj_a, proj_b, _rms(x_ref[tl:2 * tl, :], gmix_ref[...]).astype(BF16))
    tile_pass(tl, proj_b, proj_a, _rms(xnext_ref[...], gmix_ref[...]).astype(BF16))

    @pl.when(j == nj - 1)
    def _():
        conv_out[0] = xpad[SUBLANES - (CONV_W - 1):SUBLANES, :]
        c_out[0] = c_s[...]
        n_out[0] = n_s[0:H_A, :]
        m_out[0] = m_s[0:H_A, :]


def _split_dot_left(a_bf16, b):
    hi = b.astype(BF16)
    lo = (b - hi.astype(F32)).astype(BF16)
    return (jnp.dot(a_bf16, hi, preferred_element_type=F32)
            + jnp.dot(a_bf16, lo, preferred_element_type=F32))


def _mix_prompt(h_in, h_out, n_batch, seq, lw):
    tl = max(t for t in range(CHUNK, TL_MIX + 1, CHUNK) if seq % (2 * t) == 0)
    nj = seq // (2 * tl)
    last_tile = n_batch * seq // tl - 1
    const = lambda b, j: (0, 0)
    once = pl.Buffered(1)
    kern = functools.partial(_mix_prompt_kernel, tl=tl)
    return pl.pallas_call(
        kern,
        name="mix_prompt",
        grid=(n_batch, nj),
        in_specs=[
            pl.BlockSpec((2 * tl, D_MODEL), lambda b, j: (b * nj + j, 0)),
            pl.BlockSpec((tl, D_MODEL), lambda b, j: (jnp.minimum(2 * (b * nj + j) + 2, last_tile), 0)),
            pl.BlockSpec(memory_space=pl.ANY),
            pl.BlockSpec((1, D_MODEL), const),
            pl.BlockSpec((D_MODEL, P_W), const, pipeline_mode=once),
            pl.BlockSpec((CONV_W, QK_W), const),
            pl.BlockSpec((1, QK_W), const),
            pl.BlockSpec((1, LANES), const),
            pl.BlockSpec((1, W_A), const),
            pl.BlockSpec((1, W_B), const),
            pl.BlockSpec((G_B, CHUNK, CHUNK), lambda b, j: (0, 0, 0)),
            pl.BlockSpec((CHUNK, LANES), const),
            pl.BlockSpec((D_MODEL, D_MODEL), const, pipeline_mode=once),
        ],
        out_specs=[
            pl.BlockSpec((2 * tl, D_MODEL), lambda b, j: (b * nj + j, 0)),
            pl.BlockSpec((1, H_A, DH_A, DH_A), lambda b, j: (b, 0, 0, 0)),
            pl.BlockSpec((1, H_A, DH_A), lambda b, j: (b, 0, 0)),
            pl.BlockSpec((1, H_A, LANES), lambda b, j: (b, 0, 0)),
            pl.BlockSpec((1, CONV_W - 1, QK_W), lambda b, j: (b, 0, 0)),
        ],
        out_shape=[
            jax.ShapeDtypeStruct(h_out.shape, F32),
            jax.ShapeDtypeStruct((n_batch, H_A, DH_A, DH_A), F32),
            jax.ShapeDtypeStruct((n_batch, H_A, DH_A), F32),
            jax.ShapeDtypeStruct((n_batch, H_A, LANES), F32),
            jax.ShapeDtypeStruct((n_batch, CONV_W - 1, QK_W), F32),
        ],
        scratch_shapes=[
            pltpu.VMEM((tl, P_W), F32),
            pltpu.VMEM((tl, P_W), F32),
            pltpu.VMEM((SUBLANES + tl, QK_W), F32),
            pltpu.VMEM((tl, D_MODEL), BF16),
            pltpu.VMEM((H_A, DH_A, DH_A), F32),
            pltpu.VMEM((SUBLANES, LANES), F32),
            pltpu.VMEM((SUBLANES, LANES), F32),
        ],
        input_output_aliases={2: 0},
        compiler_params=_cparams("arbitrary", "arbitrary"),
    )(h_in, h_in, h_out, lw["g_mix"], lw["w_in"], lw["conv_w"], lw["conv_b"], lw["gate_b"], lw["hn_g"], lw["vn_g"],
      lw["w_s"], lw["b_s_col"], lw["w_out"])


def _mix_sample_kernel(x_ref, hprev_ref, gmix_ref, win_ref, cw_ref, cb_ref, gb_ref, hng_ref, vng_ref,
                       w00_ref, bs0_ref, wout_ref, sconv_ref, c_in, n_in, m_in,
                       out_ref, c_out, n_out, m_out, conv_out, vrow_out,
                       proj, hcat, *, bt):
    del hprev_ref
    i = pl.program_id(0)
    ni = pl.num_programs(0)

    @pl.when(i == 0)
    def _():
        x = x_ref[...]
        xn = _rms(x, gmix_ref[...]).astype(BF16)
        proj[...] = jnp.dot(xn, win_ref[...], preferred_element_type=F32)
        pre = proj[:, P_QK:P_QK + QK_W]
        qk = cb_ref[...] + cw_ref[CONV_W - 1:CONV_W, :] * pre
        for jj in range(CONV_W - 1):
            qk = qk + cw_ref[jj:jj + 1, :] * sconv_ref[jj]
        for jj in range(CONV_W - 2):
            conv_out[jj] = sconv_ref[jj + 1]
        conv_out[CONV_W - 2] = pre
        qk = qk * jax.nn.sigmoid(qk)
        proj[:, 0:W_A] = qk[:, 0:W_A]
        proj[:, W_A:QK_W] = qk[:, W_A:QK_W] * (DH_A ** -0.5)
        for g in range(G_B):
            sl = slice(g * DG_B, (g + 1) * DG_B)
            vb = proj[:, P_VB + g * DG_B:P_VB + (g + 1) * DG_B]
            u = proj[:, P_U + g * DG_B:P_U + (g + 1) * DG_B]
            vn = _rms(jax.nn.gelu(vb), vng_ref[:, sl])
            vrow_out[:, sl] = vn
            z = w00_ref[:, sl] * vn + bs0_ref[:, sl]
            hcat[:, W_A + g * DG_B:W_A + (g + 1) * DG_B] = jax.nn.gelu(u) * z

    rows = pl.ds(pl.multiple_of(i * bt, bt), bt)
    gates = proj[rows, P_G:P_G + LANES] + gb_ref[...]
    lane = lax.broadcasted_iota(jnp.int32, (bt, LANES), 1)
    sub = lax.broadcasted_iota(jnp.int32, (bt, DH_A), 0)
    m_new_all = jnp.zeros((bt, LANES), F32)
    for h in range(H_A):
        ig = gates[:, h:h + 1]
        lf = _log_sigmoid(gates[:, H_A + h:H_A + h + 1])
        m_prev = m_in[rows, h:h + 1]
        inter = lf + m_prev
        m_t = jnp.maximum(inter, ig)
        w_inter = jnp.exp(inter - m_t)
        e_d = jnp.exp(ig - m_t)
        q = proj[rows, h * DH_A:(h + 1) * DH_A]
        k = proj[rows, W_A + h * DH_A:W_A + (h + 1) * DH_A]
        v = proj[rows, P_V + h * DH_A:P_V + (h + 1) * DH_A]
        n_old = n_in[h]
        s = jnp.sum(q * k, axis=-1, keepdims=True) * e_d
        vw = v * e_d
        vw_t = jnp.concatenate([vw, jnp.zeros((DH_A - bt, DH_A), F32)], axis=0).T
        qb = q.astype(BF16)
        qc = jnp.zeros((bt, DH_A), F32)
        for t in range(bt):
            c_old = c_in[t, h]
            r = _dot_nt(qb, c_old.astype(BF16))
            qc = jnp.where(sub == t, r, qc)
            c_out[t, h] = w_inter[t:t + 1, :] * c_old + vw_t[:, t:t + 1] * k[t:t + 1, :]
        num = w_inter * qc + s * v
        nq = w_inter * jnp.sum(q * n_old, axis=-1, keepdims=True) + s
        hh = num / jnp.maximum(jnp.abs(nq), jnp.exp(-m_t))
        n_out[h] = w_inter * n_old + e_d * k
        m_new_all = jnp.where(lane == h, m_t, m_new_all)
        ha = _rms(hh, hng_ref[:, h * DH_A:(h + 1) * DH_A])
        o = proj[rows, P_O + h * DH_A:P_O + (h + 1) * DH_A]
        hcat[rows, h * DH_A:(h + 1) * DH_A] = ha * jax.nn.sigmoid(o)
    m_out[rows, :] = m_new_all

    @pl.when(i == ni - 1)
    def _():
        out_ref[...] = x_ref[...] + jnp.dot(hcat[...].astype(BF16), wout_ref[...],
                                            preferred_element_type=F32)


def _mix_sample(h_in, h_out, lw, sconv_t, c0, n0_t, m0_pad, n_s, layer):
    bt = BT_SAMPLE
    ni = n_s // bt
    const = lambda i: (0, 0)
    out_blk = h_in.shape[0] // n_s - 1
    kern = functools.partial(_mix_sample_kernel, bt=bt)
    return pl.pallas_call(
        kern,
        name="mix_sample",
        grid=(ni,),
        in_specs=[
            pl.BlockSpec((n_s, D_MODEL), lambda i: (out_blk, 0)),
            pl.BlockSpec(memory_space=pl.ANY),
            pl.BlockSpec((1, D_MODEL), const),
            pl.BlockSpec((D_MODEL, P_W), const),
            pl.BlockSpec((CONV_W, QK_W), const),
            pl.BlockSpec((1, QK_W), const),
            pl.BlockSpec((1, LANES), const),
            pl.BlockSpec((1, W_A), const),
            pl.BlockSpec((1, W_B), const),
            pl.BlockSpec((1, W_B), const),
            pl.BlockSpec((1, W_B), const),
            pl.BlockSpec((D_MODEL, D_MODEL), const),
            pl.BlockSpec((None, CONV_W - 1, n_s, QK_W), lambda i: (layer, 0, 0, 0)),
            pl.BlockSpec((None, bt, H_A, DH_A, DH_A), lambda i: (layer, i, 0, 0, 0)),
            pl.BlockSpec((None, H_A, bt, DH_A), lambda i: (layer, 0, i, 0)),
            pl.BlockSpec((None, n_s, LANES), lambda i: (layer, 0, 0)),
        ],
        out_specs=[
            pl.BlockSpec((n_s, D_MODEL), lambda i: (out_blk, 0)),
            pl.BlockSpec((bt, H_A, DH_A, DH_A), lambda i: (i, 0, 0, 0)),
            pl.BlockSpec((H_A, bt, DH_A), lambda i: (0, i, 0)),
            pl.BlockSpec((n_s, LANES), const),
            pl.BlockSpec((CONV_W - 1, n_s, QK_W), lambda i: (0, 0, 0)),
            pl.BlockSpec((n_s, W_B), const),
        ],
        out_shape=[
            jax.ShapeDtypeStruct(h_out.shape, F32),
            jax.ShapeDtypeStruct(c0.shape[1:], F32),
            jax.ShapeDtypeStruct(n0_t.shape[1:], F32),
            jax.ShapeDtypeStruct((n_s, LANES), F32),
            jax.ShapeDtypeStruct((CONV_W - 1, n_s, QK_W), F32),
            jax.ShapeDtypeStruct((n_s, W_B), F32),
        ],
        scratch_shapes=[
            pltpu.VMEM((n_s, P_W), F32),
            pltpu.VMEM((n_s, D_MODEL), F32),
        ],
        input_output_aliases={1: 0},
        compiler_params=_cparams("arbitrary"),
    )(h_in, h_out, lw["g_mix"], lw["w_in"], lw["conv_w"], lw["conv_b"], lw["gate_b"], lw["hn_g"], lw["vn_g"],
      lw["w00_row"], lw["bs0_row"], lw["w_out"], sconv_t, c0, n0_t, m0_pad)


def _router_kernel(h_ref, g_ref, whi_ref, wlo_ref, b_ref, route_ref, route_t_ref, tcnt_ref, *, tm):
    hn = _rms(h_ref[...], g_ref[...])
    hi = hn.astype(BF16)
    lo = (hn - hi.astype(F32)).astype(BF16)
    logits = (jnp.dot(hi, whi_ref[...], preferred_element_type=F32)
              + jnp.dot(lo, whi_ref[...], preferred_element_type=F32)
              + jnp.dot(hi, wlo_ref[...], preferred_element_type=F32)) + b_ref[...]

    lane_i = lax.broadcasted_iota(jnp.int32, (tm, LANES), 1)
    lane = lane_i.astype(F32)
    neg = -jnp.inf
    is_g = lane_i < N_GROUPS
    gl = jnp.where(is_g, logits, neg)
    gmax = jnp.max(gl, axis=-1, keepdims=True)
    g_sel = jnp.min(jnp.where(gl == gmax, lane, float(LANES)), axis=-1, keepdims=True)
    g_w = 1.0 / jnp.sum(jnp.where(is_g, jnp.exp(logits - gmax), 0.0), axis=-1, keepdims=True)

    e_lane = lane_i - N_GROUPS
    lane_grp = (e_lane >> 3).astype(F32)
    in_grp = (e_lane >= 0) & (e_lane < N_EXPERTS) & (lane_grp == g_sel)
    el = jnp.where(in_grp, logits, neg)
    v1 = jnp.max(el, axis=-1, keepdims=True)
    i1 = jnp.min(jnp.where(el == v1, lane, float(LANES)), axis=-1, keepdims=True)
    el2 = jnp.where(lane == i1, neg, el)
    v2 = jnp.max(el2, axis=-1, keepdims=True)
    i2 = jnp.min(jnp.where(el2 == v2, lane, float(LANES)), axis=-1, keepdims=True)
    e21 = jnp.exp(v2 - v1)
    den = 1.0 + e21
    w1 = g_w * (1.0 / den)
    w2 = g_w * (e21 / den)

    onehot = jnp.where((lane == i1) | (lane == i2), 1.0, 0.0)
    r_i = lax.broadcasted_iota(jnp.int32, (tm, tm), 0)
    c_i = lax.broadcasted_iota(jnp.int32, (tm, tm), 1)
    strict = jnp.where(r_i > c_i, 1.0, 0.0).astype(BF16)
    before = jnp.dot(strict, onehot.astype(BF16), preferred_element_type=F32)
    cnt = jnp.sum(onehot, axis=0, keepdims=True)
    cnt8 = jnp.floor((cnt + (SUBLANES - 1.0)) * (1.0 / SUBLANES)) * SUBLANES
    l_r = lax.broadcasted_iota(jnp.int32, (LANES, LANES), 0)
    l_c = lax.broadcasted_iota(jnp.int32, (LANES, LANES), 1)
    lanes_before = jnp.where(l_r < l_c, 1.0, 0.0).astype(BF16)
    start = _split_dot(jnp.broadcast_to(cnt8, (2 * SUBLANES, LANES)), lanes_before)[0:1, :]
    slot = start + before
    pos1 = jnp.sum(jnp.where(lane == i1, slot, 0.0), axis=-1, keepdims=True)
    pos2 = jnp.sum(jnp.where(lane == i2, slot, 0.0), axis=-1, keepdims=True)
    tcnt_ref[...] = jnp.broadcast_to(cnt8, tcnt_ref.shape)

    out = jnp.zeros((tm, LANES), F32)
    for idx, val in enumerate((pos1, pos2, w1, w2)):
        out = jnp.where(lane_i == idx, val, out)
    route_ref[...] = out
    route_t_ref[...] = out.T[0:SUBLANES, :]


def _router(h1, lw):
    t_total = h1.shape[0]
    tm = _token_tile(t_total)
    n_tiles = t_total // tm
    const = lambda i: (0, 0)
    return pl.pallas_call(
        functools.partial(_router_kernel, tm=tm),
        name="router",
        grid=(n_tiles,),
        in_specs=[
            pl.BlockSpec((tm, D_MODEL), lambda i: (i, 0)),
            pl.BlockSpec((1, D_MODEL), const),
            pl.BlockSpec((D_MODEL, LANES), const),
            pl.BlockSpec((D_MODEL, LANES), const),
            pl.BlockSpec((1, LANES), const),
        ],
        out_specs=[
            pl.BlockSpec((tm, LANES), lambda i: (i, 0)),
            pl.BlockSpec((SUBLANES, tm), lambda i: (0, i)),
            pl.BlockSpec((None, SUBLANES, LANES), lambda i: (i, 0, 0)),
        ],
        out_shape=[
            jax.ShapeDtypeStruct((t_total, LANES), F32),
            jax.ShapeDtypeStruct((SUBLANES, t_total), F32),
            jax.ShapeDtypeStruct((n_tiles, SUBLANES, LANES), F32),
        ],
        compiler_params=_cparams("arbitrary"),
    )(h1, lw["g_ffn"], lw["wr_hi"], lw["wr_lo"], lw["br"])


def _tile_slots(tm):
    raw = tm * TOP_K + N_EXPERTS * (SUBLANES - 1)
    return -(-raw // LANES) * LANES


def _token_tile(t_total):
    tm = TM_TOK
    while t_total % tm:
        tm //= 2
    return tm


def _run_copies(tile, rdst_ref, rsrc_ref, runits_ref, make_copy):
    def per_expert(e, carry):
        k = tile * N_EXPERTS + e
        s0 = rsrc_ref[k]
        d0 = rdst_ref[k]

        def per_unit(u, c2):
            off = u * SUBLANES
            make_copy(pl.multiple_of(s0 + off, SUBLANES), pl.multiple_of(d0 + off, SUBLANES)).start()
            return c2

        lax.fori_loop(0, runits_ref[k], per_unit, 0)
        return carry

    lax.fori_loop(0, N_EXPERTS, per_expert, 0)


def _drain_copies(n_units, make_copy):
    def wait_one(u, carry):
        make_copy(0, 0).wait()
        return carry

    lax.fori_loop(0, n_units, wait_one, 0)


def _dispatch_kernel(rdst_ref, rsrc_ref, runits_ref, tunits_ref, h_ref, g_ref, rt_ref, xs_in, xs_out,
                     srt, sem, *, tm, slots):
    del xs_in
    i = pl.program_id(0)
    xn = _rms(h_ref[...], g_ref[...]).astype(BF16)
    slot_id = lax.broadcasted_iota(jnp.int32, (slots, tm), 0).astype(F32)
    sel = (slot_id == rt_ref[0:1, :]) | (slot_id == rt_ref[1:2, :])
    srt[...] = _pack_rows(jnp.dot(jnp.where(sel, 1.0, 0.0).astype(BF16), xn, preferred_element_type=F32))

    def make_copy(tile_row, sorted_row):
        return pltpu.make_async_copy(srt.at[pl.ds(tile_row, SUBLANES)],
                                     xs_out.at[pl.ds(sorted_row, SUBLANES)], sem)

    _run_copies(i, rdst_ref, rsrc_ref, runits_ref, make_copy)
    _drain_copies(tunits_ref[i], make_copy)


def _dispatch(tables, h1, route_t, lw, xs):
    t_total = h1.shape[0]
    tm = _token_tile(t_total)
    slots = _tile_slots(tm)
    return pl.pallas_call(
        functools.partial(_dispatch_kernel, tm=tm, slots=slots),
        name="dispatch",
        grid_spec=pltpu.PrefetchScalarGridSpec(
            num_scalar_prefetch=4,
            grid=(t_total // tm,),
            in_specs=[
                pl.BlockSpec((tm, D_MODEL), lambda i, *_: (i, 0)),
                pl.BlockSpec((1, D_MODEL), lambda i, *_: (0, 0)),
                pl.BlockSpec((SUBLANES, tm), lambda i, *_: (0, i)),
                pl.BlockSpec(memory_space=pl.ANY),
            ],
            out_specs=pl.BlockSpec(memory_space=pl.ANY),
            scratch_shapes=[pltpu.VMEM((slots, HALF_D), U32), pltpu.SemaphoreType.DMA(())],
        ),
        out_shape=jax.ShapeDtypeStruct(xs.shape, U32),
        input_output_aliases={7: 0},
        compiler_params=_cparams("arbitrary"),
    )(*tables, h1, lw["g_ffn"], route_t, xs)


def _ffn_kernel(be_ref, nu_ref, x_ref, w1_ref, w3_ref, w2_ref, y_ref, w1b, w3b, w2b):
    i = pl.program_id(0)
    prev = be_ref[jnp.maximum(i - 1, 0)]
    fresh = (i == 0) | (be_ref[i] != prev)

    @pl.when(fresh)
    def _():
        w1b[...] = w1_ref[...].astype(BF16)
        w3b[...] = w3_ref[...].astype(BF16)
        w2b[...] = w2_ref[...].astype(BF16)

    @pl.when(i < nu_ref[0])
    def _():
        x_lo, x_hi = _unpack_rows(x_ref[...])
        a = (jnp.dot(x_lo, w1b[0:HALF_D, :], preferred_element_type=F32)
             + jnp.dot(x_hi, w1b[HALF_D:, :], preferred_element_type=F32))
        b = (jnp.dot(x_lo, w3b[0:HALF_D, :], preferred_element_type=F32)
             + jnp.dot(x_hi, w3b[HALF_D:, :], preferred_element_type=F32))
        hmid = (a * jax.nn.sigmoid(a) * b).astype(BF16)
        y = jnp.dot(hmid, w2b[...], preferred_element_type=F32)
        y_ref[...] = _pack_rows(y.astype(BF16).astype(F32))


def _ffn(block_expert, n_used, xs, w1, w3, w2, layer):
    n_rows = xs.shape[0]
    nb = n_rows // FFN_BLK

    def row_map(i, be, nu):
        return (jnp.minimum(i, nu[0] - 1), 0)

    def w_map(i, be, nu):
        return (layer, be[i], 0, 0)

    return pl.pallas_call(
        _ffn_kernel,
        name="expert_ffn",
        grid_spec=pltpu.PrefetchScalarGridSpec(
            num_scalar_prefetch=2,
            grid=(nb,),
            in_specs=[
                pl.BlockSpec((FFN_BLK, HALF_D), row_map),
                pl.BlockSpec((None, None, D_MODEL, D_FF), w_map),
                pl.BlockSpec((None, None, D_MODEL, D_FF), w_map),
                pl.BlockSpec((None, None, D_FF, D_MODEL), w_map),
            ],
            out_specs=pl.BlockSpec((FFN_BLK, HALF_D), row_map),
            scratch_shapes=[
                pltpu.VMEM((D_MODEL, D_FF), BF16),
                pltpu.VMEM((D_MODEL, D_FF), BF16),
                pltpu.VMEM((D_FF, D_MODEL), BF16),
            ],
        ),
        out_shape=jax.ShapeDtypeStruct((n_rows, HALF_D), U32),
        input_output_aliases={2: 0},
        compiler_params=_cparams("arbitrary"),
    )(block_expert, n_used, xs, w1, w3, w2)


def _ple_kernel(rdst_ref, rsrc_ref, runits_ref, tunits_ref, h_ref, route_ref, p_ref, wpg_ref, wple_ref,
                pg_ref, fg_ref, ys_ref, *rest, tm, slots, n_s, final):
    if final:
        yp_ref, ysm_ref, ysrt, sem = rest
    else:
        out_ref, ysrt, sem = rest
    i = pl.program_id(0)

    @pl.when(i == 0)
    def _():
        ysrt[...] = jnp.zeros_like(ysrt)

    def make_copy(tile_row, sorted_row):
        return pltpu.make_async_copy(ys_ref.at[pl.ds(sorted_row, SUBLANES)],
                                     ysrt.at[pl.ds(tile_row, SUBLANES)], sem)

    _run_copies(i, rdst_ref, rsrc_ref, runits_ref, make_copy)
    pe = _rms(jnp.dot(p_ref[...].astype(BF16), wple_ref[...], preferred_element_type=F32), pg_ref[...])
    _drain_copies(tunits_ref[i], make_copy)

    route = route_ref[...]
    y_lo, y_hi = _unpack_rows(ysrt[...])
    slot_id = lax.broadcasted_iota(jnp.int32, (tm, slots), 1).astype(F32)
    moe = None
    for kk in range(TOP_K):
        pick = jnp.where(slot_id == route[:, kk:kk + 1], 1.0, 0.0).astype(BF16)
        rows = jnp.concatenate([jnp.dot(pick, y_lo, preferred_element_type=F32),
                                jnp.dot(pick, y_hi, preferred_element_type=F32)], axis=1)
        term = route[:, TOP_K + kk:TOP_K + kk + 1] * rows
        moe = term if moe is None else moe + term
    h2 = h_ref[...] + moe
    gate = jax.nn.sigmoid(jnp.dot(h2.astype(BF16), wpg_ref[...], preferred_element_type=F32))
    out = h2 + gate * pe
    if final:
        fin = _rms(out, fg_ref[...])
        yp_ref[...] = fin

        @pl.when(i == pl.num_programs(0) - 1)
        def _():
            ysm_ref[...] = fin[tm - n_s:tm, :]
    else:
        out_ref[...] = out


def _ple(tables, h1, route, p_all, lw, final_g, ys, n_s, layer, final):
    t_total = h1.shape[0]
    tm = _token_tile(t_total)
    slots = _tile_slots(tm)
    assert n_s <= tm
    const = lambda i, *_: (0, 0)
    tok = lambda i, *_: (i, 0)
    if final:
        out_specs = [pl.BlockSpec((tm, D_MODEL), tok), pl.BlockSpec((n_s, D_MODEL), const)]
        out_shape = [jax.ShapeDtypeStruct((t_total - n_s, D_MODEL), F32),
                     jax.ShapeDtypeStruct((n_s, D_MODEL), F32)]
        aliases = {}
    else:
        out_specs = [pl.BlockSpec((tm, D_MODEL), tok)]
        out_shape = [jax.ShapeDtypeStruct((t_total, D_MODEL), F32)]
        aliases = {4: 0}
    return pl.pallas_call(
        functools.partial(_ple_kernel, tm=tm, slots=slots, n_s=n_s, final=final),
        name="combine_ple",
        grid_spec=pltpu.PrefetchScalarGridSpec(
            num_scalar_prefetch=4,
            grid=(t_total // tm,),
            in_specs=[
                pl.BlockSpec((tm, D_MODEL), tok),
                pl.BlockSpec((tm, LANES), tok),
                pl.BlockSpec((None, tm, PLE_DIM), lambda i, *_: (layer, i, 0)),
                pl.BlockSpec((D_MODEL, D_MODEL), const),
                pl.BlockSpec((PLE_DIM, D_MODEL), const),
                pl.BlockSpec((1, D_MODEL), const),
                pl.BlockSpec((1, D_MODEL), const),
                pl.BlockSpec(memory_space=pl.ANY),
            ],
            out_specs=out_specs,
            scratch_shapes=[pltpu.VMEM((slots, HALF_D), U32), pltpu.SemaphoreType.DMA(())],
        ),
        out_shape=out_shape,
        input_output_aliases=aliases,
        compiler_params=_cparams("arbitrary"),
    )(*tables, h1, route, p_all, lw["w_pg"], lw["w_ple"], lw["ple_g"], final_g, ys)


def _layer_weights(l, norm_mix_g, w_in, conv_w, conv_b, b_igate, b_fgate, hnorm_g, vnorm_g, w_s, b_s, w_out,
                   norm_ffn_g, router_g_w, router_g_b, router_e_w, router_e_b, w_ple, ple_norm_g, w_ple_gate):
    wi = w_in[l]
    a0 = QK_W + 2 * W_A
    gates_w = jnp.pad(wi[:, a0:a0 + 2 * H_A], ((0, 0), (0, LANES - 2 * H_A)))
    w_packed = jnp.concatenate(
        [wi[:, :a0], wi[:, a0 + 2 * H_A:], gates_w], axis=1).astype(BF16)
    gate_b = jnp.pad(jnp.concatenate([b_igate[l], b_fgate[l]]), (0, LANES - 2 * H_A))[None, :]
    wr = jnp.pad(jnp.concatenate([router_g_w[l], router_e_w[l]], axis=1),
                 ((0, 0), (0, LANES - N_GROUPS - N_EXPERTS)))
    wr_hi = wr.astype(BF16)
    wr_lo = (wr - wr_hi.astype(F32)).astype(BF16)
    br = jnp.pad(jnp.concatenate([router_g_b[l], router_e_b[l]]), (0, LANES - N_GROUPS - N_EXPERTS))[None, :]
    return {
        "g_mix": norm_mix_g[l][None, :],
        "w_in": w_packed,
        "conv_w": conv_w[l],
        "conv_b": conv_b[l][None, :],
        "gate_b": gate_b,
        "hn_g": hnorm_g[l][None, :],
        "vn_g": vnorm_g[l][None, :],
        "w_s": w_s[l],
        "b_s_col": jnp.pad(b_s[l].T, ((0, 0), (0, LANES - G_B))),
        "w00_row": jnp.repeat(w_s[l][:, 0, 0], DG_B)[None, :],
        "bs0_row": jnp.repeat(b_s[l][:, 0], DG_B)[None, :],
        "w_out": w_out[l].astype(BF16),
        "g_ffn": norm_ffn_g[l][None, :],
        "wr_hi": wr_hi,
        "wr_lo": wr_lo,
        "br": br,
        "w_pg": w_ple_gate[l].astype(BF16),
        "w_ple": w_ple[l].astype(BF16),
        "ple_g": ple_norm_g[l][None, :],
    }


def _routing_tables(tcnt, n_blocks):
    c8 = tcnt[:, 0, N_GROUPS:N_GROUPS + N_EXPERTS].astype(jnp.int32)
    blocks = (jnp.sum(c8, axis=0) + FFN_BLK - 1) // FFN_BLK
    blk_end = jnp.cumsum(blocks)
    pstart = (blk_end - blocks) * FFN_BLK
    run_dst = pstart[None, :] + jnp.cumsum(c8, axis=0) - c8
    run_src = jnp.cumsum(c8, axis=1) - c8
    units = c8 // SUBLANES
    n_used = blk_end[-1:]
    blk = jnp.minimum(jnp.arange(n_blocks, dtype=jnp.int32), n_used[0] - 1)
    block_expert = jnp.minimum(jnp.sum((blk_end[None, :] <= blk[:, None]).astype(jnp.int32), axis=1),
                               N_EXPERTS - 1)
    tables = (run_dst.reshape(-1), run_src.reshape(-1), units.reshape(-1), jnp.sum(units, axis=1))
    return tables, block_expert, n_used


def kernel(x_prompt, x_sample, state_C, state_n, state_m, state_conv, p_prompt, p_sample, norm_mix_g, w_in, conv_w, conv_b, b_igate, b_fgate, hnorm_g, vnorm_g, w_s, b_s, w_out, norm_ffn_g, router_g_w, router_g_b, router_e_w, router_e_b, w1, w3, w2, w_ple, ple_norm_g, w_ple_gate, final_norm_g):
    n_batch, seq, _ = x_prompt.shape
    n_s = x_sample.shape[0]
    depth = w_in.shape[0]
    t_p = n_batch * seq
    t_total = t_p + n_s
    assert seq % CHUNK == 0 and t_p % n_s == 0 and n_s % BT_SAMPLE == 0

    n_tiles = t_total // _token_tile(t_total)
    max_rows = t_total * TOP_K + N_EXPERTS * ((SUBLANES - 1) * n_tiles + FFN_BLK - 1)
    n_blocks = max_rows // FFN_BLK
    xs = jnp.zeros((n_blocks * FFN_BLK, HALF_D), U32)

    p_all = jnp.concatenate([p_prompt.reshape(depth, t_p, PLE_DIM), p_sample.reshape(depth, n_s, PLE_DIM)], axis=1)
    sconv_t = jnp.transpose(state_conv, (0, 2, 1, 3))
    n0_t = jnp.transpose(state_n, (0, 2, 1, 3))
    m0_pad = jnp.pad(state_m, ((0, 0), (0, 0), (0, LANES - H_A)))
    final_g = final_norm_g[None, :]

    hbuf = jnp.concatenate([x_prompt.reshape(t_p, D_MODEL), x_sample.reshape(n_s, D_MODEL)], axis=0)
    spare = jnp.zeros_like(hbuf)
    outs = {k: [] for k in ("Cp", "np", "mp", "cbp", "Cs", "ns", "ms", "cbs", "vs")}
    res = None
    for l in range(depth):
        lw = _layer_weights(l, norm_mix_g, w_in, conv_w, conv_b, b_igate, b_fgate, hnorm_g, vnorm_g, w_s, b_s,
                            w_out, norm_ffn_g, router_g_w, router_g_b, router_e_w, router_e_b, w_ple,
                            ple_norm_g, w_ple_gate)
        h_mix, c_p, n_p, m_p, cb_p = _mix_prompt(hbuf, spare, n_batch, seq, lw)
        h_mix, c_s, n_s_t, m_s, cb_s, v_s = _mix_sample(hbuf, h_mix, lw, sconv_t, state_C, n0_t, m0_pad, n_s, l)
        hbuf, spare = h_mix, hbuf
        route, route_t, tcnt = _router(hbuf, lw)
        tables, block_expert, n_used = _routing_tables(tcnt, n_blocks)
        xs = _dispatch(tables, hbuf, route_t, lw, xs)
        xs = _ffn(block_expert, n_used, xs, w1, w3, w2, l)
        final = l == depth - 1
        res = _ple(tables, hbuf, route, p_all, lw, final_g, xs, n_s, l, final)
        hbuf = res[0]
        outs["Cp"].append(c_p)
        outs["np"].append(n_p)
        outs["mp"].append(m_p[:, :, 0])
        outs["cbp"].append(cb_p)
        outs["Cs"].append(c_s)
        outs["ns"].append(jnp.transpose(n_s_t, (1, 0, 2)))
        outs["ms"].append(m_s[:, 0:H_A])
        outs["cbs"].append(jnp.transpose(cb_s, (1, 0, 2)))
        outs["vs"].append(v_s[:, None, :])

    y_prompt = res[0].reshape(n_batch, seq, D_MODEL)
    y_sample = res[1].reshape(n_s, 1, D_MODEL)
    st = lambda k: jnp.stack(outs[k])
    return (y_prompt, y_sample, st("Cp"), st("np"), st("mp"), st("cbp"),
            st("Cs"), st("ns"), st("ms"), st("cbs"), st("vs"))
```

```python
import functools

import jax
import jax.numpy as jnp
from jax import lax
from jax.experimental import pallas as pl
from jax.experimental.pallas import tpu as pltpu

F32 = jnp.float32
BF16 = jnp.bfloat16
U32 = jnp.uint32

D_MODEL = 1024
W_A = 512
H_A = 4
DH_A = 128
W_B = 512
G_B = 4
DG_B = 128
CHUNK = 128
CONV_W = 4
QK_W = 2 * W_A
N_GROUPS = 4
EXPERTS_PER_GROUP = 8
N_EXPERTS = N_GROUPS * EXPERTS_PER_GROUP
TOP_K = 2
D_FF = 512
PLE_DIM = 256
EPS = 1e-6
HALF_D = D_MODEL // 2

LANES = 128
SUBLANES = 8
VMEM_LIMIT_BYTES = 56 * 1024 * 1024

P_QK = 0
P_V = QK_W
P_O = P_V + W_A
P_U = P_O + W_A
P_VB = P_U + W_B
P_G = P_VB + W_B
P_W = P_G + LANES

TL_MIX = 512
BT_SAMPLE = 8
TM_TOK = 384
FFN_BLK = 256


def _cparams(*sem):
    return pltpu.CompilerParams(dimension_semantics=sem, vmem_limit_bytes=VMEM_LIMIT_BYTES)


def _rms(x, g):
    return x * lax.rsqrt(jnp.mean(x * x, axis=-1, keepdims=True) + EPS) * g


def _log_sigmoid(x):
    return -(jnp.maximum(-x, 0.0) + jnp.log1p(jnp.exp(-jnp.abs(x))))


def _split_dot(a, b_bf16):
    hi = a.astype(BF16)
    lo = (a - hi.astype(F32)).astype(BF16)
    return (jnp.dot(hi, b_bf16, preferred_element_type=F32)
            + jnp.dot(lo, b_bf16, preferred_element_type=F32))


def _dot_nt(a, b):
    return lax.dot_general(a, b, (((1,), (1,)), ((), ())), preferred_element_type=F32)


def _pack_rows(x):
    lo = lax.bitcast_convert_type(x[:, :HALF_D], U32) >> 16
    hi = lax.bitcast_convert_type(x[:, HALF_D:], U32) & jnp.uint32(0xFFFF0000)
    return lo | hi


def _unpack_rows(u):
    lo = lax.bitcast_convert_type(u << 16, F32)
    hi = lax.bitcast_convert_type(u & jnp.uint32(0xFFFF0000), F32)
    return lo.astype(BF16), hi.astype(BF16)


def _mix_prompt_kernel(x_ref, xnext_ref, hprev_ref, gmix_ref, win_ref, cw_ref, cb_ref, gb_ref, hng_ref,
                       vng_ref, ws_ref, bs_ref, wout_ref,
                       out_ref, c_out, n_out, m_out, conv_out,
                       proj_a, proj_b, xpad, hcat, c_s, n_s, m_s, *, tl):
    del hprev_ref
    b = pl.program_id(0)
    j = pl.program_id(1)
    nj = pl.num_programs(1)
    n_chunks = tl // CHUNK
    bounds = [(P_W // LANES * c // n_chunks) * LANES for c in range(n_chunks + 1)]
    col_groups = [slice(bounds[c], bounds[c + 1]) for c in range(n_chunks)]

    @pl.when(j == 0)
    def _():
        c_s[...] = jnp.zeros_like(c_s)
        n_s[...] = jnp.zeros_like(n_s)
        m_s[...] = jnp.zeros_like(m_s)
        xpad[0:SUBLANES, :] = jnp.zeros((SUBLANES, QK_W), F32)

    @pl.when((b == 0) & (j == 0))
    def _():
        xn0 = _rms(x_ref[0:tl, :], gmix_ref[...]).astype(BF16)
        proj_a[...] = jnp.dot(xn0, win_ref[...], preferred_element_type=F32)

    row = lax.broadcasted_iota(jnp.int32, (CHUNK, CHUNK), 0)
    col = lax.broadcasted_iota(jnp.int32, (CHUNK, CHUNK), 1)
    causal = row >= col
    tril = jnp.where(causal, 1.0, 0.0).astype(BF16)
    triu = jnp.where(row <= col, 1.0, 0.0).astype(BF16)
    lane = lax.broadcasted_iota(jnp.int32, (CHUNK, LANES), 1)
    wm = [jnp.where(causal, ws_ref[g], 0.0).astype(BF16) for g in range(G_B)]

    def conv_step(proj):
        pre = proj[:, P_QK:P_QK + QK_W]
        xpad[SUBLANES:SUBLANES + tl, :] = pre
        qk = cb_ref[...] + cw_ref[CONV_W - 1:CONV_W, :] * pre
        for jj in range(1, CONV_W):
            qk = qk + cw_ref[CONV_W - 1 - jj:CONV_W - jj, :] * xpad[SUBLANES - jj:SUBLANES - jj + tl, :]
        xpad[SUBLANES - (CONV_W - 1):SUBLANES, :] = pre[tl - (CONV_W - 1):tl, :]
        qk = qk * jax.nn.sigmoid(qk)
        proj[:, 0:W_A] = qk[:, 0:W_A]
        proj[:, W_A:QK_W] = qk[:, W_A:QK_W] * (DH_A ** -0.5)

    def chunk_step(proj, c):
        rows = slice(c * CHUNK, (c + 1) * CHUNK)
        gl = proj[rows, P_G:P_G + LANES] + gb_ref[...]
        gl = jnp.where(lane >= H_A, _log_sigmoid(gl), gl)
        glt = gl.T
        b_col_all = _split_dot_left(tril, gl)
        b_row_all = _split_dot(glt[0:2 * SUBLANES, :], triu)

        for h in range(H_A):
            q32 = proj[rows, h * DH_A:(h + 1) * DH_A]
            k32 = proj[rows, W_A + h * DH_A:W_A + (h + 1) * DH_A]
            v32 = proj[rows, P_V + h * DH_A:P_V + (h + 1) * DH_A]
            q = q32.astype(BF16)
            k = k32.astype(BF16)
            ig_c = gl[:, h:h + 1]
            ig_r = glt[h:h + 1, :]
            b_c = b_col_all[:, H_A + h:H_A + h + 1]
            b_r = b_row_all[H_A + h:H_A + h + 1, :]
            c_old = c_s[h]
            n_old = n_s[h:h + 1, :]
            m_prev = m_s[h:h + 1, 0:1]

            d_log = jnp.where(causal, b_c - b_r + ig_r, -jnp.inf)
            inter = b_c + m_prev
            m_t = jnp.maximum(inter, jnp.max(d_log, axis=-1, keepdims=True))
            s = _dot_nt(q, k) * jnp.exp(d_log - m_t)
            w_inter = jnp.exp(inter - m_t)
            num = (w_inter * _dot_nt(q, c_old.astype(BF16))
                   + jnp.dot(s.astype(BF16), v32.astype(BF16), preferred_element_type=F32))
            nq = (w_inter * jnp.sum(q32 * n_old, axis=-1, keepdims=True)
                  + jnp.sum(s, axis=-1, keepdims=True))
            hh = num / jnp.maximum(jnp.abs(nq), jnp.exp(-m_t))

            m_new = m_t[CHUNK - 1:CHUNK, :]
            b_last = b_c[CHUNK - 1:CHUNK, :]
            w_state = jnp.exp(b_last - b_c + ig_c - m_new)
            decay = jnp.exp(b_last + m_prev - m_new)
            vw = (v32 * w_state).astype(BF16)
            c_s[h] = decay * c_old + lax.dot_general(
                vw, k, (((0,), (0,)), ((), ())), preferred_element_type=F32)
            n_s[h:h + 1, :] = decay * n_old + jnp.sum(w_state * k32, axis=0, keepdims=True)
            m_s[h:h + 1, :] = jnp.broadcast_to(m_new, (1, LANES))

            ha = _rms(hh, hng_ref[:, h * DH_A:(h + 1) * DH_A])
            o = proj[rows, P_O + h * DH_A:P_O + (h + 1) * DH_A]
            hcat[rows, h * DH_A:(h + 1) * DH_A] = (ha * jax.nn.sigmoid(o)).astype(BF16)

        for g in range(G_B):
            vb = proj[rows, P_VB + g * DG_B:P_VB + (g + 1) * DG_B]
            u = proj[rows, P_U + g * DG_B:P_U + (g + 1) * DG_B]
            vn = _rms(jax.nn.gelu(vb), vng_ref[:, g * DG_B:(g + 1) * DG_B])
            z = jnp.dot(wm[g], vn.astype(BF16), preferred_element_type=F32) + bs_ref[:, g:g + 1]
            hcat[rows, W_A + g * DG_B:W_A + (g + 1) * DG_B] = (jax.nn.gelu(u) * z).astype(BF16)

    def tile_pass(row0, proj_cur, proj_nxt, xn_next):
        conv_step(proj_cur)
        for c in range(n_chunks):
            chunk_step(proj_cur, c)
            cs = col_groups[c]
            proj_nxt[:, cs] = jnp.dot(xn_next, win_ref[:, cs], preferred_element_type=F32)
        out_ref[row0:row0 + tl, :] = x_ref[row0:row0 + tl, :] + jnp.dot(
            hcat[...], wout_ref[...], preferred_element_type=F32)

    tile_pass(0, proj_a, proj_b, _rms(x_ref[tl:2 * tl, :], gmix_ref[...]).astype(BF16))
    tile_pass(tl, proj_b, proj_a, _rms(xnext_ref[...], gmix_ref[...]).astype(BF16))

    @pl.when(j == nj - 1)
    def _():
        conv_out[0] = xpad[SUBLANES - (CONV_W - 1):SUBLANES, :]
        c_out[0] = c_s[...]
        n_out[0] = n_s[0:H_A, :]
        m_out[0] = m_s[0:H_A, :]


def _split_dot_left(a_bf16, b):
    hi = b.astype(BF16)
    lo = (b - hi.astype(F32)).astype(BF16)
    return (jnp.dot(a_bf16, hi, preferred_element_type=F32)
            + jnp.dot(a_bf16, lo, preferred_element_type=F32))


def _mix_prompt(h_in, h_out, n_batch, seq, lw):
    tl = max(t for t in range(CHUNK, TL_MIX + 1, CHUNK) if seq % (2 * t) == 0)
    nj = seq // (2 * tl)
    last_tile = n_batch * seq // tl - 1
    const = lambda b, j: (0, 0)
    once = pl.Buffered(1)
    kern = functools.partial(_mix_prompt_kernel, tl=tl)
    return pl.pallas_call(
        kern,
        name="mix_prompt",
        grid=(n_batch, nj),
        in_specs=[
            pl.BlockSpec((2 * tl, D_MODEL), lambda b, j: (b * nj + j, 0)),
            pl.BlockSpec((tl, D_MODEL), lambda b, j: (jnp.minimum(2 * (b * nj + j) + 2, last_tile), 0)),
            pl.BlockSpec(memory_space=pl.ANY),
            pl.BlockSpec((1, D_MODEL), const),
            pl.BlockSpec((D_MODEL, P_W), const, pipeline_mode=once),
            pl.BlockSpec((CONV_W, QK_W), const),
            pl.BlockSpec((1, QK_W), const),
            pl.BlockSpec((1, LANES), const),
            pl.BlockSpec((1, W_A), const),
            pl.BlockSpec((1, W_B), const),
            pl.BlockSpec((G_B, CHUNK, CHUNK), lambda b, j: (0, 0, 0)),
            pl.BlockSpec((CHUNK, LANES), const),
            pl.BlockSpec((D_MODEL, D_MODEL), const, pipeline_mode=once),
        ],
        out_specs=[
            pl.BlockSpec((2 * tl, D_MODEL), lambda b, j: (b * nj + j, 0)),
            pl.BlockSpec((1, H_A, DH_A, DH_A), lambda b, j: (b, 0, 0, 0)),
            pl.BlockSpec((1, H_A, DH_A), lambda b, j: (b, 0, 0)),
            pl.BlockSpec((1, H_A, LANES), lambda b, j: (b, 0, 0)),
            pl.BlockSpec((1, CONV_W - 1, QK_W), lambda b, j: (b, 0, 0)),
        ],
        out_shape=[
            jax.ShapeDtypeStruct(h_out.shape, F32),
            jax.ShapeDtypeStruct((n_batch, H_A, DH_A, DH_A), F32),
            jax.ShapeDtypeStruct((n_batch, H_A, DH_A), F32),
            jax.ShapeDtypeStruct((n_batch, H_A, LANES), F32),
            jax.ShapeDtypeStruct((n_batch, CONV_W - 1, QK_W), F32),
        ],
        scratch_shapes=[
            pltpu.VMEM((tl, P_W), F32),
            pltpu.VMEM((tl, P_W), F32),
            pltpu.VMEM((SUBLANES + tl, QK_W), F32),
            pltpu.VMEM((tl, D_MODEL), BF16),
            pltpu.VMEM((H_A, DH_A, DH_A), F32),
            pltpu.VMEM((SUBLANES, LANES), F32),
            pltpu.VMEM((SUBLANES, LANES), F32),
        ],
        input_output_aliases={2: 0},
        compiler_params=_cparams("arbitrary", "arbitrary"),
    )(h_in, h_in, h_out, lw["g_mix"], lw["w_in"], lw["conv_w"], lw["conv_b"], lw["gate_b"], lw["hn_g"], lw["vn_g"],
      lw["w_s"], lw["b_s_col"], lw["w_out"])


def _mix_sample_kernel(x_ref, hprev_ref, cprev_ref, gmix_ref, win_ref, cw_ref, cb_ref, gb_ref, hng_ref, vng_ref,
                       w00_ref, bs0_ref, wout_ref, sconv_ref, c_in, n_in, m_in,
                       out_ref, c_out, n_out, m_out, conv_out, vrow_out,
                       proj, hcat, *, bt):
    del hprev_ref, cprev_ref
    i = pl.program_id(0)
    ni = pl.num_programs(0)

    @pl.when(i == 0)
    def _():
        x = x_ref[...]
        xn = _rms(x, gmix_ref[...]).astype(BF16)
        proj[...] = jnp.dot(xn, win_ref[...], preferred_element_type=F32)
        pre = proj[:, P_QK:P_QK + QK_W]
        qk = cb_ref[...] + cw_ref[CONV_W - 1:CONV_W, :] * pre
        for jj in range(CONV_W - 1):
            qk = qk + cw_ref[jj:jj + 1, :] * sconv_ref[jj]
        for jj in range(CONV_W - 2):
            conv_out[jj] = sconv_ref[jj + 1]
        conv_out[CONV_W - 2] = pre
        qk = qk * jax.nn.sigmoid(qk)
        proj[:, 0:W_A] = qk[:, 0:W_A]
        proj[:, W_A:QK_W] = qk[:, W_A:QK_W] * (DH_A ** -0.5)
        for g in range(G_B):
            sl = slice(g * DG_B, (g + 1) * DG_B)
            vb = proj[:, P_VB + g * DG_B:P_VB + (g + 1) * DG_B]
            u = proj[:, P_U + g * DG_B:P_U + (g + 1) * DG_B]
            vn = _rms(jax.nn.gelu(vb), vng_ref[:, sl])
            vrow_out[:, sl] = vn
            z = w00_ref[:, sl] * vn + bs0_ref[:, sl]
            hcat[:, W_A + g * DG_B:W_A + (g + 1) * DG_B] = jax.nn.gelu(u) * z

    rows = pl.ds(pl.multiple_of(i * bt, bt), bt)
    gates = proj[rows, P_G:P_G + LANES] + gb_ref[...]
    lane = lax.broadcasted_iota(jnp.int32, (bt, LANES), 1)
    sub = lax.broadcasted_iota(jnp.int32, (bt, DH_A), 0)
    m_new_all = jnp.zeros((bt, LANES), F32)
    for h in range(H_A):
        ig = gates[:, h:h + 1]
        lf = _log_sigmoid(gates[:, H_A + h:H_A + h + 1])
        m_prev = m_in[rows, h:h + 1]
        inter = lf + m_prev
        m_t = jnp.maximum(inter, ig)
        w_inter = jnp.exp(inter - m_t)
        e_d = jnp.exp(ig - m_t)
        q = proj[rows, h * DH_A:(h + 1) * DH_A]
        k = proj[rows, W_A + h * DH_A:W_A + (h + 1) * DH_A]
        v = proj[rows, P_V + h * DH_A:P_V + (h + 1) * DH_A]
        n_old = n_in[h]
        s = jnp.sum(q * k, axis=-1, keepdims=True) * e_d
        vw = v * e_d
        vw_t = jnp.concatenate([vw, jnp.zeros((DH_A - bt, DH_A), F32)], axis=0).T
        qb = q.astype(BF16)
        qc = jnp.zeros((bt, DH_A), F32)
        for t in range(bt):
            c_old = c_in[t, h]
            r = _dot_nt(qb, c_old.astype(BF16))
            qc = jnp.where(sub == t, r, qc)
            c_out[t, h] = w_inter[t:t + 1, :] * c_old + vw_t[:, t:t + 1] * k[t:t + 1, :]
        num = w_inter * qc + s * v
        nq = w_inter * jnp.sum(q * n_old, axis=-1, keepdims=True) + s
        hh = num / jnp.maximum(jnp.abs(nq), jnp.exp(-m_t))
        n_out[h] = w_inter * n_old + e_d * k
        m_new_all = jnp.where(lane == h, m_t, m_new_all)
        ha = _rms(hh, hng_ref[:, h * DH_A:(h + 1) * DH_A])
        o = proj[rows, P_O + h * DH_A:P_O + (h + 1) * DH_A]
        hcat[rows, h * DH_A:(h + 1) * DH_A] = ha * jax.nn.sigmoid(o)
    m_out[rows, :] = m_new_all

    @pl.when(i == ni - 1)
    def _():
        out_ref[...] = x_ref[...] + jnp.dot(hcat[...].astype(BF16), wout_ref[...],
                                            preferred_element_type=F32)


def _mix_sample(h_in, h_out, c_all, lw, sconv_t, c0, n0_t, m0_pad, n_s, layer):
    bt = BT_SAMPLE
    ni = n_s // bt
    const = lambda i: (0, 0)
    in_blk = h_in.shape[0] // n_s - 1
    out_blk = h_out.shape[0] // n_s - 1
    kern = functools.partial(_mix_sample_kernel, bt=bt)
    return pl.pallas_call(
        kern,
        name="mix_sample",
        grid=(ni,),
        in_specs=[
            pl.BlockSpec((n_s, D_MODEL), lambda i: (in_blk, 0)),
            pl.BlockSpec(memory_space=pl.ANY),
            pl.BlockSpec(memory_space=pl.ANY),
            pl.BlockSpec((1, D_MODEL), const),
            pl.BlockSpec((D_MODEL, P_W), const),
            pl.BlockSpec((CONV_W, QK_W), const),
            pl.BlockSpec((1, QK_W), const),
            pl.BlockSpec((1, LANES), const),
            pl.BlockSpec((1, W_A), const),
            pl.BlockSpec((1, W_B), const),
            pl.BlockSpec((1, W_B), const),
            pl.BlockSpec((1, W_B), const),
            pl.BlockSpec((D_MODEL, D_MODEL), const),
            pl.BlockSpec((None, CONV_W - 1, n_s, QK_W), lambda i: (layer, 0, 0, 0)),
            pl.BlockSpec((None, bt, H_A, DH_A, DH_A), lambda i: (layer, i, 0, 0, 0)),
            pl.BlockSpec((None, H_A, bt, DH_A), lambda i: (layer, 0, i, 0)),
            pl.BlockSpec((None, n_s, LANES), lambda i: (layer, 0, 0)),
        ],
        out_specs=[
            pl.BlockSpec((n_s, D_MODEL), lambda i: (out_blk, 0)),
            pl.BlockSpec((None, bt, H_A, DH_A, DH_A), lambda i: (layer, i, 0, 0, 0)),
            pl.BlockSpec((H_A, bt, DH_A), lambda i: (0, i, 0)),
            pl.BlockSpec((n_s, LANES), const),
            pl.BlockSpec((CONV_W - 1, n_s, QK_W), lambda i: (0, 0, 0)),
            pl.BlockSpec((n_s, W_B), const),
        ],
        out_shape=[
            jax.ShapeDtypeStruct(h_out.shape, F32),
            jax.ShapeDtypeStruct(c_all.shape, F32),
            jax.ShapeDtypeStruct(n0_t.shape[1:], F32),
            jax.ShapeDtypeStruct((n_s, LANES), F32),
            jax.ShapeDtypeStruct((CONV_W - 1, n_s, QK_W), F32),
            jax.ShapeDtypeStruct((n_s, W_B), F32),
        ],
        scratch_shapes=[
            pltpu.VMEM((n_s, P_W), F32),
            pltpu.VMEM((n_s, D_MODEL), F32),
        ],
        input_output_aliases={1: 0, 2: 1},
        compiler_params=_cparams("arbitrary"),
    )(h_in, h_out, c_all, lw["g_mix"], lw["w_in"], lw["conv_w"], lw["conv_b"], lw["gate_b"], lw["hn_g"], lw["vn_g"],
      lw["w00_row"], lw["bs0_row"], lw["w_out"], sconv_t, c0, n0_t, m0_pad)


def _router_kernel(h_ref, g_ref, whi_ref, wlo_ref, b_ref, route_ref, route_t_ref, tcnt_ref, *, tm):
    hn = _rms(h_ref[...], g_ref[...])
    hi = hn.astype(BF16)
    lo = (hn - hi.astype(F32)).astype(BF16)
    logits = (jnp.dot(hi, whi_ref[...], preferred_element_type=F32)
              + jnp.dot(lo, whi_ref[...], preferred_element_type=F32)
              + jnp.dot(hi, wlo_ref[...], preferred_element_type=F32)) + b_ref[...]

    lane_i = lax.broadcasted_iota(jnp.int32, (tm, LANES), 1)
    lane = lane_i.astype(F32)
    neg = -jnp.inf
    is_g = lane_i < N_GROUPS
    gl = jnp.where(is_g, logits, neg)
    gmax = jnp.max(gl, axis=-1, keepdims=True)
    g_sel = jnp.min(jnp.where(gl == gmax, lane, float(LANES)), axis=-1, keepdims=True)
    g_w = 1.0 / jnp.sum(jnp.where(is_g, jnp.exp(logits - gmax), 0.0), axis=-1, keepdims=True)

    e_lane = lane_i - N_GROUPS
    lane_grp = (e_lane >> 3).astype(F32)
    in_grp = (e_lane >= 0) & (e_lane < N_EXPERTS) & (lane_grp == g_sel)
    el = jnp.where(in_grp, logits, neg)
    v1 = jnp.max(el, axis=-1, keepdims=True)
    i1 = jnp.min(jnp.where(el == v1, lane, float(LANES)), axis=-1, keepdims=True)
    el2 = jnp.where(lane == i1, neg, el)
    v2 = jnp.max(el2, axis=-1, keepdims=True)
    i2 = jnp.min(jnp.where(el2 == v2, lane, float(LANES)), axis=-1, keepdims=True)
    e21 = jnp.exp(v2 - v1)
    den = 1.0 + e21
    w1 = g_w * (1.0 / den)
    w2 = g_w * (e21 / den)

    onehot = jnp.where((lane == i1) | (lane == i2), 1.0, 0.0)
    r_i = lax.broadcasted_iota(jnp.int32, (tm, tm), 0)
    c_i = lax.broadcasted_iota(jnp.int32, (tm, tm), 1)
    strict = jnp.where(r_i > c_i, 1.0, 0.0).astype(BF16)
    before = jnp.dot(strict, onehot.astype(BF16), preferred_element_type=F32)
    cnt = jnp.sum(onehot, axis=0, keepdims=True)
    cnt8 = jnp.floor((cnt + (SUBLANES - 1.0)) * (1.0 / SUBLANES)) * SUBLANES
    l_r = lax.broadcasted_iota(jnp.int32, (LANES, LANES), 0)
    l_c = lax.broadcasted_iota(jnp.int32, (LANES, LANES), 1)
    lanes_before = jnp.where(l_r < l_c, 1.0, 0.0).astype(BF16)
    start = _split_dot(jnp.broadcast_to(cnt8, (2 * SUBLANES, LANES)), lanes_before)[0:1, :]
    slot = start + before
    pos1 = jnp.sum(jnp.where(lane == i1, slot, 0.0), axis=-1, keepdims=True)
    pos2 = jnp.sum(jnp.where(lane == i2, slot, 0.0), axis=-1, keepdims=True)
    tcnt_ref[...] = jnp.broadcast_to(cnt8, tcnt_ref.shape)

    out = jnp.zeros((tm, LANES), F32)
    for idx, val in enumerate((pos1, pos2, w1, w2)):
        out = jnp.where(lane_i == idx, val, out)
    route_ref[...] = out
    route_t_ref[...] = out.T[0:SUBLANES, :]


def _router(h1, lw):
    t_total = h1.shape[0]
    tm = _token_tile(t_total)
    n_tiles = t_total // tm
    const = lambda i: (0, 0)
    return pl.pallas_call(
        functools.partial(_router_kernel, tm=tm),
        name="router",
        grid=(n_tiles,),
        in_specs=[
            pl.BlockSpec((tm, D_MODEL), lambda i: (i, 0)),
            pl.BlockSpec((1, D_MODEL), const),
            pl.BlockSpec((D_MODEL, LANES), const),
            pl.BlockSpec((D_MODEL, LANES), const),
            pl.BlockSpec((1, LANES), const),
        ],
        out_specs=[
            pl.BlockSpec((tm, LANES), lambda i: (i, 0)),
            pl.BlockSpec((SUBLANES, tm), lambda i: (0, i)),
            pl.BlockSpec((None, SUBLANES, LANES), lambda i: (i, 0, 0)),
        ],
        out_shape=[
            jax.ShapeDtypeStruct((t_total, LANES), F32),
            jax.ShapeDtypeStruct((SUBLANES, t_total), F32),
            jax.ShapeDtypeStruct((n_tiles, SUBLANES, LANES), F32),
        ],
        compiler_params=_cparams("arbitrary"),
    )(h1, lw["g_ffn"], lw["wr_hi"], lw["wr_lo"], lw["br"])


def _tile_slots(tm):
    raw = tm * TOP_K + N_EXPERTS * (SUBLANES - 1)
    return -(-raw // LANES) * LANES


def _token_tile(t_total):
    tm = TM_TOK
    while t_total % tm:
        tm //= 2
    return tm


def _run_copies(tile, rdst_ref, rsrc_ref, runits_ref, make_copy):
    def per_expert(e, carry):
        k = tile * N_EXPERTS + e
        s0 = rsrc_ref[k]
        d0 = rdst_ref[k]

        def per_unit(u, c2):
            off = u * SUBLANES
            make_copy(pl.multiple_of(s0 + off, SUBLANES), pl.multiple_of(d0 + off, SUBLANES)).start()
            return c2

        lax.fori_loop(0, runits_ref[k], per_unit, 0)
        return carry

    lax.fori_loop(0, N_EXPERTS, per_expert, 0)


def _drain_copies(n_units, make_copy):
    def wait_one(u, carry):
        make_copy(0, 0).wait()
        return carry

    lax.fori_loop(0, n_units, wait_one, 0)


def _dispatch_kernel(rdst_ref, rsrc_ref, runits_ref, tunits_ref, h_ref, g_ref, rt_ref, xs_in, xs_out,
                     srt, sem, *, tm, slots):
    del xs_in
    i = pl.program_id(0)
    xn = _rms(h_ref[...], g_ref[...]).astype(BF16)
    slot_id = lax.broadcasted_iota(jnp.int32, (slots, tm), 0).astype(F32)
    sel = (slot_id == rt_ref[0:1, :]) | (slot_id == rt_ref[1:2, :])
    srt[...] = _pack_rows(jnp.dot(jnp.where(sel, 1.0, 0.0).astype(BF16), xn, preferred_element_type=F32))

    def make_copy(tile_row, sorted_row):
        return pltpu.make_async_copy(srt.at[pl.ds(tile_row, SUBLANES)],
                                     xs_out.at[pl.ds(sorted_row, SUBLANES)], sem)

    _run_copies(i, rdst_ref, rsrc_ref, runits_ref, make_copy)
    _drain_copies(tunits_ref[i], make_copy)


def _dispatch(tables, h1, route_t, lw, xs):
    t_total = h1.shape[0]
    tm = _token_tile(t_total)
    slots = _tile_slots(tm)
    return pl.pallas_call(
        functools.partial(_dispatch_kernel, tm=tm, slots=slots),
        name="dispatch",
        grid_spec=pltpu.PrefetchScalarGridSpec(
            num_scalar_prefetch=4,
            grid=(t_total // tm,),
            in_specs=[
                pl.BlockSpec((tm, D_MODEL), lambda i, *_: (i, 0)),
                pl.BlockSpec((1, D_MODEL), lambda i, *_: (0, 0)),
                pl.BlockSpec((SUBLANES, tm), lambda i, *_: (0, i)),
                pl.BlockSpec(memory_space=pl.ANY),
            ],
            out_specs=pl.BlockSpec(memory_space=pl.ANY),
            scratch_shapes=[pltpu.VMEM((slots, HALF_D), U32), pltpu.SemaphoreType.DMA(())],
        ),
        out_shape=jax.ShapeDtypeStruct(xs.shape, U32),
        input_output_aliases={7: 0},
        compiler_params=_cparams("arbitrary"),
    )(*tables, h1, lw["g_ffn"], route_t, xs)


def _ffn_kernel(bstart_ref, nblk_ref, nused_ref, w1_ref, w3_ref, w2_ref, xs_ref, ys_ref,
                w1b, w3b, w2b, xbuf, ybuf, sem_in, sem_out):
    e = pl.program_id(0)
    g0 = bstart_ref[e]
    nb = nblk_ref[e]
    n_used = nused_ref[0]

    def in_copy(g, slot):
        r = pl.multiple_of(g * FFN_BLK, FFN_BLK)
        return pltpu.make_async_copy(xs_ref.at[pl.ds(r, FFN_BLK)], xbuf.at[slot], sem_in.at[slot])

    def out_copy(g, slot):
        r = pl.multiple_of(g * FFN_BLK, FFN_BLK)
        return pltpu.make_async_copy(ybuf.at[slot], ys_ref.at[pl.ds(r, FFN_BLK)], sem_out.at[slot])

    @pl.when((e == 0) & (n_used > 0))
    def _():
        in_copy(0, 0).start()

    @pl.when(nb > 0)
    def _():
        w1b[...] = w1_ref[...].astype(BF16)
        w3b[...] = w3_ref[...].astype(BF16)
        w2b[...] = w2_ref[...].astype(BF16)

    def block_step(k, carry):
        g = g0 + k
        slot = lax.rem(g, 2)
        in_copy(g, slot).wait()

        @pl.when(g + 1 < n_used)
        def _():
            in_copy(g + 1, 1 - slot).start()

        @pl.when(g >= 2)
        def _():
            out_copy(g - 2, slot).wait()

        x_lo, x_hi = _unpack_rows(xbuf[slot])
        a = (jnp.dot(x_lo, w1b[0:HALF_D, :], preferred_element_type=F32)
             + jnp.dot(x_hi, w1b[HALF_D:, :], preferred_element_type=F32))
        b = (jnp.dot(x_lo, w3b[0:HALF_D, :], preferred_element_type=F32)
             + jnp.dot(x_hi, w3b[HALF_D:, :], preferred_element_type=F32))
        hmid = (a * jax.nn.sigmoid(a) * b).astype(BF16)
        y = jnp.dot(hmid, w2b[...], preferred_element_type=F32)
        ybuf[slot] = _pack_rows(y.astype(BF16).astype(F32))
        out_copy(g, slot).start()
        return carry

    lax.fori_loop(0, nb, block_step, 0)

    @pl.when(e == pl.num_programs(0) - 1)
    def _():
        for back in (2, 1):
            @pl.when(n_used >= back)
            def _():
                g = n_used - back
                out_copy(g, lax.rem(g, 2)).wait()


def _ffn(ffn_tables, xs, w1, w3, w2, layer):
    def w_map(e, *_):
        return (layer, e, 0, 0)

    return pl.pallas_call(
        _ffn_kernel,
        name="expert_ffn",
        grid_spec=pltpu.PrefetchScalarGridSpec(
            num_scalar_prefetch=3,
            grid=(N_EXPERTS,),
            in_specs=[
                pl.BlockSpec((None, None, D_MODEL, D_FF), w_map),
                pl.BlockSpec((None, None, D_MODEL, D_FF), w_map),
                pl.BlockSpec((None, None, D_FF, D_MODEL), w_map),
                pl.BlockSpec(memory_space=pl.ANY),
            ],
            out_specs=pl.BlockSpec(memory_space=pl.ANY),
            scratch_shapes=[
                pltpu.VMEM((D_MODEL, D_FF), BF16),
                pltpu.VMEM((D_MODEL, D_FF), BF16),
                pltpu.VMEM((D_FF, D_MODEL), BF16),
                pltpu.VMEM((2, FFN_BLK, HALF_D), U32),
                pltpu.VMEM((2, FFN_BLK, HALF_D), U32),
                pltpu.SemaphoreType.DMA((2,)),
                pltpu.SemaphoreType.DMA((2,)),
            ],
        ),
        out_shape=jax.ShapeDtypeStruct(xs.shape, U32),
        input_output_aliases={6: 0},
        compiler_params=_cparams("arbitrary"),
    )(*ffn_tables, w1, w3, w2, xs)


def _ple_kernel(rdst_ref, rsrc_ref, runits_ref, tunits_ref, h_ref, route_ref, p_ref, ps_ref, wpg_ref,
                wple_ref, pg_ref, fg_ref, ys_ref, *rest, tm, slots, n_s, final):
    if final:
        yp_ref, ysm_ref, ysrt, sem = rest
    else:
        out_ref, ysrt, sem = rest
    i = pl.program_id(0)

    @pl.when(i == 0)
    def _():
        ysrt[...] = jnp.zeros_like(ysrt)

    def make_copy(tile_row, sorted_row):
        return pltpu.make_async_copy(ys_ref.at[pl.ds(sorted_row, SUBLANES)],
                                     ysrt.at[pl.ds(tile_row, SUBLANES)], sem)

    _run_copies(i, rdst_ref, rsrc_ref, runits_ref, make_copy)
    p = p_ref[...]
    p_last = jnp.concatenate([p[0:tm - n_s, :], ps_ref[...]], axis=0)
    p = jnp.where(i == pl.num_programs(0) - 1, p_last, p)
    pe = _rms(jnp.dot(p.astype(BF16), wple_ref[...], preferred_element_type=F32), pg_ref[...])
    _drain_copies(tunits_ref[i], make_copy)

    route = route_ref[...]
    y_lo, y_hi = _unpack_rows(ysrt[...])
    slot_id = lax.broadcasted_iota(jnp.int32, (tm, slots), 1).astype(F32)
    moe = None
    for kk in range(TOP_K):
        pick = jnp.where(slot_id == route[:, kk:kk + 1], 1.0, 0.0).astype(BF16)
        rows = jnp.concatenate([jnp.dot(pick, y_lo, preferred_element_type=F32),
                                jnp.dot(pick, y_hi, preferred_element_type=F32)], axis=1)
        term = route[:, TOP_K + kk:TOP_K + kk + 1] * rows
        moe = term if moe is None else moe + term
    h2 = h_ref[...] + moe
    gate = jax.nn.sigmoid(jnp.dot(h2.astype(BF16), wpg_ref[...], preferred_element_type=F32))
    out = h2 + gate * pe
    if final:
        fin = _rms(out, fg_ref[...])
        yp_ref[...] = fin

        @pl.when(i == pl.num_programs(0) - 1)
        def _():
            ysm_ref[...] = fin[tm - n_s:tm, :]
    else:
        out_ref[...] = out


def _ple(tables, h1, route, p_prompt, p_sample, lw, final_g, ys, n_s, layer, final, in_place):
    t_total = h1.shape[0]
    tm = _token_tile(t_total)
    slots = _tile_slots(tm)
    assert n_s <= tm and (t_total - n_s) % tm == tm - n_s
    const = lambda i, *_: (0, 0)
    tok = lambda i, *_: (i, 0)
    if final:
        out_specs = [pl.BlockSpec((tm, D_MODEL), tok), pl.BlockSpec((n_s, D_MODEL), const)]
        out_shape = [jax.ShapeDtypeStruct((t_total - n_s, D_MODEL), F32),
                     jax.ShapeDtypeStruct((n_s, D_MODEL), F32)]
        aliases = {}
    else:
        out_specs = [pl.BlockSpec((tm, D_MODEL), tok)]
        out_shape = [jax.ShapeDtypeStruct((t_total, D_MODEL), F32)]
        aliases = {4: 0} if in_place else {}
    return pl.pallas_call(
        functools.partial(_ple_kernel, tm=tm, slots=slots, n_s=n_s, final=final),
        name="combine_ple",
        grid_spec=pltpu.PrefetchScalarGridSpec(
            num_scalar_prefetch=4,
            grid=(t_total // tm,),
            in_specs=[
                pl.BlockSpec((tm, D_MODEL), tok),
                pl.BlockSpec((tm, LANES), tok),
                pl.BlockSpec((None, tm, PLE_DIM), lambda i, *_: (layer, i, 0)),
                pl.BlockSpec((None, n_s, PLE_DIM), lambda i, *_: (layer, 0, 0)),
                pl.BlockSpec((D_MODEL, D_MODEL), const),
                pl.BlockSpec((PLE_DIM, D_MODEL), const),
                pl.BlockSpec((1, D_MODEL), const),
                pl.BlockSpec((1, D_MODEL), const),
                pl.BlockSpec(memory_space=pl.ANY),
            ],
            out_specs=out_specs,
            scratch_shapes=[pltpu.VMEM((slots, HALF_D), U32), pltpu.SemaphoreType.DMA(())],
        ),
        out_shape=out_shape,
        input_output_aliases=aliases,
        compiler_params=_cparams("arbitrary"),
    )(*tables, h1, route, p_prompt, p_sample, lw["w_pg"], lw["w_ple"], lw["ple_g"], final_g, ys)


def _layer_weights(l, norm_mix_g, w_in, conv_w, conv_b, b_igate, b_fgate, hnorm_g, vnorm_g, w_s, b_s, w_out,
                   norm_ffn_g, router_g_w, router_g_b, router_e_w, router_e_b, w_ple, ple_norm_g, w_ple_gate):
    wi = w_in[l]
    a0 = QK_W + 2 * W_A
    gates_w = jnp.pad(wi[:, a0:a0 + 2 * H_A], ((0, 0), (0, LANES - 2 * H_A)))
    w_packed = jnp.concatenate(
        [wi[:, :a0], wi[:, a0 + 2 * H_A:], gates_w], axis=1).astype(BF16)
    gate_b = jnp.pad(jnp.concatenate([b_igate[l], b_fgate[l]]), (0, LANES - 2 * H_A))[None, :]
    wr = jnp.pad(jnp.concatenate([router_g_w[l], router_e_w[l]], axis=1),
                 ((0, 0), (0, LANES - N_GROUPS - N_EXPERTS)))
    wr_hi = wr.astype(BF16)
    wr_lo = (wr - wr_hi.astype(F32)).astype(BF16)
    br = jnp.pad(jnp.concatenate([router_g_b[l], router_e_b[l]]), (0, LANES - N_GROUPS - N_EXPERTS))[None, :]
    return {
        "g_mix": norm_mix_g[l][None, :],
        "w_in": w_packed,
        "conv_w": conv_w[l],
        "conv_b": conv_b[l][None, :],
        "gate_b": gate_b,
        "hn_g": hnorm_g[l][None, :],
        "vn_g": vnorm_g[l][None, :],
        "w_s": w_s[l],
        "b_s_col": jnp.pad(b_s[l].T, ((0, 0), (0, LANES - G_B))),
        "w00_row": jnp.repeat(w_s[l][:, 0, 0], DG_B)[None, :],
        "bs0_row": jnp.repeat(b_s[l][:, 0], DG_B)[None, :],
        "w_out": w_out[l].astype(BF16),
        "g_ffn": norm_ffn_g[l][None, :],
        "wr_hi": wr_hi,
        "wr_lo": wr_lo,
        "br": br,
        "w_pg": w_ple_gate[l].astype(BF16),
        "w_ple": w_ple[l].astype(BF16),
        "ple_g": ple_norm_g[l][None, :],
    }


def _routing_tables(tcnt):
    c8 = tcnt[:, 0, N_GROUPS:N_GROUPS + N_EXPERTS].astype(jnp.int32)
    blocks = (jnp.sum(c8, axis=0) + FFN_BLK - 1) // FFN_BLK
    blk_end = jnp.cumsum(blocks)
    pstart = (blk_end - blocks) * FFN_BLK
    run_dst = pstart[None, :] + jnp.cumsum(c8, axis=0) - c8
    run_src = jnp.cumsum(c8, axis=1) - c8
    units = c8 // SUBLANES
    tables = (run_dst.reshape(-1), run_src.reshape(-1), units.reshape(-1), jnp.sum(units, axis=1))
    ffn_tables = (blk_end - blocks, blocks, blk_end[-1:])
    return tables, ffn_tables


def kernel(x_prompt, x_sample, state_C, state_n, state_m, state_conv, p_prompt, p_sample, norm_mix_g, w_in, conv_w, conv_b, b_igate, b_fgate, hnorm_g, vnorm_g, w_s, b_s, w_out, norm_ffn_g, router_g_w, router_g_b, router_e_w, router_e_b, w1, w3, w2, w_ple, ple_norm_g, w_ple_gate, final_norm_g):
    n_batch, seq, _ = x_prompt.shape
    n_s = x_sample.shape[0]
    depth = w_in.shape[0]
    t_p = n_batch * seq
    t_total = t_p + n_s
    assert seq % CHUNK == 0 and t_p % n_s == 0 and n_s % BT_SAMPLE == 0

    n_tiles = t_total // _token_tile(t_total)
    max_rows = t_total * TOP_K + N_EXPERTS * ((SUBLANES - 1) * n_tiles + FFN_BLK - 1)
    n_blocks = max_rows // FFN_BLK
    xs = jnp.zeros((n_blocks * FFN_BLK, HALF_D), U32)

    p_prompt_t = p_prompt.reshape(depth, t_p, PLE_DIM)
    p_sample_t = p_sample.reshape(depth, n_s, PLE_DIM)
    sconv_t = jnp.transpose(state_conv, (0, 2, 1, 3))
    n0_t = jnp.transpose(state_n, (0, 2, 1, 3))
    m0_pad = jnp.pad(state_m, ((0, 0), (0, 0), (0, LANES - H_A)))
    final_g = final_norm_g[None, :]

    src_p = x_prompt.reshape(t_p, D_MODEL)
    src_s = x_sample.reshape(n_s, D_MODEL)
    spare = jnp.zeros((t_total, D_MODEL), F32)
    c_all = jnp.zeros(state_C.shape, F32)
    outs = {k: [] for k in ("Cp", "np", "mp", "cbp", "ns", "ms", "cbs", "vs")}
    res = None
    for l in range(depth):
        lw = _layer_weights(l, norm_mix_g, w_in, conv_w, conv_b, b_igate, b_fgate, hnorm_g, vnorm_g, w_s, b_s,
                            w_out, norm_ffn_g, router_g_w, router_g_b, router_e_w, router_e_b, w_ple,
                            ple_norm_g, w_ple_gate)
        h_mix, c_p, n_p, m_p, cb_p = _mix_prompt(src_p, spare, n_batch, seq, lw)
        h_mix, c_all, n_s_t, m_s, cb_s, v_s = _mix_sample(src_s, h_mix, c_all, lw, sconv_t, state_C, n0_t,
                                                          m0_pad, n_s, l)
        route, route_t, tcnt = _router(h_mix, lw)
        tables, ffn_tables = _routing_tables(tcnt)
        xs = _dispatch(tables, h_mix, route_t, lw, xs)
        xs = _ffn(ffn_tables, xs, w1, w3, w2, l)
        final = l == depth - 1
        res = _ple(tables, h_mix, route, p_prompt_t, p_sample_t, lw, final_g, xs, n_s, l, final, in_place=l > 0)
        if l == 0:
            spare = h_mix
        else:
            spare = src_p
        src_p = src_s = res[0]
        outs["Cp"].append(c_p)
        outs["np"].append(n_p)
        outs["mp"].append(m_p[:, :, 0])
        outs["cbp"].append(cb_p)
        outs["ns"].append(jnp.transpose(n_s_t, (1, 0, 2)))
        outs["ms"].append(m_s[:, 0:H_A])
        outs["cbs"].append(jnp.transpose(cb_s, (1, 0, 2)))
        outs["vs"].append(v_s[:, None, :])

    y_prompt = res[0].reshape(n_batch, seq, D_MODEL)
    y_sample = res[1].reshape(n_s, 1, D_MODEL)
    st = lambda k: jnp.stack(outs[k])
    return (y_prompt, y_sample, st("Cp"), st("np"), st("mp"), st("cbp"),
            c_all, st("ns"), st("ms"), st("cbs"), st("vs"))
```

```python
import functools

import jax
import jax.numpy as jnp
from jax import lax
from jax.experimental import pallas as pl
from jax.experimental.pallas import tpu as pltpu

F32 = jnp.float32
BF16 = jnp.bfloat16
U32 = jnp.uint32

D_MODEL = 1024
W_A = 512
H_A = 4
DH_A = 128
W_B = 512
G_B = 4
DG_B = 128
CHUNK = 128
CONV_W = 4
QK_W = 2 * W_A
N_GROUPS = 4
EXPERTS_PER_GROUP = 8
N_EXPERTS = N_GROUPS * EXPERTS_PER_GROUP
TOP_K = 2
D_FF = 512
PLE_DIM = 256
EPS = 1e-6
HALF_D = D_MODEL // 2

LANES = 128
SUBLANES = 8
VMEM_LIMIT_BYTES = 56 * 1024 * 1024

P_QK = 0
P_V = QK_W
P_O = P_V + W_A
P_U = P_O + W_A
P_VB = P_U + W_B
P_G = P_VB + W_B
P_W = P_G + LANES

TL_MIX = 512
BT_SAMPLE = 8
TM_TOK = 384
FFN_BLK = 256
FFN_RING = 4


def _cparams(*sem):
    return pltpu.CompilerParams(dimension_semantics=sem, vmem_limit_bytes=VMEM_LIMIT_BYTES)


def _rms(x, g):
    return x * lax.rsqrt(jnp.mean(x * x, axis=-1, keepdims=True) + EPS) * g


def _log_sigmoid(x):
    return -(jnp.maximum(-x, 0.0) + jnp.log1p(jnp.exp(-jnp.abs(x))))


def _split_dot(a, b_bf16):
    hi = a.astype(BF16)
    lo = (a - hi.astype(F32)).astype(BF16)
    return (jnp.dot(hi, b_bf16, preferred_element_type=F32)
            + jnp.dot(lo, b_bf16, preferred_element_type=F32))


def _dot_nt(a, b):
    return lax.dot_general(a, b, (((1,), (1,)), ((), ())), preferred_element_type=F32)


def _pack_rows(x):
    lo = lax.bitcast_convert_type(x[:, :HALF_D], U32) >> 16
    hi = lax.bitcast_convert_type(x[:, HALF_D:], U32) & jnp.uint32(0xFFFF0000)
    return lo | hi


def _unpack_rows(u):
    lo = lax.bitcast_convert_type(u << 16, F32)
    hi = lax.bitcast_convert_type(u & jnp.uint32(0xFFFF0000), F32)
    return lo.astype(BF16), hi.astype(BF16)


def _mix_prompt_kernel(x_ref, xnext_ref, hprev_ref, gmix_ref, win_ref, cw_ref, cb_ref, gb_ref, hng_ref,
                       vng_ref, ws_ref, bs_ref, wout_ref,
                       out_ref, c_out, n_out, m_out, conv_out,
                       proj_a, proj_b, xpad, hcat, c_s, n_s, m_s, *, tl):
    del hprev_ref
    b = pl.program_id(0)
    j = pl.program_id(1)
    nj = pl.num_programs(1)
    n_chunks = tl // CHUNK
    bounds = [(P_W // LANES * c // n_chunks) * LANES for c in range(n_chunks + 1)]
    col_groups = [slice(bounds[c], bounds[c + 1]) for c in range(n_chunks)]

    @pl.when(j == 0)
    def _():
        c_s[...] = jnp.zeros_like(c_s)
        n_s[...] = jnp.zeros_like(n_s)
        m_s[...] = jnp.zeros_like(m_s)
        xpad[0:SUBLANES, :] = jnp.zeros((SUBLANES, QK_W), F32)

    @pl.when((b == 0) & (j == 0))
    def _():
        xn0 = _rms(x_ref[0:tl, :], gmix_ref[...]).astype(BF16)
        proj_a[...] = jnp.dot(xn0, win_ref[...], preferred_element_type=F32)

    row = lax.broadcasted_iota(jnp.int32, (CHUNK, CHUNK), 0)
    col = lax.broadcasted_iota(jnp.int32, (CHUNK, CHUNK), 1)
    causal = row >= col
    tril = jnp.where(causal, 1.0, 0.0).astype(BF16)
    triu = jnp.where(row <= col, 1.0, 0.0).astype(BF16)
    lane = lax.broadcasted_iota(jnp.int32, (CHUNK, LANES), 1)
    wm = [jnp.where(causal, ws_ref[g], 0.0).astype(BF16) for g in range(G_B)]

    def conv_step(proj):
        pre = proj[:, P_QK:P_QK + QK_W]
        xpad[SUBLANES:SUBLANES + tl, :] = pre
        qk = cb_ref[...] + cw_ref[CONV_W - 1:CONV_W, :] * pre
        for jj in range(1, CONV_W):
            qk = qk + cw_ref[CONV_W - 1 - jj:CONV_W - jj, :] * xpad[SUBLANES - jj:SUBLANES - jj + tl, :]
        xpad[SUBLANES - (CONV_W - 1):SUBLANES, :] = pre[tl - (CONV_W - 1):tl, :]
        qk = qk * jax.nn.sigmoid(qk)
        proj[:, 0:W_A] = qk[:, 0:W_A]
        proj[:, W_A:QK_W] = qk[:, W_A:QK_W] * (DH_A ** -0.5)

    def chunk_step(proj, c):
        rows = slice(c * CHUNK, (c + 1) * CHUNK)
        gl = proj[rows, P_G:P_G + LANES] + gb_ref[...]
        gl = jnp.where(lane >= H_A, _log_sigmoid(gl), gl)
        glt = gl.T
        b_col_all = _split_dot_left(tril, gl)
        b_row_all = _split_dot(glt[0:2 * SUBLANES, :], triu)

        for h in range(H_A):
            q32 = proj[rows, h * DH_A:(h + 1) * DH_A]
            k32 = proj[rows, W_A + h * DH_A:W_A + (h + 1) * DH_A]
            v32 = proj[rows, P_V + h * DH_A:P_V + (h + 1) * DH_A]
            q = q32.astype(BF16)
            k = k32.astype(BF16)
            ig_c = gl[:, h:h + 1]
            ig_r = glt[h:h + 1, :]
            b_c = b_col_all[:, H_A + h:H_A + h + 1]
            b_r = b_row_all[H_A + h:H_A + h + 1, :]
            c_old = c_s[h]
            n_old = n_s[h:h + 1, :]
            m_prev = m_s[h:h + 1, 0:1]

            d_log = jnp.where(causal, b_c - b_r + ig_r, -jnp.inf)
            inter = b_c + m_prev
            m_t = jnp.maximum(inter, jnp.max(d_log, axis=-1, keepdims=True))
            s = _dot_nt(q, k) * jnp.exp(d_log - m_t)
            w_inter = jnp.exp(inter - m_t)
            num = (w_inter * _dot_nt(q, c_old.astype(BF16))
                   + jnp.dot(s.astype(BF16), v32.astype(BF16), preferred_element_type=F32))
            nq = (w_inter * jnp.sum(q32 * n_old, axis=-1, keepdims=True)
                  + jnp.sum(s, axis=-1, keepdims=True))
            hh = num / jnp.maximum(jnp.abs(nq), jnp.exp(-m_t))

            m_new = m_t[CHUNK - 1:CHUNK, :]
            b_last = b_c[CHUNK - 1:CHUNK, :]
            w_state = jnp.exp(b_last - b_c + ig_c - m_new)
            decay = jnp.exp(b_last + m_prev - m_new)
            vw = (v32 * w_state).astype(BF16)
            c_s[h] = decay * c_old + lax.dot_general(
                vw, k, (((0,), (0,)), ((), ())), preferred_element_type=F32)
            n_s[h:h + 1, :] = decay * n_old + jnp.sum(w_state * k32, axis=0, keepdims=True)
            m_s[h:h + 1, :] = jnp.broadcast_to(m_new, (1, LANES))

            ha = _rms(hh, hng_ref[:, h * DH_A:(h + 1) * DH_A])
            o = proj[rows, P_O + h * DH_A:P_O + (h + 1) * DH_A]
            hcat[rows, h * DH_A:(h + 1) * DH_A] = (ha * jax.nn.sigmoid(o)).astype(BF16)

        for g in range(G_B):
            vb = proj[rows, P_VB + g * DG_B:P_VB + (g + 1) * DG_B]
            u = proj[rows, P_U + g * DG_B:P_U + (g + 1) * DG_B]
            vn = _rms(jax.nn.gelu(vb), vng_ref[:, g * DG_B:(g + 1) * DG_B])
            z = jnp.dot(wm[g], vn.astype(BF16), preferred_element_type=F32) + bs_ref[:, g:g + 1]
            hcat[rows, W_A + g * DG_B:W_A + (g + 1) * DG_B] = (jax.nn.gelu(u) * z).astype(BF16)

    def tile_pass(row0, proj_cur, proj_nxt, xn_next):
        conv_step(proj_cur)
        for c in range(n_chunks):
            chunk_step(proj_cur, c)
            cs = col_groups[c]
            proj_nxt[:, cs] = jnp.dot(xn_next, win_ref[:, cs], preferred_element_type=F32)
        out_ref[row0:row0 + tl, :] = x_ref[row0:row0 + tl, :] + jnp.dot(
            hcat[...], wout_ref[...], preferred_element_type=F32)

    tile_pass(0, proj_a, proj_b, _rms(x_ref[tl:2 * tl, :], gmix_ref[...]).astype(BF16))
    tile_pass(tl, proj_b, proj_a, _rms(xnext_ref[...], gmix_ref[...]).astype(BF16))

    @pl.when(j == nj - 1)
    def _():
        conv_out[0] = xpad[SUBLANES - (CONV_W - 1):SUBLANES, :]
        c_out[0] = c_s[...]
        n_out[0] = n_s[0:H_A, :]
        m_out[0] = m_s[0:H_A, :]


def _split_dot_left(a_bf16, b):
    hi = b.astype(BF16)
    lo = (b - hi.astype(F32)).astype(BF16)
    return (jnp.dot(a_bf16, hi, preferred_element_type=F32)
            + jnp.dot(a_bf16, lo, preferred_element_type=F32))


def _mix_prompt(h_in, h_out, n_batch, seq, lw):
    tl = max(t for t in range(CHUNK, TL_MIX + 1, CHUNK) if seq % (2 * t) == 0)
    nj = seq // (2 * tl)
    last_tile = n_batch * seq // tl - 1
    const = lambda b, j: (0, 0)
    once = pl.Buffered(1)
    kern = functools.partial(_mix_prompt_kernel, tl=tl)
    return pl.pallas_call(
        kern,
        name="mix_prompt",
        grid=(n_batch, nj),
        in_specs=[
            pl.BlockSpec((2 * tl, D_MODEL), lambda b, j: (b * nj + j, 0)),
            pl.BlockSpec((tl, D_MODEL), lambda b, j: (jnp.minimum(2 * (b * nj + j) + 2, last_tile), 0)),
            pl.BlockSpec(memory_space=pl.ANY),
            pl.BlockSpec((1, D_MODEL), const),
            pl.BlockSpec((D_MODEL, P_W), const, pipeline_mode=once),
            pl.BlockSpec((CONV_W, QK_W), const),
            pl.BlockSpec((1, QK_W), const),
            pl.BlockSpec((1, LANES), const),
            pl.BlockSpec((1, W_A), const),
            pl.BlockSpec((1, W_B), const),
            pl.BlockSpec((G_B, CHUNK, CHUNK), lambda b, j: (0, 0, 0)),
            pl.BlockSpec((CHUNK, LANES), const),
            pl.BlockSpec((D_MODEL, D_MODEL), const, pipeline_mode=once),
        ],
        out_specs=[
            pl.BlockSpec((2 * tl, D_MODEL), lambda b, j: (b * nj + j, 0)),
            pl.BlockSpec((1, H_A, DH_A, DH_A), lambda b, j: (b, 0, 0, 0)),
            pl.BlockSpec((1, H_A, DH_A), lambda b, j: (b, 0, 0)),
            pl.BlockSpec((1, H_A, LANES), lambda b, j: (b, 0, 0)),
            pl.BlockSpec((1, CONV_W - 1, QK_W), lambda b, j: (b, 0, 0)),
        ],
        out_shape=[
            jax.ShapeDtypeStruct(h_out.shape, F32),
            jax.ShapeDtypeStruct((n_batch, H_A, DH_A, DH_A), F32),
            jax.ShapeDtypeStruct((n_batch, H_A, DH_A), F32),
            jax.ShapeDtypeStruct((n_batch, H_A, LANES), F32),
            jax.ShapeDtypeStruct((n_batch, CONV_W - 1, QK_W), F32),
        ],
        scratch_shapes=[
            pltpu.VMEM((tl, P_W), F32),
            pltpu.VMEM((tl, P_W), F32),
            pltpu.VMEM((SUBLANES + tl, QK_W), F32),
            pltpu.VMEM((tl, D_MODEL), BF16),
            pltpu.VMEM((H_A, DH_A, DH_A), F32),
            pltpu.VMEM((SUBLANES, LANES), F32),
            pltpu.VMEM((SUBLANES, LANES), F32),
        ],
        input_output_aliases={2: 0},
        compiler_params=_cparams("arbitrary", "arbitrary"),
    )(h_in, h_in, h_out, lw["g_mix"], lw["w_in"], lw["conv_w"], lw["conv_b"], lw["gate_b"], lw["hn_g"], lw["vn_g"],
      lw["w_s"], lw["b_s_col"], lw["w_out"])


def _mix_sample_kernel(x_ref, hprev_ref, cprev_ref, gmix_ref, win_ref, cw_ref, cb_ref, gb_ref, hng_ref, vng_ref,
                       w00_ref, bs0_ref, wout_ref, sconv_ref, c_in, n_in, m_in,
                       out_ref, c_out, n_out, m_out, conv_out, vrow_out,
                       proj, hcat, *, bt):
    del hprev_ref, cprev_ref
    i = pl.program_id(0)
    ni = pl.num_programs(0)

    @pl.when(i == 0)
    def _():
        x = x_ref[...]
        xn = _rms(x, gmix_ref[...]).astype(BF16)
        proj[...] = jnp.dot(xn, win_ref[...], preferred_element_type=F32)
        pre = proj[:, P_QK:P_QK + QK_W]
        qk = cb_ref[...] + cw_ref[CONV_W - 1:CONV_W, :] * pre
        for jj in range(CONV_W - 1):
            qk = qk + cw_ref[jj:jj + 1, :] * sconv_ref[jj]
        for jj in range(CONV_W - 2):
            conv_out[jj] = sconv_ref[jj + 1]
        conv_out[CONV_W - 2] = pre
        qk = qk * jax.nn.sigmoid(qk)
        proj[:, 0:W_A] = qk[:, 0:W_A]
        proj[:, W_A:QK_W] = qk[:, W_A:QK_W] * (DH_A ** -0.5)
        for g in range(G_B):
            sl = slice(g * DG_B, (g + 1) * DG_B)
            vb = proj[:, P_VB + g * DG_B:P_VB + (g + 1) * DG_B]
            u = proj[:, P_U + g * DG_B:P_U + (g + 1) * DG_B]
            vn = _rms(jax.nn.gelu(vb), vng_ref[:, sl])
            vrow_out[:, sl] = vn
            z = w00_ref[:, sl] * vn + bs0_ref[:, sl]
            hcat[:, W_A + g * DG_B:W_A + (g + 1) * DG_B] = jax.nn.gelu(u) * z

    rows = pl.ds(pl.multiple_of(i * bt, bt), bt)
    gates = proj[rows, P_G:P_G + LANES] + gb_ref[...]
    lane = lax.broadcasted_iota(jnp.int32, (bt, LANES), 1)
    sub = lax.broadcasted_iota(jnp.int32, (bt, DH_A), 0)
    m_new_all = jnp.zeros((bt, LANES), F32)
    for h in range(H_A):
        ig = gates[:, h:h + 1]
        lf = _log_sigmoid(gates[:, H_A + h:H_A + h + 1])
        m_prev = m_in[rows, h:h + 1]
        inter = lf + m_prev
        m_t = jnp.maximum(inter, ig)
        w_inter = jnp.exp(inter - m_t)
        e_d = jnp.exp(ig - m_t)
        q = proj[rows, h * DH_A:(h + 1) * DH_A]
        k = proj[rows, W_A + h * DH_A:W_A + (h + 1) * DH_A]
        v = proj[rows, P_V + h * DH_A:P_V + (h + 1) * DH_A]
        n_old = n_in[h]
        s = jnp.sum(q * k, axis=-1, keepdims=True) * e_d
        vw = v * e_d
        vw_t = jnp.concatenate([vw, jnp.zeros((DH_A - bt, DH_A), F32)], axis=0).T
        qb = q.astype(BF16)
        qc = jnp.zeros((bt, DH_A), F32)
        for t in range(bt):
            c_old = c_in[t, h]
            r = _dot_nt(qb, c_old.astype(BF16))
            qc = jnp.where(sub == t, r, qc)
            c_out[t, h] = w_inter[t:t + 1, :] * c_old + vw_t[:, t:t + 1] * k[t:t + 1, :]
        num = w_inter * qc + s * v
        nq = w_inter * jnp.sum(q * n_old, axis=-1, keepdims=True) + s
        hh = num / jnp.maximum(jnp.abs(nq), jnp.exp(-m_t))
        n_out[h] = w_inter * n_old + e_d * k
        m_new_all = jnp.where(lane == h, m_t, m_new_all)
        ha = _rms(hh, hng_ref[:, h * DH_A:(h + 1) * DH_A])
        o = proj[rows, P_O + h * DH_A:P_O + (h + 1) * DH_A]
        hcat[rows, h * DH_A:(h + 1) * DH_A] = ha * jax.nn.sigmoid(o)
    m_out[rows, :] = m_new_all

    @pl.when(i == ni - 1)
    def _():
        out_ref[...] = x_ref[...] + jnp.dot(hcat[...].astype(BF16), wout_ref[...],
                                            preferred_element_type=F32)


def _mix_sample(h_in, h_out, c_all, lw, sconv_t, c0, n0_t, m0_pad, n_s, layer):
    bt = BT_SAMPLE
    ni = n_s // bt
    const = lambda i: (0, 0)
    in_blk = h_in.shape[0] // n_s - 1
    out_blk = h_out.shape[0] // n_s - 1
    kern = functools.partial(_mix_sample_kernel, bt=bt)
    return pl.pallas_call(
        kern,
        name="mix_sample",
        grid=(ni,),
        in_specs=[
            pl.BlockSpec((n_s, D_MODEL), lambda i: (in_blk, 0)),
            pl.BlockSpec(memory_space=pl.ANY),
            pl.BlockSpec(memory_space=pl.ANY),
            pl.BlockSpec((1, D_MODEL), const),
            pl.BlockSpec((D_MODEL, P_W), const),
            pl.BlockSpec((CONV_W, QK_W), const),
            pl.BlockSpec((1, QK_W), const),
            pl.BlockSpec((1, LANES), const),
            pl.BlockSpec((1, W_A), const),
            pl.BlockSpec((1, W_B), const),
            pl.BlockSpec((1, W_B), const),
            pl.BlockSpec((1, W_B), const),
            pl.BlockSpec((D_MODEL, D_MODEL), const),
            pl.BlockSpec((None, CONV_W - 1, n_s, QK_W), lambda i: (layer, 0, 0, 0)),
            pl.BlockSpec((None, bt, H_A, DH_A, DH_A), lambda i: (layer, i, 0, 0, 0)),
            pl.BlockSpec((None, H_A, bt, DH_A), lambda i: (layer, 0, i, 0)),
            pl.BlockSpec((None, n_s, LANES), lambda i: (layer, 0, 0)),
        ],
        out_specs=[
            pl.BlockSpec((n_s, D_MODEL), lambda i: (out_blk, 0)),
            pl.BlockSpec((None, bt, H_A, DH_A, DH_A), lambda i: (layer, i, 0, 0, 0)),
            pl.BlockSpec((H_A, bt, DH_A), lambda i: (0, i, 0)),
            pl.BlockSpec((n_s, LANES), const),
            pl.BlockSpec((CONV_W - 1, n_s, QK_W), lambda i: (0, 0, 0)),
            pl.BlockSpec((n_s, W_B), const),
        ],
        out_shape=[
            jax.ShapeDtypeStruct(h_out.shape, F32),
            jax.ShapeDtypeStruct(c_all.shape, F32),
            jax.ShapeDtypeStruct(n0_t.shape[1:], F32),
            jax.ShapeDtypeStruct((n_s, LANES), F32),
            jax.ShapeDtypeStruct((CONV_W - 1, n_s, QK_W), F32),
            jax.ShapeDtypeStruct((n_s, W_B), F32),
        ],
        scratch_shapes=[
            pltpu.VMEM((n_s, P_W), F32),
            pltpu.VMEM((n_s, D_MODEL), F32),
        ],
        input_output_aliases={1: 0, 2: 1},
        compiler_params=_cparams("arbitrary"),
    )(h_in, h_out, c_all, lw["g_mix"], lw["w_in"], lw["conv_w"], lw["conv_b"], lw["gate_b"], lw["hn_g"], lw["vn_g"],
      lw["w00_row"], lw["bs0_row"], lw["w_out"], sconv_t, c0, n0_t, m0_pad)


def _router_kernel(h_ref, g_ref, whi_ref, wlo_ref, b_ref, route_ref, route_t_ref, tcnt_ref, *, tm):
    hn = _rms(h_ref[...], g_ref[...])
    hi = hn.astype(BF16)
    lo = (hn - hi.astype(F32)).astype(BF16)
    logits = (jnp.dot(hi, whi_ref[...], preferred_element_type=F32)
              + jnp.dot(lo, whi_ref[...], preferred_element_type=F32)
              + jnp.dot(hi, wlo_ref[...], preferred_element_type=F32)) + b_ref[...]

    lane_i = lax.broadcasted_iota(jnp.int32, (tm, LANES), 1)
    lane = lane_i.astype(F32)
    neg = -jnp.inf
    is_g = lane_i < N_GROUPS
    gl = jnp.where(is_g, logits, neg)
    gmax = jnp.max(gl, axis=-1, keepdims=True)
    g_sel = jnp.min(jnp.where(gl == gmax, lane, float(LANES)), axis=-1, keepdims=True)
    g_w = 1.0 / jnp.sum(jnp.where(is_g, jnp.exp(logits - gmax), 0.0), axis=-1, keepdims=True)

    e_lane = lane_i - N_GROUPS
    lane_grp = (e_lane >> 3).astype(F32)
    in_grp = (e_lane >= 0) & (e_lane < N_EXPERTS) & (lane_grp == g_sel)
    el = jnp.where(in_grp, logits, neg)
    v1 = jnp.max(el, axis=-1, keepdims=True)
    i1 = jnp.min(jnp.where(el == v1, lane, float(LANES)), axis=-1, keepdims=True)
    el2 = jnp.where(lane == i1, neg, el)
    v2 = jnp.max(el2, axis=-1, keepdims=True)
    i2 = jnp.min(jnp.where(el2 == v2, lane, float(LANES)), axis=-1, keepdims=True)
    e21 = jnp.exp(v2 - v1)
    den = 1.0 + e21
    w1 = g_w * (1.0 / den)
    w2 = g_w * (e21 / den)

    onehot = jnp.where((lane == i1) | (lane == i2), 1.0, 0.0)
    r_i = lax.broadcasted_iota(jnp.int32, (tm, tm), 0)
    c_i = lax.broadcasted_iota(jnp.int32, (tm, tm), 1)
    strict = jnp.where(r_i > c_i, 1.0, 0.0).astype(BF16)
    before = jnp.dot(strict, onehot.astype(BF16), preferred_element_type=F32)
    cnt = jnp.sum(onehot, axis=0, keepdims=True)
    cnt8 = jnp.floor((cnt + (SUBLANES - 1.0)) * (1.0 / SUBLANES)) * SUBLANES
    l_r = lax.broadcasted_iota(jnp.int32, (LANES, LANES), 0)
    l_c = lax.broadcasted_iota(jnp.int32, (LANES, LANES), 1)
    lanes_before = jnp.where(l_r < l_c, 1.0, 0.0).astype(BF16)
    start = _split_dot(jnp.broadcast_to(cnt8, (2 * SUBLANES, LANES)), lanes_before)[0:1, :]
    slot = start + before
    pos1 = jnp.sum(jnp.where(lane == i1, slot, 0.0), axis=-1, keepdims=True)
    pos2 = jnp.sum(jnp.where(lane == i2, slot, 0.0), axis=-1, keepdims=True)
    tcnt_ref[...] = jnp.broadcast_to(cnt8, tcnt_ref.shape)

    out = jnp.zeros((tm, LANES), F32)
    for idx, val in enumerate((pos1, pos2, w1, w2)):
        out = jnp.where(lane_i == idx, val, out)
    route_ref[...] = out
    route_t_ref[...] = out.T[0:SUBLANES, :]


def _router(h1, lw):
    t_total = h1.shape[0]
    tm = _token_tile(t_total)
    n_tiles = t_total // tm
    const = lambda i: (0, 0)
    return pl.pallas_call(
        functools.partial(_router_kernel, tm=tm),
        name="router",
        grid=(n_tiles,),
        in_specs=[
            pl.BlockSpec((tm, D_MODEL), lambda i: (i, 0)),
            pl.BlockSpec((1, D_MODEL), const),
            pl.BlockSpec((D_MODEL, LANES), const),
            pl.BlockSpec((D_MODEL, LANES), const),
            pl.BlockSpec((1, LANES), const),
        ],
        out_specs=[
            pl.BlockSpec((tm, LANES), lambda i: (i, 0)),
            pl.BlockSpec((SUBLANES, tm), lambda i: (0, i)),
            pl.BlockSpec((None, SUBLANES, LANES), lambda i: (i, 0, 0)),
        ],
        out_shape=[
            jax.ShapeDtypeStruct((t_total, LANES), F32),
            jax.ShapeDtypeStruct((SUBLANES, t_total), F32),
            jax.ShapeDtypeStruct((n_tiles, SUBLANES, LANES), F32),
        ],
        compiler_params=_cparams("arbitrary"),
    )(h1, lw["g_ffn"], lw["wr_hi"], lw["wr_lo"], lw["br"])


def _tile_slots(tm):
    raw = tm * TOP_K + N_EXPERTS * (SUBLANES - 1)
    return -(-raw // LANES) * LANES


def _token_tile(t_total):
    tm = TM_TOK
    while t_total % tm:
        tm //= 2
    return tm


def _run_copies(tile, rdst_ref, rsrc_ref, runits_ref, make_copy):
    def per_expert(e, carry):
        k = tile * N_EXPERTS + e
        s0 = rsrc_ref[k]
        d0 = rdst_ref[k]

        def per_unit(u, c2):
            off = u * SUBLANES
            make_copy(pl.multiple_of(s0 + off, SUBLANES), pl.multiple_of(d0 + off, SUBLANES)).start()
            return c2

        lax.fori_loop(0, runits_ref[k], per_unit, 0)
        return carry

    lax.fori_loop(0, N_EXPERTS, per_expert, 0)


def _drain_copies(n_units, make_copy):
    def wait_one(u, carry):
        make_copy(0, 0).wait()
        return carry

    lax.fori_loop(0, n_units, wait_one, 0)


def _dispatch_kernel(rdst_ref, rsrc_ref, runits_ref, tunits_ref, h_ref, g_ref, rt_ref, xs_in, xs_out,
                     srt, sem, *, tm, slots):
    del xs_in
    i = pl.program_id(0)
    n = pl.num_programs(0)
    buf = lax.rem(i, 2)

    def copy_maker(b):
        def make_copy(tile_row, sorted_row):
            return pltpu.make_async_copy(srt.at[b, pl.ds(tile_row, SUBLANES)],
                                         xs_out.at[pl.ds(sorted_row, SUBLANES)], sem.at[b])
        return make_copy

    @pl.when(i >= 2)
    def _():
        _drain_copies(tunits_ref[i - 2], copy_maker(buf))

    xn = _rms(h_ref[...], g_ref[...]).astype(BF16)
    slot_id = lax.broadcasted_iota(jnp.int32, (slots, tm), 0).astype(F32)
    sel = (slot_id == rt_ref[0:1, :]) | (slot_id == rt_ref[1:2, :])
    srt[buf] = _pack_rows(jnp.dot(jnp.where(sel, 1.0, 0.0).astype(BF16), xn, preferred_element_type=F32))
    _run_copies(i, rdst_ref, rsrc_ref, runits_ref, copy_maker(buf))

    @pl.when(i == n - 1)
    def _():
        @pl.when(i >= 1)
        def _():
            _drain_copies(tunits_ref[i - 1], copy_maker(1 - buf))

        _drain_copies(tunits_ref[i], copy_maker(buf))


def _dispatch(tables, h1, route_t, lw, xs):
    t_total = h1.shape[0]
    tm = _token_tile(t_total)
    slots = _tile_slots(tm)
    return pl.pallas_call(
        functools.partial(_dispatch_kernel, tm=tm, slots=slots),
        name="dispatch",
        grid_spec=pltpu.PrefetchScalarGridSpec(
            num_scalar_prefetch=4,
            grid=(t_total // tm,),
            in_specs=[
                pl.BlockSpec((tm, D_MODEL), lambda i, *_: (i, 0)),
                pl.BlockSpec((1, D_MODEL), lambda i, *_: (0, 0)),
                pl.BlockSpec((SUBLANES, tm), lambda i, *_: (0, i)),
                pl.BlockSpec(memory_space=pl.ANY),
            ],
            out_specs=pl.BlockSpec(memory_space=pl.ANY),
            scratch_shapes=[pltpu.VMEM((2, slots, HALF_D), U32), pltpu.SemaphoreType.DMA((2,))],
        ),
        out_shape=jax.ShapeDtypeStruct(xs.shape, U32),
        input_output_aliases={7: 0},
        compiler_params=_cparams("arbitrary"),
    )(*tables, h1, lw["g_ffn"], route_t, xs)


def _ffn_kernel(bstart_ref, nblk_ref, nused_ref, w1_ref, w3_ref, w2_ref, xs_ref, ys_ref,
                w1b, w3b, w2b, xbuf, ybuf, sem_in, sem_out):
    e = pl.program_id(0)
    g0 = bstart_ref[e]
    nb = nblk_ref[e]
    n_used = nused_ref[0]

    def in_copy(g):
        slot = lax.rem(g, FFN_RING)
        r = pl.multiple_of(g * FFN_BLK, FFN_BLK)
        return pltpu.make_async_copy(xs_ref.at[pl.ds(r, FFN_BLK)], xbuf.at[slot], sem_in.at[slot])

    def out_copy(g):
        slot = lax.rem(g, FFN_RING)
        r = pl.multiple_of(g * FFN_BLK, FFN_BLK)
        return pltpu.make_async_copy(ybuf.at[slot], ys_ref.at[pl.ds(r, FFN_BLK)], sem_out.at[slot])

    @pl.when(e == 0)
    def _():
        for ahead in range(FFN_RING - 1):
            @pl.when(ahead < n_used)
            def _():
                in_copy(jnp.int32(ahead)).start()

    @pl.when(nb > 0)
    def _():
        w1b[...] = w1_ref[...].astype(BF16)
        w3b[...] = w3_ref[...].astype(BF16)
        w2b[...] = w2_ref[...].astype(BF16)

    def block_step(k, carry):
        g = g0 + k
        slot = lax.rem(g, FFN_RING)
        in_copy(g).wait()

        @pl.when(g + (FFN_RING - 1) < n_used)
        def _():
            in_copy(g + (FFN_RING - 1)).start()

        @pl.when(g >= FFN_RING)
        def _():
            out_copy(g - FFN_RING).wait()

        x_lo, x_hi = _unpack_rows(xbuf[slot])
        a = (jnp.dot(x_lo, w1b[0:HALF_D, :], preferred_element_type=F32)
             + jnp.dot(x_hi, w1b[HALF_D:, :], preferred_element_type=F32))
        b = (jnp.dot(x_lo, w3b[0:HALF_D, :], preferred_element_type=F32)
             + jnp.dot(x_hi, w3b[HALF_D:, :], preferred_element_type=F32))
        hmid = (a * jax.nn.sigmoid(a) * b).astype(BF16)
        y = jnp.dot(hmid, w2b[...], preferred_element_type=F32)
        ybuf[slot] = _pack_rows(y.astype(BF16).astype(F32))
        out_copy(g).start()
        return carry

    lax.fori_loop(0, nb, block_step, 0)

    @pl.when(e == pl.num_programs(0) - 1)
    def _():
        for back in range(FFN_RING, 0, -1):
            @pl.when(n_used >= back)
            def _():
                out_copy(n_used - back).wait()


def _ffn(ffn_tables, xs, w1, w3, w2, layer):
    def w_map(e, *_):
        return (layer, e, 0, 0)

    return pl.pallas_call(
        _ffn_kernel,
        name="expert_ffn",
        grid_spec=pltpu.PrefetchScalarGridSpec(
            num_scalar_prefetch=3,
            grid=(N_EXPERTS,),
            in_specs=[
                pl.BlockSpec((None, None, D_MODEL, D_FF), w_map),
                pl.BlockSpec((None, None, D_MODEL, D_FF), w_map),
                pl.BlockSpec((None, None, D_FF, D_MODEL), w_map),
                pl.BlockSpec(memory_space=pl.ANY),
            ],
            out_specs=pl.BlockSpec(memory_space=pl.ANY),
            scratch_shapes=[
                pltpu.VMEM((D_MODEL, D_FF), BF16),
                pltpu.VMEM((D_MODEL, D_FF), BF16),
                pltpu.VMEM((D_FF, D_MODEL), BF16),
                pltpu.VMEM((FFN_RING, FFN_BLK, HALF_D), U32),
                pltpu.VMEM((FFN_RING, FFN_BLK, HALF_D), U32),
                pltpu.SemaphoreType.DMA((FFN_RING,)),
                pltpu.SemaphoreType.DMA((FFN_RING,)),
            ],
        ),
        out_shape=jax.ShapeDtypeStruct(xs.shape, U32),
        input_output_aliases={6: 0},
        compiler_params=_cparams("arbitrary"),
    )(*ffn_tables, w1, w3, w2, xs)


def _ple_kernel(rdst_ref, rsrc_ref, runits_ref, tunits_ref, h_ref, route_ref, p_ref, ps_ref, wpg_ref,
                wple_ref, pg_ref, fg_ref, ys_ref, *rest, tm, slots, n_s, final):
    if final:
        yp_ref, ysm_ref, ysrt, sem = rest
    else:
        out_ref, ysrt, sem = rest
    i = pl.program_id(0)
    n = pl.num_programs(0)
    buf = lax.rem(i, 2)

    def copy_maker(b):
        def make_copy(tile_row, sorted_row):
            return pltpu.make_async_copy(ys_ref.at[pl.ds(sorted_row, SUBLANES)],
                                         ysrt.at[b, pl.ds(tile_row, SUBLANES)], sem.at[b])
        return make_copy

    @pl.when(i == 0)
    def _():
        ysrt[...] = jnp.zeros_like(ysrt)
        _run_copies(i, rdst_ref, rsrc_ref, runits_ref, copy_maker(buf))

    @pl.when(i + 1 < n)
    def _():
        _run_copies(i + 1, rdst_ref, rsrc_ref, runits_ref, copy_maker(1 - buf))

    p = p_ref[...]
    p_last = jnp.concatenate([p[0:tm - n_s, :], ps_ref[...]], axis=0)
    p = jnp.where(i == n - 1, p_last, p)
    pe = _rms(jnp.dot(p.astype(BF16), wple_ref[...], preferred_element_type=F32), pg_ref[...])
    _drain_copies(tunits_ref[i], copy_maker(buf))

    route = route_ref[...]
    y_lo, y_hi = _unpack_rows(ysrt[buf])
    slot_id = lax.broadcasted_iota(jnp.int32, (tm, slots), 1).astype(F32)
    moe = None
    for kk in range(TOP_K):
        pick = jnp.where(slot_id == route[:, kk:kk + 1], 1.0, 0.0).astype(BF16)
        rows = jnp.concatenate([jnp.dot(pick, y_lo, preferred_element_type=F32),
                                jnp.dot(pick, y_hi, preferred_element_type=F32)], axis=1)
        term = route[:, TOP_K + kk:TOP_K + kk + 1] * rows
        moe = term if moe is None else moe + term
    h2 = h_ref[...] + moe
    gate = jax.nn.sigmoid(jnp.dot(h2.astype(BF16), wpg_ref[...], preferred_element_type=F32))
    out = h2 + gate * pe
    if final:
        fin = _rms(out, fg_ref[...])
        yp_ref[...] = fin

        @pl.when(i == pl.num_programs(0) - 1)
        def _():
            ysm_ref[...] = fin[tm - n_s:tm, :]
    else:
        out_ref[...] = out


def _ple(tables, h1, route, p_prompt, p_sample, lw, final_g, ys, n_s, layer, final, in_place):
    t_total = h1.shape[0]
    tm = _token_tile(t_total)
    slots = _tile_slots(tm)
    assert n_s <= tm and (t_total - n_s) % tm == tm - n_s
    const = lambda i, *_: (0, 0)
    tok = lambda i, *_: (i, 0)
    if final:
        out_specs = [pl.BlockSpec((tm, D_MODEL), tok), pl.BlockSpec((n_s, D_MODEL), const)]
        out_shape = [jax.ShapeDtypeStruct((t_total - n_s, D_MODEL), F32),
                     jax.ShapeDtypeStruct((n_s, D_MODEL), F32)]
        aliases = {}
    else:
        out_specs = [pl.BlockSpec((tm, D_MODEL), tok)]
        out_shape = [jax.ShapeDtypeStruct((t_total, D_MODEL), F32)]
        aliases = {4: 0} if in_place else {}
    return pl.pallas_call(
        functools.partial(_ple_kernel, tm=tm, slots=slots, n_s=n_s, final=final),
        name="combine_ple",
        grid_spec=pltpu.PrefetchScalarGridSpec(
            num_scalar_prefetch=4,
            grid=(t_total // tm,),
            in_specs=[
                pl.BlockSpec((tm, D_MODEL), tok),
                pl.BlockSpec((tm, LANES), tok),
                pl.BlockSpec((None, tm, PLE_DIM), lambda i, *_: (layer, i, 0)),
                pl.BlockSpec((None, n_s, PLE_DIM), lambda i, *_: (layer, 0, 0)),
                pl.BlockSpec((D_MODEL, D_MODEL), const),
                pl.BlockSpec((PLE_DIM, D_MODEL), const),
                pl.BlockSpec((1, D_MODEL), const),
                pl.BlockSpec((1, D_MODEL), const),
                pl.BlockSpec(memory_space=pl.ANY),
            ],
            out_specs=out_specs,
            scratch_shapes=[pltpu.VMEM((2, slots, HALF_D), U32), pltpu.SemaphoreType.DMA((2,))],
        ),
        out_shape=out_shape,
        input_output_aliases=aliases,
        compiler_params=_cparams("arbitrary"),
    )(*tables, h1, route, p_prompt, p_sample, lw["w_pg"], lw["w_ple"], lw["ple_g"], final_g, ys)


def _layer_weights(l, norm_mix_g, w_in, conv_w, conv_b, b_igate, b_fgate, hnorm_g, vnorm_g, w_s, b_s, w_out,
                   norm_ffn_g, router_g_w, router_g_b, router_e_w, router_e_b, w_ple, ple_norm_g, w_ple_gate):
    wi = w_in[l]
    a0 = QK_W + 2 * W_A
    gates_w = jnp.pad(wi[:, a0:a0 + 2 * H_A], ((0, 0), (0, LANES - 2 * H_A)))
    w_packed = jnp.concatenate(
        [wi[:, :a0], wi[:, a0 + 2 * H_A:], gates_w], axis=1).astype(BF16)
    gate_b = jnp.pad(jnp.concatenate([b_igate[l], b_fgate[l]]), (0, LANES - 2 * H_A))[None, :]
    wr = jnp.pad(jnp.concatenate([router_g_w[l], router_e_w[l]], axis=1),
                 ((0, 0), (0, LANES - N_GROUPS - N_EXPERTS)))
    wr_hi = wr.astype(BF16)
    wr_lo = (wr - wr_hi.astype(F32)).astype(BF16)
    br = jnp.pad(jnp.concatenate([router_g_b[l], router_e_b[l]]), (0, LANES - N_GROUPS - N_EXPERTS))[None, :]
    return {
        "g_mix": norm_mix_g[l][None, :],
        "w_in": w_packed,
        "conv_w": conv_w[l],
        "conv_b": conv_b[l][None, :],
        "gate_b": gate_b,
        "hn_g": hnorm_g[l][None, :],
        "vn_g": vnorm_g[l][None, :],
        "w_s": w_s[l],
        "b_s_col": jnp.pad(b_s[l].T, ((0, 0), (0, LANES - G_B))),
        "w00_row": jnp.repeat(w_s[l][:, 0, 0], DG_B)[None, :],
        "bs0_row": jnp.repeat(b_s[l][:, 0], DG_B)[None, :],
        "w_out": w_out[l].astype(BF16),
        "g_ffn": norm_ffn_g[l][None, :],
        "wr_hi": wr_hi,
        "wr_lo": wr_lo,
        "br": br,
        "w_pg": w_ple_gate[l].astype(BF16),
        "w_ple": w_ple[l].astype(BF16),
        "ple_g": ple_norm_g[l][None, :],
    }


def _routing_tables(tcnt):
    c8 = tcnt[:, 0, N_GROUPS:N_GROUPS + N_EXPERTS].astype(jnp.int32)
    blocks = (jnp.sum(c8, axis=0) + FFN_BLK - 1) // FFN_BLK
    blk_end = jnp.cumsum(blocks)
    pstart = (blk_end - blocks) * FFN_BLK
    run_dst = pstart[None, :] + jnp.cumsum(c8, axis=0) - c8
    run_src = jnp.cumsum(c8, axis=1) - c8
    units = c8 // SUBLANES
    tables = (run_dst.reshape(-1), run_src.reshape(-1), units.reshape(-1), jnp.sum(units, axis=1))
    ffn_tables = (blk_end - blocks, blocks, blk_end[-1:])
    return tables, ffn_tables


def kernel(x_prompt, x_sample, state_C, state_n, state_m, state_conv, p_prompt, p_sample, norm_mix_g, w_in, conv_w, conv_b, b_igate, b_fgate, hnorm_g, vnorm_g, w_s, b_s, w_out, norm_ffn_g, router_g_w, router_g_b, router_e_w, router_e_b, w1, w3, w2, w_ple, ple_norm_g, w_ple_gate, final_norm_g):
    n_batch, seq, _ = x_prompt.shape
    n_s = x_sample.shape[0]
    depth = w_in.shape[0]
    t_p = n_batch * seq
    t_total = t_p + n_s
    assert seq % CHUNK == 0 and t_p % n_s == 0 and n_s % BT_SAMPLE == 0

    n_tiles = t_total // _token_tile(t_total)
    max_rows = t_total * TOP_K + N_EXPERTS * ((SUBLANES - 1) * n_tiles + FFN_BLK - 1)
    n_blocks = max_rows // FFN_BLK
    xs = jnp.zeros((n_blocks * FFN_BLK, HALF_D), U32)

    p_prompt_t = p_prompt.reshape(depth, t_p, PLE_DIM)
    p_sample_t = p_sample.reshape(depth, n_s, PLE_DIM)
    sconv_t = jnp.transpose(state_conv, (0, 2, 1, 3))
    n0_t = jnp.transpose(state_n, (0, 2, 1, 3))
    m0_pad = jnp.pad(state_m, ((0, 0), (0, 0), (0, LANES - H_A)))
    final_g = final_norm_g[None, :]

    src_p = x_prompt.reshape(t_p, D_MODEL)
    src_s = x_sample.reshape(n_s, D_MODEL)
    spare = jnp.zeros((t_total, D_MODEL), F32)
    c_all = jnp.zeros(state_C.shape, F32)
    outs = {k: [] for k in ("Cp", "np", "mp", "cbp", "ns", "ms", "cbs", "vs")}
    res = None
    for l in range(depth):
        lw = _layer_weights(l, norm_mix_g, w_in, conv_w, conv_b, b_igate, b_fgate, hnorm_g, vnorm_g, w_s, b_s,
                            w_out, norm_ffn_g, router_g_w, router_g_b, router_e_w, router_e_b, w_ple,
                            ple_norm_g, w_ple_gate)
        h_mix, c_p, n_p, m_p, cb_p = _mix_prompt(src_p, spare, n_batch, seq, lw)
        h_mix, c_all, n_s_t, m_s, cb_s, v_s = _mix_sample(src_s, h_mix, c_all, lw, sconv_t, state_C, n0_t,
                                                          m0_pad, n_s, l)
        route, route_t, tcnt = _router(h_mix, lw)
        tables, ffn_tables = _routing_tables(tcnt)
        xs = _dispatch(tables, h_mix, route_t, lw, xs)
        xs = _ffn(ffn_tables, xs, w1, w3, w2, l)
        final = l == depth - 1
        res = _ple(tables, h_mix, route, p_prompt_t, p_sample_t, lw, final_g, xs, n_s, l, final, in_place=l > 0)
        if l == 0:
            spare = h_mix
        else:
            spare = src_p
        src_p = src_s = res[0]
        outs["Cp"].append(c_p)
        outs["np"].append(n_p)
        outs["mp"].append(m_p[:, :, 0])
        outs["cbp"].append(cb_p)
        outs["ns"].append(jnp.transpose(n_s_t, (1, 0, 2)))
        outs["ms"].append(m_s[:, 0:H_A])
        outs["cbs"].append(jnp.transpose(cb_s, (1, 0, 2)))
        outs["vs"].append(v_s[:, None, :])

    y_prompt = res[0].reshape(n_batch, seq, D_MODEL)
    y_sample = res[1].reshape(n_s, 1, D_MODEL)
    st = lambda k: jnp.stack(outs[k])
    return (y_prompt, y_sample, st("Cp"), st("np"), st("mp"), st("cbp"),
            c_all, st("ns"), st("ms"), st("cbs"), st("vs"))
```

```python
import functools

import jax
import jax.numpy as jnp
from jax import lax
from jax.experimental import pallas as pl
from jax.experimental.pallas import tpu as pltpu

F32 = jnp.float32
BF16 = jnp.bfloat16
U32 = jnp.uint32

D_MODEL = 1024
W_A = 512
H_A = 4
DH_A = 128
W_B = 512
G_B = 4
DG_B = 128
CHUNK = 128
CONV_W = 4
QK_W = 2 * W_A
N_GROUPS = 4
EXPERTS_PER_GROUP = 8
N_EXPERTS = N_GROUPS * EXPERTS_PER_GROUP
TOP_K = 2
D_FF = 512
PLE_DIM = 256
EPS = 1e-6
HALF_D = D_MODEL // 2

LANES = 128
SUBLANES = 8
VMEM_LIMIT_BYTES = 56 * 1024 * 1024

P_QK = 0
P_V = QK_W
P_O = P_V + W_A
P_U = P_O + W_A
P_VB = P_U + W_B
P_G = P_VB + W_B
P_W = P_G + LANES

TL_MIX = 512
BT_SAMPLE = 8
TM_TOK = 384
FFN_BLK = 256
FFN_RING = 4


def _cparams(*sem):
    return pltpu.CompilerParams(dimension_semantics=sem, vmem_limit_bytes=VMEM_LIMIT_BYTES)


def _rms(x, g):
    return x * lax.rsqrt(jnp.mean(x * x, axis=-1, keepdims=True) + EPS) * g


def _log_sigmoid(x):
    return -(jnp.maximum(-x, 0.0) + jnp.log1p(jnp.exp(-jnp.abs(x))))


def _split_dot(a, b_bf16):
    hi = a.astype(BF16)
    lo = (a - hi.astype(F32)).astype(BF16)
    return (jnp.dot(hi, b_bf16, preferred_element_type=F32)
            + jnp.dot(lo, b_bf16, preferred_element_type=F32))


def _dot_nt(a, b):
    return lax.dot_general(a, b, (((1,), (1,)), ((), ())), preferred_element_type=F32)


def _pack_rows(x):
    lo = lax.bitcast_convert_type(x[:, :HALF_D], U32) >> 16
    hi = lax.bitcast_convert_type(x[:, HALF_D:], U32) & jnp.uint32(0xFFFF0000)
    return lo | hi


def _unpack_rows(u):
    lo = lax.bitcast_convert_type(u << 16, F32)
    hi = lax.bitcast_convert_type(u & jnp.uint32(0xFFFF0000), F32)
    return lo.astype(BF16), hi.astype(BF16)


def _mix_prompt_kernel(x_ref, xnext_ref, hprev_ref, gmix_ref, win_ref, cw_ref, cb_ref, gb_ref, hng_ref,
                       vng_ref, ws_ref, bs_ref, wout_ref,
                       out_ref, c_out, n_out, m_out, conv_out,
                       proj_a, proj_b, xpad, hcat, c_s, n_s, m_s, *, tl):
    del hprev_ref
    b = pl.program_id(0)
    j = pl.program_id(1)
    nj = pl.num_programs(1)
    n_chunks = tl // CHUNK
    bounds = [(P_W // LANES * c // n_chunks) * LANES for c in range(n_chunks + 1)]
    col_groups = [slice(bounds[c], bounds[c + 1]) for c in range(n_chunks)]

    @pl.when(j == 0)
    def _():
        c_s[...] = jnp.zeros_like(c_s)
        n_s[...] = jnp.zeros_like(n_s)
        m_s[...] = jnp.zeros_like(m_s)
        xpad[0:SUBLANES, :] = jnp.zeros((SUBLANES, QK_W), F32)

    @pl.when((b == 0) & (j == 0))
    def _():
        xn0 = _rms(x_ref[0:tl, :], gmix_ref[...]).astype(BF16)
        proj_a[...] = jnp.dot(xn0, win_ref[...], preferred_element_type=F32)

    row = lax.broadcasted_iota(jnp.int32, (CHUNK, CHUNK), 0)
    col = lax.broadcasted_iota(jnp.int32, (CHUNK, CHUNK), 1)
    causal = row >= col
    tril = jnp.where(causal, 1.0, 0.0).astype(BF16)
    triu = jnp.where(row <= col, 1.0, 0.0).astype(BF16)
    lane = lax.broadcasted_iota(jnp.int32, (CHUNK, LANES), 1)
    wm = [jnp.where(causal, ws_ref[g], 0.0).astype(BF16) for g in range(G_B)]

    def conv_step(proj):
        pre = proj[:, P_QK:P_QK + QK_W]
        xpad[SUBLANES:SUBLANES + tl, :] = pre
        qk = cb_ref[...] + cw_ref[CONV_W - 1:CONV_W, :] * pre
        for jj in range(1, CONV_W):
            qk = qk + cw_ref[CONV_W - 1 - jj:CONV_W - jj, :] * xpad[SUBLANES - jj:SUBLANES - jj + tl, :]
        xpad[SUBLANES - (CONV_W - 1):SUBLANES, :] = pre[tl - (CONV_W - 1):tl, :]
        qk = qk * jax.nn.sigmoid(qk)
        proj[:, 0:W_A] = qk[:, 0:W_A]
        proj[:, W_A:QK_W] = qk[:, W_A:QK_W] * (DH_A ** -0.5)

    def chunk_step(proj, c):
        rows = slice(c * CHUNK, (c + 1) * CHUNK)
        gl = proj[rows, P_G:P_G + LANES] + gb_ref[...]
        gl = jnp.where(lane >= H_A, _log_sigmoid(gl), gl)
        glt = gl.T
        b_col_all = _split_dot_left(tril, gl)
        b_row_all = _split_dot(glt[0:2 * SUBLANES, :], triu)

        for h in range(H_A):
            q32 = proj[rows, h * DH_A:(h + 1) * DH_A]
            k32 = proj[rows, W_A + h * DH_A:W_A + (h + 1) * DH_A]
            v32 = proj[rows, P_V + h * DH_A:P_V + (h + 1) * DH_A]
            q = q32.astype(BF16)
            k = k32.astype(BF16)
            ig_c = gl[:, h:h + 1]
            ig_r = glt[h:h + 1, :]
            b_c = b_col_all[:, H_A + h:H_A + h + 1]
            b_r = b_row_all[H_A + h:H_A + h + 1, :]
            c_old = c_s[h]
            n_old = n_s[h:h + 1, :]
            m_prev = m_s[h:h + 1, 0:1]

            d_log = jnp.where(causal, b_c - b_r + ig_r, -jnp.inf)
            inter = b_c + m_prev
            m_t = jnp.maximum(inter, jnp.max(d_log, axis=-1, keepdims=True))
            s = _dot_nt(q, k) * jnp.exp(d_log - m_t)
            w_inter = jnp.exp(inter - m_t)
            num = (w_inter * _dot_nt(q, c_old.astype(BF16))
                   + jnp.dot(s.astype(BF16), v32.astype(BF16), preferred_element_type=F32))
            nq = (w_inter * jnp.sum(q32 * n_old, axis=-1, keepdims=True)
                  + jnp.sum(s, axis=-1, keepdims=True))
            hh = num / jnp.maximum(jnp.abs(nq), jnp.exp(-m_t))

            m_new = m_t[CHUNK - 1:CHUNK, :]
            b_last = b_c[CHUNK - 1:CHUNK, :]
            w_state = jnp.exp(b_last - b_c + ig_c - m_new)
            decay = jnp.exp(b_last + m_prev - m_new)
            vw = (v32 * w_state).astype(BF16)
            c_s[h] = decay * c_old + lax.dot_general(
                vw, k, (((0,), (0,)), ((), ())), preferred_element_type=F32)
            n_s[h:h + 1, :] = decay * n_old + jnp.sum(w_state * k32, axis=0, keepdims=True)
            m_s[h:h + 1, :] = jnp.broadcast_to(m_new, (1, LANES))

            ha = _rms(hh, hng_ref[:, h * DH_A:(h + 1) * DH_A])
            o = proj[rows, P_O + h * DH_A:P_O + (h + 1) * DH_A]
            hcat[rows, h * DH_A:(h + 1) * DH_A] = (ha * jax.nn.sigmoid(o)).astype(BF16)

        for g in range(G_B):
            vb = proj[rows, P_VB + g * DG_B:P_VB + (g + 1) * DG_B]
            u = proj[rows, P_U + g * DG_B:P_U + (g + 1) * DG_B]
            vn = _rms(jax.nn.gelu(vb), vng_ref[:, g * DG_B:(g + 1) * DG_B])
            z = jnp.dot(wm[g], vn.astype(BF16), preferred_element_type=F32) + bs_ref[:, g:g + 1]
            hcat[rows, W_A + g * DG_B:W_A + (g + 1) * DG_B] = (jax.nn.gelu(u) * z).astype(BF16)

    def tile_pass(row0, proj_cur, proj_nxt, xn_next):
        conv_step(proj_cur)
        for c in range(n_chunks):
            chunk_step(proj_cur, c)
            cs = col_groups[c]
            proj_nxt[:, cs] = jnp.dot(xn_next, win_ref[:, cs], preferred_element_type=F32)
        out_ref[row0:row0 + tl, :] = x_ref[row0:row0 + tl, :] + jnp.dot(
            hcat[...], wout_ref[...], preferred_element_type=F32)

    tile_pass(0, proj_a, proj_b, _rms(x_ref[tl:2 * tl, :], gmix_ref[...]).astype(BF16))
    tile_pass(tl, proj_b, proj_a, _rms(xnext_ref[...], gmix_ref[...]).astype(BF16))

    @pl.when(j == nj - 1)
    def _():
        conv_out[0] = xpad[SUBLANES - (CONV_W - 1):SUBLANES, :]
        c_out[0] = c_s[...]
        n_out[0] = n_s[0:H_A, :]
        m_out[0] = m_s[0:H_A, :]


def _split_dot_left(a_bf16, b):
    hi = b.astype(BF16)
    lo = (b - hi.astype(F32)).astype(BF16)
    return (jnp.dot(a_bf16, hi, preferred_element_type=F32)
            + jnp.dot(a_bf16, lo, preferred_element_type=F32))


def _mix_prompt(h_in, h_out, n_batch, seq, lw):
    tl = max(t for t in range(CHUNK, TL_MIX + 1, CHUNK) if seq % (2 * t) == 0)
    nj = seq // (2 * tl)
    last_tile = n_batch * seq // tl - 1
    const = lambda b, j: (0, 0)
    once = pl.Buffered(1)
    kern = functools.partial(_mix_prompt_kernel, tl=tl)
    return pl.pallas_call(
        kern,
        name="mix_prompt",
        grid=(n_batch, nj),
        in_specs=[
            pl.BlockSpec((2 * tl, D_MODEL), lambda b, j: (b * nj + j, 0)),
            pl.BlockSpec((tl, D_MODEL), lambda b, j: (jnp.minimum(2 * (b * nj + j) + 2, last_tile), 0)),
            pl.BlockSpec(memory_space=pl.ANY),
            pl.BlockSpec((1, D_MODEL), const),
            pl.BlockSpec((D_MODEL, P_W), const, pipeline_mode=once),
            pl.BlockSpec((CONV_W, QK_W), const),
            pl.BlockSpec((1, QK_W), const),
            pl.BlockSpec((1, LANES), const),
            pl.BlockSpec((1, W_A), const),
            pl.BlockSpec((1, W_B), const),
            pl.BlockSpec((G_B, CHUNK, CHUNK), lambda b, j: (0, 0, 0)),
            pl.BlockSpec((CHUNK, LANES), const),
            pl.BlockSpec((D_MODEL, D_MODEL), const, pipeline_mode=once),
        ],
        out_specs=[
            pl.BlockSpec((2 * tl, D_MODEL), lambda b, j: (b * nj + j, 0)),
            pl.BlockSpec((1, H_A, DH_A, DH_A), lambda b, j: (b, 0, 0, 0)),
            pl.BlockSpec((1, H_A, DH_A), lambda b, j: (b, 0, 0)),
            pl.BlockSpec((1, H_A, LANES), lambda b, j: (b, 0, 0)),
            pl.BlockSpec((1, CONV_W - 1, QK_W), lambda b, j: (b, 0, 0)),
        ],
        out_shape=[
            jax.ShapeDtypeStruct(h_out.shape, F32),
            jax.ShapeDtypeStruct((n_batch, H_A, DH_A, DH_A), F32),
            jax.ShapeDtypeStruct((n_batch, H_A, DH_A), F32),
            jax.ShapeDtypeStruct((n_batch, H_A, LANES), F32),
            jax.ShapeDtypeStruct((n_batch, CONV_W - 1, QK_W), F32),
        ],
        scratch_shapes=[
            pltpu.VMEM((tl, P_W), F32),
            pltpu.VMEM((tl, P_W), F32),
            pltpu.VMEM((SUBLANES + tl, QK_W), F32),
            pltpu.VMEM((tl, D_MODEL), BF16),
            pltpu.VMEM((H_A, DH_A, DH_A), F32),
            pltpu.VMEM((SUBLANES, LANES), F32),
            pltpu.VMEM((SUBLANES, LANES), F32),
        ],
        input_output_aliases={2: 0},
        compiler_params=_cparams("arbitrary", "arbitrary"),
    )(h_in, h_in, h_out, lw["g_mix"], lw["w_in"], lw["conv_w"], lw["conv_b"], lw["gate_b"], lw["hn_g"], lw["vn_g"],
      lw["w_s"], lw["b_s_col"], lw["w_out"])


def _mix_sample_kernel(x_ref, hprev_ref, cprev_ref, gmix_ref, win_ref, cw_ref, cb_ref, gb_ref, hng_ref, vng_ref,
                       w00_ref, bs0_ref, wout_ref, sconv_ref, c_in, n_in, m_in,
                       out_ref, c_out, n_out, m_out, conv_out, vrow_out,
                       proj, hcat, *, bt):
    del hprev_ref, cprev_ref
    i = pl.program_id(0)
    ni = pl.num_programs(0)

    @pl.when(i == 0)
    def _():
        x = x_ref[...]
        xn = _rms(x, gmix_ref[...]).astype(BF16)
        proj[...] = jnp.dot(xn, win_ref[...], preferred_element_type=F32)
        pre = proj[:, P_QK:P_QK + QK_W]
        qk = cb_ref[...] + cw_ref[CONV_W - 1:CONV_W, :] * pre
        for jj in range(CONV_W - 1):
            qk = qk + cw_ref[jj:jj + 1, :] * sconv_ref[jj]
        for jj in range(CONV_W - 2):
            conv_out[jj] = sconv_ref[jj + 1]
        conv_out[CONV_W - 2] = pre
        qk = qk * jax.nn.sigmoid(qk)
        proj[:, 0:W_A] = qk[:, 0:W_A]
        proj[:, W_A:QK_W] = qk[:, W_A:QK_W] * (DH_A ** -0.5)
        for g in range(G_B):
            sl = slice(g * DG_B, (g + 1) * DG_B)
            vb = proj[:, P_VB + g * DG_B:P_VB + (g + 1) * DG_B]
            u = proj[:, P_U + g * DG_B:P_U + (g + 1) * DG_B]
            vn = _rms(jax.nn.gelu(vb), vng_ref[:, sl])
            vrow_out[:, sl] = vn
            z = w00_ref[:, sl] * vn + bs0_ref[:, sl]
            hcat[:, W_A + g * DG_B:W_A + (g + 1) * DG_B] = jax.nn.gelu(u) * z

    rows = pl.ds(pl.multiple_of(i * bt, bt), bt)
    gates = proj[rows, P_G:P_G + LANES] + gb_ref[...]
    lane = lax.broadcasted_iota(jnp.int32, (bt, LANES), 1)
    sub = lax.broadcasted_iota(jnp.int32, (bt, DH_A), 0)
    m_new_all = jnp.zeros((bt, LANES), F32)
    for h in range(H_A):
        ig = gates[:, h:h + 1]
        lf = _log_sigmoid(gates[:, H_A + h:H_A + h + 1])
        m_prev = m_in[rows, h:h + 1]
        inter = lf + m_prev
        m_t = jnp.maximum(inter, ig)
        w_inter = jnp.exp(inter - m_t)
        e_d = jnp.exp(ig - m_t)
        q = proj[rows, h * DH_A:(h + 1) * DH_A]
        k = proj[rows, W_A + h * DH_A:W_A + (h + 1) * DH_A]
        v = proj[rows, P_V + h * DH_A:P_V + (h + 1) * DH_A]
        n_old = n_in[h]
        s = jnp.sum(q * k, axis=-1, keepdims=True) * e_d
        vw = v * e_d
        vw_t = jnp.concatenate([vw, jnp.zeros((DH_A - bt, DH_A), F32)], axis=0).T
        qb = q.astype(BF16)
        qc = jnp.zeros((bt, DH_A), F32)
        for t in range(bt):
            c_old = c_in[t, h]
            r = _dot_nt(qb, c_old.astype(BF16))
            qc = jnp.where(sub == t, r, qc)
            c_out[t, h] = w_inter[t:t + 1, :] * c_old + vw_t[:, t:t + 1] * k[t:t + 1, :]
        num = w_inter * qc + s * v
        nq = w_inter * jnp.sum(q * n_old, axis=-1, keepdims=True) + s
        hh = num / jnp.maximum(jnp.abs(nq), jnp.exp(-m_t))
        n_out[h] = w_inter * n_old + e_d * k
        m_new_all = jnp.where(lane == h, m_t, m_new_all)
        ha = _rms(hh, hng_ref[:, h * DH_A:(h + 1) * DH_A])
        o = proj[rows, P_O + h * DH_A:P_O + (h + 1) * DH_A]
        hcat[rows, h * DH_A:(h + 1) * DH_A] = ha * jax.nn.sigmoid(o)
    m_out[rows, :] = m_new_all

    @pl.when(i == ni - 1)
    def _():
        out_ref[...] = x_ref[...] + jnp.dot(hcat[...].astype(BF16), wout_ref[...],
                                            preferred_element_type=F32)


def _mix_sample(h_in, h_out, c_all, lw, sconv_t, c0, n0_t, m0_pad, n_s, layer):
    bt = BT_SAMPLE
    ni = n_s // bt
    const = lambda i: (0, 0)
    in_blk = h_in.shape[0] // n_s - 1
    out_blk = h_out.shape[0] // n_s - 1
    kern = functools.partial(_mix_sample_kernel, bt=bt)
    return pl.pallas_call(
        kern,
        name="mix_sample",
        grid=(ni,),
        in_specs=[
            pl.BlockSpec((n_s, D_MODEL), lambda i: (in_blk, 0)),
            pl.BlockSpec(memory_space=pl.ANY),
            pl.BlockSpec(memory_space=pl.ANY),
            pl.BlockSpec((1, D_MODEL), const),
            pl.BlockSpec((D_MODEL, P_W), const),
            pl.BlockSpec((CONV_W, QK_W), const),
            pl.BlockSpec((1, QK_W), const),
            pl.BlockSpec((1, LANES), const),
            pl.BlockSpec((1, W_A), const),
            pl.BlockSpec((1, W_B), const),
            pl.BlockSpec((1, W_B), const),
            pl.BlockSpec((1, W_B), const),
            pl.BlockSpec((D_MODEL, D_MODEL), const),
            pl.BlockSpec((None, CONV_W - 1, n_s, QK_W), lambda i: (layer, 0, 0, 0)),
            pl.BlockSpec((None, bt, H_A, DH_A, DH_A), lambda i: (layer, i, 0, 0, 0)),
            pl.BlockSpec((None, H_A, bt, DH_A), lambda i: (layer, 0, i, 0)),
            pl.BlockSpec((None, n_s, LANES), lambda i: (layer, 0, 0)),
        ],
        out_specs=[
            pl.BlockSpec((n_s, D_MODEL), lambda i: (out_blk, 0)),
            pl.BlockSpec((None, bt, H_A, DH_A, DH_A), lambda i: (layer, i, 0, 0, 0)),
            pl.BlockSpec((H_A, bt, DH_A), lambda i: (0, i, 0)),
            pl.BlockSpec((n_s, LANES), const),
            pl.BlockSpec((CONV_W - 1, n_s, QK_W), lambda i: (0, 0, 0)),
            pl.BlockSpec((n_s, W_B), const),
        ],
        out_shape=[
            jax.ShapeDtypeStruct(h_out.shape, F32),
            jax.ShapeDtypeStruct(c_all.shape, F32),
            jax.ShapeDtypeStruct(n0_t.shape[1:], F32),
            jax.ShapeDtypeStruct((n_s, LANES), F32),
            jax.ShapeDtypeStruct((CONV_W - 1, n_s, QK_W), F32),
            jax.ShapeDtypeStruct((n_s, W_B), F32),
        ],
        scratch_shapes=[
            pltpu.VMEM((n_s, P_W), F32),
            pltpu.VMEM((n_s, D_MODEL), F32),
        ],
        input_output_aliases={1: 0, 2: 1},
        compiler_params=_cparams("arbitrary"),
    )(h_in, h_out, c_all, lw["g_mix"], lw["w_in"], lw["conv_w"], lw["conv_b"], lw["gate_b"], lw["hn_g"], lw["vn_g"],
      lw["w00_row"], lw["bs0_row"], lw["w_out"], sconv_t, c0, n0_t, m0_pad)


def _router_kernel(h_ref, g_ref, wr_ref, b_ref, route_ref, route_t_ref, tcnt_ref, *, tm):
    hn = _rms(h_ref[...], g_ref[...]).astype(BF16)
    logits = jnp.dot(hn, wr_ref[...], preferred_element_type=F32) + b_ref[...]

    lane_i = lax.broadcasted_iota(jnp.int32, (tm, LANES), 1)
    lane = lane_i.astype(F32)
    neg = -jnp.inf
    is_g = lane_i < N_GROUPS
    gl = jnp.where(is_g, logits, neg)
    gmax = jnp.max(gl, axis=-1, keepdims=True)
    g_sel = jnp.min(jnp.where(gl == gmax, lane, float(LANES)), axis=-1, keepdims=True)
    g_w = 1.0 / jnp.sum(jnp.where(is_g, jnp.exp(logits - gmax), 0.0), axis=-1, keepdims=True)

    e_lane = lane_i - N_GROUPS
    lane_grp = (e_lane >> 3).astype(F32)
    in_grp = (e_lane >= 0) & (e_lane < N_EXPERTS) & (lane_grp == g_sel)
    el = jnp.where(in_grp, logits, neg)
    v1 = jnp.max(el, axis=-1, keepdims=True)
    i1 = jnp.min(jnp.where(el == v1, lane, float(LANES)), axis=-1, keepdims=True)
    el2 = jnp.where(lane == i1, neg, el)
    v2 = jnp.max(el2, axis=-1, keepdims=True)
    i2 = jnp.min(jnp.where(el2 == v2, lane, float(LANES)), axis=-1, keepdims=True)
    e21 = jnp.exp(v2 - v1)
    den = 1.0 + e21
    w1 = g_w * (1.0 / den)
    w2 = g_w * (e21 / den)

    onehot = jnp.where((lane == i1) | (lane == i2), 1.0, 0.0)
    r_i = lax.broadcasted_iota(jnp.int32, (tm, tm), 0)
    c_i = lax.broadcasted_iota(jnp.int32, (tm, tm), 1)
    strict = jnp.where(r_i > c_i, 1.0, 0.0).astype(BF16)
    before = jnp.dot(strict, onehot.astype(BF16), preferred_element_type=F32)
    cnt = jnp.sum(onehot, axis=0, keepdims=True)
    cnt8 = jnp.floor((cnt + (SUBLANES - 1.0)) * (1.0 / SUBLANES)) * SUBLANES
    l_r = lax.broadcasted_iota(jnp.int32, (LANES, LANES), 0)
    l_c = lax.broadcasted_iota(jnp.int32, (LANES, LANES), 1)
    lanes_before = jnp.where(l_r < l_c, 1.0, 0.0).astype(BF16)
    start = _split_dot(jnp.broadcast_to(cnt8, (2 * SUBLANES, LANES)), lanes_before)[0:1, :]
    slot = start + before
    pos1 = jnp.sum(jnp.where(lane == i1, slot, 0.0), axis=-1, keepdims=True)
    pos2 = jnp.sum(jnp.where(lane == i2, slot, 0.0), axis=-1, keepdims=True)
    tcnt_ref[...] = jnp.broadcast_to(cnt8, tcnt_ref.shape)

    out = jnp.zeros((tm, LANES), F32)
    for idx, val in enumerate((pos1, pos2, w1, w2)):
        out = jnp.where(lane_i == idx, val, out)
    route_ref[...] = out
    route_t_ref[...] = out.T[0:SUBLANES, :]


def _router(h1, lw):
    t_total = h1.shape[0]
    tm = _token_tile(t_total)
    n_tiles = t_total // tm
    const = lambda i: (0, 0)
    return pl.pallas_call(
        functools.partial(_router_kernel, tm=tm),
        name="router",
        grid=(n_tiles,),
        in_specs=[
            pl.BlockSpec((tm, D_MODEL), lambda i: (i, 0)),
            pl.BlockSpec((1, D_MODEL), const),
            pl.BlockSpec((D_MODEL, LANES), const),
            pl.BlockSpec((1, LANES), const),
        ],
        out_specs=[
            pl.BlockSpec((tm, LANES), lambda i: (i, 0)),
            pl.BlockSpec((SUBLANES, tm), lambda i: (0, i)),
            pl.BlockSpec((None, SUBLANES, LANES), lambda i: (i, 0, 0)),
        ],
        out_shape=[
            jax.ShapeDtypeStruct((t_total, LANES), F32),
            jax.ShapeDtypeStruct((SUBLANES, t_total), F32),
            jax.ShapeDtypeStruct((n_tiles, SUBLANES, LANES), F32),
        ],
        compiler_params=_cparams("arbitrary"),
    )(h1, lw["g_ffn"], lw["wr"], lw["br"])


def _tile_slots(tm):
    raw = tm * TOP_K + N_EXPERTS * (SUBLANES - 1)
    return -(-raw // LANES) * LANES


def _token_tile(t_total):
    tm = TM_TOK
    while t_total % tm:
        tm //= 2
    return tm


def _run_copies(tile, rdst_ref, rsrc_ref, runits_ref, make_copy):
    def per_expert(e, carry):
        k = tile * N_EXPERTS + e
        s0 = rsrc_ref[k]
        d0 = rdst_ref[k]

        def per_unit(u, c2):
            off = u * SUBLANES
            make_copy(pl.multiple_of(s0 + off, SUBLANES), pl.multiple_of(d0 + off, SUBLANES)).start()
            return c2

        lax.fori_loop(0, runits_ref[k], per_unit, 0)
        return carry

    lax.fori_loop(0, N_EXPERTS, per_expert, 0)


def _drain_copies(n_units, make_copy):
    def wait_one(u, carry):
        make_copy(0, 0).wait()
        return carry

    lax.fori_loop(0, n_units, wait_one, 0)


def _dispatch_kernel(rdst_ref, rsrc_ref, runits_ref, tunits_ref, h_ref, g_ref, rt_ref, xs_in, xs_out,
                     srt, sem, *, tm, slots):
    del xs_in
    i = pl.program_id(0)
    n = pl.num_programs(0)
    buf = lax.rem(i, 2)

    def copy_maker(b):
        def make_copy(tile_row, sorted_row):
            return pltpu.make_async_copy(srt.at[b, pl.ds(tile_row, SUBLANES)],
                                         xs_out.at[pl.ds(sorted_row, SUBLANES), pl.ds(0, HALF_D)], sem.at[b])
        return make_copy

    @pl.when(i >= 2)
    def _():
        _drain_copies(tunits_ref[i - 2], copy_maker(buf))

    xn = _rms(h_ref[...], g_ref[...]).astype(BF16)
    slot_id = lax.broadcasted_iota(jnp.int32, (slots, tm), 0).astype(F32)
    sel = (slot_id == rt_ref[0:1, :]) | (slot_id == rt_ref[1:2, :])
    srt[buf] = _pack_rows(jnp.dot(jnp.where(sel, 1.0, 0.0).astype(BF16), xn, preferred_element_type=F32))
    _run_copies(i, rdst_ref, rsrc_ref, runits_ref, copy_maker(buf))

    @pl.when(i == n - 1)
    def _():
        @pl.when(i >= 1)
        def _():
            _drain_copies(tunits_ref[i - 1], copy_maker(1 - buf))

        _drain_copies(tunits_ref[i], copy_maker(buf))


def _dispatch(tables, h1, route_t, lw, xs):
    t_total = h1.shape[0]
    tm = _token_tile(t_total)
    slots = _tile_slots(tm)
    return pl.pallas_call(
        functools.partial(_dispatch_kernel, tm=tm, slots=slots),
        name="dispatch",
        grid_spec=pltpu.PrefetchScalarGridSpec(
            num_scalar_prefetch=4,
            grid=(t_total // tm,),
            in_specs=[
                pl.BlockSpec((tm, D_MODEL), lambda i, *_: (i, 0)),
                pl.BlockSpec((1, D_MODEL), lambda i, *_: (0, 0)),
                pl.BlockSpec((SUBLANES, tm), lambda i, *_: (0, i)),
                pl.BlockSpec(memory_space=pl.ANY),
            ],
            out_specs=pl.BlockSpec(memory_space=pl.ANY),
            scratch_shapes=[pltpu.VMEM((2, slots, HALF_D), U32), pltpu.SemaphoreType.DMA((2,))],
        ),
        out_shape=jax.ShapeDtypeStruct(xs.shape, U32),
        input_output_aliases={7: 0},
        compiler_params=_cparams("arbitrary"),
    )(*tables, h1, lw["g_ffn"], route_t, xs)


def _ffn_kernel(bstart_ref, nblk_ref, nused_ref, w1_ref, w3_ref, w2_ref, xs_ref, ys_ref,
                w1b, w3b, w2b, xbuf, ybuf, sem_in, sem_out):
    e = pl.program_id(0)
    g0 = bstart_ref[e]
    nb = nblk_ref[e]
    n_used = nused_ref[0]

    def in_copy(g):
        slot = lax.rem(g, FFN_RING)
        r = pl.multiple_of(g * FFN_BLK, FFN_BLK)
        return pltpu.make_async_copy(xs_ref.at[pl.ds(r, FFN_BLK), pl.ds(0, HALF_D)], xbuf.at[slot],
                                     sem_in.at[slot])

    def out_copy(g):
        slot = lax.rem(g, FFN_RING)
        r = pl.multiple_of(g * FFN_BLK, FFN_BLK)
        return pltpu.make_async_copy(ybuf.at[slot], ys_ref.at[pl.ds(r, FFN_BLK)], sem_out.at[slot])

    @pl.when(e == 0)
    def _():
        for ahead in range(FFN_RING - 1):
            @pl.when(ahead < n_used)
            def _():
                in_copy(jnp.int32(ahead)).start()

    @pl.when(nb > 0)
    def _():
        w1b[...] = w1_ref[...].astype(BF16)
        w3b[...] = w3_ref[...].astype(BF16)
        w2b[...] = w2_ref[...].astype(BF16)

    def block_step(k, carry):
        g = g0 + k
        slot = lax.rem(g, FFN_RING)
        in_copy(g).wait()

        @pl.when(g + (FFN_RING - 1) < n_used)
        def _():
            in_copy(g + (FFN_RING - 1)).start()

        @pl.when(g >= FFN_RING)
        def _():
            out_copy(g - FFN_RING).wait()

        x_lo, x_hi = _unpack_rows(xbuf[slot])
        a = (jnp.dot(x_lo, w1b[0:HALF_D, :], preferred_element_type=F32)
             + jnp.dot(x_hi, w1b[HALF_D:, :], preferred_element_type=F32))
        b = (jnp.dot(x_lo, w3b[0:HALF_D, :], preferred_element_type=F32)
             + jnp.dot(x_hi, w3b[HALF_D:, :], preferred_element_type=F32))
        hmid = (a * jax.nn.sigmoid(a) * b).astype(BF16)
        y = jnp.dot(hmid, w2b[...], preferred_element_type=F32)
        ybuf[slot] = lax.bitcast_convert_type(y, U32)
        out_copy(g).start()
        return carry

    lax.fori_loop(0, nb, block_step, 0)

    @pl.when(e == pl.num_programs(0) - 1)
    def _():
        for back in range(FFN_RING, 0, -1):
            @pl.when(n_used >= back)
            def _():
                out_copy(n_used - back).wait()


def _ffn(ffn_tables, xs, w1, w3, w2, layer):
    def w_map(e, *_):
        return (layer, e, 0, 0)

    return pl.pallas_call(
        _ffn_kernel,
        name="expert_ffn",
        grid_spec=pltpu.PrefetchScalarGridSpec(
            num_scalar_prefetch=3,
            grid=(N_EXPERTS,),
            in_specs=[
                pl.BlockSpec((None, None, D_MODEL, D_FF), w_map),
                pl.BlockSpec((None, None, D_MODEL, D_FF), w_map),
                pl.BlockSpec((None, None, D_FF, D_MODEL), w_map),
                pl.BlockSpec(memory_space=pl.ANY),
            ],
            out_specs=pl.BlockSpec(memory_space=pl.ANY),
            scratch_shapes=[
                pltpu.VMEM((D_MODEL, D_FF), BF16),
                pltpu.VMEM((D_MODEL, D_FF), BF16),
                pltpu.VMEM((D_FF, D_MODEL), BF16),
                pltpu.VMEM((FFN_RING, FFN_BLK, HALF_D), U32),
                pltpu.VMEM((FFN_RING, FFN_BLK, D_MODEL), U32),
                pltpu.SemaphoreType.DMA((FFN_RING,)),
                pltpu.SemaphoreType.DMA((FFN_RING,)),
            ],
        ),
        out_shape=jax.ShapeDtypeStruct(xs.shape, U32),
        input_output_aliases={6: 0},
        compiler_params=_cparams("arbitrary"),
    )(*ffn_tables, w1, w3, w2, xs)


def _ple_kernel(rdst_ref, rsrc_ref, runits_ref, tunits_ref, h_ref, route_ref, p_ref, ps_ref, wpg_ref,
                wple_ref, pg_ref, fg_ref, ys_ref, *rest, tm, slots, n_s, final):
    if final:
        yp_ref, ysm_ref, ysrt, sem = rest
    else:
        out_ref, ysrt, sem = rest
    i = pl.program_id(0)
    n = pl.num_programs(0)
    buf = lax.rem(i, 2)

    def copy_maker(b):
        def make_copy(tile_row, sorted_row):
            return pltpu.make_async_copy(ys_ref.at[pl.ds(sorted_row, SUBLANES)],
                                         ysrt.at[b, pl.ds(tile_row, SUBLANES)], sem.at[b])
        return make_copy

    @pl.when(i == 0)
    def _():
        ysrt[...] = jnp.zeros_like(ysrt)
        _run_copies(i, rdst_ref, rsrc_ref, runits_ref, copy_maker(buf))

    @pl.when(i + 1 < n)
    def _():
        _run_copies(i + 1, rdst_ref, rsrc_ref, runits_ref, copy_maker(1 - buf))

    p = p_ref[...]
    p_last = jnp.concatenate([p[0:tm - n_s, :], ps_ref[...]], axis=0)
    p = jnp.where(i == n - 1, p_last, p)
    pe = _rms(jnp.dot(p.astype(BF16), wple_ref[...], preferred_element_type=F32), pg_ref[...])
    _drain_copies(tunits_ref[i], copy_maker(buf))

    route = route_ref[...]
    y = lax.bitcast_convert_type(ysrt[buf], F32)
    y_head = y.astype(BF16)
    y_rem = (y - y_head.astype(F32)).astype(BF16)
    slot_id = lax.broadcasted_iota(jnp.int32, (tm, slots), 1).astype(F32)
    moe = None
    weighted_pick = jnp.zeros((tm, slots), F32)
    for kk in range(TOP_K):
        hit = slot_id == route[:, kk:kk + 1]
        w_kk = route[:, TOP_K + kk:TOP_K + kk + 1]
        term = w_kk * jnp.dot(jnp.where(hit, 1.0, 0.0).astype(BF16), y_head, preferred_element_type=F32)
        moe = term if moe is None else moe + term
        weighted_pick = jnp.where(hit, w_kk, weighted_pick)
    moe = moe + jnp.dot(weighted_pick.astype(BF16), y_rem, preferred_element_type=F32)
    h2 = h_ref[...] + moe
    gate = jax.nn.sigmoid(jnp.dot(h2.astype(BF16), wpg_ref[...], preferred_element_type=F32))
    out = h2 + gate * pe
    if final:
        fin = _rms(out, fg_ref[...])
        yp_ref[...] = fin

        @pl.when(i == pl.num_programs(0) - 1)
        def _():
            ysm_ref[...] = fin[tm - n_s:tm, :]
    else:
        out_ref[...] = out


def _ple(tables, h1, route, p_prompt, p_sample, lw, final_g, ys, n_s, layer, final, in_place):
    t_total = h1.shape[0]
    tm = _token_tile(t_total)
    slots = _tile_slots(tm)
    assert n_s <= tm and (t_total - n_s) % tm == tm - n_s
    const = lambda i, *_: (0, 0)
    tok = lambda i, *_: (i, 0)
    if final:
        out_specs = [pl.BlockSpec((tm, D_MODEL), tok), pl.BlockSpec((n_s, D_MODEL), const)]
        out_shape = [jax.ShapeDtypeStruct((t_total - n_s, D_MODEL), F32),
                     jax.ShapeDtypeStruct((n_s, D_MODEL), F32)]
        aliases = {}
    else:
        out_specs = [pl.BlockSpec((tm, D_MODEL), tok)]
        out_shape = [jax.ShapeDtypeStruct((t_total, D_MODEL), F32)]
        aliases = {4: 0} if in_place else {}
    return pl.pallas_call(
        functools.partial(_ple_kernel, tm=tm, slots=slots, n_s=n_s, final=final),
        name="combine_ple",
        grid_spec=pltpu.PrefetchScalarGridSpec(
            num_scalar_prefetch=4,
            grid=(t_total // tm,),
            in_specs=[
                pl.BlockSpec((tm, D_MODEL), tok),
                pl.BlockSpec((tm, LANES), tok),
                pl.BlockSpec((None, tm, PLE_DIM), lambda i, *_: (layer, i, 0)),
                pl.BlockSpec((None, n_s, PLE_DIM), lambda i, *_: (layer, 0, 0)),
                pl.BlockSpec((D_MODEL, D_MODEL), const),
                pl.BlockSpec((PLE_DIM, D_MODEL), const),
                pl.BlockSpec((1, D_MODEL), const),
                pl.BlockSpec((1, D_MODEL), const),
                pl.BlockSpec(memory_space=pl.ANY),
            ],
            out_specs=out_specs,
            scratch_shapes=[pltpu.VMEM((2, slots, D_MODEL), U32), pltpu.SemaphoreType.DMA((2,))],
        ),
        out_shape=out_shape,
        input_output_aliases=aliases,
        compiler_params=_cparams("arbitrary"),
    )(*tables, h1, route, p_prompt, p_sample, lw["w_pg"], lw["w_ple"], lw["ple_g"], final_g, ys)


def _layer_weights(l, norm_mix_g, w_in, conv_w, conv_b, b_igate, b_fgate, hnorm_g, vnorm_g, w_s, b_s, w_out,
                   norm_ffn_g, router_g_w, router_g_b, router_e_w, router_e_b, w_ple, ple_norm_g, w_ple_gate):
    wi = w_in[l]
    a0 = QK_W + 2 * W_A
    gates_w = jnp.pad(wi[:, a0:a0 + 2 * H_A], ((0, 0), (0, LANES - 2 * H_A)))
    w_packed = jnp.concatenate(
        [wi[:, :a0], wi[:, a0 + 2 * H_A:], gates_w], axis=1).astype(BF16)
    gate_b = jnp.pad(jnp.concatenate([b_igate[l], b_fgate[l]]), (0, LANES - 2 * H_A))[None, :]
    wr = jnp.pad(jnp.concatenate([router_g_w[l], router_e_w[l]], axis=1),
                 ((0, 0), (0, LANES - N_GROUPS - N_EXPERTS))).astype(BF16)
    br = jnp.pad(jnp.concatenate([router_g_b[l], router_e_b[l]]), (0, LANES - N_GROUPS - N_EXPERTS))[None, :]
    return {
        "g_mix": norm_mix_g[l][None, :],
        "w_in": w_packed,
        "conv_w": conv_w[l],
        "conv_b": conv_b[l][None, :],
        "gate_b": gate_b,
        "hn_g": hnorm_g[l][None, :],
        "vn_g": vnorm_g[l][None, :],
        "w_s": w_s[l],
        "b_s_col": jnp.pad(b_s[l].T, ((0, 0), (0, LANES - G_B))),
        "w00_row": jnp.repeat(w_s[l][:, 0, 0], DG_B)[None, :],
        "bs0_row": jnp.repeat(b_s[l][:, 0], DG_B)[None, :],
        "w_out": w_out[l].astype(BF16),
        "g_ffn": norm_ffn_g[l][None, :],
        "wr": wr,
        "br": br,
        "w_pg": w_ple_gate[l].astype(BF16),
        "w_ple": w_ple[l].astype(BF16),
        "ple_g": ple_norm_g[l][None, :],
    }


def _routing_tables(tcnt):
    c8 = tcnt[:, 0, N_GROUPS:N_GROUPS + N_EXPERTS].astype(jnp.int32)
    blocks = (jnp.sum(c8, axis=0) + FFN_BLK - 1) // FFN_BLK
    blk_end = jnp.cumsum(blocks)
    pstart = (blk_end - blocks) * FFN_BLK
    run_dst = pstart[None, :] + jnp.cumsum(c8, axis=0) - c8
    run_src = jnp.cumsum(c8, axis=1) - c8
    units = c8 // SUBLANES
    tables = (run_dst.reshape(-1), run_src.reshape(-1), units.reshape(-1), jnp.sum(units, axis=1))
    ffn_tables = (blk_end - blocks, blocks, blk_end[-1:])
    return tables, ffn_tables


def kernel(x_prompt, x_sample, state_C, state_n, state_m, state_conv, p_prompt, p_sample, norm_mix_g, w_in, conv_w, conv_b, b_igate, b_fgate, hnorm_g, vnorm_g, w_s, b_s, w_out, norm_ffn_g, router_g_w, router_g_b, router_e_w, router_e_b, w1, w3, w2, w_ple, ple_norm_g, w_ple_gate, final_norm_g):
    n_batch, seq, _ = x_prompt.shape
    n_s = x_sample.shape[0]
    depth = w_in.shape[0]
    t_p = n_batch * seq
    t_total = t_p + n_s
    assert seq % CHUNK == 0 and t_p % n_s == 0 and n_s % BT_SAMPLE == 0

    n_tiles = t_total // _token_tile(t_total)
    max_rows = t_total * TOP_K + N_EXPERTS * ((SUBLANES - 1) * n_tiles + FFN_BLK - 1)
    n_blocks = max_rows // FFN_BLK
    xs = jnp.zeros((n_blocks * FFN_BLK, D_MODEL), U32)

    p_prompt_t = p_prompt.reshape(depth, t_p, PLE_DIM)
    p_sample_t = p_sample.reshape(depth, n_s, PLE_DIM)
    sconv_t = jnp.transpose(state_conv, (0, 2, 1, 3))
    n0_t = jnp.transpose(state_n, (0, 2, 1, 3))
    m0_pad = jnp.pad(state_m, ((0, 0), (0, 0), (0, LANES - H_A)))
    final_g = final_norm_g[None, :]

    src_p = x_prompt.reshape(t_p, D_MODEL)
    src_s = x_sample.reshape(n_s, D_MODEL)
    spare = jnp.zeros((t_total, D_MODEL), F32)
    c_all = jnp.zeros(state_C.shape, F32)
    outs = {k: [] for k in ("Cp", "np", "mp", "cbp", "ns", "ms", "cbs", "vs")}
    res = None
    for l in range(depth):
        lw = _layer_weights(l, norm_mix_g, w_in, conv_w, conv_b, b_igate, b_fgate, hnorm_g, vnorm_g, w_s, b_s,
                            w_out, norm_ffn_g, router_g_w, router_g_b, router_e_w, router_e_b, w_ple,
                            ple_norm_g, w_ple_gate)
        h_mix, c_p, n_p, m_p, cb_p = _mix_prompt(src_p, spare, n_batch, seq, lw)
        h_mix, c_all, n_s_t, m_s, cb_s, v_s = _mix_sample(src_s, h_mix, c_all, lw, sconv_t, state_C, n0_t,
                                                          m0_pad, n_s, l)
        route, route_t, tcnt = _router(h_mix, lw)
        tables, ffn_tables = _routing_tables(tcnt)
        xs = _dispatch(tables, h_mix, route_t, lw, xs)
        xs = _ffn(ffn_tables, xs, w1, w3, w2, l)
        final = l == depth - 1
        res = _ple(tables, h_mix, route, p_prompt_t, p_sample_t, lw, final_g, xs, n_s, l, final, in_place=l > 0)
        if l == 0:
            spare = h_mix
        else:
            spare = src_p
        src_p = src_s = res[0]
        outs["Cp"].append(c_p)
        outs["np"].append(n_p)
        outs["mp"].append(m_p[:, :, 0])
        outs["cbp"].append(cb_p)
        outs["ns"].append(jnp.transpose(n_s_t, (1, 0, 2)))
        outs["ms"].append(m_s[:, 0:H_A])
        outs["cbs"].append(jnp.transpose(cb_s, (1, 0, 2)))
        outs["vs"].append(v_s[:, None, :])

    y_prompt = res[0].reshape(n_batch, seq, D_MODEL)
    y_sample = res[1].reshape(n_s, 1, D_MODEL)
    st = lambda k: jnp.stack(outs[k])
    return (y_prompt, y_sample, st("Cp"), st("np"), st("mp"), st("cbp"),
            c_all, st("ns"), st("ms"), st("cbs"), st("vs"))
```

```python
import functools

import jax
import jax.numpy as jnp
from jax import lax
from jax.experimental import pallas as pl
from jax.experimental.pallas import tpu as pltpu

F32 = jnp.float32
BF16 = jnp.bfloat16
U32 = jnp.uint32

D_MODEL = 1024
W_A = 512
H_A = 4
DH_A = 128
W_B = 512
G_B = 4
DG_B = 128
CHUNK = 128
CONV_W = 4
QK_W = 2 * W_A
N_GROUPS = 4
EXPERTS_PER_GROUP = 8
N_EXPERTS = N_GROUPS * EXPERTS_PER_GROUP
TOP_K = 2
D_FF = 512
PLE_DIM = 256
EPS = 1e-6
HALF_D = D_MODEL // 2

LANES = 128
SUBLANES = 8
VMEM_LIMIT_BYTES = 56 * 1024 * 1024

P_QK = 0
P_V = QK_W
P_O = P_V + W_A
P_U = P_O + W_A
P_VB = P_U + W_B
P_G = P_VB + W_B
P_W = P_G + LANES

TL_MIX = 512
BT_SAMPLE = 8
TM_TOK = 384
FFN_BLK = 256
FFN_RING = 4


def _cparams(*sem):
    return pltpu.CompilerParams(dimension_semantics=sem, vmem_limit_bytes=VMEM_LIMIT_BYTES)


def _rms(x, g):
    return x * lax.rsqrt(jnp.mean(x * x, axis=-1, keepdims=True) + EPS) * g


def _log_sigmoid(x):
    return -(jnp.maximum(-x, 0.0) + jnp.log1p(jnp.exp(-jnp.abs(x))))


def _split_dot(a, b_bf16):
    hi = a.astype(BF16)
    lo = (a - hi.astype(F32)).astype(BF16)
    return (jnp.dot(hi, b_bf16, preferred_element_type=F32)
            + jnp.dot(lo, b_bf16, preferred_element_type=F32))


def _dot_nt(a, b):
    return lax.dot_general(a, b, (((1,), (1,)), ((), ())), preferred_element_type=F32)


def _pack_rows(x):
    lo = lax.bitcast_convert_type(x[:, :HALF_D], U32) >> 16
    hi = lax.bitcast_convert_type(x[:, HALF_D:], U32) & jnp.uint32(0xFFFF0000)
    return lo | hi


def _unpack_rows(u):
    lo = lax.bitcast_convert_type(u << 16, F32)
    hi = lax.bitcast_convert_type(u & jnp.uint32(0xFFFF0000), F32)
    return lo.astype(BF16), hi.astype(BF16)


def _mix_prompt_kernel(x_ref, xnext_ref, hprev_ref, gmix_ref, win_ref, cw_ref, cb_ref, gb_ref, hng_ref,
                       vng_ref, ws_ref, bs_ref, wout_ref,
                       out_ref, c_out, n_out, m_out, conv_out,
                       proj_a, proj_b, xpad, hcat, c_s, n_s, m_s, *, tl):
    del hprev_ref
    b = pl.program_id(0)
    j = pl.program_id(1)
    nj = pl.num_programs(1)
    n_chunks = tl // CHUNK
    bounds = [(P_W // LANES * c // n_chunks) * LANES for c in range(n_chunks + 1)]
    col_groups = [slice(bounds[c], bounds[c + 1]) for c in range(n_chunks)]

    @pl.when(j == 0)
    def _():
        c_s[...] = jnp.zeros_like(c_s)
        n_s[...] = jnp.zeros_like(n_s)
        m_s[...] = jnp.zeros_like(m_s)
        xpad[0:SUBLANES, :] = jnp.zeros((SUBLANES, QK_W), F32)

    @pl.when((b == 0) & (j == 0))
    def _():
        xn0 = _rms(x_ref[0:tl, :], gmix_ref[...]).astype(BF16)
        proj_a[...] = jnp.dot(xn0, win_ref[...], preferred_element_type=F32)

    row = lax.broadcasted_iota(jnp.int32, (CHUNK, CHUNK), 0)
    col = lax.broadcasted_iota(jnp.int32, (CHUNK, CHUNK), 1)
    causal = row >= col
    tril = jnp.where(causal, 1.0, 0.0).astype(BF16)
    triu = jnp.where(row <= col, 1.0, 0.0).astype(BF16)
    lane = lax.broadcasted_iota(jnp.int32, (CHUNK, LANES), 1)
    wm = [jnp.where(causal, ws_ref[g], 0.0).astype(BF16) for g in range(G_B)]

    def conv_step(proj):
        pre = proj[:, P_QK:P_QK + QK_W]
        xpad[SUBLANES:SUBLANES + tl, :] = pre
        qk = cb_ref[...] + cw_ref[CONV_W - 1:CONV_W, :] * pre
        for jj in range(1, CONV_W):
            qk = qk + cw_ref[CONV_W - 1 - jj:CONV_W - jj, :] * xpad[SUBLANES - jj:SUBLANES - jj + tl, :]
        xpad[SUBLANES - (CONV_W - 1):SUBLANES, :] = pre[tl - (CONV_W - 1):tl, :]
        qk = qk * jax.nn.sigmoid(qk)
        proj[:, 0:W_A] = qk[:, 0:W_A]
        proj[:, W_A:QK_W] = qk[:, W_A:QK_W] * (DH_A ** -0.5)

    def chunk_step(proj, c):
        rows = slice(c * CHUNK, (c + 1) * CHUNK)
        gl = proj[rows, P_G:P_G + LANES] + gb_ref[...]
        gl = jnp.where(lane >= H_A, _log_sigmoid(gl), gl)
        glt = gl.T
        b_col_all = _split_dot_left(tril, gl)
        b_row_all = _split_dot(glt[0:2 * SUBLANES, :], triu)

        for h in range(H_A):
            q32 = proj[rows, h * DH_A:(h + 1) * DH_A]
            k32 = proj[rows, W_A + h * DH_A:W_A + (h + 1) * DH_A]
            v32 = proj[rows, P_V + h * DH_A:P_V + (h + 1) * DH_A]
            q = q32.astype(BF16)
            k = k32.astype(BF16)
            ig_c = gl[:, h:h + 1]
            ig_r = glt[h:h + 1, :]
            b_c = b_col_all[:, H_A + h:H_A + h + 1]
            b_r = b_row_all[H_A + h:H_A + h + 1, :]
            c_old = c_s[h]
            n_old = n_s[h:h + 1, :]
            m_prev = m_s[h:h + 1, 0:1]

            d_log = jnp.where(causal, b_c - b_r + ig_r, -jnp.inf)
            inter = b_c + m_prev
            m_t = jnp.maximum(inter, jnp.max(d_log, axis=-1, keepdims=True))
            s = _dot_nt(q, k) * jnp.exp(d_log - m_t)
            w_inter = jnp.exp(inter - m_t)
            num = (w_inter * _dot_nt(q, c_old.astype(BF16))
                   + jnp.dot(s.astype(BF16), v32.astype(BF16), preferred_element_type=F32))
            nq = (w_inter * jnp.sum(q32 * n_old, axis=-1, keepdims=True)
                  + jnp.sum(s, axis=-1, keepdims=True))
            hh = num / jnp.maximum(jnp.abs(nq), jnp.exp(-m_t))

            m_new = m_t[CHUNK - 1:CHUNK, :]
            b_last = b_c[CHUNK - 1:CHUNK, :]
            w_state = jnp.exp(b_last - b_c + ig_c - m_new)
            decay = jnp.exp(b_last + m_prev - m_new)
            vw = (v32 * w_state).astype(BF16)
            c_s[h] = decay * c_old + lax.dot_general(
                vw, k, (((0,), (0,)), ((), ())), preferred_element_type=F32)
            n_s[h:h + 1, :] = decay * n_old + jnp.sum(w_state * k32, axis=0, keepdims=True)
            m_s[h:h + 1, :] = jnp.broadcast_to(m_new, (1, LANES))

            ha = _rms(hh, hng_ref[:, h * DH_A:(h + 1) * DH_A])
            o = proj[rows, P_O + h * DH_A:P_O + (h + 1) * DH_A]
            hcat[rows, h * DH_A:(h + 1) * DH_A] = (ha * jax.nn.sigmoid(o)).astype(BF16)

        for g in range(G_B):
            vb = proj[rows, P_VB + g * DG_B:P_VB + (g + 1) * DG_B]
            u = proj[rows, P_U + g * DG_B:P_U + (g + 1) * DG_B]
            vn = _rms(jax.nn.gelu(vb), vng_ref[:, g * DG_B:(g + 1) * DG_B])
            z = jnp.dot(wm[g], vn.astype(BF16), preferred_element_type=F32) + bs_ref[:, g:g + 1]
            hcat[rows, W_A + g * DG_B:W_A + (g + 1) * DG_B] = (jax.nn.gelu(u) * z).astype(BF16)

    def tile_pass(row0, proj_cur, proj_nxt, xn_next):
        conv_step(proj_cur)
        for c in range(n_chunks):
            chunk_step(proj_cur, c)
            cs = col_groups[c]
            proj_nxt[:, cs] = jnp.dot(xn_next, win_ref[:, cs], preferred_element_type=F32)
        out_ref[row0:row0 + tl, :] = x_ref[row0:row0 + tl, :] + jnp.dot(
            hcat[...], wout_ref[...], preferred_element_type=F32)

    tile_pass(0, proj_a, proj_b, _rms(x_ref[tl:2 * tl, :], gmix_ref[...]).astype(BF16))
    tile_pass(tl, proj_b, proj_a, _rms(xnext_ref[...], gmix_ref[...]).astype(BF16))

    @pl.when(j == nj - 1)
    def _():
        conv_out[0] = xpad[SUBLANES - (CONV_W - 1):SUBLANES, :]
        c_out[0] = c_s[...]
        n_out[0] = n_s[0:H_A, :]
        m_out[0] = m_s[0:H_A, :]


def _split_dot_left(a_bf16, b):
    hi = b.astype(BF16)
    lo = (b - hi.astype(F32)).astype(BF16)
    return (jnp.dot(a_bf16, hi, preferred_element_type=F32)
            + jnp.dot(a_bf16, lo, preferred_element_type=F32))


def _mix_prompt(h_in, h_out, n_batch, seq, lw):
    tl = max(t for t in range(CHUNK, TL_MIX + 1, CHUNK) if seq % (2 * t) == 0)
    nj = seq // (2 * tl)
    last_tile = n_batch * seq // tl - 1
    const = lambda b, j: (0, 0)
    once = pl.Buffered(1)
    kern = functools.partial(_mix_prompt_kernel, tl=tl)
    return pl.pallas_call(
        kern,
        name="mix_prompt",
        grid=(n_batch, nj),
        in_specs=[
            pl.BlockSpec((2 * tl, D_MODEL), lambda b, j: (b * nj + j, 0)),
            pl.BlockSpec((tl, D_MODEL), lambda b, j: (jnp.minimum(2 * (b * nj + j) + 2, last_tile), 0)),
            pl.BlockSpec(memory_space=pl.ANY),
            pl.BlockSpec((1, D_MODEL), const),
            pl.BlockSpec((D_MODEL, P_W), const, pipeline_mode=once),
            pl.BlockSpec((CONV_W, QK_W), const),
            pl.BlockSpec((1, QK_W), const),
            pl.BlockSpec((1, LANES), const),
            pl.BlockSpec((1, W_A), const),
            pl.BlockSpec((1, W_B), const),
            pl.BlockSpec((G_B, CHUNK, CHUNK), lambda b, j: (0, 0, 0)),
            pl.BlockSpec((CHUNK, LANES), const),
            pl.BlockSpec((D_MODEL, D_MODEL), const, pipeline_mode=once),
        ],
        out_specs=[
            pl.BlockSpec((2 * tl, D_MODEL), lambda b, j: (b * nj + j, 0)),
            pl.BlockSpec((1, H_A, DH_A, DH_A), lambda b, j: (b, 0, 0, 0)),
            pl.BlockSpec((1, H_A, DH_A), lambda b, j: (b, 0, 0)),
            pl.BlockSpec((1, H_A, LANES), lambda b, j: (b, 0, 0)),
            pl.BlockSpec((1, CONV_W - 1, QK_W), lambda b, j: (b, 0, 0)),
        ],
        out_shape=[
            jax.ShapeDtypeStruct(h_out.shape, F32),
            jax.ShapeDtypeStruct((n_batch, H_A, DH_A, DH_A), F32),
            jax.ShapeDtypeStruct((n_batch, H_A, DH_A), F32),
            jax.ShapeDtypeStruct((n_batch, H_A, LANES), F32),
            jax.ShapeDtypeStruct((n_batch, CONV_W - 1, QK_W), F32),
        ],
        scratch_shapes=[
            pltpu.VMEM((tl, P_W), F32),
            pltpu.VMEM((tl, P_W), F32),
            pltpu.VMEM((SUBLANES + tl, QK_W), F32),
            pltpu.VMEM((tl, D_MODEL), BF16),
            pltpu.VMEM((H_A, DH_A, DH_A), F32),
            pltpu.VMEM((SUBLANES, LANES), F32),
            pltpu.VMEM((SUBLANES, LANES), F32),
        ],
        input_output_aliases={2: 0},
        compiler_params=_cparams("arbitrary", "arbitrary"),
    )(h_in, h_in, h_out, lw["g_mix"], lw["w_in"], lw["conv_w"], lw["conv_b"], lw["gate_b"], lw["hn_g"], lw["vn_g"],
      lw["w_s"], lw["b_s_col"], lw["w_out"])


def _mix_sample_kernel(x_ref, hprev_ref, cprev_ref, gmix_ref, win_ref, cw_ref, cb_ref, gb_ref, hng_ref, vng_ref,
                       w00_ref, bs0_ref, wout_ref, sconv_ref, c_in, n_in, m_in,
                       out_ref, c_out, n_out, m_out, conv_out, vrow_out,
                       proj, hcat, *, bt):
    del hprev_ref, cprev_ref
    i = pl.program_id(0)
    ni = pl.num_programs(0)

    @pl.when(i == 0)
    def _():
        x = x_ref[...]
        xn = _rms(x, gmix_ref[...]).astype(BF16)
        proj[...] = jnp.dot(xn, win_ref[...], preferred_element_type=F32)
        pre = proj[:, P_QK:P_QK + QK_W]
        qk = cb_ref[...] + cw_ref[CONV_W - 1:CONV_W, :] * pre
        for jj in range(CONV_W - 1):
            qk = qk + cw_ref[jj:jj + 1, :] * sconv_ref[jj]
        for jj in range(CONV_W - 2):
            conv_out[jj] = sconv_ref[jj + 1]
        conv_out[CONV_W - 2] = pre
        qk = qk * jax.nn.sigmoid(qk)
        proj[:, 0:W_A] = qk[:, 0:W_A]
        proj[:, W_A:QK_W] = qk[:, W_A:QK_W] * (DH_A ** -0.5)
        for g in range(G_B):
            sl = slice(g * DG_B, (g + 1) * DG_B)
            vb = proj[:, P_VB + g * DG_B:P_VB + (g + 1) * DG_B]
            u = proj[:, P_U + g * DG_B:P_U + (g + 1) * DG_B]
            vn = _rms(jax.nn.gelu(vb), vng_ref[:, sl])
            vrow_out[:, sl] = vn
            z = w00_ref[:, sl] * vn + bs0_ref[:, sl]
            hcat[:, W_A + g * DG_B:W_A + (g + 1) * DG_B] = jax.nn.gelu(u) * z

    rows = pl.ds(pl.multiple_of(i * bt, bt), bt)
    gates = proj[rows, P_G:P_G + LANES] + gb_ref[...]
    lane = lax.broadcasted_iota(jnp.int32, (bt, LANES), 1)
    sub = lax.broadcasted_iota(jnp.int32, (bt, DH_A), 0)
    m_new_all = jnp.zeros((bt, LANES), F32)
    for h in range(H_A):
        ig = gates[:, h:h + 1]
        lf = _log_sigmoid(gates[:, H_A + h:H_A + h + 1])
        m_prev = m_in[rows, h:h + 1]
        inter = lf + m_prev
        m_t = jnp.maximum(inter, ig)
        w_inter = jnp.exp(inter - m_t)
        e_d = jnp.exp(ig - m_t)
        q = proj[rows, h * DH_A:(h + 1) * DH_A]
        k = proj[rows, W_A + h * DH_A:W_A + (h + 1) * DH_A]
        v = proj[rows, P_V + h * DH_A:P_V + (h + 1) * DH_A]
        n_old = n_in[h]
        s = jnp.sum(q * k, axis=-1, keepdims=True) * e_d
        vw = v * e_d
        vw_t = jnp.concatenate([vw, jnp.zeros((DH_A - bt, DH_A), F32)], axis=0).T
        qb = q.astype(BF16)
        qc = jnp.zeros((bt, DH_A), F32)
        for t in range(bt):
            c_old = c_in[t, h]
            r = _dot_nt(qb, c_old.astype(BF16))
            qc = jnp.where(sub == t, r, qc)
            c_out[t, h] = w_inter[t:t + 1, :] * c_old + vw_t[:, t:t + 1] * k[t:t + 1, :]
        num = w_inter * qc + s * v
        nq = w_inter * jnp.sum(q * n_old, axis=-1, keepdims=True) + s
        hh = num / jnp.maximum(jnp.abs(nq), jnp.exp(-m_t))
        n_out[h] = w_inter * n_old + e_d * k
        m_new_all = jnp.where(lane == h, m_t, m_new_all)
        ha = _rms(hh, hng_ref[:, h * DH_A:(h + 1) * DH_A])
        o = proj[rows, P_O + h * DH_A:P_O + (h + 1) * DH_A]
        hcat[rows, h * DH_A:(h + 1) * DH_A] = ha * jax.nn.sigmoid(o)
    m_out[rows, :] = m_new_all

    @pl.when(i == ni - 1)
    def _():
        out_ref[...] = x_ref[...] + jnp.dot(hcat[...].astype(BF16), wout_ref[...],
                                            preferred_element_type=F32)


def _mix_sample(h_in, h_out, c_all, lw, sconv_t, c0, n0_t, m0_pad, n_s, layer):
    bt = BT_SAMPLE
    ni = n_s // bt
    const = lambda i: (0, 0)
    in_blk = h_in.shape[0] // n_s - 1
    out_blk = h_out.shape[0] // n_s - 1
    kern = functools.partial(_mix_sample_kernel, bt=bt)
    return pl.pallas_call(
        kern,
        name="mix_sample",
        grid=(ni,),
        in_specs=[
            pl.BlockSpec((n_s, D_MODEL), lambda i: (in_blk, 0)),
            pl.BlockSpec(memory_space=pl.ANY),
            pl.BlockSpec(memory_space=pl.ANY),
            pl.BlockSpec((1, D_MODEL), const),
            pl.BlockSpec((D_MODEL, P_W), const),
            pl.BlockSpec((CONV_W, QK_W), const),
            pl.BlockSpec((1, QK_W), const),
            pl.BlockSpec((1, LANES), const),
            pl.BlockSpec((1, W_A), const),
            pl.BlockSpec((1, W_B), const),
            pl.BlockSpec((1, W_B), const),
            pl.BlockSpec((1, W_B), const),
            pl.BlockSpec((D_MODEL, D_MODEL), const),
            pl.BlockSpec((None, CONV_W - 1, n_s, QK_W), lambda i: (layer, 0, 0, 0)),
            pl.BlockSpec((None, bt, H_A, DH_A, DH_A), lambda i: (layer, i, 0, 0, 0)),
            pl.BlockSpec((None, H_A, bt, DH_A), lambda i: (layer, 0, i, 0)),
            pl.BlockSpec((None, n_s, LANES), lambda i: (layer, 0, 0)),
        ],
        out_specs=[
            pl.BlockSpec((n_s, D_MODEL), lambda i: (out_blk, 0)),
            pl.BlockSpec((None, bt, H_A, DH_A, DH_A), lambda i: (layer, i, 0, 0, 0)),
            pl.BlockSpec((H_A, bt, DH_A), lambda i: (0, i, 0)),
            pl.BlockSpec((n_s, LANES), const),
            pl.BlockSpec((CONV_W - 1, n_s, QK_W), lambda i: (0, 0, 0)),
            pl.BlockSpec((n_s, W_B), const),
        ],
        out_shape=[
            jax.ShapeDtypeStruct(h_out.shape, F32),
            jax.ShapeDtypeStruct(c_all.shape, F32),
            jax.ShapeDtypeStruct(n0_t.shape[1:], F32),
            jax.ShapeDtypeStruct((n_s, LANES), F32),
            jax.ShapeDtypeStruct((CONV_W - 1, n_s, QK_W), F32),
            jax.ShapeDtypeStruct((n_s, W_B), F32),
        ],
        scratch_shapes=[
            pltpu.VMEM((n_s, P_W), F32),
            pltpu.VMEM((n_s, D_MODEL), F32),
        ],
        input_output_aliases={1: 0, 2: 1},
        compiler_params=_cparams("arbitrary"),
    )(h_in, h_out, c_all, lw["g_mix"], lw["w_in"], lw["conv_w"], lw["conv_b"], lw["gate_b"], lw["hn_g"], lw["vn_g"],
      lw["w00_row"], lw["bs0_row"], lw["w_out"], sconv_t, c0, n0_t, m0_pad)


def _router_kernel(h_ref, g_ref, wr_ref, b_ref, route_ref, route_t_ref, tcnt_ref, *, tm):
    hn = _rms(h_ref[...], g_ref[...]).astype(BF16)
    logits = jnp.dot(hn, wr_ref[...], preferred_element_type=F32) + b_ref[...]

    lane_i = lax.broadcasted_iota(jnp.int32, (tm, LANES), 1)
    lane = lane_i.astype(F32)
    neg = -jnp.inf
    is_g = lane_i < N_GROUPS
    gl = jnp.where(is_g, logits, neg)
    gmax = jnp.max(gl, axis=-1, keepdims=True)
    g_sel = jnp.min(jnp.where(gl == gmax, lane, float(LANES)), axis=-1, keepdims=True)
    g_w = 1.0 / jnp.sum(jnp.where(is_g, jnp.exp(logits - gmax), 0.0), axis=-1, keepdims=True)

    e_lane = lane_i - N_GROUPS
    lane_grp = (e_lane >> 3).astype(F32)
    in_grp = (e_lane >= 0) & (e_lane < N_EXPERTS) & (lane_grp == g_sel)
    el = jnp.where(in_grp, logits, neg)
    v1 = jnp.max(el, axis=-1, keepdims=True)
    i1 = jnp.min(jnp.where(el == v1, lane, float(LANES)), axis=-1, keepdims=True)
    el2 = jnp.where(lane == i1, neg, el)
    v2 = jnp.max(el2, axis=-1, keepdims=True)
    i2 = jnp.min(jnp.where(el2 == v2, lane, float(LANES)), axis=-1, keepdims=True)
    e21 = jnp.exp(v2 - v1)
    den = 1.0 + e21
    w1 = g_w * (1.0 / den)
    w2 = g_w * (e21 / den)

    onehot = jnp.where((lane == i1) | (lane == i2), 1.0, 0.0)
    r_i = lax.broadcasted_iota(jnp.int32, (tm, tm), 0)
    c_i = lax.broadcasted_iota(jnp.int32, (tm, tm), 1)
    strict = jnp.where(r_i > c_i, 1.0, 0.0).astype(BF16)
    before = jnp.dot(strict, onehot.astype(BF16), preferred_element_type=F32)
    cnt = jnp.sum(onehot, axis=0, keepdims=True)
    cnt8 = jnp.floor((cnt + (SUBLANES - 1.0)) * (1.0 / SUBLANES)) * SUBLANES
    l_r = lax.broadcasted_iota(jnp.int32, (LANES, LANES), 0)
    l_c = lax.broadcasted_iota(jnp.int32, (LANES, LANES), 1)
    lanes_before = jnp.where(l_r < l_c, 1.0, 0.0).astype(BF16)
    start = _split_dot(jnp.broadcast_to(cnt8, (2 * SUBLANES, LANES)), lanes_before)[0:1, :]
    slot = start + before
    pos1 = jnp.sum(jnp.where(lane == i1, slot, 0.0), axis=-1, keepdims=True)
    pos2 = jnp.sum(jnp.where(lane == i2, slot, 0.0), axis=-1, keepdims=True)
    tcnt_ref[...] = jnp.broadcast_to(cnt8, tcnt_ref.shape)

    out = jnp.zeros((tm, LANES), F32)
    for idx, val in enumerate((pos1, pos2, w1, w2)):
        out = jnp.where(lane_i == idx, val, out)
    route_ref[...] = out
    route_t_ref[...] = out.T[0:SUBLANES, :]


def _router(h1, lw):
    t_total = h1.shape[0]
    tm = _token_tile(t_total)
    n_tiles = t_total // tm
    const = lambda i: (0, 0)
    return pl.pallas_call(
        functools.partial(_router_kernel, tm=tm),
        name="router",
        grid=(n_tiles,),
        in_specs=[
            pl.BlockSpec((tm, D_MODEL), lambda i: (i, 0)),
            pl.BlockSpec((1, D_MODEL), const),
            pl.BlockSpec((D_MODEL, LANES), const),
            pl.BlockSpec((1, LANES), const),
        ],
        out_specs=[
            pl.BlockSpec((tm, LANES), lambda i: (i, 0)),
            pl.BlockSpec((SUBLANES, tm), lambda i: (0, i)),
            pl.BlockSpec((None, SUBLANES, LANES), lambda i: (i, 0, 0)),
        ],
        out_shape=[
            jax.ShapeDtypeStruct((t_total, LANES), F32),
            jax.ShapeDtypeStruct((SUBLANES, t_total), F32),
            jax.ShapeDtypeStruct((n_tiles, SUBLANES, LANES), F32),
        ],
        compiler_params=_cparams("arbitrary"),
    )(h1, lw["g_ffn"], lw["wr"], lw["br"])


def _tile_slots(tm):
    raw = tm * TOP_K + N_EXPERTS * (SUBLANES - 1)
    return -(-raw // LANES) * LANES


def _token_tile(t_total):
    tm = TM_TOK
    while t_total % tm:
        tm //= 2
    return tm


def _run_copies(tile, udst_ref, tunits_ref, units_per_tile, make_copy):
    base = tile * units_per_tile

    def per_unit(u, carry):
        make_copy(pl.multiple_of(u * SUBLANES, SUBLANES),
                  pl.multiple_of(udst_ref[base + u], SUBLANES)).start()
        return carry

    lax.fori_loop(0, tunits_ref[tile], per_unit, 0)


def _drain_copies(n_units, make_copy):
    def wait_one(u, carry):
        make_copy(0, 0).wait()
        return carry

    lax.fori_loop(0, n_units, wait_one, 0)


def _dispatch_kernel(udst_ref, tunits_ref, h_ref, g_ref, rt_ref, xs_in, xs_out,
                     srt, sem, *, tm, slots):
    del xs_in
    i = pl.program_id(0)
    n = pl.num_programs(0)
    buf = lax.rem(i, 2)

    def copy_maker(b):
        def make_copy(tile_row, sorted_row):
            return pltpu.make_async_copy(srt.at[b, pl.ds(tile_row, SUBLANES)],
                                         xs_out.at[pl.ds(sorted_row, SUBLANES), pl.ds(0, HALF_D)], sem.at[b])
        return make_copy

    @pl.when(i >= 2)
    def _():
        _drain_copies(tunits_ref[i - 2], copy_maker(buf))

    xn = _rms(h_ref[...], g_ref[...]).astype(BF16)
    slot_id = lax.broadcasted_iota(jnp.int32, (slots, tm), 0).astype(F32)
    sel = (slot_id == rt_ref[0:1, :]) | (slot_id == rt_ref[1:2, :])
    srt[buf] = _pack_rows(jnp.dot(jnp.where(sel, 1.0, 0.0).astype(BF16), xn, preferred_element_type=F32))
    _run_copies(i, udst_ref, tunits_ref, slots // SUBLANES, copy_maker(buf))

    @pl.when(i == n - 1)
    def _():
        @pl.when(i >= 1)
        def _():
            _drain_copies(tunits_ref[i - 1], copy_maker(1 - buf))

        _drain_copies(tunits_ref[i], copy_maker(buf))


def _dispatch(tables, h1, route_t, lw, xs):
    t_total = h1.shape[0]
    tm = _token_tile(t_total)
    slots = _tile_slots(tm)
    return pl.pallas_call(
        functools.partial(_dispatch_kernel, tm=tm, slots=slots),
        name="dispatch",
        grid_spec=pltpu.PrefetchScalarGridSpec(
            num_scalar_prefetch=2,
            grid=(t_total // tm,),
            in_specs=[
                pl.BlockSpec((tm, D_MODEL), lambda i, *_: (i, 0)),
                pl.BlockSpec((1, D_MODEL), lambda i, *_: (0, 0)),
                pl.BlockSpec((SUBLANES, tm), lambda i, *_: (0, i)),
                pl.BlockSpec(memory_space=pl.ANY),
            ],
            out_specs=pl.BlockSpec(memory_space=pl.ANY),
            scratch_shapes=[pltpu.VMEM((2, slots, HALF_D), U32), pltpu.SemaphoreType.DMA((2,))],
        ),
        out_shape=jax.ShapeDtypeStruct(xs.shape, U32),
        input_output_aliases={5: 0},
        compiler_params=_cparams("arbitrary"),
    )(*tables, h1, lw["g_ffn"], route_t, xs)


def _ffn_kernel(bstart_ref, nblk_ref, nused_ref, w1_ref, w3_ref, w2_ref, xs_ref, ys_ref,
                w1b, w3b, w2b, xbuf, ybuf, sem_in, sem_out):
    e = pl.program_id(0)
    g0 = bstart_ref[e]
    nb = nblk_ref[e]
    n_used = nused_ref[0]

    def in_copy(g):
        slot = lax.rem(g, FFN_RING)
        r = pl.multiple_of(g * FFN_BLK, FFN_BLK)
        return pltpu.make_async_copy(xs_ref.at[pl.ds(r, FFN_BLK), pl.ds(0, HALF_D)], xbuf.at[slot],
                                     sem_in.at[slot])

    def out_copy(g):
        slot = lax.rem(g, FFN_RING)
        r = pl.multiple_of(g * FFN_BLK, FFN_BLK)
        return pltpu.make_async_copy(ybuf.at[slot], ys_ref.at[pl.ds(r, FFN_BLK)], sem_out.at[slot])

    @pl.when(e == 0)
    def _():
        for ahead in range(FFN_RING - 1):
            @pl.when(ahead < n_used)
            def _():
                in_copy(jnp.int32(ahead)).start()

    @pl.when(nb > 0)
    def _():
        w1b[...] = w1_ref[...].astype(BF16)
        w3b[...] = w3_ref[...].astype(BF16)
        w2b[...] = w2_ref[...].astype(BF16)

    def block_step(k, carry):
        g = g0 + k
        slot = lax.rem(g, FFN_RING)
        in_copy(g).wait()

        @pl.when(g + (FFN_RING - 1) < n_used)
        def _():
            in_copy(g + (FFN_RING - 1)).start()

        @pl.when(g >= FFN_RING)
        def _():
            out_copy(g - FFN_RING).wait()

        x_lo, x_hi = _unpack_rows(xbuf[slot])
        a = (jnp.dot(x_lo, w1b[0:HALF_D, :], preferred_element_type=F32)
             + jnp.dot(x_hi, w1b[HALF_D:, :], preferred_element_type=F32))
        b = (jnp.dot(x_lo, w3b[0:HALF_D, :], preferred_element_type=F32)
             + jnp.dot(x_hi, w3b[HALF_D:, :], preferred_element_type=F32))
        hmid = (a * jax.nn.sigmoid(a) * b).astype(BF16)
        y = jnp.dot(hmid, w2b[...], preferred_element_type=F32)
        ybuf[slot] = lax.bitcast_convert_type(y, U32)
        out_copy(g).start()
        return carry

    lax.fori_loop(0, nb, block_step, 0)

    @pl.when(e == pl.num_programs(0) - 1)
    def _():
        for back in range(FFN_RING, 0, -1):
            @pl.when(n_used >= back)
            def _():
                out_copy(n_used - back).wait()


def _ffn(ffn_tables, xs, w1, w3, w2, layer):
    def w_map(e, *_):
        return (layer, e, 0, 0)

    return pl.pallas_call(
        _ffn_kernel,
        name="expert_ffn",
        grid_spec=pltpu.PrefetchScalarGridSpec(
            num_scalar_prefetch=3,
            grid=(N_EXPERTS,),
            in_specs=[
                pl.BlockSpec((None, None, D_MODEL, D_FF), w_map),
                pl.BlockSpec((None, None, D_MODEL, D_FF), w_map),
                pl.BlockSpec((None, None, D_FF, D_MODEL), w_map),
                pl.BlockSpec(memory_space=pl.ANY),
            ],
            out_specs=pl.BlockSpec(memory_space=pl.ANY),
            scratch_shapes=[
                pltpu.VMEM((D_MODEL, D_FF), BF16),
                pltpu.VMEM((D_MODEL, D_FF), BF16),
                pltpu.VMEM((D_FF, D_MODEL), BF16),
                pltpu.VMEM((FFN_RING, FFN_BLK, HALF_D), U32),
                pltpu.VMEM((FFN_RING, FFN_BLK, D_MODEL), U32),
                pltpu.SemaphoreType.DMA((FFN_RING,)),
                pltpu.SemaphoreType.DMA((FFN_RING,)),
            ],
        ),
        out_shape=jax.ShapeDtypeStruct(xs.shape, U32),
        input_output_aliases={6: 0},
        compiler_params=_cparams("arbitrary"),
    )(*ffn_tables, w1, w3, w2, xs)


def _ple_kernel(udst_ref, tunits_ref, h_ref, route_ref, p_ref, ps_ref, wpg_ref,
                wple_ref, pg_ref, fg_ref, ys_ref, *rest, tm, slots, n_s, final):
    if final:
        yp_ref, ysm_ref, ysrt, sem = rest
    else:
        out_ref, ysrt, sem = rest
    i = pl.program_id(0)
    n = pl.num_programs(0)
    buf = lax.rem(i, 2)

    def copy_maker(b):
        def make_copy(tile_row, sorted_row):
            return pltpu.make_async_copy(ys_ref.at[pl.ds(sorted_row, SUBLANES)],
                                         ysrt.at[b, pl.ds(tile_row, SUBLANES)], sem.at[b])
        return make_copy

    @pl.when(i == 0)
    def _():
        ysrt[...] = jnp.zeros_like(ysrt)
        _run_copies(i, udst_ref, tunits_ref, slots // SUBLANES, copy_maker(buf))

    @pl.when(i + 1 < n)
    def _():
        _run_copies(i + 1, udst_ref, tunits_ref, slots // SUBLANES, copy_maker(1 - buf))

    p = p_ref[...]
    p_last = jnp.concatenate([p[0:tm - n_s, :], ps_ref[...]], axis=0)
    p = jnp.where(i == n - 1, p_last, p)
    pe = _rms(jnp.dot(p.astype(BF16), wple_ref[...], preferred_element_type=F32), pg_ref[...])
    _drain_copies(tunits_ref[i], copy_maker(buf))

    route = route_ref[...]
    y = lax.bitcast_convert_type(ysrt[buf], F32)
    y_head = y.astype(BF16)
    y_rem = (y - y_head.astype(F32)).astype(BF16)
    slot_id = lax.broadcasted_iota(jnp.int32, (tm, slots), 1).astype(F32)
    moe = None
    weighted_pick = jnp.zeros((tm, slots), F32)
    for kk in range(TOP_K):
        hit = slot_id == route[:, kk:kk + 1]
        w_kk = route[:, TOP_K + kk:TOP_K + kk + 1]
        term = w_kk * jnp.dot(jnp.where(hit, 1.0, 0.0).astype(BF16), y_head, preferred_element_type=F32)
        moe = term if moe is None else moe + term
        weighted_pick = jnp.where(hit, w_kk, weighted_pick)
    moe = moe + jnp.dot(weighted_pick.astype(BF16), y_rem, preferred_element_type=F32)
    h2 = h_ref[...] + moe
    gate = jax.nn.sigmoid(jnp.dot(h2.astype(BF16), wpg_ref[...], preferred_element_type=F32))
    out = h2 + gate * pe
    if final:
        fin = _rms(out, fg_ref[...])
        yp_ref[...] = fin

        @pl.when(i == pl.num_programs(0) - 1)
        def _():
            ysm_ref[...] = fin[tm - n_s:tm, :]
    else:
        out_ref[...] = out


def _ple(tables, h1, route, p_prompt, p_sample, lw, final_g, ys, n_s, layer, final, in_place):
    t_total = h1.shape[0]
    tm = _token_tile(t_total)
    slots = _tile_slots(tm)
    assert n_s <= tm and (t_total - n_s) % tm == tm - n_s
    const = lambda i, *_: (0, 0)
    tok = lambda i, *_: (i, 0)
    if final:
        out_specs = [pl.BlockSpec((tm, D_MODEL), tok), pl.BlockSpec((n_s, D_MODEL), const)]
        out_shape = [jax.ShapeDtypeStruct((t_total - n_s, D_MODEL), F32),
                     jax.ShapeDtypeStruct((n_s, D_MODEL), F32)]
        aliases = {}
    else:
        out_specs = [pl.BlockSpec((tm, D_MODEL), tok)]
        out_shape = [jax.ShapeDtypeStruct((t_total, D_MODEL), F32)]
        aliases = {2: 0} if in_place else {}
    return pl.pallas_call(
        functools.partial(_ple_kernel, tm=tm, slots=slots, n_s=n_s, final=final),
        name="combine_ple",
        grid_spec=pltpu.PrefetchScalarGridSpec(
            num_scalar_prefetch=2,
            grid=(t_total // tm,),
            in_specs=[
                pl.BlockSpec((tm, D_MODEL), tok),
                pl.BlockSpec((tm, LANES), tok),
                pl.BlockSpec((None, tm, PLE_DIM), lambda i, *_: (layer, i, 0)),
                pl.BlockSpec((None, n_s, PLE_DIM), lambda i, *_: (layer, 0, 0)),
                pl.BlockSpec((D_MODEL, D_MODEL), const),
                pl.BlockSpec((PLE_DIM, D_MODEL), const),
                pl.BlockSpec((1, D_MODEL), const),
                pl.BlockSpec((1, D_MODEL), const),
                pl.BlockSpec(memory_space=pl.ANY),
            ],
            out_specs=out_specs,
            scratch_shapes=[pltpu.VMEM((2, slots, D_MODEL), U32), pltpu.SemaphoreType.DMA((2,))],
        ),
        out_shape=out_shape,
        input_output_aliases=aliases,
        compiler_params=_cparams("arbitrary"),
    )(*tables, h1, route, p_prompt, p_sample, lw["w_pg"], lw["w_ple"], lw["ple_g"], final_g, ys)


def _layer_weights(l, norm_mix_g, w_in, conv_w, conv_b, b_igate, b_fgate, hnorm_g, vnorm_g, w_s, b_s, w_out,
                   norm_ffn_g, router_g_w, router_g_b, router_e_w, router_e_b, w_ple, ple_norm_g, w_ple_gate):
    wi = w_in[l]
    a0 = QK_W + 2 * W_A
    gates_w = jnp.pad(wi[:, a0:a0 + 2 * H_A], ((0, 0), (0, LANES - 2 * H_A)))
    w_packed = jnp.concatenate(
        [wi[:, :a0], wi[:, a0 + 2 * H_A:], gates_w], axis=1).astype(BF16)
    gate_b = jnp.pad(jnp.concatenate([b_igate[l], b_fgate[l]]), (0, LANES - 2 * H_A))[None, :]
    wr = jnp.pad(jnp.concatenate([router_g_w[l], router_e_w[l]], axis=1),
                 ((0, 0), (0, LANES - N_GROUPS - N_EXPERTS))).astype(BF16)
    br = jnp.pad(jnp.concatenate([router_g_b[l], router_e_b[l]]), (0, LANES - N_GROUPS - N_EXPERTS))[None, :]
    return {
        "g_mix": norm_mix_g[l][None, :],
        "w_in": w_packed,
        "conv_w": conv_w[l],
        "conv_b": conv_b[l][None, :],
        "gate_b": gate_b,
        "hn_g": hnorm_g[l][None, :],
        "vn_g": vnorm_g[l][None, :],
        "w_s": w_s[l],
        "b_s_col": jnp.pad(b_s[l].T, ((0, 0), (0, LANES - G_B))),
        "w00_row": jnp.repeat(w_s[l][:, 0, 0], DG_B)[None, :],
        "bs0_row": jnp.repeat(b_s[l][:, 0], DG_B)[None, :],
        "w_out": w_out[l].astype(BF16),
        "g_ffn": norm_ffn_g[l][None, :],
        "wr": wr,
        "br": br,
        "w_pg": w_ple_gate[l].astype(BF16),
        "w_ple": w_ple[l].astype(BF16),
        "ple_g": ple_norm_g[l][None, :],
    }


def _routing_tables(tcnt, units_per_tile):
    c8 = tcnt[:, 0, N_GROUPS:N_GROUPS + N_EXPERTS].astype(jnp.int32)
    blocks = (jnp.sum(c8, axis=0) + FFN_BLK - 1) // FFN_BLK
    blk_end = jnp.cumsum(blocks)
    pstart = (blk_end - blocks) * FFN_BLK
    run_dst = pstart[None, :] + jnp.cumsum(c8, axis=0) - c8
    run_src = jnp.cumsum(c8, axis=1) - c8
    slot = (jnp.arange(units_per_tile, dtype=jnp.int32) * SUBLANES)[None, :, None]
    in_run = (slot >= run_src[:, None, :]) & (slot < (run_src + c8)[:, None, :])
    unit_dst = jnp.sum(jnp.where(in_run, run_dst[:, None, :] + slot - run_src[:, None, :], 0), axis=2)
    tables = (unit_dst.reshape(-1), jnp.sum(c8, axis=1) // SUBLANES)
    ffn_tables = (blk_end - blocks, blocks, blk_end[-1:])
    return tables, ffn_tables


def kernel(x_prompt, x_sample, state_C, state_n, state_m, state_conv, p_prompt, p_sample, norm_mix_g, w_in, conv_w, conv_b, b_igate, b_fgate, hnorm_g, vnorm_g, w_s, b_s, w_out, norm_ffn_g, router_g_w, router_g_b, router_e_w, router_e_b, w1, w3, w2, w_ple, ple_norm_g, w_ple_gate, final_norm_g):
    n_batch, seq, _ = x_prompt.shape
    n_s = x_sample.shape[0]
    depth = w_in.shape[0]
    t_p = n_batch * seq
    t_total = t_p + n_s
    assert seq % CHUNK == 0 and t_p % n_s == 0 and n_s % BT_SAMPLE == 0

    n_tiles = t_total // _token_tile(t_total)
    max_rows = t_total * TOP_K + N_EXPERTS * ((SUBLANES - 1) * n_tiles + FFN_BLK - 1)
    n_blocks = max_rows // FFN_BLK
    xs = jnp.zeros((n_blocks * FFN_BLK, D_MODEL), U32)

    p_prompt_t = p_prompt.reshape(depth, t_p, PLE_DIM)
    p_sample_t = p_sample.reshape(depth, n_s, PLE_DIM)
    sconv_t = jnp.transpose(state_conv, (0, 2, 1, 3))
    n0_t = jnp.transpose(state_n, (0, 2, 1, 3))
    m0_pad = jnp.pad(state_m, ((0, 0), (0, 0), (0, LANES - H_A)))
    final_g = final_norm_g[None, :]

    src_p = x_prompt.reshape(t_p, D_MODEL)
    src_s = x_sample.reshape(n_s, D_MODEL)
    spare = jnp.zeros((t_total, D_MODEL), F32)
    c_all = jnp.zeros(state_C.shape, F32)
    outs = {k: [] for k in ("Cp", "np", "mp", "cbp", "ns", "ms", "cbs", "vs")}
    res = None
    for l in range(depth):
        lw = _layer_weights(l, norm_mix_g, w_in, conv_w, conv_b, b_igate, b_fgate, hnorm_g, vnorm_g, w_s, b_s,
                            w_out, norm_ffn_g, router_g_w, router_g_b, router_e_w, router_e_b, w_ple,
                            ple_norm_g, w_ple_gate)
        h_mix, c_p, n_p, m_p, cb_p = _mix_prompt(src_p, spare, n_batch, seq, lw)
        h_mix, c_all, n_s_t, m_s, cb_s, v_s = _mix_sample(src_s, h_mix, c_all, lw, sconv_t, state_C, n0_t,
                                                          m0_pad, n_s, l)
        route, route_t, tcnt = _router(h_mix, lw)
        tables, ffn_tables = _routing_tables(tcnt, _tile_slots(_token_tile(t_total)) // SUBLANES)
        xs = _dispatch(tables, h_mix, route_t, lw, xs)
        xs = _ffn(ffn_tables, xs, w1, w3, w2, l)
        final = l == depth - 1
        res = _ple(tables, h_mix, route, p_prompt_t, p_sample_t, lw, final_g, xs, n_s, l, final, in_place=l > 0)
        if l == 0:
            spare = h_mix
        else:
            spare = src_p
        src_p = src_s = res[0]
        outs["Cp"].append(c_p)
        outs["np"].append(n_p)
        outs["mp"].append(m_p[:, :, 0])
        outs["cbp"].append(cb_p)
        outs["ns"].append(jnp.transpose(n_s_t, (1, 0, 2)))
        outs["ms"].append(m_s[:, 0:H_A])
        outs["cbs"].append(jnp.transpose(cb_s, (1, 0, 2)))
        outs["vs"].append(v_s[:, None, :])

    y_prompt = res[0].reshape(n_batch, seq, D_MODEL)
    y_sample = res[1].reshape(n_s, 1, D_MODEL)
    st = lambda k: jnp.stack(outs[k])
    return (y_prompt, y_sample, st("Cp"), st("np"), st("mp"), st("cbp"),
            c_all, st("ns"), st("ms"), st("cbs"), st("vs"))
```

```python
import functools

import jax
import jax.numpy as jnp
from jax import lax
from jax.experimental import pallas as pl
from jax.experimental.pallas import tpu as pltpu

F32 = jnp.float32
BF16 = jnp.bfloat16
U32 = jnp.uint32

D_MODEL = 1024
W_A = 512
H_A = 4
DH_A = 128
W_B = 512
G_B = 4
DG_B = 128
CHUNK = 128
CONV_W = 4
QK_W = 2 * W_A
N_GROUPS = 4
EXPERTS_PER_GROUP = 8
N_EXPERTS = N_GROUPS * EXPERTS_PER_GROUP
TOP_K = 2
D_FF = 512
PLE_DIM = 256
EPS = 1e-6
HALF_D = D_MODEL // 2

LANES = 128
SUBLANES = 8
VMEM_LIMIT_BYTES = 56 * 1024 * 1024

P_QK = 0
P_V = QK_W
P_O = P_V + W_A
P_U = P_O + W_A
P_VB = P_U + W_B
P_G = P_VB + W_B
P_W = P_G + LANES

TL_MIX = 512
BT_SAMPLE = 8
TM_TOK = 384
FFN_BLK = 256
FFN_RING = 4
DRAIN_GROUP = 16


def _cparams(*sem):
    return pltpu.CompilerParams(dimension_semantics=sem, vmem_limit_bytes=VMEM_LIMIT_BYTES)


def _rms(x, g):
    return x * lax.rsqrt(jnp.mean(x * x, axis=-1, keepdims=True) + EPS) * g


def _log_sigmoid(x):
    return -(jnp.maximum(-x, 0.0) + jnp.log1p(jnp.exp(-jnp.abs(x))))


def _split_dot(a, b_bf16):
    hi = a.astype(BF16)
    lo = (a - hi.astype(F32)).astype(BF16)
    return (jnp.dot(hi, b_bf16, preferred_element_type=F32)
            + jnp.dot(lo, b_bf16, preferred_element_type=F32))


def _dot_nt(a, b):
    return lax.dot_general(a, b, (((1,), (1,)), ((), ())), preferred_element_type=F32)


def _pack_rows(x):
    lo = lax.bitcast_convert_type(x[:, :HALF_D], U32) >> 16
    hi = lax.bitcast_convert_type(x[:, HALF_D:], U32) & jnp.uint32(0xFFFF0000)
    return lo | hi


def _unpack_rows(u):
    lo = lax.bitcast_convert_type(u << 16, F32)
    hi = lax.bitcast_convert_type(u & jnp.uint32(0xFFFF0000), F32)
    return lo.astype(BF16), hi.astype(BF16)


def _mix_prompt_kernel(x_ref, xnext_ref, hprev_ref, gmix_ref, win_ref, cw_ref, cb_ref, gb_ref, hng_ref,
                       vng_ref, ws_ref, bs_ref, wout_ref,
                       out_ref, c_out, n_out, m_out, conv_out,
                       proj_a, proj_b, xpad, hcat, c_s, n_s, m_s, *, tl):
    del hprev_ref
    b = pl.program_id(0)
    j = pl.program_id(1)
    nj = pl.num_programs(1)
    n_chunks = tl // CHUNK
    bounds = [(P_W // LANES * c // n_chunks) * LANES for c in range(n_chunks + 1)]
    col_groups = [slice(bounds[c], bounds[c + 1]) for c in range(n_chunks)]

    @pl.when(j == 0)
    def _():
        c_s[...] = jnp.zeros_like(c_s)
        n_s[...] = jnp.zeros_like(n_s)
        m_s[...] = jnp.zeros_like(m_s)
        xpad[0:SUBLANES, :] = jnp.zeros((SUBLANES, QK_W), F32)

    @pl.when((b == 0) & (j == 0))
    def _():
        xn0 = _rms(x_ref[0:tl, :], gmix_ref[...]).astype(BF16)
        proj_a[...] = jnp.dot(xn0, win_ref[...], preferred_element_type=F32)

    row = lax.broadcasted_iota(jnp.int32, (CHUNK, CHUNK), 0)
    col = lax.broadcasted_iota(jnp.int32, (CHUNK, CHUNK), 1)
    causal = row >= col
    tril = jnp.where(causal, 1.0, 0.0).astype(BF16)
    triu = jnp.where(row <= col, 1.0, 0.0).astype(BF16)
    lane = lax.broadcasted_iota(jnp.int32, (CHUNK, LANES), 1)
    wm = [jnp.where(causal, ws_ref[g], 0.0).astype(BF16) for g in range(G_B)]

    def conv_step(proj):
        pre = proj[:, P_QK:P_QK + QK_W]
        xpad[SUBLANES:SUBLANES + tl, :] = pre
        qk = cb_ref[...] + cw_ref[CONV_W - 1:CONV_W, :] * pre
        for jj in range(1, CONV_W):
            qk = qk + cw_ref[CONV_W - 1 - jj:CONV_W - jj, :] * xpad[SUBLANES - jj:SUBLANES - jj + tl, :]
        xpad[SUBLANES - (CONV_W - 1):SUBLANES, :] = pre[tl - (CONV_W - 1):tl, :]
        qk = qk * jax.nn.sigmoid(qk)
        proj[:, 0:W_A] = qk[:, 0:W_A]
        proj[:, W_A:QK_W] = qk[:, W_A:QK_W] * (DH_A ** -0.5)

    def chunk_step(proj, c):
        rows = slice(c * CHUNK, (c + 1) * CHUNK)
        gl = proj[rows, P_G:P_G + LANES] + gb_ref[...]
        gl = jnp.where(lane >= H_A, _log_sigmoid(gl), gl)
        glt = gl.T
        b_col_all = _split_dot_left(tril, gl)
        b_row_all = _split_dot(glt[0:2 * SUBLANES, :], triu)

        for h in range(H_A):
            q32 = proj[rows, h * DH_A:(h + 1) * DH_A]
            k32 = proj[rows, W_A + h * DH_A:W_A + (h + 1) * DH_A]
            v32 = proj[rows, P_V + h * DH_A:P_V + (h + 1) * DH_A]
            q = q32.astype(BF16)
            k = k32.astype(BF16)
            ig_c = gl[:, h:h + 1]
            ig_r = glt[h:h + 1, :]
            b_c = b_col_all[:, H_A + h:H_A + h + 1]
            b_r = b_row_all[H_A + h:H_A + h + 1, :]
            c_old = c_s[h]
            n_old = n_s[h:h + 1, :]
            m_prev = m_s[h:h + 1, 0:1]

            d_log = jnp.where(causal, b_c - b_r + ig_r, -jnp.inf)
            inter = b_c + m_prev
            m_t = jnp.maximum(inter, jnp.max(d_log, axis=-1, keepdims=True))
            s = _dot_nt(q, k) * jnp.exp(d_log - m_t)
            w_inter = jnp.exp(inter - m_t)
            num = (w_inter * _dot_nt(q, c_old.astype(BF16))
                   + jnp.dot(s.astype(BF16), v32.astype(BF16), preferred_element_type=F32))
            nq = (w_inter * jnp.sum(q32 * n_old, axis=-1, keepdims=True)
                  + jnp.sum(s, axis=-1, keepdims=True))
            hh = num / jnp.maximum(jnp.abs(nq), jnp.exp(-m_t))

            m_new = m_t[CHUNK - 1:CHUNK, :]
            b_last = b_c[CHUNK - 1:CHUNK, :]
            w_state = jnp.exp(b_last - b_c + ig_c - m_new)
            decay = jnp.exp(b_last + m_prev - m_new)
            vw = (v32 * w_state).astype(BF16)
            c_s[h] = decay * c_old + lax.dot_general(
                vw, k, (((0,), (0,)), ((), ())), preferred_element_type=F32)
            n_s[h:h + 1, :] = decay * n_old + jnp.sum(w_state * k32, axis=0, keepdims=True)
            m_s[h:h + 1, :] = jnp.broadcast_to(m_new, (1, LANES))

            ha = _rms(hh, hng_ref[:, h * DH_A:(h + 1) * DH_A])
            o = proj[rows, P_O + h * DH_A:P_O + (h + 1) * DH_A]
            hcat[rows, h * DH_A:(h + 1) * DH_A] = (ha * jax.nn.sigmoid(o)).astype(BF16)

        for g in range(G_B):
            vb = proj[rows, P_VB + g * DG_B:P_VB + (g + 1) * DG_B]
            u = proj[rows, P_U + g * DG_B:P_U + (g + 1) * DG_B]
            vn = _rms(jax.nn.gelu(vb), vng_ref[:, g * DG_B:(g + 1) * DG_B])
            z = jnp.dot(wm[g], vn.astype(BF16), preferred_element_type=F32) + bs_ref[:, g:g + 1]
            hcat[rows, W_A + g * DG_B:W_A + (g + 1) * DG_B] = (jax.nn.gelu(u) * z).astype(BF16)

    def tile_pass(row0, proj_cur, proj_nxt, xn_next):
        conv_step(proj_cur)
        for c in range(n_chunks):
            chunk_step(proj_cur, c)
            cs = col_groups[c]
            proj_nxt[:, cs] = jnp.dot(xn_next, win_ref[:, cs], preferred_element_type=F32)
        out_ref[row0:row0 + tl, :] = x_ref[row0:row0 + tl, :] + jnp.dot(
            hcat[...], wout_ref[...], preferred_element_type=F32)

    tile_pass(0, proj_a, proj_b, _rms(x_ref[tl:2 * tl, :], gmix_ref[...]).astype(BF16))
    tile_pass(tl, proj_b, proj_a, _rms(xnext_ref[...], gmix_ref[...]).astype(BF16))

    @pl.when(j == nj - 1)
    def _():
        conv_out[0] = xpad[SUBLANES - (CONV_W - 1):SUBLANES, :]
        c_out[0] = c_s[...]
        n_out[0] = n_s[0:H_A, :]
        m_out[0] = m_s[0:H_A, :]


def _split_dot_left(a_bf16, b):
    hi = b.astype(BF16)
    lo = (b - hi.astype(F32)).astype(BF16)
    return (jnp.dot(a_bf16, hi, preferred_element_type=F32)
            + jnp.dot(a_bf16, lo, preferred_element_type=F32))


def _mix_prompt(h_in, h_out, n_batch, seq, lw):
    tl = max(t for t in range(CHUNK, TL_MIX + 1, CHUNK) if seq % (2 * t) == 0)
    nj = seq // (2 * tl)
    last_tile = n_batch * seq // tl - 1
    const = lambda b, j: (0, 0)
    once = pl.Buffered(1)
    kern = functools.partial(_mix_prompt_kernel, tl=tl)
    return pl.pallas_call(
        kern,
        name="mix_prompt",
        grid=(n_batch, nj),
        in_specs=[
            pl.BlockSpec((2 * tl, D_MODEL), lambda b, j: (b * nj + j, 0)),
            pl.BlockSpec((tl, D_MODEL), lambda b, j: (jnp.minimum(2 * (b * nj + j) + 2, last_tile), 0)),
            pl.BlockSpec(memory_space=pl.ANY),
            pl.BlockSpec((1, D_MODEL), const),
            pl.BlockSpec((D_MODEL, P_W), const, pipeline_mode=once),
            pl.BlockSpec((CONV_W, QK_W), const),
            pl.BlockSpec((1, QK_W), const),
            pl.BlockSpec((1, LANES), const),
            pl.BlockSpec((1, W_A), const),
            pl.BlockSpec((1, W_B), const),
            pl.BlockSpec((G_B, CHUNK, CHUNK), lambda b, j: (0, 0, 0)),
            pl.BlockSpec((CHUNK, LANES), const),
            pl.BlockSpec((D_MODEL, D_MODEL), const, pipeline_mode=once),
        ],
        out_specs=[
            pl.BlockSpec((2 * tl, D_MODEL), lambda b, j: (b * nj + j, 0)),
            pl.BlockSpec((1, H_A, DH_A, DH_A), lambda b, j: (b, 0, 0, 0)),
            pl.BlockSpec((1, H_A, DH_A), lambda b, j: (b, 0, 0)),
            pl.BlockSpec((1, H_A, LANES), lambda b, j: (b, 0, 0)),
            pl.BlockSpec((1, CONV_W - 1, QK_W), lambda b, j: (b, 0, 0)),
        ],
        out_shape=[
            jax.ShapeDtypeStruct(h_out.shape, F32),
            jax.ShapeDtypeStruct((n_batch, H_A, DH_A, DH_A), F32),
            jax.ShapeDtypeStruct((n_batch, H_A, DH_A), F32),
            jax.ShapeDtypeStruct((n_batch, H_A, LANES), F32),
            jax.ShapeDtypeStruct((n_batch, CONV_W - 1, QK_W), F32),
        ],
        scratch_shapes=[
            pltpu.VMEM((tl, P_W), F32),
            pltpu.VMEM((tl, P_W), F32),
            pltpu.VMEM((SUBLANES + tl, QK_W), F32),
            pltpu.VMEM((tl, D_MODEL), BF16),
            pltpu.VMEM((H_A, DH_A, DH_A), F32),
            pltpu.VMEM((SUBLANES, LANES), F32),
            pltpu.VMEM((SUBLANES, LANES), F32),
        ],
        input_output_aliases={2: 0},
        compiler_params=_cparams("arbitrary", "arbitrary"),
    )(h_in, h_in, h_out, lw["g_mix"], lw["w_in"], lw["conv_w"], lw["conv_b"], lw["gate_b"], lw["hn_g"], lw["vn_g"],
      lw["w_s"], lw["b_s_col"], lw["w_out"])


def _mix_sample_kernel(x_ref, hprev_ref, cprev_ref, gmix_ref, win_ref, cw_ref, cb_ref, gb_ref, hng_ref, vng_ref,
                       w00_ref, bs0_ref, wout_ref, sconv_ref, c_in, n_in, m_in,
                       out_ref, c_out, n_out, m_out, conv_out, vrow_out,
                       proj, hcat, *, bt):
    del hprev_ref, cprev_ref
    i = pl.program_id(0)
    ni = pl.num_programs(0)

    @pl.when(i == 0)
    def _():
        x = x_ref[...]
        xn = _rms(x, gmix_ref[...]).astype(BF16)
        proj[...] = jnp.dot(xn, win_ref[...], preferred_element_type=F32)
        pre = proj[:, P_QK:P_QK + QK_W]
        qk = cb_ref[...] + cw_ref[CONV_W - 1:CONV_W, :] * pre
        for jj in range(CONV_W - 1):
            qk = qk + cw_ref[jj:jj + 1, :] * sconv_ref[jj]
        for jj in range(CONV_W - 2):
            conv_out[jj] = sconv_ref[jj + 1]
        conv_out[CONV_W - 2] = pre
        qk = qk * jax.nn.sigmoid(qk)
        proj[:, 0:W_A] = qk[:, 0:W_A]
        proj[:, W_A:QK_W] = qk[:, W_A:QK_W] * (DH_A ** -0.5)
        for g in range(G_B):
            sl = slice(g * DG_B, (g + 1) * DG_B)
            vb = proj[:, P_VB + g * DG_B:P_VB + (g + 1) * DG_B]
            u = proj[:, P_U + g * DG_B:P_U + (g + 1) * DG_B]
            vn = _rms(jax.nn.gelu(vb), vng_ref[:, sl])
            vrow_out[:, sl] = vn
            z = w00_ref[:, sl] * vn + bs0_ref[:, sl]
            hcat[:, W_A + g * DG_B:W_A + (g + 1) * DG_B] = jax.nn.gelu(u) * z

    rows = pl.ds(pl.multiple_of(i * bt, bt), bt)
    gates = proj[rows, P_G:P_G + LANES] + gb_ref[...]
    lane = lax.broadcasted_iota(jnp.int32, (bt, LANES), 1)
    sub = lax.broadcasted_iota(jnp.int32, (bt, DH_A), 0)
    m_new_all = jnp.zeros((bt, LANES), F32)
    for h in range(H_A):
        ig = gates[:, h:h + 1]
        lf = _log_sigmoid(gates[:, H_A + h:H_A + h + 1])
        m_prev = m_in[rows, h:h + 1]
        inter = lf + m_prev
        m_t = jnp.maximum(inter, ig)
        w_inter = jnp.exp(inter - m_t)
        e_d = jnp.exp(ig - m_t)
        q = proj[rows, h * DH_A:(h + 1) * DH_A]
        k = proj[rows, W_A + h * DH_A:W_A + (h + 1) * DH_A]
        v = proj[rows, P_V + h * DH_A:P_V + (h + 1) * DH_A]
        n_old = n_in[h]
        s = jnp.sum(q * k, axis=-1, keepdims=True) * e_d
        vw = v * e_d
        vw_t = jnp.concatenate([vw, jnp.zeros((DH_A - bt, DH_A), F32)], axis=0).T
        qb = q.astype(BF16)
        qc = jnp.zeros((bt, DH_A), F32)
        for t in range(bt):
            c_old = c_in[t, h]
            r = _dot_nt(qb, c_old.astype(BF16))
            qc = jnp.where(sub == t, r, qc)
            c_out[t, h] = w_inter[t:t + 1, :] * c_old + vw_t[:, t:t + 1] * k[t:t + 1, :]
        num = w_inter * qc + s * v
        nq = w_inter * jnp.sum(q * n_old, axis=-1, keepdims=True) + s
        hh = num / jnp.maximum(jnp.abs(nq), jnp.exp(-m_t))
        n_out[h] = w_inter * n_old + e_d * k
        m_new_all = jnp.where(lane == h, m_t, m_new_all)
        ha = _rms(hh, hng_ref[:, h * DH_A:(h + 1) * DH_A])
        o = proj[rows, P_O + h * DH_A:P_O + (h + 1) * DH_A]
        hcat[rows, h * DH_A:(h + 1) * DH_A] = ha * jax.nn.sigmoid(o)
    m_out[rows, :] = m_new_all

    @pl.when(i == ni - 1)
    def _():
        out_ref[...] = x_ref[...] + jnp.dot(hcat[...].astype(BF16), wout_ref[...],
                                            preferred_element_type=F32)


def _mix_sample(h_in, h_out, c_all, lw, sconv_t, c0, n0_t, m0_pad, n_s, layer):
    bt = BT_SAMPLE
    ni = n_s // bt
    const = lambda i: (0, 0)
    in_blk = h_in.shape[0] // n_s - 1
    out_blk = h_out.shape[0] // n_s - 1
    kern = functools.partial(_mix_sample_kernel, bt=bt)
    return pl.pallas_call(
        kern,
        name="mix_sample",
        grid=(ni,),
        in_specs=[
            pl.BlockSpec((n_s, D_MODEL), lambda i: (in_blk, 0)),
            pl.BlockSpec(memory_space=pl.ANY),
            pl.BlockSpec(memory_space=pl.ANY),
            pl.BlockSpec((1, D_MODEL), const),
            pl.BlockSpec((D_MODEL, P_W), const),
            pl.BlockSpec((CONV_W, QK_W), const),
            pl.BlockSpec((1, QK_W), const),
            pl.BlockSpec((1, LANES), const),
            pl.BlockSpec((1, W_A), const),
            pl.BlockSpec((1, W_B), const),
            pl.BlockSpec((1, W_B), const),
            pl.BlockSpec((1, W_B), const),
            pl.BlockSpec((D_MODEL, D_MODEL), const),
            pl.BlockSpec((None, CONV_W - 1, n_s, QK_W), lambda i: (layer, 0, 0, 0)),
            pl.BlockSpec((None, bt, H_A, DH_A, DH_A), lambda i: (layer, i, 0, 0, 0)),
            pl.BlockSpec((None, H_A, bt, DH_A), lambda i: (layer, 0, i, 0)),
            pl.BlockSpec((None, n_s, LANES), lambda i: (layer, 0, 0)),
        ],
        out_specs=[
            pl.BlockSpec((n_s, D_MODEL), lambda i: (out_blk, 0)),
            pl.BlockSpec((None, bt, H_A, DH_A, DH_A), lambda i: (layer, i, 0, 0, 0)),
            pl.BlockSpec((H_A, bt, DH_A), lambda i: (0, i, 0)),
            pl.BlockSpec((n_s, LANES), const),
            pl.BlockSpec((CONV_W - 1, n_s, QK_W), lambda i: (0, 0, 0)),
            pl.BlockSpec((n_s, W_B), const),
        ],
        out_shape=[
            jax.ShapeDtypeStruct(h_out.shape, F32),
            jax.ShapeDtypeStruct(c_all.shape, F32),
            jax.ShapeDtypeStruct(n0_t.shape[1:], F32),
            jax.ShapeDtypeStruct((n_s, LANES), F32),
            jax.ShapeDtypeStruct((CONV_W - 1, n_s, QK_W), F32),
            jax.ShapeDtypeStruct((n_s, W_B), F32),
        ],
        scratch_shapes=[
            pltpu.VMEM((n_s, P_W), F32),
            pltpu.VMEM((n_s, D_MODEL), F32),
        ],
        input_output_aliases={1: 0, 2: 1},
        compiler_params=_cparams("arbitrary"),
    )(h_in, h_out, c_all, lw["g_mix"], lw["w_in"], lw["conv_w"], lw["conv_b"], lw["gate_b"], lw["hn_g"], lw["vn_g"],
      lw["w00_row"], lw["bs0_row"], lw["w_out"], sconv_t, c0, n0_t, m0_pad)


def _router_kernel(h_ref, g_ref, wr_ref, b_ref, route_ref, route_t_ref, tcnt_ref, *, tm):
    hn = _rms(h_ref[...], g_ref[...]).astype(BF16)
    logits = jnp.dot(hn, wr_ref[...], preferred_element_type=F32) + b_ref[...]

    lane_i = lax.broadcasted_iota(jnp.int32, (tm, LANES), 1)
    lane = lane_i.astype(F32)
    neg = -jnp.inf
    is_g = lane_i < N_GROUPS
    gl = jnp.where(is_g, logits, neg)
    gmax = jnp.max(gl, axis=-1, keepdims=True)
    g_sel = jnp.min(jnp.where(gl == gmax, lane, float(LANES)), axis=-1, keepdims=True)
    g_w = 1.0 / jnp.sum(jnp.where(is_g, jnp.exp(logits - gmax), 0.0), axis=-1, keepdims=True)

    e_lane = lane_i - N_GROUPS
    lane_grp = (e_lane >> 3).astype(F32)
    in_grp = (e_lane >= 0) & (e_lane < N_EXPERTS) & (lane_grp == g_sel)
    el = jnp.where(in_grp, logits, neg)
    v1 = jnp.max(el, axis=-1, keepdims=True)
    i1 = jnp.min(jnp.where(el == v1, lane, float(LANES)), axis=-1, keepdims=True)
    el2 = jnp.where(lane == i1, neg, el)
    v2 = jnp.max(el2, axis=-1, keepdims=True)
    i2 = jnp.min(jnp.where(el2 == v2, lane, float(LANES)), axis=-1, keepdims=True)
    e21 = jnp.exp(v2 - v1)
    den = 1.0 + e21
    w1 = g_w * (1.0 / den)
    w2 = g_w * (e21 / den)

    onehot = jnp.where((lane == i1) | (lane == i2), 1.0, 0.0)
    r_i = lax.broadcasted_iota(jnp.int32, (tm, tm), 0)
    c_i = lax.broadcasted_iota(jnp.int32, (tm, tm), 1)
    strict = jnp.where(r_i > c_i, 1.0, 0.0).astype(BF16)
    before = jnp.dot(strict, onehot.astype(BF16), preferred_element_type=F32)
    cnt = jnp.sum(onehot, axis=0, keepdims=True)
    cnt8 = jnp.floor((cnt + (SUBLANES - 1.0)) * (1.0 / SUBLANES)) * SUBLANES
    l_r = lax.broadcasted_iota(jnp.int32, (LANES, LANES), 0)
    l_c = lax.broadcasted_iota(jnp.int32, (LANES, LANES), 1)
    lanes_before = jnp.where(l_r < l_c, 1.0, 0.0).astype(BF16)
    start = _split_dot(jnp.broadcast_to(cnt8, (2 * SUBLANES, LANES)), lanes_before)[0:1, :]
    slot = start + before
    pos1 = jnp.sum(jnp.where(lane == i1, slot, 0.0), axis=-1, keepdims=True)
    pos2 = jnp.sum(jnp.where(lane == i2, slot, 0.0), axis=-1, keepdims=True)
    tcnt_ref[...] = jnp.broadcast_to(cnt8, tcnt_ref.shape)

    out = jnp.zeros((tm, LANES), F32)
    for idx, val in enumerate((pos1, pos2, w1, w2)):
        out = jnp.where(lane_i == idx, val, out)
    route_ref[...] = out
    route_t_ref[...] = out.T[0:SUBLANES, :]


def _router(h1, lw):
    t_total = h1.shape[0]
    tm = _token_tile(t_total)
    n_tiles = t_total // tm
    const = lambda i: (0, 0)
    return pl.pallas_call(
        functools.partial(_router_kernel, tm=tm),
        name="router",
        grid=(n_tiles,),
        in_specs=[
            pl.BlockSpec((tm, D_MODEL), lambda i: (i, 0)),
            pl.BlockSpec((1, D_MODEL), const),
            pl.BlockSpec((D_MODEL, LANES), const),
            pl.BlockSpec((1, LANES), const),
        ],
        out_specs=[
            pl.BlockSpec((tm, LANES), lambda i: (i, 0)),
            pl.BlockSpec((SUBLANES, tm), lambda i: (0, i)),
            pl.BlockSpec((None, SUBLANES, LANES), lambda i: (i, 0, 0)),
        ],
        out_shape=[
            jax.ShapeDtypeStruct((t_total, LANES), F32),
            jax.ShapeDtypeStruct((SUBLANES, t_total), F32),
            jax.ShapeDtypeStruct((n_tiles, SUBLANES, LANES), F32),
        ],
        compiler_params=_cparams("arbitrary"),
    )(h1, lw["g_ffn"], lw["wr"], lw["br"])


def _tile_slots(tm):
    raw = tm * TOP_K + N_EXPERTS * (SUBLANES - 1)
    return -(-raw // LANES) * LANES


def _token_tile(t_total):
    tm = TM_TOK
    while t_total % tm:
        tm //= 2
    return tm


def _run_copies(tile, udst_ref, tunits_ref, units_per_tile, make_copy):
    base = tile * units_per_tile

    def per_unit(u, carry):
        make_copy(pl.multiple_of(u * SUBLANES, SUBLANES),
                  pl.multiple_of(udst_ref[base + u], SUBLANES), SUBLANES).start()
        return carry

    lax.fori_loop(0, tunits_ref[tile], per_unit, 0)


def _drain_copies(n_units, make_copy):
    def wait_group(u, carry):
        make_copy(0, 0, DRAIN_GROUP * SUBLANES).wait()
        return carry

    def wait_one(u, carry):
        make_copy(0, 0, SUBLANES).wait()
        return carry

    lax.fori_loop(0, lax.div(n_units, DRAIN_GROUP), wait_group, 0)
    lax.fori_loop(0, lax.rem(n_units, DRAIN_GROUP), wait_one, 0)


def _dispatch_kernel(udst_ref, tunits_ref, h_ref, g_ref, rt_ref, xs_in, xs_out,
                     srt, sem, *, tm, slots):
    del xs_in
    i = pl.program_id(0)
    n = pl.num_programs(0)
    buf = lax.rem(i, 2)

    def copy_maker(b):
        def make_copy(tile_row, sorted_row, rows):
            return pltpu.make_async_copy(srt.at[b, pl.ds(tile_row, rows)],
                                         xs_out.at[pl.ds(sorted_row, rows), pl.ds(0, HALF_D)], sem.at[b])
        return make_copy

    @pl.when(i >= 2)
    def _():
        _drain_copies(tunits_ref[i - 2], copy_maker(buf))

    xn = _rms(h_ref[...], g_ref[...]).astype(BF16)
    slot_id = lax.broadcasted_iota(jnp.int32, (slots, tm), 0).astype(F32)
    sel = (slot_id == rt_ref[0:1, :]) | (slot_id == rt_ref[1:2, :])
    srt[buf] = _pack_rows(jnp.dot(jnp.where(sel, 1.0, 0.0).astype(BF16), xn, preferred_element_type=F32))
    _run_copies(i, udst_ref, tunits_ref, slots // SUBLANES, copy_maker(buf))

    @pl.when(i == n - 1)
    def _():
        @pl.when(i >= 1)
        def _():
            _drain_copies(tunits_ref[i - 1], copy_maker(1 - buf))

        _drain_copies(tunits_ref[i], copy_maker(buf))


def _dispatch(tables, h1, route_t, lw, xs):
    t_total = h1.shape[0]
    tm = _token_tile(t_total)
    slots = _tile_slots(tm)
    return pl.pallas_call(
        functools.partial(_dispatch_kernel, tm=tm, slots=slots),
        name="dispatch",
        grid_spec=pltpu.PrefetchScalarGridSpec(
            num_scalar_prefetch=2,
            grid=(t_total // tm,),
            in_specs=[
                pl.BlockSpec((tm, D_MODEL), lambda i, *_: (i, 0)),
                pl.BlockSpec((1, D_MODEL), lambda i, *_: (0, 0)),
                pl.BlockSpec((SUBLANES, tm), lambda i, *_: (0, i)),
                pl.BlockSpec(memory_space=pl.ANY),
            ],
            out_specs=pl.BlockSpec(memory_space=pl.ANY),
            scratch_shapes=[pltpu.VMEM((2, slots, HALF_D), U32), pltpu.SemaphoreType.DMA((2,))],
        ),
        out_shape=jax.ShapeDtypeStruct(xs.shape, U32),
        input_output_aliases={5: 0},
        compiler_params=_cparams("arbitrary"),
    )(*tables, h1, lw["g_ffn"], route_t, xs)


def _ffn_kernel(bstart_ref, nblk_ref, nused_ref, w1_ref, w3_ref, w2_ref, xs_ref, ys_ref,
                w1b, w3b, w2b, xbuf, ybuf, sem_in, sem_out):
    e = pl.program_id(0)
    g0 = bstart_ref[e]
    nb = nblk_ref[e]
    n_used = nused_ref[0]

    def in_copy(g):
        slot = lax.rem(g, FFN_RING)
        r = pl.multiple_of(g * FFN_BLK, FFN_BLK)
        return pltpu.make_async_copy(xs_ref.at[pl.ds(r, FFN_BLK), pl.ds(0, HALF_D)], xbuf.at[slot],
                                     sem_in.at[slot])

    def out_copy(g):
        slot = lax.rem(g, FFN_RING)
        r = pl.multiple_of(g * FFN_BLK, FFN_BLK)
        return pltpu.make_async_copy(ybuf.at[slot], ys_ref.at[pl.ds(r, FFN_BLK)], sem_out.at[slot])

    @pl.when(e == 0)
    def _():
        for ahead in range(FFN_RING - 1):
            @pl.when(ahead < n_used)
            def _():
                in_copy(jnp.int32(ahead)).start()

    @pl.when(nb > 0)
    def _():
        w1b[...] = w1_ref[...].astype(BF16)
        w3b[...] = w3_ref[...].astype(BF16)
        w2b[...] = w2_ref[...].astype(BF16)

    def block_step(k, carry):
        g = g0 + k
        slot = lax.rem(g, FFN_RING)
        in_copy(g).wait()

        @pl.when(g + (FFN_RING - 1) < n_used)
        def _():
            in_copy(g + (FFN_RING - 1)).start()

        @pl.when(g >= FFN_RING)
        def _():
            out_copy(g - FFN_RING).wait()

        x_lo, x_hi = _unpack_rows(xbuf[slot])
        a = (jnp.dot(x_lo, w1b[0:HALF_D, :], preferred_element_type=F32)
             + jnp.dot(x_hi, w1b[HALF_D:, :], preferred_element_type=F32))
        b = (jnp.dot(x_lo, w3b[0:HALF_D, :], preferred_element_type=F32)
             + jnp.dot(x_hi, w3b[HALF_D:, :], preferred_element_type=F32))
        hmid = (a * jax.nn.sigmoid(a) * b).astype(BF16)
        y = jnp.dot(hmid, w2b[...], preferred_element_type=F32)
        ybuf[slot] = lax.bitcast_convert_type(y, U32)
        out_copy(g).start()
        return carry

    lax.fori_loop(0, nb, block_step, 0)

    @pl.when(e == pl.num_programs(0) - 1)
    def _():
        for back in range(FFN_RING, 0, -1):
            @pl.when(n_used >= back)
            def _():
                out_copy(n_used - back).wait()


def _ffn(ffn_tables, xs, w1, w3, w2, layer):
    def w_map(e, *_):
        return (layer, e, 0, 0)

    return pl.pallas_call(
        _ffn_kernel,
        name="expert_ffn",
        grid_spec=pltpu.PrefetchScalarGridSpec(
            num_scalar_prefetch=3,
            grid=(N_EXPERTS,),
            in_specs=[
                pl.BlockSpec((None, None, D_MODEL, D_FF), w_map),
                pl.BlockSpec((None, None, D_MODEL, D_FF), w_map),
                pl.BlockSpec((None, None, D_FF, D_MODEL), w_map),
                pl.BlockSpec(memory_space=pl.ANY),
            ],
            out_specs=pl.BlockSpec(memory_space=pl.ANY),
            scratch_shapes=[
                pltpu.VMEM((D_MODEL, D_FF), BF16),
                pltpu.VMEM((D_MODEL, D_FF), BF16),
                pltpu.VMEM((D_FF, D_MODEL), BF16),
                pltpu.VMEM((FFN_RING, FFN_BLK, HALF_D), U32),
                pltpu.VMEM((FFN_RING, FFN_BLK, D_MODEL), U32),
                pltpu.SemaphoreType.DMA((FFN_RING,)),
                pltpu.SemaphoreType.DMA((FFN_RING,)),
            ],
        ),
        out_shape=jax.ShapeDtypeStruct(xs.shape, U32),
        input_output_aliases={6: 0},
        compiler_params=_cparams("arbitrary"),
    )(*ffn_tables, w1, w3, w2, xs)


def _ple_kernel(udst_ref, tunits_ref, h_ref, route_ref, p_ref, ps_ref, wpg_ref,
                wple_ref, pg_ref, fg_ref, ys_ref, *rest, tm, slots, n_s, final):
    if final:
        yp_ref, ysm_ref, ysrt, sem = rest
    else:
        out_ref, ysrt, sem = rest
    i = pl.program_id(0)
    n = pl.num_programs(0)
    buf = lax.rem(i, 2)

    def copy_maker(b):
        def make_copy(tile_row, sorted_row, rows):
            return pltpu.make_async_copy(ys_ref.at[pl.ds(sorted_row, rows)],
                                         ysrt.at[b, pl.ds(tile_row, rows)], sem.at[b])
        return make_copy

    @pl.when(i == 0)
    def _():
        ysrt[...] = jnp.zeros_like(ysrt)
        _run_copies(i, udst_ref, tunits_ref, slots // SUBLANES, copy_maker(buf))

    @pl.when(i + 1 < n)
    def _():
        _run_copies(i + 1, udst_ref, tunits_ref, slots // SUBLANES, copy_maker(1 - buf))

    p = p_ref[...]
    p_last = jnp.concatenate([p[0:tm - n_s, :], ps_ref[...]], axis=0)
    p = jnp.where(i == n - 1, p_last, p)
    pe = _rms(jnp.dot(p.astype(BF16), wple_ref[...], preferred_element_type=F32), pg_ref[...])
    _drain_copies(tunits_ref[i], copy_maker(buf))

    route = route_ref[...]
    y = lax.bitcast_convert_type(ysrt[buf], F32)
    y_head = y.astype(BF16)
    y_rem = (y - y_head.astype(F32)).astype(BF16)
    slot_id = lax.broadcasted_iota(jnp.int32, (tm, slots), 1).astype(F32)
    moe = None
    weighted_pick = jnp.zeros((tm, slots), F32)
    for kk in range(TOP_K):
        hit = slot_id == route[:, kk:kk + 1]
        w_kk = route[:, TOP_K + kk:TOP_K + kk + 1]
        term = w_kk * jnp.dot(jnp.where(hit, 1.0, 0.0).astype(BF16), y_head, preferred_element_type=F32)
        moe = term if moe is None else moe + term
        weighted_pick = jnp.where(hit, w_kk, weighted_pick)
    moe = moe + jnp.dot(weighted_pick.astype(BF16), y_rem, preferred_element_type=F32)
    h2 = h_ref[...] + moe
    gate = jax.nn.sigmoid(jnp.dot(h2.astype(BF16), wpg_ref[...], preferred_element_type=F32))
    out = h2 + gate * pe
    if final:
        fin = _rms(out, fg_ref[...])
        yp_ref[...] = fin

        @pl.when(i == pl.num_programs(0) - 1)
        def _():
            ysm_ref[...] = fin[tm - n_s:tm, :]
    else:
        out_ref[...] = out


def _ple(tables, h1, route, p_prompt, p_sample, lw, final_g, ys, n_s, layer, final, in_place):
    t_total = h1.shape[0]
    tm = _token_tile(t_total)
    slots = _tile_slots(tm)
    assert n_s <= tm and (t_total - n_s) % tm == tm - n_s
    const = lambda i, *_: (0, 0)
    tok = lambda i, *_: (i, 0)
    if final:
        out_specs = [pl.BlockSpec((tm, D_MODEL), tok), pl.BlockSpec((n_s, D_MODEL), const)]
        out_shape = [jax.ShapeDtypeStruct((t_total - n_s, D_MODEL), F32),
                     jax.ShapeDtypeStruct((n_s, D_MODEL), F32)]
        aliases = {}
    else:
        out_specs = [pl.BlockSpec((tm, D_MODEL), tok)]
        out_shape = [jax.ShapeDtypeStruct((t_total, D_MODEL), F32)]
        aliases = {2: 0} if in_place else {}
    return pl.pallas_call(
        functools.partial(_ple_kernel, tm=tm, slots=slots, n_s=n_s, final=final),
        name="combine_ple",
        grid_spec=pltpu.PrefetchScalarGridSpec(
            num_scalar_prefetch=2,
            grid=(t_total // tm,),
            in_specs=[
                pl.BlockSpec((tm, D_MODEL), tok),
                pl.BlockSpec((tm, LANES), tok),
                pl.BlockSpec((None, tm, PLE_DIM), lambda i, *_: (layer, i, 0)),
                pl.BlockSpec((None, n_s, PLE_DIM), lambda i, *_: (layer, 0, 0)),
                pl.BlockSpec((D_MODEL, D_MODEL), const),
                pl.BlockSpec((PLE_DIM, D_MODEL), const),
                pl.BlockSpec((1, D_MODEL), const),
                pl.BlockSpec((1, D_MODEL), const),
                pl.BlockSpec(memory_space=pl.ANY),
            ],
            out_specs=out_specs,
            scratch_shapes=[pltpu.VMEM((2, slots, D_MODEL), U32), pltpu.SemaphoreType.DMA((2,))],
        ),
        out_shape=out_shape,
        input_output_aliases=aliases,
        compiler_params=_cparams("arbitrary"),
    )(*tables, h1, route, p_prompt, p_sample, lw["w_pg"], lw["w_ple"], lw["ple_g"], final_g, ys)


def _layer_weights(l, norm_mix_g, w_in, conv_w, conv_b, b_igate, b_fgate, hnorm_g, vnorm_g, w_s, b_s, w_out,
                   norm_ffn_g, router_g_w, router_g_b, router_e_w, router_e_b, w_ple, ple_norm_g, w_ple_gate):
    wi = w_in[l]
    a0 = QK_W + 2 * W_A
    gates_w = jnp.pad(wi[:, a0:a0 + 2 * H_A], ((0, 0), (0, LANES - 2 * H_A)))
    w_packed = jnp.concatenate(
        [wi[:, :a0], wi[:, a0 + 2 * H_A:], gates_w], axis=1).astype(BF16)
    gate_b = jnp.pad(jnp.concatenate([b_igate[l], b_fgate[l]]), (0, LANES - 2 * H_A))[None, :]
    wr = jnp.pad(jnp.concatenate([router_g_w[l], router_e_w[l]], axis=1),
                 ((0, 0), (0, LANES - N_GROUPS - N_EXPERTS))).astype(BF16)
    br = jnp.pad(jnp.concatenate([router_g_b[l], router_e_b[l]]), (0, LANES - N_GROUPS - N_EXPERTS))[None, :]
    return {
        "g_mix": norm_mix_g[l][None, :],
        "w_in": w_packed,
        "conv_w": conv_w[l],
        "conv_b": conv_b[l][None, :],
        "gate_b": gate_b,
        "hn_g": hnorm_g[l][None, :],
        "vn_g": vnorm_g[l][None, :],
        "w_s": w_s[l],
        "b_s_col": jnp.pad(b_s[l].T, ((0, 0), (0, LANES - G_B))),
        "w00_row": jnp.repeat(w_s[l][:, 0, 0], DG_B)[None, :],
        "bs0_row": jnp.repeat(b_s[l][:, 0], DG_B)[None, :],
        "w_out": w_out[l].astype(BF16),
        "g_ffn": norm_ffn_g[l][None, :],
        "wr": wr,
        "br": br,
        "w_pg": w_ple_gate[l].astype(BF16),
        "w_ple": w_ple[l].astype(BF16),
        "ple_g": ple_norm_g[l][None, :],
    }


def _routing_tables(tcnt, units_per_tile):
    c8 = tcnt[:, 0, N_GROUPS:N_GROUPS + N_EXPERTS].astype(jnp.int32)
    blocks = (jnp.sum(c8, axis=0) + FFN_BLK - 1) // FFN_BLK
    blk_end = jnp.cumsum(blocks)
    pstart = (blk_end - blocks) * FFN_BLK
    run_dst = pstart[None, :] + jnp.cumsum(c8, axis=0) - c8
    run_src = jnp.cumsum(c8, axis=1) - c8
    slot = (jnp.arange(units_per_tile, dtype=jnp.int32) * SUBLANES)[None, :, None]
    in_run = (slot >= run_src[:, None, :]) & (slot < (run_src + c8)[:, None, :])
    unit_dst = jnp.sum(jnp.where(in_run, run_dst[:, None, :] + slot - run_src[:, None, :], 0), axis=2)
    tables = (unit_dst.reshape(-1), jnp.sum(c8, axis=1) // SUBLANES)
    ffn_tables = (blk_end - blocks, blocks, blk_end[-1:])
    return tables, ffn_tables


def kernel(x_prompt, x_sample, state_C, state_n, state_m, state_conv, p_prompt, p_sample, norm_mix_g, w_in, conv_w, conv_b, b_igate, b_fgate, hnorm_g, vnorm_g, w_s, b_s, w_out, norm_ffn_g, router_g_w, router_g_b, router_e_w, router_e_b, w1, w3, w2, w_ple, ple_norm_g, w_ple_gate, final_norm_g):
    n_batch, seq, _ = x_prompt.shape
    n_s = x_sample.shape[0]
    depth = w_in.shape[0]
    t_p = n_batch * seq
    t_total = t_p + n_s
    assert seq % CHUNK == 0 and t_p % n_s == 0 and n_s % BT_SAMPLE == 0

    n_tiles = t_total // _token_tile(t_total)
    max_rows = t_total * TOP_K + N_EXPERTS * ((SUBLANES - 1) * n_tiles + FFN_BLK - 1)
    n_blocks = max_rows // FFN_BLK
    xs = jnp.zeros((n_blocks * FFN_BLK, D_MODEL), U32)

    p_prompt_t = p_prompt.reshape(depth, t_p, PLE_DIM)
    p_sample_t = p_sample.reshape(depth, n_s, PLE_DIM)
    sconv_t = jnp.transpose(state_conv, (0, 2, 1, 3))
    n0_t = jnp.transpose(state_n, (0, 2, 1, 3))
    m0_pad = jnp.pad(state_m, ((0, 0), (0, 0), (0, LANES - H_A)))
    final_g = final_norm_g[None, :]

    src_p = x_prompt.reshape(t_p, D_MODEL)
    src_s = x_sample.reshape(n_s, D_MODEL)
    spare = jnp.zeros((t_total, D_MODEL), F32)
    c_all = jnp.zeros(state_C.shape, F32)
    outs = {k: [] for k in ("Cp", "np", "mp", "cbp", "ns", "ms", "cbs", "vs")}
    res = None
    for l in range(depth):
        lw = _layer_weights(l, norm_mix_g, w_in, conv_w, conv_b, b_igate, b_fgate, hnorm_g, vnorm_g, w_s, b_s,
                            w_out, norm_ffn_g, router_g_w, router_g_b, router_e_w, router_e_b, w_ple,
                            ple_norm_g, w_ple_gate)
        h_mix, c_p, n_p, m_p, cb_p = _mix_prompt(src_p, spare, n_batch, seq, lw)
        h_mix, c_all, n_s_t, m_s, cb_s, v_s = _mix_sample(src_s, h_mix, c_all, lw, sconv_t, state_C, n0_t,
                                                          m0_pad, n_s, l)
        route, route_t, tcnt = _router(h_mix, lw)
        tables, ffn_tables = _routing_tables(tcnt, _tile_slots(_token_tile(t_total)) // SUBLANES)
        xs = _dispatch(tables, h_mix, route_t, lw, xs)
        xs = _ffn(ffn_tables, xs, w1, w3, w2, l)
        final = l == depth - 1
        res = _ple(tables, h_mix, route, p_prompt_t, p_sample_t, lw, final_g, xs, n_s, l, final, in_place=l > 0)
        if l == 0:
            spare = h_mix
        else:
            spare = src_p
        src_p = src_s = res[0]
        outs["Cp"].append(c_p)
        outs["np"].append(n_p)
        outs["mp"].append(m_p[:, :, 0])
        outs["cbp"].append(cb_p)
        outs["ns"].append(jnp.transpose(n_s_t, (1, 0, 2)))
        outs["ms"].append(m_s[:, 0:H_A])
        outs["cbs"].append(jnp.transpose(cb_s, (1, 0, 2)))
        outs["vs"].append(v_s[:, None, :])

    y_prompt = res[0].reshape(n_batch, seq, D_MODEL)
    y_sample = res[1].reshape(n_s, 1, D_MODEL)
    st = lambda k: jnp.stack(outs[k])
    return (y_prompt, y_sample, st("Cp"), st("np"), st("mp"), st("cbp"),
            c_all, st("ns"), st("ms"), st("cbs"), st("vs"))
```

```python
import functools

import jax
import jax.numpy as jnp
from jax import lax
from jax.experimental import pallas as pl
from jax.experimental.pallas import tpu as pltpu

F32 = jnp.float32
BF16 = jnp.bfloat16
U32 = jnp.uint32

D_MODEL = 1024
W_A = 512
H_A = 4
DH_A = 128
W_B = 512
G_B = 4
DG_B = 128
CHUNK = 128
CONV_W = 4
QK_W = 2 * W_A
N_GROUPS = 4
EXPERTS_PER_GROUP = 8
N_EXPERTS = N_GROUPS * EXPERTS_PER_GROUP
TOP_K = 2
D_FF = 512
PLE_DIM = 256
EPS = 1e-6
HALF_D = D_MODEL // 2

LANES = 128
SUBLANES = 8
VMEM_LIMIT_BYTES = 56 * 1024 * 1024

P_QK = 0
P_V = QK_W
P_O = P_V + W_A
P_U = P_O + W_A
P_VB = P_U + W_B
P_G = P_VB + W_B
P_W = P_G + LANES

TL_MIX = 512
BT_SAMPLE = 8
TM_TOK = 384
FFN_BLK = 256
FFN_RING = 4
DRAIN_GROUP = 16


def _cparams(*sem):
    return pltpu.CompilerParams(dimension_semantics=sem, vmem_limit_bytes=VMEM_LIMIT_BYTES)


def _rms(x, g):
    return x * lax.rsqrt(jnp.mean(x * x, axis=-1, keepdims=True) + EPS) * g


def _log_sigmoid(x):
    return -(jnp.maximum(-x, 0.0) + jnp.log1p(jnp.exp(-jnp.abs(x))))


def _split_dot(a, b_bf16):
    hi = a.astype(BF16)
    lo = (a - hi.astype(F32)).astype(BF16)
    return (jnp.dot(hi, b_bf16, preferred_element_type=F32)
            + jnp.dot(lo, b_bf16, preferred_element_type=F32))


def _dot_nt(a, b):
    return lax.dot_general(a, b, (((1,), (1,)), ((), ())), preferred_element_type=F32)


def _pack_rows(x):
    lo = lax.bitcast_convert_type(x[:, :HALF_D], U32) >> 16
    hi = lax.bitcast_convert_type(x[:, HALF_D:], U32) & jnp.uint32(0xFFFF0000)
    return lo | hi


def _unpack_rows(u):
    lo = lax.bitcast_convert_type(u << 16, F32)
    hi = lax.bitcast_convert_type(u & jnp.uint32(0xFFFF0000), F32)
    return lo.astype(BF16), hi.astype(BF16)


def _mix_prompt_kernel(x_ref, xnext_ref, hprev_ref, gmix_ref, win_ref, cw_ref, cb_ref, gb_ref, hng_ref,
                       vng_ref, ws_ref, bs_ref, wout_ref,
                       out_ref, c_out, n_out, m_out, conv_out,
                       proj_a, proj_b, xpad, hcat, c_s, n_s, m_s, *, tl):
    del hprev_ref
    b = pl.program_id(0)
    j = pl.program_id(1)
    nj = pl.num_programs(1)
    n_chunks = tl // CHUNK
    bounds = [(P_W // LANES * c // n_chunks) * LANES for c in range(n_chunks + 1)]
    col_groups = [slice(bounds[c], bounds[c + 1]) for c in range(n_chunks)]

    @pl.when(j == 0)
    def _():
        c_s[...] = jnp.zeros_like(c_s)
        n_s[...] = jnp.zeros_like(n_s)
        m_s[...] = jnp.zeros_like(m_s)
        xpad[0:SUBLANES, :] = jnp.zeros((SUBLANES, QK_W), F32)

    @pl.when((b == 0) & (j == 0))
    def _():
        xn0 = _rms(x_ref[0:tl, :], gmix_ref[...]).astype(BF16)
        proj_a[...] = jnp.dot(xn0, win_ref[...], preferred_element_type=F32)

    row = lax.broadcasted_iota(jnp.int32, (CHUNK, CHUNK), 0)
    col = lax.broadcasted_iota(jnp.int32, (CHUNK, CHUNK), 1)
    causal = row >= col
    tril = jnp.where(causal, 1.0, 0.0).astype(BF16)
    triu = jnp.where(row <= col, 1.0, 0.0).astype(BF16)
    lane = lax.broadcasted_iota(jnp.int32, (CHUNK, LANES), 1)
    wm = [jnp.where(causal, ws_ref[g], 0.0).astype(BF16) for g in range(G_B)]

    def conv_step(proj):
        pre = proj[:, P_QK:P_QK + QK_W]
        xpad[SUBLANES:SUBLANES + tl, :] = pre
        qk = cb_ref[...] + cw_ref[CONV_W - 1:CONV_W, :] * pre
        for jj in range(1, CONV_W):
            qk = qk + cw_ref[CONV_W - 1 - jj:CONV_W - jj, :] * xpad[SUBLANES - jj:SUBLANES - jj + tl, :]
        xpad[SUBLANES - (CONV_W - 1):SUBLANES, :] = pre[tl - (CONV_W - 1):tl, :]
        qk = qk * jax.nn.sigmoid(qk)
        proj[:, 0:W_A] = qk[:, 0:W_A]
        proj[:, W_A:QK_W] = qk[:, W_A:QK_W] * (DH_A ** -0.5)

    def chunk_step(proj, c):
        rows = slice(c * CHUNK, (c + 1) * CHUNK)
        gl = proj[rows, P_G:P_G + LANES] + gb_ref[...]
        gl = jnp.where(lane >= H_A, _log_sigmoid(gl), gl)
        glt = gl.T
        b_col_all = _split_dot_left(tril, gl)
        b_row_all = _split_dot(glt[0:2 * SUBLANES, :], triu)

        for h in range(H_A):
            q32 = proj[rows, h * DH_A:(h + 1) * DH_A]
            k32 = proj[rows, W_A + h * DH_A:W_A + (h + 1) * DH_A]
            v32 = proj[rows, P_V + h * DH_A:P_V + (h + 1) * DH_A]
            q = q32.astype(BF16)
            k = k32.astype(BF16)
            ig_c = gl[:, h:h + 1]
            ig_r = glt[h:h + 1, :]
            b_c = b_col_all[:, H_A + h:H_A + h + 1]
            b_r = b_row_all[H_A + h:H_A + h + 1, :]
            c_old = c_s[h]
            n_old = n_s[h:h + 1, :]
            m_prev = m_s[h:h + 1, 0:1]

            d_log = jnp.where(causal, b_c - b_r + ig_r, -jnp.inf)
            inter = b_c + m_prev
            m_t = jnp.maximum(inter, jnp.max(d_log, axis=-1, keepdims=True))
            s = _dot_nt(q, k) * jnp.exp(d_log - m_t)
            w_inter = jnp.exp(inter - m_t)
            num = (w_inter * _dot_nt(q, c_old.astype(BF16))
                   + jnp.dot(s.astype(BF16), v32.astype(BF16), preferred_element_type=F32))
            nq = (w_inter * jnp.sum(q32 * n_old, axis=-1, keepdims=True)
                  + jnp.sum(s, axis=-1, keepdims=True))
            hh = num / jnp.maximum(jnp.abs(nq), jnp.exp(-m_t))

            m_new = m_t[CHUNK - 1:CHUNK, :]
            b_last = b_c[CHUNK - 1:CHUNK, :]
            w_state = jnp.exp(b_last - b_c + ig_c - m_new)
            decay = jnp.exp(b_last + m_prev - m_new)
            vw = (v32 * w_state).astype(BF16)
            c_s[h] = decay * c_old + lax.dot_general(
                vw, k, (((0,), (0,)), ((), ())), preferred_element_type=F32)
            n_s[h:h + 1, :] = decay * n_old + jnp.sum(w_state * k32, axis=0, keepdims=True)
            m_s[h:h + 1, :] = jnp.broadcast_to(m_new, (1, LANES))

            ha = _rms(hh, hng_ref[:, h * DH_A:(h + 1) * DH_A])
            o = proj[rows, P_O + h * DH_A:P_O + (h + 1) * DH_A]
            hcat[rows, h * DH_A:(h + 1) * DH_A] = (ha * jax.nn.sigmoid(o)).astype(BF16)

        for g in range(G_B):
            vb = proj[rows, P_VB + g * DG_B:P_VB + (g + 1) * DG_B]
            u = proj[rows, P_U + g * DG_B:P_U + (g + 1) * DG_B]
            vn = _rms(jax.nn.gelu(vb), vng_ref[:, g * DG_B:(g + 1) * DG_B])
            z = jnp.dot(wm[g], vn.astype(BF16), preferred_element_type=F32) + bs_ref[:, g:g + 1]
            hcat[rows, W_A + g * DG_B:W_A + (g + 1) * DG_B] = (jax.nn.gelu(u) * z).astype(BF16)

    def tile_pass(row0, proj_cur, proj_nxt, xn_next):
        conv_step(proj_cur)
        for c in range(n_chunks):
            chunk_step(proj_cur, c)
            cs = col_groups[c]
            proj_nxt[:, cs] = jnp.dot(xn_next, win_ref[:, cs], preferred_element_type=F32)
        out_ref[row0:row0 + tl, :] = x_ref[row0:row0 + tl, :] + jnp.dot(
            hcat[...], wout_ref[...], preferred_element_type=F32)

    tile_pass(0, proj_a, proj_b, _rms(x_ref[tl:2 * tl, :], gmix_ref[...]).astype(BF16))
    tile_pass(tl, proj_b, proj_a, _rms(xnext_ref[...], gmix_ref[...]).astype(BF16))

    @pl.when(j == nj - 1)
    def _():
        conv_out[0] = xpad[SUBLANES - (CONV_W - 1):SUBLANES, :]
        c_out[0] = c_s[...]
        n_out[0] = n_s[0:H_A, :]
        m_out[0] = m_s[0:H_A, :]


def _split_dot_left(a_bf16, b):
    hi = b.astype(BF16)
    lo = (b - hi.astype(F32)).astype(BF16)
    return (jnp.dot(a_bf16, hi, preferred_element_type=F32)
            + jnp.dot(a_bf16, lo, preferred_element_type=F32))


def _layer_spec(shape, layer, **kwargs):
    return pl.BlockSpec((None,) + tuple(shape), lambda *_: (layer,) + (0,) * len(shape), **kwargs)


def _mix_prompt(h_in, h_out, n_batch, seq, lw, layer):
    tl = max(t for t in range(CHUNK, TL_MIX + 1, CHUNK) if seq % (2 * t) == 0)
    nj = seq // (2 * tl)
    last_tile = n_batch * seq // tl - 1
    once = pl.Buffered(1)
    kern = functools.partial(_mix_prompt_kernel, tl=tl)
    return pl.pallas_call(
        kern,
        name="mix_prompt",
        grid=(n_batch, nj),
        in_specs=[
            pl.BlockSpec((2 * tl, D_MODEL), lambda b, j: (b * nj + j, 0)),
            pl.BlockSpec((tl, D_MODEL), lambda b, j: (jnp.minimum(2 * (b * nj + j) + 2, last_tile), 0)),
            pl.BlockSpec(memory_space=pl.ANY),
            _layer_spec((1, D_MODEL), layer),
            _layer_spec((D_MODEL, P_W), layer, pipeline_mode=once),
            _layer_spec((CONV_W, QK_W), layer),
            _layer_spec((1, QK_W), layer),
            _layer_spec((1, LANES), layer),
            _layer_spec((1, W_A), layer),
            _layer_spec((1, W_B), layer),
            _layer_spec((G_B, CHUNK, CHUNK), layer),
            _layer_spec((CHUNK, LANES), layer),
            _layer_spec((D_MODEL, D_MODEL), layer, pipeline_mode=once),
        ],
        out_specs=[
            pl.BlockSpec((2 * tl, D_MODEL), lambda b, j: (b * nj + j, 0)),
            pl.BlockSpec((1, H_A, DH_A, DH_A), lambda b, j: (b, 0, 0, 0)),
            pl.BlockSpec((1, H_A, DH_A), lambda b, j: (b, 0, 0)),
            pl.BlockSpec((1, H_A, LANES), lambda b, j: (b, 0, 0)),
            pl.BlockSpec((1, CONV_W - 1, QK_W), lambda b, j: (b, 0, 0)),
        ],
        out_shape=[
            jax.ShapeDtypeStruct(h_out.shape, F32),
            jax.ShapeDtypeStruct((n_batch, H_A, DH_A, DH_A), F32),
            jax.ShapeDtypeStruct((n_batch, H_A, DH_A), F32),
            jax.ShapeDtypeStruct((n_batch, H_A, LANES), F32),
            jax.ShapeDtypeStruct((n_batch, CONV_W - 1, QK_W), F32),
        ],
        scratch_shapes=[
            pltpu.VMEM((tl, P_W), F32),
            pltpu.VMEM((tl, P_W), F32),
            pltpu.VMEM((SUBLANES + tl, QK_W), F32),
            pltpu.VMEM((tl, D_MODEL), BF16),
            pltpu.VMEM((H_A, DH_A, DH_A), F32),
            pltpu.VMEM((SUBLANES, LANES), F32),
            pltpu.VMEM((SUBLANES, LANES), F32),
        ],
        input_output_aliases={2: 0},
        compiler_params=_cparams("arbitrary", "arbitrary"),
    )(h_in, h_in, h_out, lw["g_mix"], lw["w_in"], lw["conv_w"], lw["conv_b"], lw["gate_b"], lw["hn_g"], lw["vn_g"],
      lw["w_s"], lw["b_s_col"], lw["w_out"])


def _mix_sample_kernel(x_ref, hprev_ref, cprev_ref, gmix_ref, win_ref, cw_ref, cb_ref, gb_ref, hng_ref, vng_ref,
                       w00_ref, bs0_ref, wout_ref, sconv_ref, c_in, n_in, m_in,
                       out_ref, c_out, n_out, m_out, conv_out, vrow_out,
                       proj, hcat, *, bt):
    del hprev_ref, cprev_ref
    i = pl.program_id(0)
    ni = pl.num_programs(0)

    @pl.when(i == 0)
    def _():
        x = x_ref[...]
        xn = _rms(x, gmix_ref[...]).astype(BF16)
        proj[...] = jnp.dot(xn, win_ref[...], preferred_element_type=F32)
        pre = proj[:, P_QK:P_QK + QK_W]
        qk = cb_ref[...] + cw_ref[CONV_W - 1:CONV_W, :] * pre
        for jj in range(CONV_W - 1):
            qk = qk + cw_ref[jj:jj + 1, :] * sconv_ref[jj]
        for jj in range(CONV_W - 2):
            conv_out[jj] = sconv_ref[jj + 1]
        conv_out[CONV_W - 2] = pre
        qk = qk * jax.nn.sigmoid(qk)
        proj[:, 0:W_A] = qk[:, 0:W_A]
        proj[:, W_A:QK_W] = qk[:, W_A:QK_W] * (DH_A ** -0.5)
        for g in range(G_B):
            sl = slice(g * DG_B, (g + 1) * DG_B)
            vb = proj[:, P_VB + g * DG_B:P_VB + (g + 1) * DG_B]
            u = proj[:, P_U + g * DG_B:P_U + (g + 1) * DG_B]
            vn = _rms(jax.nn.gelu(vb), vng_ref[:, sl])
            vrow_out[:, sl] = vn
            z = w00_ref[:, sl] * vn + bs0_ref[:, sl]
            hcat[:, W_A + g * DG_B:W_A + (g + 1) * DG_B] = jax.nn.gelu(u) * z

    rows = pl.ds(pl.multiple_of(i * bt, bt), bt)
    gates = proj[rows, P_G:P_G + LANES] + gb_ref[...]
    lane = lax.broadcasted_iota(jnp.int32, (bt, LANES), 1)
    sub = lax.broadcasted_iota(jnp.int32, (bt, DH_A), 0)
    m_new_all = jnp.zeros((bt, LANES), F32)
    for h in range(H_A):
        ig = gates[:, h:h + 1]
        lf = _log_sigmoid(gates[:, H_A + h:H_A + h + 1])
        m_prev = m_in[rows, h:h + 1]
        inter = lf + m_prev
        m_t = jnp.maximum(inter, ig)
        w_inter = jnp.exp(inter - m_t)
        e_d = jnp.exp(ig - m_t)
        q = proj[rows, h * DH_A:(h + 1) * DH_A]
        k = proj[rows, W_A + h * DH_A:W_A + (h + 1) * DH_A]
        v = proj[rows, P_V + h * DH_A:P_V + (h + 1) * DH_A]
        n_old = n_in[h]
        s = jnp.sum(q * k, axis=-1, keepdims=True) * e_d
        vw = v * e_d
        vw_t = jnp.concatenate([vw, jnp.zeros((DH_A - bt, DH_A), F32)], axis=0).T
        qb = q.astype(BF16)
        qc = jnp.zeros((bt, DH_A), F32)
        for t in range(bt):
            c_old = c_in[t, h]
            r = _dot_nt(qb, c_old.astype(BF16))
            qc = jnp.where(sub == t, r, qc)
            c_out[t, h] = w_inter[t:t + 1, :] * c_old + vw_t[:, t:t + 1] * k[t:t + 1, :]
        num = w_inter * qc + s * v
        nq = w_inter * jnp.sum(q * n_old, axis=-1, keepdims=True) + s
        hh = num / jnp.maximum(jnp.abs(nq), jnp.exp(-m_t))
        n_out[h] = w_inter * n_old + e_d * k
        m_new_all = jnp.where(lane == h, m_t, m_new_all)
        ha = _rms(hh, hng_ref[:, h * DH_A:(h + 1) * DH_A])
        o = proj[rows, P_O + h * DH_A:P_O + (h + 1) * DH_A]
        hcat[rows, h * DH_A:(h + 1) * DH_A] = ha * jax.nn.sigmoid(o)
    m_out[rows, :] = m_new_all

    @pl.when(i == ni - 1)
    def _():
        out_ref[...] = x_ref[...] + jnp.dot(hcat[...].astype(BF16), wout_ref[...],
                                            preferred_element_type=F32)


def _mix_sample(h_in, h_out, c_all, lw, sconv_t, c0, n0_t, m0_pad, n_s, layer):
    bt = BT_SAMPLE
    ni = n_s // bt
    const = lambda i: (0, 0)
    in_blk = h_in.shape[0] // n_s - 1
    out_blk = h_out.shape[0] // n_s - 1
    kern = functools.partial(_mix_sample_kernel, bt=bt)
    return pl.pallas_call(
        kern,
        name="mix_sample",
        grid=(ni,),
        in_specs=[
            pl.BlockSpec((n_s, D_MODEL), lambda i: (in_blk, 0)),
            pl.BlockSpec(memory_space=pl.ANY),
            pl.BlockSpec(memory_space=pl.ANY),
            _layer_spec((1, D_MODEL), layer),
            _layer_spec((D_MODEL, P_W), layer),
            _layer_spec((CONV_W, QK_W), layer),
            _layer_spec((1, QK_W), layer),
            _layer_spec((1, LANES), layer),
            _layer_spec((1, W_A), layer),
            _layer_spec((1, W_B), layer),
            _layer_spec((1, W_B), layer),
            _layer_spec((1, W_B), layer),
            _layer_spec((D_MODEL, D_MODEL), layer),
            pl.BlockSpec((None, CONV_W - 1, n_s, QK_W), lambda i: (layer, 0, 0, 0)),
            pl.BlockSpec((None, bt, H_A, DH_A, DH_A), lambda i: (layer, i, 0, 0, 0)),
            pl.BlockSpec((None, H_A, bt, DH_A), lambda i: (layer, 0, i, 0)),
            pl.BlockSpec((None, n_s, LANES), lambda i: (layer, 0, 0)),
        ],
        out_specs=[
            pl.BlockSpec((n_s, D_MODEL), lambda i: (out_blk, 0)),
            pl.BlockSpec((None, bt, H_A, DH_A, DH_A), lambda i: (layer, i, 0, 0, 0)),
            pl.BlockSpec((H_A, bt, DH_A), lambda i: (0, i, 0)),
            pl.BlockSpec((n_s, LANES), const),
            pl.BlockSpec((CONV_W - 1, n_s, QK_W), lambda i: (0, 0, 0)),
            pl.BlockSpec((n_s, W_B), const),
        ],
        out_shape=[
            jax.ShapeDtypeStruct(h_out.shape, F32),
            jax.ShapeDtypeStruct(c_all.shape, F32),
            jax.ShapeDtypeStruct(n0_t.shape[1:], F32),
            jax.ShapeDtypeStruct((n_s, LANES), F32),
            jax.ShapeDtypeStruct((CONV_W - 1, n_s, QK_W), F32),
            jax.ShapeDtypeStruct((n_s, W_B), F32),
        ],
        scratch_shapes=[
            pltpu.VMEM((n_s, P_W), F32),
            pltpu.VMEM((n_s, D_MODEL), F32),
        ],
        input_output_aliases={1: 0, 2: 1},
        compiler_params=_cparams("arbitrary"),
    )(h_in, h_out, c_all, lw["g_mix"], lw["w_in"], lw["conv_w"], lw["conv_b"], lw["gate_b"], lw["hn_g"], lw["vn_g"],
      lw["w00_row"], lw["bs0_row"], lw["w_out"], sconv_t, c0, n0_t, m0_pad)


def _router_kernel(h_ref, g_ref, wr_ref, b_ref, route_ref, route_t_ref, tcnt_ref, *, tm):
    hn = _rms(h_ref[...], g_ref[...]).astype(BF16)
    logits = jnp.dot(hn, wr_ref[...], preferred_element_type=F32) + b_ref[...]

    lane_i = lax.broadcasted_iota(jnp.int32, (tm, LANES), 1)
    lane = lane_i.astype(F32)
    neg = -jnp.inf
    is_g = lane_i < N_GROUPS
    gl = jnp.where(is_g, logits, neg)
    gmax = jnp.max(gl, axis=-1, keepdims=True)
    g_sel = jnp.min(jnp.where(gl == gmax, lane, float(LANES)), axis=-1, keepdims=True)
    g_w = 1.0 / jnp.sum(jnp.where(is_g, jnp.exp(logits - gmax), 0.0), axis=-1, keepdims=True)

    e_lane = lane_i - N_GROUPS
    lane_grp = (e_lane >> 3).astype(F32)
    in_grp = (e_lane >= 0) & (e_lane < N_EXPERTS) & (lane_grp == g_sel)
    el = jnp.where(in_grp, logits, neg)
    v1 = jnp.max(el, axis=-1, keepdims=True)
    i1 = jnp.min(jnp.where(el == v1, lane, float(LANES)), axis=-1, keepdims=True)
    el2 = jnp.where(lane == i1, neg, el)
    v2 = jnp.max(el2, axis=-1, keepdims=True)
    i2 = jnp.min(jnp.where(el2 == v2, lane, float(LANES)), axis=-1, keepdims=True)
    e21 = jnp.exp(v2 - v1)
    den = 1.0 + e21
    w1 = g_w * (1.0 / den)
    w2 = g_w * (e21 / den)

    onehot = jnp.where((lane == i1) | (lane == i2), 1.0, 0.0)
    r_i = lax.broadcasted_iota(jnp.int32, (tm, tm), 0)
    c_i = lax.broadcasted_iota(jnp.int32, (tm, tm), 1)
    strict = jnp.where(r_i > c_i, 1.0, 0.0).astype(BF16)
    before = jnp.dot(strict, onehot.astype(BF16), preferred_element_type=F32)
    cnt = jnp.sum(onehot, axis=0, keepdims=True)
    cnt8 = jnp.floor((cnt + (SUBLANES - 1.0)) * (1.0 / SUBLANES)) * SUBLANES
    l_r = lax.broadcasted_iota(jnp.int32, (LANES, LANES), 0)
    l_c = lax.broadcasted_iota(jnp.int32, (LANES, LANES), 1)
    lanes_before = jnp.where(l_r < l_c, 1.0, 0.0).astype(BF16)
    start = _split_dot(jnp.broadcast_to(cnt8, (2 * SUBLANES, LANES)), lanes_before)[0:1, :]
    slot = start + before
    pos1 = jnp.sum(jnp.where(lane == i1, slot, 0.0), axis=-1, keepdims=True)
    pos2 = jnp.sum(jnp.where(lane == i2, slot, 0.0), axis=-1, keepdims=True)
    tcnt_ref[...] = jnp.broadcast_to(cnt8, tcnt_ref.shape)

    out = jnp.zeros((tm, LANES), F32)
    for idx, val in enumerate((pos1, pos2, w1, w2)):
        out = jnp.where(lane_i == idx, val, out)
    route_ref[...] = out
    route_t_ref[...] = out.T[0:SUBLANES, :]


def _router(h1, lw, layer):
    t_total = h1.shape[0]
    tm = _token_tile(t_total)
    n_tiles = t_total // tm
    return pl.pallas_call(
        functools.partial(_router_kernel, tm=tm),
        name="router",
        grid=(n_tiles,),
        in_specs=[
            pl.BlockSpec((tm, D_MODEL), lambda i: (i, 0)),
            _layer_spec((1, D_MODEL), layer),
            _layer_spec((D_MODEL, LANES), layer),
            _layer_spec((1, LANES), layer),
        ],
        out_specs=[
            pl.BlockSpec((tm, LANES), lambda i: (i, 0)),
            pl.BlockSpec((SUBLANES, tm), lambda i: (0, i)),
            pl.BlockSpec((None, SUBLANES, LANES), lambda i: (i, 0, 0)),
        ],
        out_shape=[
            jax.ShapeDtypeStruct((t_total, LANES), F32),
            jax.ShapeDtypeStruct((SUBLANES, t_total), F32),
            jax.ShapeDtypeStruct((n_tiles, SUBLANES, LANES), F32),
        ],
        compiler_params=_cparams("arbitrary"),
    )(h1, lw["g_ffn"], lw["wr"], lw["br"])


def _tile_slots(tm):
    raw = tm * TOP_K + N_EXPERTS * (SUBLANES - 1)
    return -(-raw // LANES) * LANES


def _token_tile(t_total):
    tm = TM_TOK
    while t_total % tm:
        tm //= 2
    return tm


def _run_copies(tile, udst_ref, tunits_ref, units_per_tile, make_copy):
    base = tile * units_per_tile

    def per_unit(u, carry):
        make_copy(pl.multiple_of(u * SUBLANES, SUBLANES),
                  pl.multiple_of(udst_ref[base + u], SUBLANES), SUBLANES).start()
        return carry

    lax.fori_loop(0, tunits_ref[tile], per_unit, 0)


def _drain_copies(n_units, make_copy):
    def wait_group(u, carry):
        make_copy(0, 0, DRAIN_GROUP * SUBLANES).wait()
        return carry

    def wait_one(u, carry):
        make_copy(0, 0, SUBLANES).wait()
        return carry

    lax.fori_loop(0, lax.div(n_units, DRAIN_GROUP), wait_group, 0)
    lax.fori_loop(0, lax.rem(n_units, DRAIN_GROUP), wait_one, 0)


def _dispatch_kernel(udst_ref, tunits_ref, h_ref, g_ref, rt_ref, xs_in, xs_out,
                     srt, sem, *, tm, slots):
    del xs_in
    i = pl.program_id(0)
    n = pl.num_programs(0)
    buf = lax.rem(i, 2)

    def copy_maker(b):
        def make_copy(tile_row, sorted_row, rows):
            return pltpu.make_async_copy(srt.at[b, pl.ds(tile_row, rows)],
                                         xs_out.at[pl.ds(sorted_row, rows), pl.ds(0, HALF_D)], sem.at[b])
        return make_copy

    @pl.when(i >= 2)
    def _():
        _drain_copies(tunits_ref[i - 2], copy_maker(buf))

    xn = _rms(h_ref[...], g_ref[...]).astype(BF16)
    slot_id = lax.broadcasted_iota(jnp.int32, (slots, tm), 0).astype(F32)
    sel = (slot_id == rt_ref[0:1, :]) | (slot_id == rt_ref[1:2, :])
    srt[buf] = _pack_rows(jnp.dot(jnp.where(sel, 1.0, 0.0).astype(BF16), xn, preferred_element_type=F32))
    _run_copies(i, udst_ref, tunits_ref, slots // SUBLANES, copy_maker(buf))

    @pl.when(i == n - 1)
    def _():
        @pl.when(i >= 1)
        def _():
            _drain_copies(tunits_ref[i - 1], copy_maker(1 - buf))

        _drain_copies(tunits_ref[i], copy_maker(buf))


def _dispatch(tables, h1, route_t, lw, xs, layer):
    t_total = h1.shape[0]
    tm = _token_tile(t_total)
    slots = _tile_slots(tm)
    return pl.pallas_call(
        functools.partial(_dispatch_kernel, tm=tm, slots=slots),
        name="dispatch",
        grid_spec=pltpu.PrefetchScalarGridSpec(
            num_scalar_prefetch=2,
            grid=(t_total // tm,),
            in_specs=[
                pl.BlockSpec((tm, D_MODEL), lambda i, *_: (i, 0)),
                _layer_spec((1, D_MODEL), layer),
                pl.BlockSpec((SUBLANES, tm), lambda i, *_: (0, i)),
                pl.BlockSpec(memory_space=pl.ANY),
            ],
            out_specs=pl.BlockSpec(memory_space=pl.ANY),
            scratch_shapes=[pltpu.VMEM((2, slots, HALF_D), U32), pltpu.SemaphoreType.DMA((2,))],
        ),
        out_shape=jax.ShapeDtypeStruct(xs.shape, U32),
        input_output_aliases={5: 0},
        compiler_params=_cparams("arbitrary"),
    )(*tables, h1, lw["g_ffn"], route_t, xs)


def _ffn_kernel(bstart_ref, nblk_ref, nused_ref, w1_ref, w3_ref, w2_ref, xs_ref, ys_ref,
                w1b, w3b, w2b, xbuf, ybuf, sem_in, sem_out):
    e = pl.program_id(0)
    g0 = bstart_ref[e]
    nb = nblk_ref[e]
    n_used = nused_ref[0]

    def in_copy(g):
        slot = lax.rem(g, FFN_RING)
        r = pl.multiple_of(g * FFN_BLK, FFN_BLK)
        return pltpu.make_async_copy(xs_ref.at[pl.ds(r, FFN_BLK), pl.ds(0, HALF_D)], xbuf.at[slot],
                                     sem_in.at[slot])

    def out_copy(g):
        slot = lax.rem(g, FFN_RING)
        r = pl.multiple_of(g * FFN_BLK, FFN_BLK)
        return pltpu.make_async_copy(ybuf.at[slot], ys_ref.at[pl.ds(r, FFN_BLK)], sem_out.at[slot])

    @pl.when(e == 0)
    def _():
        for ahead in range(FFN_RING - 1):
            @pl.when(ahead < n_used)
            def _():
                in_copy(jnp.int32(ahead)).start()

    @pl.when(nb > 0)
    def _():
        w1b[...] = w1_ref[...].astype(BF16)
        w3b[...] = w3_ref[...].astype(BF16)
        w2b[...] = w2_ref[...].astype(BF16)

    def block_step(k, carry):
        g = g0 + k
        slot = lax.rem(g, FFN_RING)
        in_copy(g).wait()

        @pl.when(g + (FFN_RING - 1) < n_used)
        def _():
            in_copy(g + (FFN_RING - 1)).start()

        @pl.when(g >= FFN_RING)
        def _():
            out_copy(g - FFN_RING).wait()

        x_lo, x_hi = _unpack_rows(xbuf[slot])
        a = (jnp.dot(x_lo, w1b[0:HALF_D, :], preferred_element_type=F32)
             + jnp.dot(x_hi, w1b[HALF_D:, :], preferred_element_type=F32))
        b = (jnp.dot(x_lo, w3b[0:HALF_D, :], preferred_element_type=F32)
             + jnp.dot(x_hi, w3b[HALF_D:, :], preferred_element_type=F32))
        hmid = (a * jax.nn.sigmoid(a) * b).astype(BF16)
        y = jnp.dot(hmid, w2b[...], preferred_element_type=F32)
        ybuf[slot] = lax.bitcast_convert_type(y, U32)
        out_copy(g).start()
        return carry

    lax.fori_loop(0, nb, block_step, 0)

    @pl.when(e == pl.num_programs(0) - 1)
    def _():
        for back in range(FFN_RING, 0, -1):
            @pl.when(n_used >= back)
            def _():
                out_copy(n_used - back).wait()


def _ffn(ffn_tables, xs, w1, w3, w2, layer):
    def w_map(e, *_):
        return (layer, e, 0, 0)

    return pl.pallas_call(
        _ffn_kernel,
        name="expert_ffn",
        grid_spec=pltpu.PrefetchScalarGridSpec(
            num_scalar_prefetch=3,
            grid=(N_EXPERTS,),
            in_specs=[
                pl.BlockSpec((None, None, D_MODEL, D_FF), w_map),
                pl.BlockSpec((None, None, D_MODEL, D_FF), w_map),
                pl.BlockSpec((None, None, D_FF, D_MODEL), w_map),
                pl.BlockSpec(memory_space=pl.ANY),
            ],
            out_specs=pl.BlockSpec(memory_space=pl.ANY),
            scratch_shapes=[
                pltpu.VMEM((D_MODEL, D_FF), BF16),
                pltpu.VMEM((D_MODEL, D_FF), BF16),
                pltpu.VMEM((D_FF, D_MODEL), BF16),
                pltpu.VMEM((FFN_RING, FFN_BLK, HALF_D), U32),
                pltpu.VMEM((FFN_RING, FFN_BLK, D_MODEL), U32),
                pltpu.SemaphoreType.DMA((FFN_RING,)),
                pltpu.SemaphoreType.DMA((FFN_RING,)),
            ],
        ),
        out_shape=jax.ShapeDtypeStruct(xs.shape, U32),
        input_output_aliases={6: 0},
        compiler_params=_cparams("arbitrary"),
    )(*ffn_tables, w1, w3, w2, xs)


def _ple_kernel(udst_ref, tunits_ref, h_ref, route_ref, p_ref, ps_ref, wpg_ref,
                wple_ref, pg_ref, fg_ref, ys_ref, *rest, tm, slots, n_s, final):
    if final:
        yp_ref, ysm_ref, ysrt, sem = rest
    else:
        out_ref, ysrt, sem = rest
    i = pl.program_id(0)
    n = pl.num_programs(0)
    buf = lax.rem(i, 2)

    def copy_maker(b):
        def make_copy(tile_row, sorted_row, rows):
            return pltpu.make_async_copy(ys_ref.at[pl.ds(sorted_row, rows)],
                                         ysrt.at[b, pl.ds(tile_row, rows)], sem.at[b])
        return make_copy

    @pl.when(i == 0)
    def _():
        ysrt[...] = jnp.zeros_like(ysrt)
        _run_copies(i, udst_ref, tunits_ref, slots // SUBLANES, copy_maker(buf))

    @pl.when(i + 1 < n)
    def _():
        _run_copies(i + 1, udst_ref, tunits_ref, slots // SUBLANES, copy_maker(1 - buf))

    p = p_ref[...]
    p_last = jnp.concatenate([p[0:tm - n_s, :], ps_ref[...]], axis=0)
    p = jnp.where(i == n - 1, p_last, p)
    pe = _rms(jnp.dot(p.astype(BF16), wple_ref[...], preferred_element_type=F32), pg_ref[...])
    _drain_copies(tunits_ref[i], copy_maker(buf))

    route = route_ref[...]
    y = lax.bitcast_convert_type(ysrt[buf], F32)
    y_head = y.astype(BF16)
    y_rem = (y - y_head.astype(F32)).astype(BF16)
    slot_id = lax.broadcasted_iota(jnp.int32, (tm, slots), 1).astype(F32)
    moe = None
    weighted_pick = jnp.zeros((tm, slots), F32)
    for kk in range(TOP_K):
        hit = slot_id == route[:, kk:kk + 1]
        w_kk = route[:, TOP_K + kk:TOP_K + kk + 1]
        term = w_kk * jnp.dot(jnp.where(hit, 1.0, 0.0).astype(BF16), y_head, preferred_element_type=F32)
        moe = term if moe is None else moe + term
        weighted_pick = jnp.where(hit, w_kk, weighted_pick)
    moe = moe + jnp.dot(weighted_pick.astype(BF16), y_rem, preferred_element_type=F32)
    h2 = h_ref[...] + moe
    gate = jax.nn.sigmoid(jnp.dot(h2.astype(BF16), wpg_ref[...], preferred_element_type=F32))
    out = h2 + gate * pe
    if final:
        fin = _rms(out, fg_ref[...])
        yp_ref[...] = fin

        @pl.when(i == pl.num_programs(0) - 1)
        def _():
            ysm_ref[...] = fin[tm - n_s:tm, :]
    else:
        out_ref[...] = out


def _ple(tables, h1, route, p_prompt, p_sample, lw, final_g, ys, n_s, layer, final, in_place):
    t_total = h1.shape[0]
    tm = _token_tile(t_total)
    slots = _tile_slots(tm)
    assert n_s <= tm and (t_total - n_s) % tm == tm - n_s
    const = lambda i, *_: (0, 0)
    tok = lambda i, *_: (i, 0)
    if final:
        out_specs = [pl.BlockSpec((tm, D_MODEL), tok), pl.BlockSpec((n_s, D_MODEL), const)]
        out_shape = [jax.ShapeDtypeStruct((t_total - n_s, D_MODEL), F32),
                     jax.ShapeDtypeStruct((n_s, D_MODEL), F32)]
        aliases = {}
    else:
        out_specs = [pl.BlockSpec((tm, D_MODEL), tok)]
        out_shape = [jax.ShapeDtypeStruct((t_total, D_MODEL), F32)]
        aliases = {2: 0} if in_place else {}
    return pl.pallas_call(
        functools.partial(_ple_kernel, tm=tm, slots=slots, n_s=n_s, final=final),
        name="combine_ple",
        grid_spec=pltpu.PrefetchScalarGridSpec(
            num_scalar_prefetch=2,
            grid=(t_total // tm,),
            in_specs=[
                pl.BlockSpec((tm, D_MODEL), tok),
                pl.BlockSpec((tm, LANES), tok),
                pl.BlockSpec((None, tm, PLE_DIM), lambda i, *_: (layer, i, 0)),
                pl.BlockSpec((None, n_s, PLE_DIM), lambda i, *_: (layer, 0, 0)),
                _layer_spec((D_MODEL, D_MODEL), layer),
                _layer_spec((PLE_DIM, D_MODEL), layer),
                _layer_spec((1, D_MODEL), layer),
                pl.BlockSpec((1, D_MODEL), const),
                pl.BlockSpec(memory_space=pl.ANY),
            ],
            out_specs=out_specs,
            scratch_shapes=[pltpu.VMEM((2, slots, D_MODEL), U32), pltpu.SemaphoreType.DMA((2,))],
        ),
        out_shape=out_shape,
        input_output_aliases=aliases,
        compiler_params=_cparams("arbitrary"),
    )(*tables, h1, route, p_prompt, p_sample, lw["w_pg"], lw["w_ple"], lw["ple_g"], final_g, ys)


def _stacked_weights(norm_mix_g, w_in, conv_w, conv_b, b_igate, b_fgate, hnorm_g, vnorm_g, w_s, b_s, w_out,
                     norm_ffn_g, router_g_w, router_g_b, router_e_w, router_e_b, w_ple, ple_norm_g, w_ple_gate):
    a0 = QK_W + 2 * W_A
    lane_pad = lambda x, used: jnp.pad(x, ((0, 0),) * (x.ndim - 1) + ((0, LANES - used),))
    w_packed = jnp.concatenate(
        [w_in[:, :, :a0], w_in[:, :, a0 + 2 * H_A:], lane_pad(w_in[:, :, a0:a0 + 2 * H_A], 2 * H_A)],
        axis=2).astype(BF16)
    gate_b = lane_pad(jnp.concatenate([b_igate, b_fgate], axis=1), 2 * H_A)[:, None, :]
    wr = lane_pad(jnp.concatenate([router_g_w, router_e_w], axis=2), N_GROUPS + N_EXPERTS).astype(BF16)
    br = lane_pad(jnp.concatenate([router_g_b, router_e_b], axis=1), N_GROUPS + N_EXPERTS)[:, None, :]
    return {
        "g_mix": norm_mix_g[:, None, :],
        "w_in": w_packed,
        "conv_w": conv_w,
        "conv_b": conv_b[:, None, :],
        "gate_b": gate_b,
        "hn_g": hnorm_g[:, None, :],
        "vn_g": vnorm_g[:, None, :],
        "w_s": w_s,
        "b_s_col": lane_pad(jnp.transpose(b_s, (0, 2, 1)), G_B),
        "w00_row": jnp.repeat(w_s[:, :, 0, 0], DG_B, axis=1)[:, None, :],
        "bs0_row": jnp.repeat(b_s[:, :, 0], DG_B, axis=1)[:, None, :],
        "w_out": w_out.astype(BF16),
        "g_ffn": norm_ffn_g[:, None, :],
        "wr": wr,
        "br": br,
        "w_pg": w_ple_gate.astype(BF16),
        "w_ple": w_ple.astype(BF16),
        "ple_g": ple_norm_g[:, None, :],
    }


def _routing_tables(tcnt, units_per_tile):
    c8 = tcnt[:, 0, N_GROUPS:N_GROUPS + N_EXPERTS].astype(jnp.int32)
    blocks = (jnp.sum(c8, axis=0) + FFN_BLK - 1) // FFN_BLK
    blk_end = jnp.cumsum(blocks)
    pstart = (blk_end - blocks) * FFN_BLK
    run_dst = pstart[None, :] + jnp.cumsum(c8, axis=0) - c8
    run_src = jnp.cumsum(c8, axis=1) - c8
    slot = (jnp.arange(units_per_tile, dtype=jnp.int32) * SUBLANES)[None, :, None]
    in_run = (slot >= run_src[:, None, :]) & (slot < (run_src + c8)[:, None, :])
    unit_dst = jnp.sum(jnp.where(in_run, run_dst[:, None, :] + slot - run_src[:, None, :], 0), axis=2)
    tables = (unit_dst.reshape(-1), jnp.sum(c8, axis=1) // SUBLANES)
    ffn_tables = (blk_end - blocks, blocks, blk_end[-1:])
    return tables, ffn_tables


def kernel(x_prompt, x_sample, state_C, state_n, state_m, state_conv, p_prompt, p_sample, norm_mix_g, w_in, conv_w, conv_b, b_igate, b_fgate, hnorm_g, vnorm_g, w_s, b_s, w_out, norm_ffn_g, router_g_w, router_g_b, router_e_w, router_e_b, w1, w3, w2, w_ple, ple_norm_g, w_ple_gate, final_norm_g):
    n_batch, seq, _ = x_prompt.shape
    n_s = x_sample.shape[0]
    depth = w_in.shape[0]
    t_p = n_batch * seq
    t_total = t_p + n_s
    assert seq % CHUNK == 0 and t_p % n_s == 0 and n_s % BT_SAMPLE == 0

    n_tiles = t_total // _token_tile(t_total)
    max_rows = t_total * TOP_K + N_EXPERTS * ((SUBLANES - 1) * n_tiles + FFN_BLK - 1)
    n_blocks = max_rows // FFN_BLK
    xs = jnp.zeros((n_blocks * FFN_BLK, D_MODEL), U32)

    p_prompt_t = p_prompt.reshape(depth, t_p, PLE_DIM)
    p_sample_t = p_sample.reshape(depth, n_s, PLE_DIM)
    sconv_t = jnp.transpose(state_conv, (0, 2, 1, 3))
    n0_t = jnp.transpose(state_n, (0, 2, 1, 3))
    m0_pad = jnp.pad(state_m, ((0, 0), (0, 0), (0, LANES - H_A)))
    final_g = final_norm_g[None, :]

    src_p = x_prompt.reshape(t_p, D_MODEL)
    src_s = x_sample.reshape(n_s, D_MODEL)
    spare = jnp.zeros((t_total, D_MODEL), F32)
    c_all = jnp.zeros(state_C.shape, F32)
    outs = {k: [] for k in ("Cp", "np", "mp", "cbp", "ns", "ms", "cbs", "vs")}
    res = None
    lw = _stacked_weights(norm_mix_g, w_in, conv_w, conv_b, b_igate, b_fgate, hnorm_g, vnorm_g, w_s, b_s,
                          w_out, norm_ffn_g, router_g_w, router_g_b, router_e_w, router_e_b, w_ple,
                          ple_norm_g, w_ple_gate)
    for l in range(depth):
        h_mix, c_p, n_p, m_p, cb_p = _mix_prompt(src_p, spare, n_batch, seq, lw, l)
        h_mix, c_all, n_s_t, m_s, cb_s, v_s = _mix_sample(src_s, h_mix, c_all, lw, sconv_t, state_C, n0_t,
                                                          m0_pad, n_s, l)
        route, route_t, tcnt = _router(h_mix, lw, l)
        tables, ffn_tables = _routing_tables(tcnt, _tile_slots(_token_tile(t_total)) // SUBLANES)
        xs = _dispatch(tables, h_mix, route_t, lw, xs, l)
        xs = _ffn(ffn_tables, xs, w1, w3, w2, l)
        final = l == depth - 1
        res = _ple(tables, h_mix, route, p_prompt_t, p_sample_t, lw, final_g, xs, n_s, l, final, in_place=l > 0)
        if l == 0:
            spare = h_mix
        else:
            spare = src_p
        src_p = src_s = res[0]
        outs["Cp"].append(c_p)
        outs["np"].append(n_p)
        outs["mp"].append(m_p[:, :, 0])
        outs["cbp"].append(cb_p)
        outs["ns"].append(jnp.transpose(n_s_t, (1, 0, 2)))
        outs["ms"].append(m_s[:, 0:H_A])
        outs["cbs"].append(jnp.transpose(cb_s, (1, 0, 2)))
        outs["vs"].append(v_s[:, None, :])

    y_prompt = res[0].reshape(n_batch, seq, D_MODEL)
    y_sample = res[1].reshape(n_s, 1, D_MODEL)
    st = lambda k: jnp.stack(outs[k])
    return (y_prompt, y_sample, st("Cp"), st("np"), st("mp"), st("cbp"),
            c_all, st("ns"), st("ms"), st("cbs"), st("vs"))
```

```python
import functools

import jax
import jax.numpy as jnp
from jax import lax
from jax.experimental import pallas as pl
from jax.experimental.pallas import tpu as pltpu

F32 = jnp.float32
BF16 = jnp.bfloat16
U32 = jnp.uint32

D_MODEL = 1024
W_A = 512
H_A = 4
DH_A = 128
W_B = 512
G_B = 4
DG_B = 128
CHUNK = 128
CONV_W = 4
QK_W = 2 * W_A
N_GROUPS = 4
EXPERTS_PER_GROUP = 8
N_EXPERTS = N_GROUPS * EXPERTS_PER_GROUP
TOP_K = 2
D_FF = 512
PLE_DIM = 256
EPS = 1e-6
HALF_D = D_MODEL // 2

LANES = 128
SUBLANES = 8
VMEM_LIMIT_BYTES = 56 * 1024 * 1024

P_QK = 0
P_V = QK_W
P_O = P_V + W_A
P_U = P_O + W_A
P_VB = P_U + W_B
P_G = P_VB + W_B
P_W = P_G + LANES

TL_MIX = 512
BT_SAMPLE = 8
TM_TOK = 384
FFN_BLK = 256
FFN_RING = 4
DRAIN_GROUP = 16
N_COPY_TABLES = 7


def _cparams(*sem):
    return pltpu.CompilerParams(dimension_semantics=sem, vmem_limit_bytes=VMEM_LIMIT_BYTES)


def _rms(x, g):
    return x * lax.rsqrt(jnp.mean(x * x, axis=-1, keepdims=True) + EPS) * g


def _log_sigmoid(x):
    return -(jnp.maximum(-x, 0.0) + jnp.log1p(jnp.exp(-jnp.abs(x))))


def _split_dot(a, b_bf16):
    hi = a.astype(BF16)
    lo = (a - hi.astype(F32)).astype(BF16)
    return (jnp.dot(hi, b_bf16, preferred_element_type=F32)
            + jnp.dot(lo, b_bf16, preferred_element_type=F32))


def _dot_nt(a, b):
    return lax.dot_general(a, b, (((1,), (1,)), ((), ())), preferred_element_type=F32)


def _pack_rows(x):
    lo = lax.bitcast_convert_type(x[:, :HALF_D], U32) >> 16
    hi = lax.bitcast_convert_type(x[:, HALF_D:], U32) & jnp.uint32(0xFFFF0000)
    return lo | hi


def _unpack_rows(u):
    lo = lax.bitcast_convert_type(u << 16, F32)
    hi = lax.bitcast_convert_type(u & jnp.uint32(0xFFFF0000), F32)
    return lo.astype(BF16), hi.astype(BF16)


def _mix_prompt_kernel(x_ref, xnext_ref, hprev_ref, gmix_ref, win_ref, cw_ref, cb_ref, gb_ref, hng_ref,
                       vng_ref, ws_ref, bs_ref, wout_ref,
                       out_ref, c_out, n_out, m_out, conv_out,
                       proj_a, proj_b, xpad, hcat, c_s, n_s, m_s, *, tl):
    del hprev_ref
    b = pl.program_id(0)
    j = pl.program_id(1)
    nj = pl.num_programs(1)
    n_chunks = tl // CHUNK
    bounds = [(P_W // LANES * c // n_chunks) * LANES for c in range(n_chunks + 1)]
    col_groups = [slice(bounds[c], bounds[c + 1]) for c in range(n_chunks)]

    @pl.when(j == 0)
    def _():
        c_s[...] = jnp.zeros_like(c_s)
        n_s[...] = jnp.zeros_like(n_s)
        m_s[...] = jnp.zeros_like(m_s)
        xpad[0:SUBLANES, :] = jnp.zeros((SUBLANES, QK_W), F32)

    @pl.when((b == 0) & (j == 0))
    def _():
        xn0 = _rms(x_ref[0:tl, :], gmix_ref[...]).astype(BF16)
        proj_a[...] = jnp.dot(xn0, win_ref[...], preferred_element_type=F32)

    row = lax.broadcasted_iota(jnp.int32, (CHUNK, CHUNK), 0)
    col = lax.broadcasted_iota(jnp.int32, (CHUNK, CHUNK), 1)
    causal = row >= col
    tril = jnp.where(causal, 1.0, 0.0).astype(BF16)
    triu = jnp.where(row <= col, 1.0, 0.0).astype(BF16)
    lane = lax.broadcasted_iota(jnp.int32, (CHUNK, LANES), 1)
    wm = [jnp.where(causal, ws_ref[g], 0.0).astype(BF16) for g in range(G_B)]

    def conv_step(proj):
        pre = proj[:, P_QK:P_QK + QK_W]
        xpad[SUBLANES:SUBLANES + tl, :] = pre
        qk = cb_ref[...] + cw_ref[CONV_W - 1:CONV_W, :] * pre
        for jj in range(1, CONV_W):
            qk = qk + cw_ref[CONV_W - 1 - jj:CONV_W - jj, :] * xpad[SUBLANES - jj:SUBLANES - jj + tl, :]
        xpad[SUBLANES - (CONV_W - 1):SUBLANES, :] = pre[tl - (CONV_W - 1):tl, :]
        qk = qk * jax.nn.sigmoid(qk)
        proj[:, 0:W_A] = qk[:, 0:W_A]
        proj[:, W_A:QK_W] = qk[:, W_A:QK_W] * (DH_A ** -0.5)

    def chunk_step(proj, c):
        rows = slice(c * CHUNK, (c + 1) * CHUNK)
        gl = proj[rows, P_G:P_G + LANES] + gb_ref[...]
        gl = jnp.where(lane >= H_A, _log_sigmoid(gl), gl)
        glt = gl.T
        b_col_all = _split_dot_left(tril, gl)
        b_row_all = _split_dot(glt[0:2 * SUBLANES, :], triu)

        for h in range(H_A):
            q32 = proj[rows, h * DH_A:(h + 1) * DH_A]
            k32 = proj[rows, W_A + h * DH_A:W_A + (h + 1) * DH_A]
            v32 = proj[rows, P_V + h * DH_A:P_V + (h + 1) * DH_A]
            q = q32.astype(BF16)
            k = k32.astype(BF16)
            ig_c = gl[:, h:h + 1]
            ig_r = glt[h:h + 1, :]
            b_c = b_col_all[:, H_A + h:H_A + h + 1]
            b_r = b_row_all[H_A + h:H_A + h + 1, :]
            c_old = c_s[h]
            n_old = n_s[h:h + 1, :]
            m_prev = m_s[h:h + 1, 0:1]

            d_log = jnp.where(causal, b_c - b_r + ig_r, -jnp.inf)
            inter = b_c + m_prev
            m_t = jnp.maximum(inter, jnp.max(d_log, axis=-1, keepdims=True))
            s = _dot_nt(q, k) * jnp.exp(d_log - m_t)
            w_inter = jnp.exp(inter - m_t)
            num = (w_inter * _dot_nt(q, c_old.astype(BF16))
                   + jnp.dot(s.astype(BF16), v32.astype(BF16), preferred_element_type=F32))
            nq = (w_inter * jnp.sum(q32 * n_old, axis=-1, keepdims=True)
                  + jnp.sum(s, axis=-1, keepdims=True))
            hh = num / jnp.maximum(jnp.abs(nq), jnp.exp(-m_t))

            m_new = m_t[CHUNK - 1:CHUNK, :]
            b_last = b_c[CHUNK - 1:CHUNK, :]
            w_state = jnp.exp(b_last - b_c + ig_c - m_new)
            decay = jnp.exp(b_last + m_prev - m_new)
            vw = (v32 * w_state).astype(BF16)
            c_s[h] = decay * c_old + lax.dot_general(
                vw, k, (((0,), (0,)), ((), ())), preferred_element_type=F32)
            n_s[h:h + 1, :] = decay * n_old + jnp.sum(w_state * k32, axis=0, keepdims=True)
            m_s[h:h + 1, :] = jnp.broadcast_to(m_new, (1, LANES))

            ha = _rms(hh, hng_ref[:, h * DH_A:(h + 1) * DH_A])
            o = proj[rows, P_O + h * DH_A:P_O + (h + 1) * DH_A]
            hcat[rows, h * DH_A:(h + 1) * DH_A] = (ha * jax.nn.sigmoid(o)).astype(BF16)

        for g in range(G_B):
            vb = proj[rows, P_VB + g * DG_B:P_VB + (g + 1) * DG_B]
            u = proj[rows, P_U + g * DG_B:P_U + (g + 1) * DG_B]
            vn = _rms(jax.nn.gelu(vb), vng_ref[:, g * DG_B:(g + 1) * DG_B])
            z = jnp.dot(wm[g], vn.astype(BF16), preferred_element_type=F32) + bs_ref[:, g:g + 1]
            hcat[rows, W_A + g * DG_B:W_A + (g + 1) * DG_B] = (jax.nn.gelu(u) * z).astype(BF16)

    def tile_pass(row0, proj_cur, proj_nxt, xn_next):
        conv_step(proj_cur)
        for c in range(n_chunks):
            chunk_step(proj_cur, c)
            cs = col_groups[c]
            proj_nxt[:, cs] = jnp.dot(xn_next, win_ref[:, cs], preferred_element_type=F32)
        out_ref[row0:row0 + tl, :] = x_ref[row0:row0 + tl, :] + jnp.dot(
            hcat[...], wout_ref[...], preferred_element_type=F32)

    tile_pass(0, proj_a, proj_b, _rms(x_ref[tl:2 * tl, :], gmix_ref[...]).astype(BF16))
    tile_pass(tl, proj_b, proj_a, _rms(xnext_ref[...], gmix_ref[...]).astype(BF16))

    @pl.when(j == nj - 1)
    def _():
        conv_out[0] = xpad[SUBLANES - (CONV_W - 1):SUBLANES, :]
        c_out[0] = c_s[...]
        n_out[0] = n_s[0:H_A, :]
        m_out[0] = m_s[0:H_A, :]


def _split_dot_left(a_bf16, b):
    hi = b.astype(BF16)
    lo = (b - hi.astype(F32)).astype(BF16)
    return (jnp.dot(a_bf16, hi, preferred_element_type=F32)
            + jnp.dot(a_bf16, lo, preferred_element_type=F32))


def _layer_spec(shape, layer, **kwargs):
    return pl.BlockSpec((None,) + tuple(shape), lambda *_: (layer,) + (0,) * len(shape), **kwargs)


def _mix_prompt(h_in, h_out, n_batch, seq, lw, layer):
    tl = max(t for t in range(CHUNK, TL_MIX + 1, CHUNK) if seq % (2 * t) == 0)
    nj = seq // (2 * tl)
    last_tile = n_batch * seq // tl - 1
    once = pl.Buffered(1)
    kern = functools.partial(_mix_prompt_kernel, tl=tl)
    return pl.pallas_call(
        kern,
        name="mix_prompt",
        grid=(n_batch, nj),
        in_specs=[
            pl.BlockSpec((2 * tl, D_MODEL), lambda b, j: (b * nj + j, 0)),
            pl.BlockSpec((tl, D_MODEL), lambda b, j: (jnp.minimum(2 * (b * nj + j) + 2, last_tile), 0)),
            pl.BlockSpec(memory_space=pl.ANY),
            _layer_spec((1, D_MODEL), layer),
            _layer_spec((D_MODEL, P_W), layer, pipeline_mode=once),
            _layer_spec((CONV_W, QK_W), layer),
            _layer_spec((1, QK_W), layer),
            _layer_spec((1, LANES), layer),
            _layer_spec((1, W_A), layer),
            _layer_spec((1, W_B), layer),
            _layer_spec((G_B, CHUNK, CHUNK), layer),
            _layer_spec((CHUNK, LANES), layer),
            _layer_spec((D_MODEL, D_MODEL), layer, pipeline_mode=once),
        ],
        out_specs=[
            pl.BlockSpec((2 * tl, D_MODEL), lambda b, j: (b * nj + j, 0)),
            pl.BlockSpec((1, H_A, DH_A, DH_A), lambda b, j: (b, 0, 0, 0)),
            pl.BlockSpec((1, H_A, DH_A), lambda b, j: (b, 0, 0)),
            pl.BlockSpec((1, H_A, LANES), lambda b, j: (b, 0, 0)),
            pl.BlockSpec((1, CONV_W - 1, QK_W), lambda b, j: (b, 0, 0)),
        ],
        out_shape=[
            jax.ShapeDtypeStruct(h_out.shape, F32),
            jax.ShapeDtypeStruct((n_batch, H_A, DH_A, DH_A), F32),
            jax.ShapeDtypeStruct((n_batch, H_A, DH_A), F32),
            jax.ShapeDtypeStruct((n_batch, H_A, LANES), F32),
            jax.ShapeDtypeStruct((n_batch, CONV_W - 1, QK_W), F32),
        ],
        scratch_shapes=[
            pltpu.VMEM((tl, P_W), F32),
            pltpu.VMEM((tl, P_W), F32),
            pltpu.VMEM((SUBLANES + tl, QK_W), F32),
            pltpu.VMEM((tl, D_MODEL), BF16),
            pltpu.VMEM((H_A, DH_A, DH_A), F32),
            pltpu.VMEM((SUBLANES, LANES), F32),
            pltpu.VMEM((SUBLANES, LANES), F32),
        ],
        input_output_aliases={2: 0},
        compiler_params=_cparams("arbitrary", "arbitrary"),
    )(h_in, h_in, h_out, lw["g_mix"], lw["w_in"], lw["conv_w"], lw["conv_b"], lw["gate_b"], lw["hn_g"], lw["vn_g"],
      lw["w_s"], lw["b_s_col"], lw["w_out"])


def _mix_sample_kernel(x_ref, hprev_ref, cprev_ref, gmix_ref, win_ref, cw_ref, cb_ref, gb_ref, hng_ref, vng_ref,
                       w00_ref, bs0_ref, wout_ref, sconv_ref, c_in, n_in, m_in,
                       out_ref, c_out, n_out, m_out, conv_out, vrow_out,
                       proj, hcat, *, bt):
    del hprev_ref, cprev_ref
    i = pl.program_id(0)
    ni = pl.num_programs(0)

    @pl.when(i == 0)
    def _():
        x = x_ref[...]
        xn = _rms(x, gmix_ref[...]).astype(BF16)
        proj[...] = jnp.dot(xn, win_ref[...], preferred_element_type=F32)
        pre = proj[:, P_QK:P_QK + QK_W]
        qk = cb_ref[...] + cw_ref[CONV_W - 1:CONV_W, :] * pre
        for jj in range(CONV_W - 1):
            qk = qk + cw_ref[jj:jj + 1, :] * sconv_ref[jj]
        for jj in range(CONV_W - 2):
            conv_out[jj] = sconv_ref[jj + 1]
        conv_out[CONV_W - 2] = pre
        qk = qk * jax.nn.sigmoid(qk)
        proj[:, 0:W_A] = qk[:, 0:W_A]
        proj[:, W_A:QK_W] = qk[:, W_A:QK_W] * (DH_A ** -0.5)
        for g in range(G_B):
            sl = slice(g * DG_B, (g + 1) * DG_B)
            vb = proj[:, P_VB + g * DG_B:P_VB + (g + 1) * DG_B]
            u = proj[:, P_U + g * DG_B:P_U + (g + 1) * DG_B]
            vn = _rms(jax.nn.gelu(vb), vng_ref[:, sl])
            vrow_out[:, sl] = vn
            z = w00_ref[:, sl] * vn + bs0_ref[:, sl]
            hcat[:, W_A + g * DG_B:W_A + (g + 1) * DG_B] = jax.nn.gelu(u) * z

    rows = pl.ds(pl.multiple_of(i * bt, bt), bt)
    gates = proj[rows, P_G:P_G + LANES] + gb_ref[...]
    lane = lax.broadcasted_iota(jnp.int32, (bt, LANES), 1)
    sub = lax.broadcasted_iota(jnp.int32, (bt, DH_A), 0)
    m_new_all = jnp.zeros((bt, LANES), F32)
    for h in range(H_A):
        ig = gates[:, h:h + 1]
        lf = _log_sigmoid(gates[:, H_A + h:H_A + h + 1])
        m_prev = m_in[rows, h:h + 1]
        inter = lf + m_prev
        m_t = jnp.maximum(inter, ig)
        w_inter = jnp.exp(inter - m_t)
        e_d = jnp.exp(ig - m_t)
        q = proj[rows, h * DH_A:(h + 1) * DH_A]
        k = proj[rows, W_A + h * DH_A:W_A + (h + 1) * DH_A]
        v = proj[rows, P_V + h * DH_A:P_V + (h + 1) * DH_A]
        n_old = n_in[h]
        s = jnp.sum(q * k, axis=-1, keepdims=True) * e_d
        vw = v * e_d
        vw_t = jnp.concatenate([vw, jnp.zeros((DH_A - bt, DH_A), F32)], axis=0).T
        qb = q.astype(BF16)
        qc = jnp.zeros((bt, DH_A), F32)
        for t in range(bt):
            c_old = c_in[t, h]
            r = _dot_nt(qb, c_old.astype(BF16))
            qc = jnp.where(sub == t, r, qc)
            c_out[t, h] = w_inter[t:t + 1, :] * c_old + vw_t[:, t:t + 1] * k[t:t + 1, :]
        num = w_inter * qc + s * v
        nq = w_inter * jnp.sum(q * n_old, axis=-1, keepdims=True) + s
        hh = num / jnp.maximum(jnp.abs(nq), jnp.exp(-m_t))
        n_out[h] = w_inter * n_old + e_d * k
        m_new_all = jnp.where(lane == h, m_t, m_new_all)
        ha = _rms(hh, hng_ref[:, h * DH_A:(h + 1) * DH_A])
        o = proj[rows, P_O + h * DH_A:P_O + (h + 1) * DH_A]
        hcat[rows, h * DH_A:(h + 1) * DH_A] = ha * jax.nn.sigmoid(o)
    m_out[rows, :] = m_new_all

    @pl.when(i == ni - 1)
    def _():
        out_ref[...] = x_ref[...] + jnp.dot(hcat[...].astype(BF16), wout_ref[...],
                                            preferred_element_type=F32)


def _mix_sample(h_in, h_out, c_all, lw, sconv_t, c0, n0_t, m0_pad, n_s, layer):
    bt = BT_SAMPLE
    ni = n_s // bt
    const = lambda i: (0, 0)
    in_blk = h_in.shape[0] // n_s - 1
    out_blk = h_out.shape[0] // n_s - 1
    kern = functools.partial(_mix_sample_kernel, bt=bt)
    return pl.pallas_call(
        kern,
        name="mix_sample",
        grid=(ni,),
        in_specs=[
            pl.BlockSpec((n_s, D_MODEL), lambda i: (in_blk, 0)),
            pl.BlockSpec(memory_space=pl.ANY),
            pl.BlockSpec(memory_space=pl.ANY),
            _layer_spec((1, D_MODEL), layer),
            _layer_spec((D_MODEL, P_W), layer),
            _layer_spec((CONV_W, QK_W), layer),
            _layer_spec((1, QK_W), layer),
            _layer_spec((1, LANES), layer),
            _layer_spec((1, W_A), layer),
            _layer_spec((1, W_B), layer),
            _layer_spec((1, W_B), layer),
            _layer_spec((1, W_B), layer),
            _layer_spec((D_MODEL, D_MODEL), layer),
            pl.BlockSpec((None, CONV_W - 1, n_s, QK_W), lambda i: (layer, 0, 0, 0)),
            pl.BlockSpec((None, bt, H_A, DH_A, DH_A), lambda i: (layer, i, 0, 0, 0)),
            pl.BlockSpec((None, H_A, bt, DH_A), lambda i: (layer, 0, i, 0)),
            pl.BlockSpec((None, n_s, LANES), lambda i: (layer, 0, 0)),
        ],
        out_specs=[
            pl.BlockSpec((n_s, D_MODEL), lambda i: (out_blk, 0)),
            pl.BlockSpec((None, bt, H_A, DH_A, DH_A), lambda i: (layer, i, 0, 0, 0)),
            pl.BlockSpec((H_A, bt, DH_A), lambda i: (0, i, 0)),
            pl.BlockSpec((n_s, LANES), const),
            pl.BlockSpec((CONV_W - 1, n_s, QK_W), lambda i: (0, 0, 0)),
            pl.BlockSpec((n_s, W_B), const),
        ],
        out_shape=[
            jax.ShapeDtypeStruct(h_out.shape, F32),
            jax.ShapeDtypeStruct(c_all.shape, F32),
            jax.ShapeDtypeStruct(n0_t.shape[1:], F32),
            jax.ShapeDtypeStruct((n_s, LANES), F32),
            jax.ShapeDtypeStruct((CONV_W - 1, n_s, QK_W), F32),
            jax.ShapeDtypeStruct((n_s, W_B), F32),
        ],
        scratch_shapes=[
            pltpu.VMEM((n_s, P_W), F32),
            pltpu.VMEM((n_s, D_MODEL), F32),
        ],
        input_output_aliases={1: 0, 2: 1},
        compiler_params=_cparams("arbitrary"),
    )(h_in, h_out, c_all, lw["g_mix"], lw["w_in"], lw["conv_w"], lw["conv_b"], lw["gate_b"], lw["hn_g"], lw["vn_g"],
      lw["w00_row"], lw["bs0_row"], lw["w_out"], sconv_t, c0, n0_t, m0_pad)


def _router_kernel(h_ref, g_ref, wr_ref, b_ref, route_ref, route_t_ref, tcnt_ref, *, tm):
    hn = _rms(h_ref[...], g_ref[...]).astype(BF16)
    logits = jnp.dot(hn, wr_ref[...], preferred_element_type=F32) + b_ref[...]

    lane_i = lax.broadcasted_iota(jnp.int32, (tm, LANES), 1)
    lane = lane_i.astype(F32)
    neg = -jnp.inf
    is_g = lane_i < N_GROUPS
    gl = jnp.where(is_g, logits, neg)
    gmax = jnp.max(gl, axis=-1, keepdims=True)
    g_sel = jnp.min(jnp.where(gl == gmax, lane, float(LANES)), axis=-1, keepdims=True)
    g_w = 1.0 / jnp.sum(jnp.where(is_g, jnp.exp(logits - gmax), 0.0), axis=-1, keepdims=True)

    e_lane = lane_i - N_GROUPS
    lane_grp = (e_lane >> 3).astype(F32)
    in_grp = (e_lane >= 0) & (e_lane < N_EXPERTS) & (lane_grp == g_sel)
    el = jnp.where(in_grp, logits, neg)
    v1 = jnp.max(el, axis=-1, keepdims=True)
    i1 = jnp.min(jnp.where(el == v1, lane, float(LANES)), axis=-1, keepdims=True)
    el2 = jnp.where(lane == i1, neg, el)
    v2 = jnp.max(el2, axis=-1, keepdims=True)
    i2 = jnp.min(jnp.where(el2 == v2, lane, float(LANES)), axis=-1, keepdims=True)
    e21 = jnp.exp(v2 - v1)
    den = 1.0 + e21
    w1 = g_w * (1.0 / den)
    w2 = g_w * (e21 / den)

    onehot = jnp.where((lane == i1) | (lane == i2), 1.0, 0.0)
    r_i = lax.broadcasted_iota(jnp.int32, (tm, tm), 0)
    c_i = lax.broadcasted_iota(jnp.int32, (tm, tm), 1)
    strict = jnp.where(r_i > c_i, 1.0, 0.0).astype(BF16)
    before = jnp.dot(strict, onehot.astype(BF16), preferred_element_type=F32)
    cnt = jnp.sum(onehot, axis=0, keepdims=True)
    cnt8 = jnp.floor((cnt + (SUBLANES - 1.0)) * (1.0 / SUBLANES)) * SUBLANES
    l_r = lax.broadcasted_iota(jnp.int32, (LANES, LANES), 0)
    l_c = lax.broadcasted_iota(jnp.int32, (LANES, LANES), 1)
    lanes_before = jnp.where(l_r < l_c, 1.0, 0.0).astype(BF16)
    start = _split_dot(jnp.broadcast_to(cnt8, (2 * SUBLANES, LANES)), lanes_before)[0:1, :]
    slot = start + before
    pos1 = jnp.sum(jnp.where(lane == i1, slot, 0.0), axis=-1, keepdims=True)
    pos2 = jnp.sum(jnp.where(lane == i2, slot, 0.0), axis=-1, keepdims=True)
    tcnt_ref[...] = jnp.broadcast_to(cnt8, tcnt_ref.shape)

    out = jnp.zeros((tm, LANES), F32)
    for idx, val in enumerate((pos1, pos2, w1, w2)):
        out = jnp.where(lane_i == idx, val, out)
    route_ref[...] = out
    route_t_ref[...] = out.T[0:SUBLANES, :]


def _router(h1, lw, layer):
    t_total = h1.shape[0]
    tm = _token_tile(t_total)
    n_tiles = t_total // tm
    return pl.pallas_call(
        functools.partial(_router_kernel, tm=tm),
        name="router",
        grid=(n_tiles,),
        in_specs=[
            pl.BlockSpec((tm, D_MODEL), lambda i: (i, 0)),
            _layer_spec((1, D_MODEL), layer),
            _layer_spec((D_MODEL, LANES), layer),
            _layer_spec((1, LANES), layer),
        ],
        out_specs=[
            pl.BlockSpec((tm, LANES), lambda i: (i, 0)),
            pl.BlockSpec((SUBLANES, tm), lambda i: (0, i)),
            pl.BlockSpec((None, SUBLANES, LANES), lambda i: (i, 0, 0)),
        ],
        out_shape=[
            jax.ShapeDtypeStruct((t_total, LANES), F32),
            jax.ShapeDtypeStruct((SUBLANES, t_total), F32),
            jax.ShapeDtypeStruct((n_tiles, SUBLANES, LANES), F32),
        ],
        compiler_params=_cparams("arbitrary"),
    )(h1, lw["g_ffn"], lw["wr"], lw["br"])


def _tile_slots(tm):
    raw = tm * TOP_K + N_EXPERTS * (SUBLANES - 1)
    return -(-raw // LANES) * LANES


def _token_tile(t_total):
    tm = TM_TOK
    while t_total % tm:
        tm //= 2
    return tm


def _run_copies(tile, tables, units_per_tile, make_copy):
    psrc_ref, pdst_ref, npair_ref, ssrc_ref, sdst_ref, nsingle_ref, _ = tables
    pair_base = tile * (units_per_tile // 2)
    single_base = tile * N_EXPERTS

    def per_pair(u, carry):
        make_copy(pl.multiple_of(psrc_ref[pair_base + u], SUBLANES),
                  pl.multiple_of(pdst_ref[pair_base + u], SUBLANES), 2 * SUBLANES).start()
        return carry

    def per_single(u, carry):
        make_copy(pl.multiple_of(ssrc_ref[single_base + u], SUBLANES),
                  pl.multiple_of(sdst_ref[single_base + u], SUBLANES), SUBLANES).start()
        return carry

    lax.fori_loop(0, npair_ref[tile], per_pair, 0)
    lax.fori_loop(0, nsingle_ref[tile], per_single, 0)


def _drain_copies(n_units, make_copy):
    def wait_group(u, carry):
        make_copy(0, 0, DRAIN_GROUP * SUBLANES).wait()
        return carry

    def wait_one(u, carry):
        make_copy(0, 0, SUBLANES).wait()
        return carry

    lax.fori_loop(0, lax.div(n_units, DRAIN_GROUP), wait_group, 0)
    lax.fori_loop(0, lax.rem(n_units, DRAIN_GROUP), wait_one, 0)


def _dispatch_kernel(*refs, tm, slots):
    tables = refs[:N_COPY_TABLES]
    tunits_ref = tables[-1]
    h_ref, g_ref, rt_ref, xs_in, xs_out, srt, sem = refs[N_COPY_TABLES:]
    del xs_in
    i = pl.program_id(0)
    n = pl.num_programs(0)
    buf = lax.rem(i, 2)

    def copy_maker(b):
        def make_copy(tile_row, sorted_row, rows):
            return pltpu.make_async_copy(srt.at[b, pl.ds(tile_row, rows)],
                                         xs_out.at[pl.ds(sorted_row, rows), pl.ds(0, HALF_D)], sem.at[b])
        return make_copy

    @pl.when(i >= 2)
    def _():
        _drain_copies(tunits_ref[i - 2], copy_maker(buf))

    xn = _rms(h_ref[...], g_ref[...]).astype(BF16)
    slot_id = lax.broadcasted_iota(jnp.int32, (slots, tm), 0).astype(F32)
    sel = (slot_id == rt_ref[0:1, :]) | (slot_id == rt_ref[1:2, :])
    srt[buf] = _pack_rows(jnp.dot(jnp.where(sel, 1.0, 0.0).astype(BF16), xn, preferred_element_type=F32))
    _run_copies(i, tables, slots // SUBLANES, copy_maker(buf))

    @pl.when(i == n - 1)
    def _():
        @pl.when(i >= 1)
        def _():
            _drain_copies(tunits_ref[i - 1], copy_maker(1 - buf))

        _drain_copies(tunits_ref[i], copy_maker(buf))


def _dispatch(tables, h1, route_t, lw, xs, layer):
    t_total = h1.shape[0]
    tm = _token_tile(t_total)
    slots = _tile_slots(tm)
    return pl.pallas_call(
        functools.partial(_dispatch_kernel, tm=tm, slots=slots),
        name="dispatch",
        grid_spec=pltpu.PrefetchScalarGridSpec(
            num_scalar_prefetch=N_COPY_TABLES,
            grid=(t_total // tm,),
            in_specs=[
                pl.BlockSpec((tm, D_MODEL), lambda i, *_: (i, 0)),
                _layer_spec((1, D_MODEL), layer),
                pl.BlockSpec((SUBLANES, tm), lambda i, *_: (0, i)),
                pl.BlockSpec(memory_space=pl.ANY),
            ],
            out_specs=pl.BlockSpec(memory_space=pl.ANY),
            scratch_shapes=[pltpu.VMEM((2, slots, HALF_D), U32), pltpu.SemaphoreType.DMA((2,))],
        ),
        out_shape=jax.ShapeDtypeStruct(xs.shape, U32),
        input_output_aliases={N_COPY_TABLES + 3: 0},
        compiler_params=_cparams("arbitrary"),
    )(*tables, h1, lw["g_ffn"], route_t, xs)


def _ffn_kernel(bstart_ref, nblk_ref, nused_ref, w1_ref, w3_ref, w2_ref, xs_ref, ys_ref,
                w1b, w3b, w2b, xbuf, ybuf, sem_in, sem_out):
    e = pl.program_id(0)
    g0 = bstart_ref[e]
    nb = nblk_ref[e]
    n_used = nused_ref[0]

    def in_copy(g):
        slot = lax.rem(g, FFN_RING)
        r = pl.multiple_of(g * FFN_BLK, FFN_BLK)
        return pltpu.make_async_copy(xs_ref.at[pl.ds(r, FFN_BLK), pl.ds(0, HALF_D)], xbuf.at[slot],
                                     sem_in.at[slot])

    def out_copy(g):
        slot = lax.rem(g, FFN_RING)
        r = pl.multiple_of(g * FFN_BLK, FFN_BLK)
        return pltpu.make_async_copy(ybuf.at[slot], ys_ref.at[pl.ds(r, FFN_BLK)], sem_out.at[slot])

    @pl.when(e == 0)
    def _():
        for ahead in range(FFN_RING - 1):
            @pl.when(ahead < n_used)
            def _():
                in_copy(jnp.int32(ahead)).start()

    @pl.when(nb > 0)
    def _():
        w1b[...] = w1_ref[...].astype(BF16)
        w3b[...] = w3_ref[...].astype(BF16)
        w2b[...] = w2_ref[...].astype(BF16)

    def block_step(k, carry):
        g = g0 + k
        slot = lax.rem(g, FFN_RING)
        in_copy(g).wait()

        @pl.when(g + (FFN_RING - 1) < n_used)
        def _():
            in_copy(g + (FFN_RING - 1)).start()

        @pl.when(g >= FFN_RING)
        def _():
            out_copy(g - FFN_RING).wait()

        x_lo, x_hi = _unpack_rows(xbuf[slot])
        a = (jnp.dot(x_lo, w1b[0:HALF_D, :], preferred_element_type=F32)
             + jnp.dot(x_hi, w1b[HALF_D:, :], preferred_element_type=F32))
        b = (jnp.dot(x_lo, w3b[0:HALF_D, :], preferred_element_type=F32)
             + jnp.dot(x_hi, w3b[HALF_D:, :], preferred_element_type=F32))
        hmid = (a * jax.nn.sigmoid(a) * b).astype(BF16)
        y = jnp.dot(hmid, w2b[...], preferred_element_type=F32)
        ybuf[slot] = lax.bitcast_convert_type(y, U32)
        out_copy(g).start()
        return carry

    lax.fori_loop(0, nb, block_step, 0)

    @pl.when(e == pl.num_programs(0) - 1)
    def _():
        for back in range(FFN_RING, 0, -1):
            @pl.when(n_used >= back)
            def _():
                out_copy(n_used - back).wait()


def _ffn(ffn_tables, xs, w1, w3, w2, layer):
    def w_map(e, *_):
        return (layer, e, 0, 0)

    return pl.pallas_call(
        _ffn_kernel,
        name="expert_ffn",
        grid_spec=pltpu.PrefetchScalarGridSpec(
            num_scalar_prefetch=3,
            grid=(N_EXPERTS,),
            in_specs=[
                pl.BlockSpec((None, None, D_MODEL, D_FF), w_map),
                pl.BlockSpec((None, None, D_MODEL, D_FF), w_map),
                pl.BlockSpec((None, None, D_FF, D_MODEL), w_map),
                pl.BlockSpec(memory_space=pl.ANY),
            ],
            out_specs=pl.BlockSpec(memory_space=pl.ANY),
            scratch_shapes=[
                pltpu.VMEM((D_MODEL, D_FF), BF16),
                pltpu.VMEM((D_MODEL, D_FF), BF16),
                pltpu.VMEM((D_FF, D_MODEL), BF16),
                pltpu.VMEM((FFN_RING, FFN_BLK, HALF_D), U32),
                pltpu.VMEM((FFN_RING, FFN_BLK, D_MODEL), U32),
                pltpu.SemaphoreType.DMA((FFN_RING,)),
                pltpu.SemaphoreType.DMA((FFN_RING,)),
            ],
        ),
        out_shape=jax.ShapeDtypeStruct(xs.shape, U32),
        input_output_aliases={6: 0},
        compiler_params=_cparams("arbitrary"),
    )(*ffn_tables, w1, w3, w2, xs)


def _ple_kernel(*refs, tm, slots, n_s, final):
    tables = refs[:N_COPY_TABLES]
    tunits_ref = tables[-1]
    h_ref, route_ref, p_ref, ps_ref, wpg_ref, wple_ref, pg_ref, fg_ref, ys_ref = refs[N_COPY_TABLES:N_COPY_TABLES + 9]
    rest = refs[N_COPY_TABLES + 9:]
    if final:
        yp_ref, ysm_ref, ysrt, sem = rest
    else:
        out_ref, ysrt, sem = rest
    i = pl.program_id(0)
    n = pl.num_programs(0)
    buf = lax.rem(i, 2)

    def copy_maker(b):
        def make_copy(tile_row, sorted_row, rows):
            return pltpu.make_async_copy(ys_ref.at[pl.ds(sorted_row, rows)],
                                         ysrt.at[b, pl.ds(tile_row, rows)], sem.at[b])
        return make_copy

    @pl.when(i == 0)
    def _():
        ysrt[...] = jnp.zeros_like(ysrt)
        _run_copies(i, tables, slots // SUBLANES, copy_maker(buf))

    @pl.when(i + 1 < n)
    def _():
        _run_copies(i + 1, tables, slots // SUBLANES, copy_maker(1 - buf))

    p = p_ref[...]
    p_last = jnp.concatenate([p[0:tm - n_s, :], ps_ref[...]], axis=0)
    p = jnp.where(i == n - 1, p_last, p)
    pe = _rms(jnp.dot(p.astype(BF16), wple_ref[...], preferred_element_type=F32), pg_ref[...])
    _drain_copies(tunits_ref[i], copy_maker(buf))

    route = route_ref[...]
    y = lax.bitcast_convert_type(ysrt[buf], F32)
    y_head = y.astype(BF16)
    y_rem = (y - y_head.astype(F32)).astype(BF16)
    slot_id = lax.broadcasted_iota(jnp.int32, (tm, slots), 1).astype(F32)
    moe = None
    weighted_pick = jnp.zeros((tm, slots), F32)
    for kk in range(TOP_K):
        hit = slot_id == route[:, kk:kk + 1]
        w_kk = route[:, TOP_K + kk:TOP_K + kk + 1]
        term = w_kk * jnp.dot(jnp.where(hit, 1.0, 0.0).astype(BF16), y_head, preferred_element_type=F32)
        moe = term if moe is None else moe + term
        weighted_pick = jnp.where(hit, w_kk, weighted_pick)
    moe = moe + jnp.dot(weighted_pick.astype(BF16), y_rem, preferred_element_type=F32)
    h2 = h_ref[...] + moe
    gate = jax.nn.sigmoid(jnp.dot(h2.astype(BF16), wpg_ref[...], preferred_element_type=F32))
    out = h2 + gate * pe
    if final:
        fin = _rms(out, fg_ref[...])
        yp_ref[...] = fin

        @pl.when(i == pl.num_programs(0) - 1)
        def _():
            ysm_ref[...] = fin[tm - n_s:tm, :]
    else:
        out_ref[...] = out


def _ple(tables, h1, route, p_prompt, p_sample, lw, final_g, ys, n_s, layer, final, in_place):
    t_total = h1.shape[0]
    tm = _token_tile(t_total)
    slots = _tile_slots(tm)
    assert n_s <= tm and (t_total - n_s) % tm == tm - n_s
    const = lambda i, *_: (0, 0)
    tok = lambda i, *_: (i, 0)
    if final:
        out_specs = [pl.BlockSpec((tm, D_MODEL), tok), pl.BlockSpec((n_s, D_MODEL), const)]
        out_shape = [jax.ShapeDtypeStruct((t_total - n_s, D_MODEL), F32),
                     jax.ShapeDtypeStruct((n_s, D_MODEL), F32)]
        aliases = {}
    else:
        out_specs = [pl.BlockSpec((tm, D_MODEL), tok)]
        out_shape = [jax.ShapeDtypeStruct((t_total, D_MODEL), F32)]
        aliases = {N_COPY_TABLES: 0} if in_place else {}
    return pl.pallas_call(
        functools.partial(_ple_kernel, tm=tm, slots=slots, n_s=n_s, final=final),
        name="combine_ple",
        grid_spec=pltpu.PrefetchScalarGridSpec(
            num_scalar_prefetch=N_COPY_TABLES,
            grid=(t_total // tm,),
            in_specs=[
                pl.BlockSpec((tm, D_MODEL), tok),
                pl.BlockSpec((tm, LANES), tok),
                pl.BlockSpec((None, tm, PLE_DIM), lambda i, *_: (layer, i, 0)),
                pl.BlockSpec((None, n_s, PLE_DIM), lambda i, *_: (layer, 0, 0)),
                _layer_spec((D_MODEL, D_MODEL), layer),
                _layer_spec((PLE_DIM, D_MODEL), layer),
                _layer_spec((1, D_MODEL), layer),
                pl.BlockSpec((1, D_MODEL), const),
                pl.BlockSpec(memory_space=pl.ANY),
            ],
            out_specs=out_specs,
            scratch_shapes=[pltpu.VMEM((2, slots, D_MODEL), U32), pltpu.SemaphoreType.DMA((2,))],
        ),
        out_shape=out_shape,
        input_output_aliases=aliases,
        compiler_params=_cparams("arbitrary"),
    )(*tables, h1, route, p_prompt, p_sample, lw["w_pg"], lw["w_ple"], lw["ple_g"], final_g, ys)


def _stacked_weights(norm_mix_g, w_in, conv_w, conv_b, b_igate, b_fgate, hnorm_g, vnorm_g, w_s, b_s, w_out,
                     norm_ffn_g, router_g_w, router_g_b, router_e_w, router_e_b, w_ple, ple_norm_g, w_ple_gate):
    a0 = QK_W + 2 * W_A
    lane_pad = lambda x, used: jnp.pad(x, ((0, 0),) * (x.ndim - 1) + ((0, LANES - used),))
    w_packed = jnp.concatenate(
        [w_in[:, :, :a0], w_in[:, :, a0 + 2 * H_A:], lane_pad(w_in[:, :, a0:a0 + 2 * H_A], 2 * H_A)],
        axis=2).astype(BF16)
    gate_b = lane_pad(jnp.concatenate([b_igate, b_fgate], axis=1), 2 * H_A)[:, None, :]
    wr = lane_pad(jnp.concatenate([router_g_w, router_e_w], axis=2), N_GROUPS + N_EXPERTS).astype(BF16)
    br = lane_pad(jnp.concatenate([router_g_b, router_e_b], axis=1), N_GROUPS + N_EXPERTS)[:, None, :]
    return {
        "g_mix": norm_mix_g[:, None, :],
        "w_in": w_packed,
        "conv_w": conv_w,
        "conv_b": conv_b[:, None, :],
        "gate_b": gate_b,
        "hn_g": hnorm_g[:, None, :],
        "vn_g": vnorm_g[:, None, :],
        "w_s": w_s,
        "b_s_col": lane_pad(jnp.transpose(b_s, (0, 2, 1)), G_B),
        "w00_row": jnp.repeat(w_s[:, :, 0, 0], DG_B, axis=1)[:, None, :],
        "bs0_row": jnp.repeat(b_s[:, :, 0], DG_B, axis=1)[:, None, :],
        "w_out": w_out.astype(BF16),
        "g_ffn": norm_ffn_g[:, None, :],
        "wr": wr,
        "br": br,
        "w_pg": w_ple_gate.astype(BF16),
        "w_ple": w_ple.astype(BF16),
        "ple_g": ple_norm_g[:, None, :],
    }


def _routing_tables(tcnt, units_per_tile):
    c8 = tcnt[:, 0, N_GROUPS:N_GROUPS + N_EXPERTS].astype(jnp.int32)
    blocks = (jnp.sum(c8, axis=0) + FFN_BLK - 1) // FFN_BLK
    blk_end = jnp.cumsum(blocks)
    pstart = (blk_end - blocks) * FFN_BLK
    run_dst = pstart[None, :] + jnp.cumsum(c8, axis=0) - c8
    run_src = jnp.cumsum(c8, axis=1) - c8
    units = c8 // SUBLANES
    n_pair = units // 2
    n_single = units - 2 * n_pair
    pair_first = jnp.cumsum(n_pair, axis=1) - n_pair
    single_first = jnp.cumsum(n_single, axis=1) - n_single

    def piece_lists(first, count, run_off, n_list, rows):
        j = jnp.arange(n_list, dtype=jnp.int32)[None, :, None]
        hit = (j >= first[:, None, :]) & (j < (first + count)[:, None, :])
        off = run_off[:, None, :] + (j - first[:, None, :]) * rows
        src = jnp.sum(jnp.where(hit, run_src[:, None, :] + off, 0), axis=2)
        dst = jnp.sum(jnp.where(hit, run_dst[:, None, :] + off, 0), axis=2)
        return src.reshape(-1), dst.reshape(-1)

    psrc, pdst = piece_lists(pair_first, n_pair, jnp.zeros_like(c8), units_per_tile // 2, 2 * SUBLANES)
    ssrc, sdst = piece_lists(single_first, n_single, n_pair * (2 * SUBLANES), N_EXPERTS, SUBLANES)
    tables = (psrc, pdst, jnp.sum(n_pair, axis=1), ssrc, sdst, jnp.sum(n_single, axis=1),
              jnp.sum(units, axis=1))
    assert len(tables) == N_COPY_TABLES
    ffn_tables = (blk_end - blocks, blocks, blk_end[-1:])
    return tables, ffn_tables


def kernel(x_prompt, x_sample, state_C, state_n, state_m, state_conv, p_prompt, p_sample, norm_mix_g, w_in, conv_w, conv_b, b_igate, b_fgate, hnorm_g, vnorm_g, w_s, b_s, w_out, norm_ffn_g, router_g_w, router_g_b, router_e_w, router_e_b, w1, w3, w2, w_ple, ple_norm_g, w_ple_gate, final_norm_g):
    n_batch, seq, _ = x_prompt.shape
    n_s = x_sample.shape[0]
    depth = w_in.shape[0]
    t_p = n_batch * seq
    t_total = t_p + n_s
    assert seq % CHUNK == 0 and t_p % n_s == 0 and n_s % BT_SAMPLE == 0

    n_tiles = t_total // _token_tile(t_total)
    max_rows = t_total * TOP_K + N_EXPERTS * ((SUBLANES - 1) * n_tiles + FFN_BLK - 1)
    n_blocks = max_rows // FFN_BLK
    xs = jnp.zeros((n_blocks * FFN_BLK, D_MODEL), U32)

    p_prompt_t = p_prompt.reshape(depth, t_p, PLE_DIM)
    p_sample_t = p_sample.reshape(depth, n_s, PLE_DIM)
    sconv_t = jnp.transpose(state_conv, (0, 2, 1, 3))
    n0_t = jnp.transpose(state_n, (0, 2, 1, 3))
    m0_pad = jnp.pad(state_m, ((0, 0), (0, 0), (0, LANES - H_A)))
    final_g = final_norm_g[None, :]

    src_p = x_prompt.reshape(t_p, D_MODEL)
    src_s = x_sample.reshape(n_s, D_MODEL)
    spare = jnp.zeros((t_total, D_MODEL), F32)
    c_all = jnp.zeros(state_C.shape, F32)
    outs = {k: [] for k in ("Cp", "np", "mp", "cbp", "ns", "ms", "cbs", "vs")}
    res = None
    lw = _stacked_weights(norm_mix_g, w_in, conv_w, conv_b, b_igate, b_fgate, hnorm_g, vnorm_g, w_s, b_s,
                          w_out, norm_ffn_g, router_g_w, router_g_b, router_e_w, router_e_b, w_ple,
                          ple_norm_g, w_ple_gate)
    for l in range(depth):
        h_mix, c_p, n_p, m_p, cb_p = _mix_prompt(src_p, spare, n_batch, seq, lw, l)
        h_mix, c_all, n_s_t, m_s, cb_s, v_s = _mix_sample(src_s, h_mix, c_all, lw, sconv_t, state_C, n0_t,
                                                          m0_pad, n_s, l)
        route, route_t, tcnt = _router(h_mix, lw, l)
        tables, ffn_tables = _routing_tables(tcnt, _tile_slots(_token_tile(t_total)) // SUBLANES)
        xs = _dispatch(tables, h_mix, route_t, lw, xs, l)
        xs = _ffn(ffn_tables, xs, w1, w3, w2, l)
        final = l == depth - 1
        res = _ple(tables, h_mix, route, p_prompt_t, p_sample_t, lw, final_g, xs, n_s, l, final, in_place=l > 0)
        if l == 0:
            spare = h_mix
        else:
            spare = src_p
        src_p = src_s = res[0]
        outs["Cp"].append(c_p)
        outs["np"].append(n_p)
        outs["mp"].append(m_p[:, :, 0])
        outs["cbp"].append(cb_p)
        outs["ns"].append(jnp.transpose(n_s_t, (1, 0, 2)))
        outs["ms"].append(m_s[:, 0:H_A])
        outs["cbs"].append(jnp.transpose(cb_s, (1, 0, 2)))
        outs["vs"].append(v_s[:, None, :])

    y_prompt = res[0].reshape(n_batch, seq, D_MODEL)
    y_sample = res[1].reshape(n_s, 1, D_MODEL)
    st = lambda k: jnp.stack(outs[k])
    return (y_prompt, y_sample, st("Cp"), st("np"), st("mp"), st("cbp"),
            c_all, st("ns"), st("ms"), st("cbs"), st("vs"))
```

```python
import functools

import jax
import jax.numpy as jnp
from jax import lax
from jax.experimental import pallas as pl
from jax.experimental.pallas import tpu as pltpu

F32 = jnp.float32
BF16 = jnp.bfloat16
U32 = jnp.uint32

D_MODEL = 1024
W_A = 512
H_A = 4
DH_A = 128
W_B = 512
G_B = 4
DG_B = 128
CHUNK = 128
CONV_W = 4
QK_W = 2 * W_A
N_GROUPS = 4
EXPERTS_PER_GROUP = 8
N_EXPERTS = N_GROUPS * EXPERTS_PER_GROUP
TOP_K = 2
D_FF = 512
PLE_DIM = 256
EPS = 1e-6
HALF_D = D_MODEL // 2

LANES = 128
SUBLANES = 8
VMEM_LIMIT_BYTES = 56 * 1024 * 1024

P_QK = 0
P_V = QK_W
P_O = P_V + W_A
P_U = P_O + W_A
P_VB = P_U + W_B
P_G = P_VB + W_B
P_W = P_G + LANES

TL_MIX = 512
BT_SAMPLE = 8
TM_TOK = 384
FFN_BLK = 256
FFN_RING = 6
DRAIN_GROUP = 32
N_COPY_TABLES = 7


def _cparams(*sem):
    return pltpu.CompilerParams(dimension_semantics=sem, vmem_limit_bytes=VMEM_LIMIT_BYTES)


def _rms(x, g):
    return x * lax.rsqrt(jnp.mean(x * x, axis=-1, keepdims=True) + EPS) * g


def _log_sigmoid(x):
    return -(jnp.maximum(-x, 0.0) + jnp.log1p(jnp.exp(-jnp.abs(x))))


def _split_dot(a, b_bf16):
    hi = a.astype(BF16)
    lo = (a - hi.astype(F32)).astype(BF16)
    return (jnp.dot(hi, b_bf16, preferred_element_type=F32)
            + jnp.dot(lo, b_bf16, preferred_element_type=F32))


def _dot_nt(a, b):
    return lax.dot_general(a, b, (((1,), (1,)), ((), ())), preferred_element_type=F32)


def _pack_rows(x):
    lo = lax.bitcast_convert_type(x[:, :HALF_D], U32) >> 16
    hi = lax.bitcast_convert_type(x[:, HALF_D:], U32) & jnp.uint32(0xFFFF0000)
    return lo | hi


def _unpack_rows(u):
    lo = lax.bitcast_convert_type(u << 16, F32)
    hi = lax.bitcast_convert_type(u & jnp.uint32(0xFFFF0000), F32)
    return lo.astype(BF16), hi.astype(BF16)


def _mix_prompt_kernel(x_ref, xnext_ref, hprev_ref, gmix_ref, win_ref, cw_ref, cb_ref, gb_ref, hng_ref,
                       vng_ref, ws_ref, bs_ref, wout_ref,
                       out_ref, c_out, n_out, m_out, conv_out,
                       proj_a, proj_b, xpad, hcat, c_s, n_s, m_s, *, tl):
    del hprev_ref
    b = pl.program_id(0)
    j = pl.program_id(1)
    nj = pl.num_programs(1)
    n_chunks = tl // CHUNK
    bounds = [(P_W // LANES * c // n_chunks) * LANES for c in range(n_chunks + 1)]
    col_groups = [slice(bounds[c], bounds[c + 1]) for c in range(n_chunks)]

    @pl.when(j == 0)
    def _():
        c_s[...] = jnp.zeros_like(c_s)
        n_s[...] = jnp.zeros_like(n_s)
        m_s[...] = jnp.zeros_like(m_s)
        xpad[0:SUBLANES, :] = jnp.zeros((SUBLANES, QK_W), F32)

    @pl.when((b == 0) & (j == 0))
    def _():
        xn0 = _rms(x_ref[0:tl, :], gmix_ref[...]).astype(BF16)
        proj_a[...] = jnp.dot(xn0, win_ref[...], preferred_element_type=F32)

    row = lax.broadcasted_iota(jnp.int32, (CHUNK, CHUNK), 0)
    col = lax.broadcasted_iota(jnp.int32, (CHUNK, CHUNK), 1)
    causal = row >= col
    tril = jnp.where(causal, 1.0, 0.0).astype(BF16)
    triu = jnp.where(row <= col, 1.0, 0.0).astype(BF16)
    lane = lax.broadcasted_iota(jnp.int32, (CHUNK, LANES), 1)
    wm = [jnp.where(causal, ws_ref[g], 0.0).astype(BF16) for g in range(G_B)]

    def conv_step(proj):
        pre = proj[:, P_QK:P_QK + QK_W]
        xpad[SUBLANES:SUBLANES + tl, :] = pre
        qk = cb_ref[...] + cw_ref[CONV_W - 1:CONV_W, :] * pre
        for jj in range(1, CONV_W):
            qk = qk + cw_ref[CONV_W - 1 - jj:CONV_W - jj, :] * xpad[SUBLANES - jj:SUBLANES - jj + tl, :]
        xpad[SUBLANES - (CONV_W - 1):SUBLANES, :] = pre[tl - (CONV_W - 1):tl, :]
        qk = qk * jax.nn.sigmoid(qk)
        proj[:, 0:W_A] = qk[:, 0:W_A]
        proj[:, W_A:QK_W] = qk[:, W_A:QK_W] * (DH_A ** -0.5)

    def chunk_step(proj, c):
        rows = slice(c * CHUNK, (c + 1) * CHUNK)
        gl = proj[rows, P_G:P_G + LANES] + gb_ref[...]
        gl = jnp.where(lane >= H_A, _log_sigmoid(gl), gl)
        glt = gl.T
        b_col_all = _split_dot_left(tril, gl)
        b_row_all = _split_dot(glt[0:2 * SUBLANES, :], triu)

        for h in range(H_A):
            q32 = proj[rows, h * DH_A:(h + 1) * DH_A]
            k32 = proj[rows, W_A + h * DH_A:W_A + (h + 1) * DH_A]
            v32 = proj[rows, P_V + h * DH_A:P_V + (h + 1) * DH_A]
            q = q32.astype(BF16)
            k = k32.astype(BF16)
            ig_c = gl[:, h:h + 1]
            ig_r = glt[h:h + 1, :]
            b_c = b_col_all[:, H_A + h:H_A + h + 1]
            b_r = b_row_all[H_A + h:H_A + h + 1, :]
            c_old = c_s[h]
            n_old = n_s[h:h + 1, :]
            m_prev = m_s[h:h + 1, 0:1]

            d_log = jnp.where(causal, b_c - b_r + ig_r, -jnp.inf)
            inter = b_c + m_prev
            m_t = jnp.maximum(inter, jnp.max(d_log, axis=-1, keepdims=True))
            s = _dot_nt(q, k) * jnp.exp(d_log - m_t)
            w_inter = jnp.exp(inter - m_t)
            num = (w_inter * _dot_nt(q, c_old.astype(BF16))
                   + jnp.dot(s.astype(BF16), v32.astype(BF16), preferred_element_type=F32))
            nq = (w_inter * jnp.sum(q32 * n_old, axis=-1, keepdims=True)
                  + jnp.sum(s, axis=-1, keepdims=True))
            hh = num / jnp.maximum(jnp.abs(nq), jnp.exp(-m_t))

            m_new = m_t[CHUNK - 1:CHUNK, :]
            b_last = b_c[CHUNK - 1:CHUNK, :]
            w_state = jnp.exp(b_last - b_c + ig_c - m_new)
            decay = jnp.exp(b_last + m_prev - m_new)
            vw = (v32 * w_state).astype(BF16)
            c_s[h] = decay * c_old + lax.dot_general(
                vw, k, (((0,), (0,)), ((), ())), preferred_element_type=F32)
            n_s[h:h + 1, :] = decay * n_old + jnp.sum(w_state * k32, axis=0, keepdims=True)
            m_s[h:h + 1, :] = jnp.broadcast_to(m_new, (1, LANES))

            ha = _rms(hh, hng_ref[:, h * DH_A:(h + 1) * DH_A])
            o = proj[rows, P_O + h * DH_A:P_O + (h + 1) * DH_A]
            hcat[rows, h * DH_A:(h + 1) * DH_A] = (ha * jax.nn.sigmoid(o)).astype(BF16)

        for g in range(G_B):
            vb = proj[rows, P_VB + g * DG_B:P_VB + (g + 1) * DG_B]
            u = proj[rows, P_U + g * DG_B:P_U + (g + 1) * DG_B]
            vn = _rms(jax.nn.gelu(vb), vng_ref[:, g * DG_B:(g + 1) * DG_B])
            z = jnp.dot(wm[g], vn.astype(BF16), preferred_element_type=F32) + bs_ref[:, g:g + 1]
            hcat[rows, W_A + g * DG_B:W_A + (g + 1) * DG_B] = (jax.nn.gelu(u) * z).astype(BF16)

    def tile_pass(row0, proj_cur, proj_nxt, xn_next):
        conv_step(proj_cur)
        for c in range(n_chunks):
            chunk_step(proj_cur, c)
            cs = col_groups[c]
            proj_nxt[:, cs] = jnp.dot(xn_next, win_ref[:, cs], preferred_element_type=F32)
        out_ref[row0:row0 + tl, :] = x_ref[row0:row0 + tl, :] + jnp.dot(
            hcat[...], wout_ref[...], preferred_element_type=F32)

    tile_pass(0, proj_a, proj_b, _rms(x_ref[tl:2 * tl, :], gmix_ref[...]).astype(BF16))
    tile_pass(tl, proj_b, proj_a, _rms(xnext_ref[...], gmix_ref[...]).astype(BF16))

    @pl.when(j == nj - 1)
    def _():
        conv_out[0] = xpad[SUBLANES - (CONV_W - 1):SUBLANES, :]
        c_out[0] = c_s[...]
        n_out[0] = n_s[0:H_A, :]
        m_out[0] = m_s[0:H_A, :]


def _split_dot_left(a_bf16, b):
    hi = b.astype(BF16)
    lo = (b - hi.astype(F32)).astype(BF16)
    return (jnp.dot(a_bf16, hi, preferred_element_type=F32)
            + jnp.dot(a_bf16, lo, preferred_element_type=F32))


def _layer_spec(shape, layer, **kwargs):
    return pl.BlockSpec((None,) + tuple(shape), lambda *_: (layer,) + (0,) * len(shape), **kwargs)


def _mix_prompt(h_in, h_out, n_batch, seq, lw, layer):
    tl = max(t for t in range(CHUNK, TL_MIX + 1, CHUNK) if seq % (2 * t) == 0)
    nj = seq // (2 * tl)
    last_tile = n_batch * seq // tl - 1
    once = pl.Buffered(1)
    kern = functools.partial(_mix_prompt_kernel, tl=tl)
    return pl.pallas_call(
        kern,
        name="mix_prompt",
        grid=(n_batch, nj),
        in_specs=[
            pl.BlockSpec((2 * tl, D_MODEL), lambda b, j: (b * nj + j, 0)),
            pl.BlockSpec((tl, D_MODEL), lambda b, j: (jnp.minimum(2 * (b * nj + j) + 2, last_tile), 0)),
            pl.BlockSpec(memory_space=pl.ANY),
            _layer_spec((1, D_MODEL), layer),
            _layer_spec((D_MODEL, P_W), layer, pipeline_mode=once),
            _layer_spec((CONV_W, QK_W), layer),
            _layer_spec((1, QK_W), layer),
            _layer_spec((1, LANES), layer),
            _layer_spec((1, W_A), layer),
            _layer_spec((1, W_B), layer),
            _layer_spec((G_B, CHUNK, CHUNK), layer),
            _layer_spec((CHUNK, LANES), layer),
            _layer_spec((D_MODEL, D_MODEL), layer, pipeline_mode=once),
        ],
        out_specs=[
            pl.BlockSpec((2 * tl, D_MODEL), lambda b, j: (b * nj + j, 0)),
            pl.BlockSpec((1, H_A, DH_A, DH_A), lambda b, j: (b, 0, 0, 0)),
            pl.BlockSpec((1, H_A, DH_A), lambda b, j: (b, 0, 0)),
            pl.BlockSpec((1, H_A, LANES), lambda b, j: (b, 0, 0)),
            pl.BlockSpec((1, CONV_W - 1, QK_W), lambda b, j: (b, 0, 0)),
        ],
        out_shape=[
            jax.ShapeDtypeStruct(h_out.shape, F32),
            jax.ShapeDtypeStruct((n_batch, H_A, DH_A, DH_A), F32),
            jax.ShapeDtypeStruct((n_batch, H_A, DH_A), F32),
            jax.ShapeDtypeStruct((n_batch, H_A, LANES), F32),
            jax.ShapeDtypeStruct((n_batch, CONV_W - 1, QK_W), F32),
        ],
        scratch_shapes=[
            pltpu.VMEM((tl, P_W), F32),
            pltpu.VMEM((tl, P_W), F32),
            pltpu.VMEM((SUBLANES + tl, QK_W), F32),
            pltpu.VMEM((tl, D_MODEL), BF16),
            pltpu.VMEM((H_A, DH_A, DH_A), F32),
            pltpu.VMEM((SUBLANES, LANES), F32),
            pltpu.VMEM((SUBLANES, LANES), F32),
        ],
        input_output_aliases={2: 0},
        compiler_params=_cparams("arbitrary", "arbitrary"),
    )(h_in, h_in, h_out, lw["g_mix"], lw["w_in"], lw["conv_w"], lw["conv_b"], lw["gate_b"], lw["hn_g"], lw["vn_g"],
      lw["w_s"], lw["b_s_col"], lw["w_out"])


def _mix_sample_kernel(x_ref, hprev_ref, cprev_ref, gmix_ref, win_ref, cw_ref, cb_ref, gb_ref, hng_ref, vng_ref,
                       w00_ref, bs0_ref, wout_ref, sconv_ref, c_in, n_in, m_in,
                       out_ref, c_out, n_out, m_out, conv_out, vrow_out,
                       proj, hcat, *, bt):
    del hprev_ref, cprev_ref
    i = pl.program_id(0)
    ni = pl.num_programs(0)

    @pl.when(i == 0)
    def _():
        x = x_ref[...]
        xn = _rms(x, gmix_ref[...]).astype(BF16)
        proj[...] = jnp.dot(xn, win_ref[...], preferred_element_type=F32)
        pre = proj[:, P_QK:P_QK + QK_W]
        qk = cb_ref[...] + cw_ref[CONV_W - 1:CONV_W, :] * pre
        for jj in range(CONV_W - 1):
            qk = qk + cw_ref[jj:jj + 1, :] * sconv_ref[jj]
        for jj in range(CONV_W - 2):
            conv_out[jj] = sconv_ref[jj + 1]
        conv_out[CONV_W - 2] = pre
        qk = qk * jax.nn.sigmoid(qk)
        proj[:, 0:W_A] = qk[:, 0:W_A]
        proj[:, W_A:QK_W] = qk[:, W_A:QK_W] * (DH_A ** -0.5)
        for g in range(G_B):
            sl = slice(g * DG_B, (g + 1) * DG_B)
            vb = proj[:, P_VB + g * DG_B:P_VB + (g + 1) * DG_B]
            u = proj[:, P_U + g * DG_B:P_U + (g + 1) * DG_B]
            vn = _rms(jax.nn.gelu(vb), vng_ref[:, sl])
            vrow_out[:, sl] = vn
            z = w00_ref[:, sl] * vn + bs0_ref[:, sl]
            hcat[:, W_A + g * DG_B:W_A + (g + 1) * DG_B] = jax.nn.gelu(u) * z

    rows = pl.ds(pl.multiple_of(i * bt, bt), bt)
    gates = proj[rows, P_G:P_G + LANES] + gb_ref[...]
    lane = lax.broadcasted_iota(jnp.int32, (bt, LANES), 1)
    sub = lax.broadcasted_iota(jnp.int32, (bt, DH_A), 0)
    m_new_all = jnp.zeros((bt, LANES), F32)
    for h in range(H_A):
        ig = gates[:, h:h + 1]
        lf = _log_sigmoid(gates[:, H_A + h:H_A + h + 1])
        m_prev = m_in[rows, h:h + 1]
        inter = lf + m_prev
        m_t = jnp.maximum(inter, ig)
        w_inter = jnp.exp(inter - m_t)
        e_d = jnp.exp(ig - m_t)
        q = proj[rows, h * DH_A:(h + 1) * DH_A]
        k = proj[rows, W_A + h * DH_A:W_A + (h + 1) * DH_A]
        v = proj[rows, P_V + h * DH_A:P_V + (h + 1) * DH_A]
        n_old = n_in[h]
        s = jnp.sum(q * k, axis=-1, keepdims=True) * e_d
        vw = v * e_d
        vw_t = jnp.concatenate([vw, jnp.zeros((DH_A - bt, DH_A), F32)], axis=0).T
        qb = q.astype(BF16)
        qc = jnp.zeros((bt, DH_A), F32)
        for t in range(bt):
            c_old = c_in[t, h]
            r = _dot_nt(qb, c_old.astype(BF16))
            qc = jnp.where(sub == t, r, qc)
            c_out[t, h] = w_inter[t:t + 1, :] * c_old + vw_t[:, t:t + 1] * k[t:t + 1, :]
        num = w_inter * qc + s * v
        nq = w_inter * jnp.sum(q * n_old, axis=-1, keepdims=True) + s
        hh = num / jnp.maximum(jnp.abs(nq), jnp.exp(-m_t))
        n_out[h] = w_inter * n_old + e_d * k
        m_new_all = jnp.where(lane == h, m_t, m_new_all)
        ha = _rms(hh, hng_ref[:, h * DH_A:(h + 1) * DH_A])
        o = proj[rows, P_O + h * DH_A:P_O + (h + 1) * DH_A]
        hcat[rows, h * DH_A:(h + 1) * DH_A] = ha * jax.nn.sigmoid(o)
    m_out[rows, :] = m_new_all

    @pl.when(i == ni - 1)
    def _():
        out_ref[...] = x_ref[...] + jnp.dot(hcat[...].astype(BF16), wout_ref[...],
                                            preferred_element_type=F32)


def _mix_sample(h_in, h_out, c_all, lw, sconv_t, c0, n0_t, m0_pad, n_s, layer):
    bt = BT_SAMPLE
    ni = n_s // bt
    const = lambda i: (0, 0)
    in_blk = h_in.shape[0] // n_s - 1
    out_blk = h_out.shape[0] // n_s - 1
    kern = functools.partial(_mix_sample_kernel, bt=bt)
    return pl.pallas_call(
        kern,
        name="mix_sample",
        grid=(ni,),
        in_specs=[
            pl.BlockSpec((n_s, D_MODEL), lambda i: (in_blk, 0)),
            pl.BlockSpec(memory_space=pl.ANY),
            pl.BlockSpec(memory_space=pl.ANY),
            _layer_spec((1, D_MODEL), layer),
            _layer_spec((D_MODEL, P_W), layer),
            _layer_spec((CONV_W, QK_W), layer),
            _layer_spec((1, QK_W), layer),
            _layer_spec((1, LANES), layer),
            _layer_spec((1, W_A), layer),
            _layer_spec((1, W_B), layer),
            _layer_spec((1, W_B), layer),
            _layer_spec((1, W_B), layer),
            _layer_spec((D_MODEL, D_MODEL), layer),
            pl.BlockSpec((None, CONV_W - 1, n_s, QK_W), lambda i: (layer, 0, 0, 0)),
            pl.BlockSpec((None, bt, H_A, DH_A, DH_A), lambda i: (layer, i, 0, 0, 0)),
            pl.BlockSpec((None, H_A, bt, DH_A), lambda i: (layer, 0, i, 0)),
            pl.BlockSpec((None, n_s, LANES), lambda i: (layer, 0, 0)),
        ],
        out_specs=[
            pl.BlockSpec((n_s, D_MODEL), lambda i: (out_blk, 0)),
            pl.BlockSpec((None, bt, H_A, DH_A, DH_A), lambda i: (layer, i, 0, 0, 0)),
            pl.BlockSpec((H_A, bt, DH_A), lambda i: (0, i, 0)),
            pl.BlockSpec((n_s, LANES), const),
            pl.BlockSpec((CONV_W - 1, n_s, QK_W), lambda i: (0, 0, 0)),
            pl.BlockSpec((n_s, W_B), const),
        ],
        out_shape=[
            jax.ShapeDtypeStruct(h_out.shape, F32),
            jax.ShapeDtypeStruct(c_all.shape, F32),
            jax.ShapeDtypeStruct(n0_t.shape[1:], F32),
            jax.ShapeDtypeStruct((n_s, LANES), F32),
            jax.ShapeDtypeStruct((CONV_W - 1, n_s, QK_W), F32),
            jax.ShapeDtypeStruct((n_s, W_B), F32),
        ],
        scratch_shapes=[
            pltpu.VMEM((n_s, P_W), F32),
            pltpu.VMEM((n_s, D_MODEL), F32),
        ],
        input_output_aliases={1: 0, 2: 1},
        compiler_params=_cparams("arbitrary"),
    )(h_in, h_out, c_all, lw["g_mix"], lw["w_in"], lw["conv_w"], lw["conv_b"], lw["gate_b"], lw["hn_g"], lw["vn_g"],
      lw["w00_row"], lw["bs0_row"], lw["w_out"], sconv_t, c0, n0_t, m0_pad)


def _router_kernel(h_ref, g_ref, wr_ref, b_ref, route_ref, route_t_ref, tcnt_ref, *, tm):
    hn = _rms(h_ref[...], g_ref[...]).astype(BF16)
    logits = jnp.dot(hn, wr_ref[...], preferred_element_type=F32) + b_ref[...]

    lane_i = lax.broadcasted_iota(jnp.int32, (tm, LANES), 1)
    lane = lane_i.astype(F32)
    neg = -jnp.inf
    is_g = lane_i < N_GROUPS
    gl = jnp.where(is_g, logits, neg)
    gmax = jnp.max(gl, axis=-1, keepdims=True)
    g_sel = jnp.min(jnp.where(gl == gmax, lane, float(LANES)), axis=-1, keepdims=True)
    g_w = 1.0 / jnp.sum(jnp.where(is_g, jnp.exp(logits - gmax), 0.0), axis=-1, keepdims=True)

    e_lane = lane_i - N_GROUPS
    lane_grp = (e_lane >> 3).astype(F32)
    in_grp = (e_lane >= 0) & (e_lane < N_EXPERTS) & (lane_grp == g_sel)
    el = jnp.where(in_grp, logits, neg)
    v1 = jnp.max(el, axis=-1, keepdims=True)
    i1 = jnp.min(jnp.where(el == v1, lane, float(LANES)), axis=-1, keepdims=True)
    el2 = jnp.where(lane == i1, neg, el)
    v2 = jnp.max(el2, axis=-1, keepdims=True)
    i2 = jnp.min(jnp.where(el2 == v2, lane, float(LANES)), axis=-1, keepdims=True)
    e21 = jnp.exp(v2 - v1)
    den = 1.0 + e21
    w1 = g_w * (1.0 / den)
    w2 = g_w * (e21 / den)

    onehot = jnp.where((lane == i1) | (lane == i2), 1.0, 0.0)
    r_i = lax.broadcasted_iota(jnp.int32, (tm, tm), 0)
    c_i = lax.broadcasted_iota(jnp.int32, (tm, tm), 1)
    strict = jnp.where(r_i > c_i, 1.0, 0.0).astype(BF16)
    before = jnp.dot(strict, onehot.astype(BF16), preferred_element_type=F32)
    cnt = jnp.sum(onehot, axis=0, keepdims=True)
    cnt8 = jnp.floor((cnt + (SUBLANES - 1.0)) * (1.0 / SUBLANES)) * SUBLANES
    l_r = lax.broadcasted_iota(jnp.int32, (LANES, LANES), 0)
    l_c = lax.broadcasted_iota(jnp.int32, (LANES, LANES), 1)
    lanes_before = jnp.where(l_r < l_c, 1.0, 0.0).astype(BF16)
    start = _split_dot(jnp.broadcast_to(cnt8, (2 * SUBLANES, LANES)), lanes_before)[0:1, :]
    slot = start + before
    pos1 = jnp.sum(jnp.where(lane == i1, slot, 0.0), axis=-1, keepdims=True)
    pos2 = jnp.sum(jnp.where(lane == i2, slot, 0.0), axis=-1, keepdims=True)
    tcnt_ref[...] = jnp.broadcast_to(cnt8, tcnt_ref.shape)

    out = jnp.zeros((tm, LANES), F32)
    for idx, val in enumerate((pos1, pos2, w1, w2)):
        out = jnp.where(lane_i == idx, val, out)
    route_ref[...] = out
    route_t_ref[...] = out.T[0:SUBLANES, :]


def _router(h1, lw, layer):
    t_total = h1.shape[0]
    tm = _token_tile(t_total)
    n_tiles = t_total // tm
    return pl.pallas_call(
        functools.partial(_router_kernel, tm=tm),
        name="router",
        grid=(n_tiles,),
        in_specs=[
            pl.BlockSpec((tm, D_MODEL), lambda i: (i, 0)),
            _layer_spec((1, D_MODEL), layer),
            _layer_spec((D_MODEL, LANES), layer),
            _layer_spec((1, LANES), layer),
        ],
        out_specs=[
            pl.BlockSpec((tm, LANES), lambda i: (i, 0)),
            pl.BlockSpec((SUBLANES, tm), lambda i: (0, i)),
            pl.BlockSpec((None, SUBLANES, LANES), lambda i: (i, 0, 0)),
        ],
        out_shape=[
            jax.ShapeDtypeStruct((t_total, LANES), F32),
            jax.ShapeDtypeStruct((SUBLANES, t_total), F32),
            jax.ShapeDtypeStruct((n_tiles, SUBLANES, LANES), F32),
        ],
        compiler_params=_cparams("arbitrary"),
    )(h1, lw["g_ffn"], lw["wr"], lw["br"])


def _tile_slots(tm):
    raw = tm * TOP_K + N_EXPERTS * (SUBLANES - 1)
    return -(-raw // LANES) * LANES


def _token_tile(t_total):
    tm = TM_TOK
    while t_total % tm:
        tm //= 2
    return tm


def _run_copies(tile, tables, units_per_tile, make_copy):
    psrc_ref, pdst_ref, npair_ref, ssrc_ref, sdst_ref, nsingle_ref, _ = tables
    pair_base = tile * (units_per_tile // 2)
    single_base = tile * N_EXPERTS

    def per_pair(u, carry):
        make_copy(pl.multiple_of(psrc_ref[pair_base + u], SUBLANES),
                  pl.multiple_of(pdst_ref[pair_base + u], SUBLANES), 2 * SUBLANES).start()
        return carry

    def per_single(u, carry):
        make_copy(pl.multiple_of(ssrc_ref[single_base + u], SUBLANES),
                  pl.multiple_of(sdst_ref[single_base + u], SUBLANES), SUBLANES).start()
        return carry

    lax.fori_loop(0, npair_ref[tile], per_pair, 0)
    lax.fori_loop(0, nsingle_ref[tile], per_single, 0)


def _drain_copies(n_units, make_copy):
    def wait_group(u, carry):
        make_copy(0, 0, DRAIN_GROUP * SUBLANES).wait()
        return carry

    def wait_one(u, carry):
        make_copy(0, 0, SUBLANES).wait()
        return carry

    lax.fori_loop(0, lax.div(n_units, DRAIN_GROUP), wait_group, 0)
    lax.fori_loop(0, lax.rem(n_units, DRAIN_GROUP), wait_one, 0)


def _dispatch_kernel(*refs, tm, slots):
    tables = refs[:N_COPY_TABLES]
    tunits_ref = tables[-1]
    h_ref, g_ref, rt_ref, xs_in, xs_out, srt, sem = refs[N_COPY_TABLES:]
    del xs_in
    i = pl.program_id(0)
    n = pl.num_programs(0)
    buf = lax.rem(i, 2)

    def copy_maker(b):
        def make_copy(tile_row, sorted_row, rows):
            return pltpu.make_async_copy(srt.at[b, pl.ds(tile_row, rows)],
                                         xs_out.at[pl.ds(sorted_row, rows), pl.ds(0, HALF_D)], sem.at[b])
        return make_copy

    @pl.when(i >= 2)
    def _():
        _drain_copies(tunits_ref[i - 2], copy_maker(buf))

    xn = _rms(h_ref[...], g_ref[...]).astype(BF16)
    slot_id = lax.broadcasted_iota(jnp.int32, (slots, tm), 0).astype(F32)
    sel = (slot_id == rt_ref[0:1, :]) | (slot_id == rt_ref[1:2, :])
    srt[buf] = _pack_rows(jnp.dot(jnp.where(sel, 1.0, 0.0).astype(BF16), xn, preferred_element_type=F32))
    _run_copies(i, tables, slots // SUBLANES, copy_maker(buf))

    @pl.when(i == n - 1)
    def _():
        @pl.when(i >= 1)
        def _():
            _drain_copies(tunits_ref[i - 1], copy_maker(1 - buf))

        _drain_copies(tunits_ref[i], copy_maker(buf))


def _dispatch(tables, h1, route_t, lw, xs, layer):
    t_total = h1.shape[0]
    tm = _token_tile(t_total)
    slots = _tile_slots(tm)
    return pl.pallas_call(
        functools.partial(_dispatch_kernel, tm=tm, slots=slots),
        name="dispatch",
        grid_spec=pltpu.PrefetchScalarGridSpec(
            num_scalar_prefetch=N_COPY_TABLES,
            grid=(t_total // tm,),
            in_specs=[
                pl.BlockSpec((tm, D_MODEL), lambda i, *_: (i, 0)),
                _layer_spec((1, D_MODEL), layer),
                pl.BlockSpec((SUBLANES, tm), lambda i, *_: (0, i)),
                pl.BlockSpec(memory_space=pl.ANY),
            ],
            out_specs=pl.BlockSpec(memory_space=pl.ANY),
            scratch_shapes=[pltpu.VMEM((2, slots, HALF_D), U32), pltpu.SemaphoreType.DMA((2,))],
        ),
        out_shape=jax.ShapeDtypeStruct(xs.shape, U32),
        input_output_aliases={N_COPY_TABLES + 3: 0},
        compiler_params=_cparams("arbitrary"),
    )(*tables, h1, lw["g_ffn"], route_t, xs)


def _ffn_kernel(bstart_ref, nblk_ref, nused_ref, w1_ref, w3_ref, w2_ref, xs_ref, ys_ref,
                w1b, w3b, w2b, xbuf, ybuf, sem_in, sem_out):
    e = pl.program_id(0)
    g0 = bstart_ref[e]
    nb = nblk_ref[e]
    n_used = nused_ref[0]

    def in_copy(g):
        slot = lax.rem(g, FFN_RING)
        r = pl.multiple_of(g * FFN_BLK, FFN_BLK)
        return pltpu.make_async_copy(xs_ref.at[pl.ds(r, FFN_BLK), pl.ds(0, HALF_D)], xbuf.at[slot],
                                     sem_in.at[slot])

    def out_copy(g):
        slot = lax.rem(g, FFN_RING)
        r = pl.multiple_of(g * FFN_BLK, FFN_BLK)
        return pltpu.make_async_copy(ybuf.at[slot], ys_ref.at[pl.ds(r, FFN_BLK)], sem_out.at[slot])

    @pl.when(e == 0)
    def _():
        for ahead in range(FFN_RING - 1):
            @pl.when(ahead < n_used)
            def _():
                in_copy(jnp.int32(ahead)).start()

    @pl.when(nb > 0)
    def _():
        w1b[...] = w1_ref[...].astype(BF16)
        w3b[...] = w3_ref[...].astype(BF16)
        w2b[...] = w2_ref[...].astype(BF16)

    def block_step(k, carry):
        g = g0 + k
        slot = lax.rem(g, FFN_RING)
        in_copy(g).wait()

        @pl.when(g + (FFN_RING - 1) < n_used)
        def _():
            in_copy(g + (FFN_RING - 1)).start()

        @pl.when(g >= FFN_RING)
        def _():
            out_copy(g - FFN_RING).wait()

        x_lo, x_hi = _unpack_rows(xbuf[slot])
        a = (jnp.dot(x_lo, w1b[0:HALF_D, :], preferred_element_type=F32)
             + jnp.dot(x_hi, w1b[HALF_D:, :], preferred_element_type=F32))
        b = (jnp.dot(x_lo, w3b[0:HALF_D, :], preferred_element_type=F32)
             + jnp.dot(x_hi, w3b[HALF_D:, :], preferred_element_type=F32))
        hmid = (a * jax.nn.sigmoid(a) * b).astype(BF16)
        y = jnp.dot(hmid, w2b[...], preferred_element_type=F32)
        ybuf[slot] = lax.bitcast_convert_type(y, U32)
        out_copy(g).start()
        return carry

    lax.fori_loop(0, nb, block_step, 0)

    @pl.when(e == pl.num_programs(0) - 1)
    def _():
        for back in range(FFN_RING, 0, -1):
            @pl.when(n_used >= back)
            def _():
                out_copy(n_used - back).wait()


def _ffn(ffn_tables, xs, w1, w3, w2, layer):
    def w_map(e, *_):
        return (layer, e, 0, 0)

    return pl.pallas_call(
        _ffn_kernel,
        name="expert_ffn",
        grid_spec=pltpu.PrefetchScalarGridSpec(
            num_scalar_prefetch=3,
            grid=(N_EXPERTS,),
            in_specs=[
                pl.BlockSpec((None, None, D_MODEL, D_FF), w_map),
                pl.BlockSpec((None, None, D_MODEL, D_FF), w_map),
                pl.BlockSpec((None, None, D_FF, D_MODEL), w_map),
                pl.BlockSpec(memory_space=pl.ANY),
            ],
            out_specs=pl.BlockSpec(memory_space=pl.ANY),
            scratch_shapes=[
                pltpu.VMEM((D_MODEL, D_FF), BF16),
                pltpu.VMEM((D_MODEL, D_FF), BF16),
                pltpu.VMEM((D_FF, D_MODEL), BF16),
                pltpu.VMEM((FFN_RING, FFN_BLK, HALF_D), U32),
                pltpu.VMEM((FFN_RING, FFN_BLK, D_MODEL), U32),
                pltpu.SemaphoreType.DMA((FFN_RING,)),
                pltpu.SemaphoreType.DMA((FFN_RING,)),
            ],
        ),
        out_shape=jax.ShapeDtypeStruct(xs.shape, U32),
        input_output_aliases={6: 0},
        compiler_params=_cparams("arbitrary"),
    )(*ffn_tables, w1, w3, w2, xs)


def _ple_kernel(*refs, tm, slots, n_s, final):
    tables = refs[:N_COPY_TABLES]
    tunits_ref = tables[-1]
    h_ref, route_ref, p_ref, ps_ref, wpg_ref, wple_ref, pg_ref, fg_ref, ys_ref = refs[N_COPY_TABLES:N_COPY_TABLES + 9]
    rest = refs[N_COPY_TABLES + 9:]
    if final:
        yp_ref, ysm_ref, ysrt, sem = rest
    else:
        out_ref, ysrt, sem = rest
    i = pl.program_id(0)
    n = pl.num_programs(0)
    buf = lax.rem(i, 2)

    def copy_maker(b):
        def make_copy(tile_row, sorted_row, rows):
            return pltpu.make_async_copy(ys_ref.at[pl.ds(sorted_row, rows)],
                                         ysrt.at[b, pl.ds(tile_row, rows)], sem.at[b])
        return make_copy

    @pl.when(i == 0)
    def _():
        ysrt[...] = jnp.zeros_like(ysrt)
        _run_copies(i, tables, slots // SUBLANES, copy_maker(buf))

    @pl.when(i + 1 < n)
    def _():
        _run_copies(i + 1, tables, slots // SUBLANES, copy_maker(1 - buf))

    p = p_ref[...]
    p_last = jnp.concatenate([p[0:tm - n_s, :], ps_ref[...]], axis=0)
    p = jnp.where(i == n - 1, p_last, p)
    pe = _rms(jnp.dot(p.astype(BF16), wple_ref[...], preferred_element_type=F32), pg_ref[...])
    _drain_copies(tunits_ref[i], copy_maker(buf))

    route = route_ref[...]
    y = lax.bitcast_convert_type(ysrt[buf], F32)
    y_head = y.astype(BF16)
    y_rem = (y - y_head.astype(F32)).astype(BF16)
    slot_id = lax.broadcasted_iota(jnp.int32, (tm, slots), 1).astype(F32)
    moe = None
    weighted_pick = jnp.zeros((tm, slots), F32)
    for kk in range(TOP_K):
        hit = slot_id == route[:, kk:kk + 1]
        w_kk = route[:, TOP_K + kk:TOP_K + kk + 1]
        term = w_kk * jnp.dot(jnp.where(hit, 1.0, 0.0).astype(BF16), y_head, preferred_element_type=F32)
        moe = term if moe is None else moe + term
        weighted_pick = jnp.where(hit, w_kk, weighted_pick)
    moe = moe + jnp.dot(weighted_pick.astype(BF16), y_rem, preferred_element_type=F32)
    h2 = h_ref[...] + moe
    gate = jax.nn.sigmoid(jnp.dot(h2.astype(BF16), wpg_ref[...], preferred_element_type=F32))
    out = h2 + gate * pe
    if final:
        fin = _rms(out, fg_ref[...])
        yp_ref[...] = fin

        @pl.when(i == pl.num_programs(0) - 1)
        def _():
            ysm_ref[...] = fin[tm - n_s:tm, :]
    else:
        out_ref[...] = out


def _ple(tables, h1, route, p_prompt, p_sample, lw, final_g, ys, n_s, layer, final, in_place):
    t_total = h1.shape[0]
    tm = _token_tile(t_total)
    slots = _tile_slots(tm)
    assert n_s <= tm and (t_total - n_s) % tm == tm - n_s
    const = lambda i, *_: (0, 0)
    tok = lambda i, *_: (i, 0)
    if final:
        out_specs = [pl.BlockSpec((tm, D_MODEL), tok), pl.BlockSpec((n_s, D_MODEL), const)]
        out_shape = [jax.ShapeDtypeStruct((t_total - n_s, D_MODEL), F32),
                     jax.ShapeDtypeStruct((n_s, D_MODEL), F32)]
        aliases = {}
    else:
        out_specs = [pl.BlockSpec((tm, D_MODEL), tok)]
        out_shape = [jax.ShapeDtypeStruct((t_total, D_MODEL), F32)]
        aliases = {N_COPY_TABLES: 0} if in_place else {}
    return pl.pallas_call(
        functools.partial(_ple_kernel, tm=tm, slots=slots, n_s=n_s, final=final),
        name="combine_ple",
        grid_spec=pltpu.PrefetchScalarGridSpec(
            num_scalar_prefetch=N_COPY_TABLES,
            grid=(t_total // tm,),
            in_specs=[
                pl.BlockSpec((tm, D_MODEL), tok),
                pl.BlockSpec((tm, LANES), tok),
                pl.BlockSpec((None, tm, PLE_DIM), lambda i, *_: (layer, i, 0)),
                pl.BlockSpec((None, n_s, PLE_DIM), lambda i, *_: (layer, 0, 0)),
                _layer_spec((D_MODEL, D_MODEL), layer),
                _layer_spec((PLE_DIM, D_MODEL), layer),
                _layer_spec((1, D_MODEL), layer),
                pl.BlockSpec((1, D_MODEL), const),
                pl.BlockSpec(memory_space=pl.ANY),
            ],
            out_specs=out_specs,
            scratch_shapes=[pltpu.VMEM((2, slots, D_MODEL), U32), pltpu.SemaphoreType.DMA((2,))],
        ),
        out_shape=out_shape,
        input_output_aliases=aliases,
        compiler_params=_cparams("arbitrary"),
    )(*tables, h1, route, p_prompt, p_sample, lw["w_pg"], lw["w_ple"], lw["ple_g"], final_g, ys)


def _stacked_weights(norm_mix_g, w_in, conv_w, conv_b, b_igate, b_fgate, hnorm_g, vnorm_g, w_s, b_s, w_out,
                     norm_ffn_g, router_g_w, router_g_b, router_e_w, router_e_b, w_ple, ple_norm_g, w_ple_gate):
    a0 = QK_W + 2 * W_A
    lane_pad = lambda x, used: jnp.pad(x, ((0, 0),) * (x.ndim - 1) + ((0, LANES - used),))
    w_packed = jnp.concatenate(
        [w_in[:, :, :a0], w_in[:, :, a0 + 2 * H_A:], lane_pad(w_in[:, :, a0:a0 + 2 * H_A], 2 * H_A)],
        axis=2).astype(BF16)
    gate_b = lane_pad(jnp.concatenate([b_igate, b_fgate], axis=1), 2 * H_A)[:, None, :]
    wr = lane_pad(jnp.concatenate([router_g_w, router_e_w], axis=2), N_GROUPS + N_EXPERTS).astype(BF16)
    br = lane_pad(jnp.concatenate([router_g_b, router_e_b], axis=1), N_GROUPS + N_EXPERTS)[:, None, :]
    return {
        "g_mix": norm_mix_g[:, None, :],
        "w_in": w_packed,
        "conv_w": conv_w,
        "conv_b": conv_b[:, None, :],
        "gate_b": gate_b,
        "hn_g": hnorm_g[:, None, :],
        "vn_g": vnorm_g[:, None, :],
        "w_s": w_s,
        "b_s_col": lane_pad(jnp.transpose(b_s, (0, 2, 1)), G_B),
        "w00_row": jnp.repeat(w_s[:, :, 0, 0], DG_B, axis=1)[:, None, :],
        "bs0_row": jnp.repeat(b_s[:, :, 0], DG_B, axis=1)[:, None, :],
        "w_out": w_out.astype(BF16),
        "g_ffn": norm_ffn_g[:, None, :],
        "wr": wr,
        "br": br,
        "w_pg": w_ple_gate.astype(BF16),
        "w_ple": w_ple.astype(BF16),
        "ple_g": ple_norm_g[:, None, :],
    }


def _routing_tables(tcnt, units_per_tile):
    c8 = tcnt[:, 0, N_GROUPS:N_GROUPS + N_EXPERTS].astype(jnp.int32)
    blocks = (jnp.sum(c8, axis=0) + FFN_BLK - 1) // FFN_BLK
    blk_end = jnp.cumsum(blocks)
    pstart = (blk_end - blocks) * FFN_BLK
    run_dst = pstart[None, :] + jnp.cumsum(c8, axis=0) - c8
    run_src = jnp.cumsum(c8, axis=1) - c8
    units = c8 // SUBLANES
    n_pair = units // 2
    n_single = units - 2 * n_pair
    pair_first = jnp.cumsum(n_pair, axis=1) - n_pair
    single_first = jnp.cumsum(n_single, axis=1) - n_single

    def piece_lists(first, count, run_off, n_list, rows):
        j = jnp.arange(n_list, dtype=jnp.int32)[None, :, None]
        hit = (j >= first[:, None, :]) & (j < (first + count)[:, None, :])
        off = run_off[:, None, :] + (j - first[:, None, :]) * rows
        src = jnp.sum(jnp.where(hit, run_src[:, None, :] + off, 0), axis=2)
        dst = jnp.sum(jnp.where(hit, run_dst[:, None, :] + off, 0), axis=2)
        return src.reshape(-1), dst.reshape(-1)

    psrc, pdst = piece_lists(pair_first, n_pair, jnp.zeros_like(c8), units_per_tile // 2, 2 * SUBLANES)
    ssrc, sdst = piece_lists(single_first, n_single, n_pair * (2 * SUBLANES), N_EXPERTS, SUBLANES)
    tables = (psrc, pdst, jnp.sum(n_pair, axis=1), ssrc, sdst, jnp.sum(n_single, axis=1),
              jnp.sum(units, axis=1))
    assert len(tables) == N_COPY_TABLES
    ffn_tables = (blk_end - blocks, blocks, blk_end[-1:])
    return tables, ffn_tables


def kernel(x_prompt, x_sample, state_C, state_n, state_m, state_conv, p_prompt, p_sample, norm_mix_g, w_in, conv_w, conv_b, b_igate, b_fgate, hnorm_g, vnorm_g, w_s, b_s, w_out, norm_ffn_g, router_g_w, router_g_b, router_e_w, router_e_b, w1, w3, w2, w_ple, ple_norm_g, w_ple_gate, final_norm_g):
    n_batch, seq, _ = x_prompt.shape
    n_s = x_sample.shape[0]
    depth = w_in.shape[0]
    t_p = n_batch * seq
    t_total = t_p + n_s
    assert seq % CHUNK == 0 and t_p % n_s == 0 and n_s % BT_SAMPLE == 0

    n_tiles = t_total // _token_tile(t_total)
    max_rows = t_total * TOP_K + N_EXPERTS * ((SUBLANES - 1) * n_tiles + FFN_BLK - 1)
    n_blocks = max_rows // FFN_BLK
    xs = jnp.zeros((n_blocks * FFN_BLK, D_MODEL), U32)

    p_prompt_t = p_prompt.reshape(depth, t_p, PLE_DIM)
    p_sample_t = p_sample.reshape(depth, n_s, PLE_DIM)
    sconv_t = jnp.transpose(state_conv, (0, 2, 1, 3))
    n0_t = jnp.transpose(state_n, (0, 2, 1, 3))
    m0_pad = jnp.pad(state_m, ((0, 0), (0, 0), (0, LANES - H_A)))
    final_g = final_norm_g[None, :]

    src_p = x_prompt.reshape(t_p, D_MODEL)
    src_s = x_sample.reshape(n_s, D_MODEL)
    spare = jnp.zeros((t_total, D_MODEL), F32)
    c_all = jnp.zeros(state_C.shape, F32)
    outs = {k: [] for k in ("Cp", "np", "mp", "cbp", "ns", "ms", "cbs", "vs")}
    res = None
    lw = _stacked_weights(norm_mix_g, w_in, conv_w, conv_b, b_igate, b_fgate, hnorm_g, vnorm_g, w_s, b_s,
                          w_out, norm_ffn_g, router_g_w, router_g_b, router_e_w, router_e_b, w_ple,
                          ple_norm_g, w_ple_gate)
    for l in range(depth):
        h_mix, c_p, n_p, m_p, cb_p = _mix_prompt(src_p, spare, n_batch, seq, lw, l)
        h_mix, c_all, n_s_t, m_s, cb_s, v_s = _mix_sample(src_s, h_mix, c_all, lw, sconv_t, state_C, n0_t,
                                                          m0_pad, n_s, l)
        route, route_t, tcnt = _router(h_mix, lw, l)
        tables, ffn_tables = _routing_tables(tcnt, _tile_slots(_token_tile(t_total)) // SUBLANES)
        xs = _dispatch(tables, h_mix, route_t, lw, xs, l)
        xs = _ffn(ffn_tables, xs, w1, w3, w2, l)
        final = l == depth - 1
        res = _ple(tables, h_mix, route, p_prompt_t, p_sample_t, lw, final_g, xs, n_s, l, final, in_place=l > 0)
        if l == 0:
            spare = h_mix
        else:
            spare = src_p
        src_p = src_s = res[0]
        outs["Cp"].append(c_p)
        outs["np"].append(n_p)
        outs["mp"].append(m_p[:, :, 0])
        outs["cbp"].append(cb_p)
        outs["ns"].append(jnp.transpose(n_s_t, (1, 0, 2)))
        outs["ms"].append(m_s[:, 0:H_A])
        outs["cbs"].append(jnp.transpose(cb_s, (1, 0, 2)))
        outs["vs"].append(v_s[:, None, :])

    y_prompt = res[0].reshape(n_batch, seq, D_MODEL)
    y_sample = res[1].reshape(n_s, 1, D_MODEL)
    st = lambda k: jnp.stack(outs[k])
    return (y_prompt, y_sample, st("Cp"), st("np"), st("mp"), st("cbp"),
            c_all, st("ns"), st("ms"), st("cbs"), st("vs"))
```

```python
import functools

import jax
import jax.numpy as jnp
from jax import lax
from jax.experimental import pallas as pl
from jax.experimental.pallas import tpu as pltpu

F32 = jnp.float32
BF16 = jnp.bfloat16
U32 = jnp.uint32

D_MODEL = 1024
W_A = 512
H_A = 4
DH_A = 128
W_B = 512
G_B = 4
DG_B = 128
CHUNK = 128
CONV_W = 4
QK_W = 2 * W_A
N_GROUPS = 4
EXPERTS_PER_GROUP = 8
N_EXPERTS = N_GROUPS * EXPERTS_PER_GROUP
TOP_K = 2
D_FF = 512
PLE_DIM = 256
EPS = 1e-6
HALF_D = D_MODEL // 2

LANES = 128
SUBLANES = 8
VMEM_LIMIT_BYTES = 56 * 1024 * 1024

P_QK = 0
P_V = QK_W
P_O = P_V + W_A
P_U = P_O + W_A
P_VB = P_U + W_B
P_G = P_VB + W_B
P_W = P_G + LANES

TL_MIX = 512
BT_SAMPLE = 16
TM_TOK = 384
FFN_BLK = 256
FFN_RING = 6
DRAIN_GROUP = 32
N_COPY_TABLES = 7


def _cparams(*sem):
    return pltpu.CompilerParams(dimension_semantics=sem, vmem_limit_bytes=VMEM_LIMIT_BYTES)


def _rms(x, g):
    return x * lax.rsqrt(jnp.mean(x * x, axis=-1, keepdims=True) + EPS) * g


def _log_sigmoid(x):
    return -(jnp.maximum(-x, 0.0) + jnp.log1p(jnp.exp(-jnp.abs(x))))


def _split_dot(a, b_bf16):
    hi = a.astype(BF16)
    lo = (a - hi.astype(F32)).astype(BF16)
    return (jnp.dot(hi, b_bf16, preferred_element_type=F32)
            + jnp.dot(lo, b_bf16, preferred_element_type=F32))


def _dot_nt(a, b):
    return lax.dot_general(a, b, (((1,), (1,)), ((), ())), preferred_element_type=F32)


def _pack_rows(x):
    lo = lax.bitcast_convert_type(x[:, :HALF_D], U32) >> 16
    hi = lax.bitcast_convert_type(x[:, HALF_D:], U32) & jnp.uint32(0xFFFF0000)
    return lo | hi


def _unpack_rows(u):
    lo = lax.bitcast_convert_type(u << 16, F32)
    hi = lax.bitcast_convert_type(u & jnp.uint32(0xFFFF0000), F32)
    return lo.astype(BF16), hi.astype(BF16)


def _mix_prompt_kernel(x_ref, xnext_ref, hprev_ref, gmix_ref, win_ref, cw_ref, cb_ref, gb_ref, hng_ref,
                       vng_ref, ws_ref, bs_ref, wout_ref,
                       out_ref, c_out, n_out, m_out, conv_out,
                       proj_a, proj_b, xpad, hcat, c_s, n_s, m_s, *, tl):
    del hprev_ref
    b = pl.program_id(0)
    j = pl.program_id(1)
    nj = pl.num_programs(1)
    n_chunks = tl // CHUNK
    bounds = [(P_W // LANES * c // n_chunks) * LANES for c in range(n_chunks + 1)]
    col_groups = [slice(bounds[c], bounds[c + 1]) for c in range(n_chunks)]

    @pl.when(j == 0)
    def _():
        c_s[...] = jnp.zeros_like(c_s)
        n_s[...] = jnp.zeros_like(n_s)
        m_s[...] = jnp.zeros_like(m_s)
        xpad[0:SUBLANES, :] = jnp.zeros((SUBLANES, QK_W), F32)

    @pl.when((b == 0) & (j == 0))
    def _():
        xn0 = _rms(x_ref[0:tl, :], gmix_ref[...]).astype(BF16)
        proj_a[...] = jnp.dot(xn0, win_ref[...], preferred_element_type=F32)

    row = lax.broadcasted_iota(jnp.int32, (CHUNK, CHUNK), 0)
    col = lax.broadcasted_iota(jnp.int32, (CHUNK, CHUNK), 1)
    causal = row >= col
    tril = jnp.where(causal, 1.0, 0.0).astype(BF16)
    triu = jnp.where(row <= col, 1.0, 0.0).astype(BF16)
    lane = lax.broadcasted_iota(jnp.int32, (CHUNK, LANES), 1)
    wm = [jnp.where(causal, ws_ref[g], 0.0).astype(BF16) for g in range(G_B)]

    def conv_step(proj):
        pre = proj[:, P_QK:P_QK + QK_W]
        xpad[SUBLANES:SUBLANES + tl, :] = pre
        qk = cb_ref[...] + cw_ref[CONV_W - 1:CONV_W, :] * pre
        for jj in range(1, CONV_W):
            qk = qk + cw_ref[CONV_W - 1 - jj:CONV_W - jj, :] * xpad[SUBLANES - jj:SUBLANES - jj + tl, :]
        xpad[SUBLANES - (CONV_W - 1):SUBLANES, :] = pre[tl - (CONV_W - 1):tl, :]
        qk = qk * jax.nn.sigmoid(qk)
        proj[:, 0:W_A] = qk[:, 0:W_A]
        proj[:, W_A:QK_W] = qk[:, W_A:QK_W] * (DH_A ** -0.5)

    def chunk_step(proj, c):
        rows = slice(c * CHUNK, (c + 1) * CHUNK)
        gl = proj[rows, P_G:P_G + LANES] + gb_ref[...]
        gl = jnp.where(lane >= H_A, _log_sigmoid(gl), gl)
        glt = gl.T
        b_col_all = _split_dot_left(tril, gl)
        b_row_all = _split_dot(glt[0:2 * SUBLANES, :], triu)

        for h in range(H_A):
            q32 = proj[rows, h * DH_A:(h + 1) * DH_A]
            k32 = proj[rows, W_A + h * DH_A:W_A + (h + 1) * DH_A]
            v32 = proj[rows, P_V + h * DH_A:P_V + (h + 1) * DH_A]
            q = q32.astype(BF16)
            k = k32.astype(BF16)
            ig_c = gl[:, h:h + 1]
            ig_r = glt[h:h + 1, :]
            b_c = b_col_all[:, H_A + h:H_A + h + 1]
            b_r = b_row_all[H_A + h:H_A + h + 1, :]
            c_old = c_s[h]
            n_old = n_s[h:h + 1, :]
            m_prev = m_s[h:h + 1, 0:1]

            d_log = jnp.where(causal, b_c - b_r + ig_r, -jnp.inf)
            inter = b_c + m_prev
            m_t = jnp.maximum(inter, jnp.max(d_log, axis=-1, keepdims=True))
            s = _dot_nt(q, k) * jnp.exp(d_log - m_t)
            w_inter = jnp.exp(inter - m_t)
            num = (w_inter * _dot_nt(q, c_old.astype(BF16))
                   + jnp.dot(s.astype(BF16), v32.astype(BF16), preferred_element_type=F32))
            nq = (w_inter * jnp.sum(q32 * n_old, axis=-1, keepdims=True)
                  + jnp.sum(s, axis=-1, keepdims=True))
            hh = num / jnp.maximum(jnp.abs(nq), jnp.exp(-m_t))

            m_new = m_t[CHUNK - 1:CHUNK, :]
            b_last = b_c[CHUNK - 1:CHUNK, :]
            w_state = jnp.exp(b_last - b_c + ig_c - m_new)
            decay = jnp.exp(b_last + m_prev - m_new)
            vw = (v32 * w_state).astype(BF16)
            c_s[h] = decay * c_old + lax.dot_general(
                vw, k, (((0,), (0,)), ((), ())), preferred_element_type=F32)
            n_s[h:h + 1, :] = decay * n_old + jnp.sum(w_state * k32, axis=0, keepdims=True)
            m_s[h:h + 1, :] = jnp.broadcast_to(m_new, (1, LANES))

            ha = _rms(hh, hng_ref[:, h * DH_A:(h + 1) * DH_A])
            o = proj[rows, P_O + h * DH_A:P_O + (h + 1) * DH_A]
            hcat[rows, h * DH_A:(h + 1) * DH_A] = (ha * jax.nn.sigmoid(o)).astype(BF16)

        for g in range(G_B):
            vb = proj[rows, P_VB + g * DG_B:P_VB + (g + 1) * DG_B]
            u = proj[rows, P_U + g * DG_B:P_U + (g + 1) * DG_B]
            vn = _rms(jax.nn.gelu(vb), vng_ref[:, g * DG_B:(g + 1) * DG_B])
            z = jnp.dot(wm[g], vn.astype(BF16), preferred_element_type=F32) + bs_ref[:, g:g + 1]
            hcat[rows, W_A + g * DG_B:W_A + (g + 1) * DG_B] = (jax.nn.gelu(u) * z).astype(BF16)

    def tile_pass(row0, proj_cur, proj_nxt, xn_next):
        conv_step(proj_cur)
        for c in range(n_chunks):
            chunk_step(proj_cur, c)
            cs = col_groups[c]
            proj_nxt[:, cs] = jnp.dot(xn_next, win_ref[:, cs], preferred_element_type=F32)
        out_ref[row0:row0 + tl, :] = x_ref[row0:row0 + tl, :] + jnp.dot(
            hcat[...], wout_ref[...], preferred_element_type=F32)

    tile_pass(0, proj_a, proj_b, _rms(x_ref[tl:2 * tl, :], gmix_ref[...]).astype(BF16))
    tile_pass(tl, proj_b, proj_a, _rms(xnext_ref[...], gmix_ref[...]).astype(BF16))

    @pl.when(j == nj - 1)
    def _():
        conv_out[0] = xpad[SUBLANES - (CONV_W - 1):SUBLANES, :]
        c_out[0] = c_s[...]
        n_out[0] = n_s[0:H_A, :]
        m_out[0] = m_s[0:H_A, :]


def _split_dot_left(a_bf16, b):
    hi = b.astype(BF16)
    lo = (b - hi.astype(F32)).astype(BF16)
    return (jnp.dot(a_bf16, hi, preferred_element_type=F32)
            + jnp.dot(a_bf16, lo, preferred_element_type=F32))


def _layer_spec(shape, layer, **kwargs):
    return pl.BlockSpec((None,) + tuple(shape), lambda *_: (layer,) + (0,) * len(shape), **kwargs)


def _mix_prompt(h_in, h_out, n_batch, seq, lw, layer):
    tl = max(t for t in range(CHUNK, TL_MIX + 1, CHUNK) if seq % (2 * t) == 0)
    nj = seq // (2 * tl)
    last_tile = n_batch * seq // tl - 1
    once = pl.Buffered(1)
    kern = functools.partial(_mix_prompt_kernel, tl=tl)
    return pl.pallas_call(
        kern,
        name="mix_prompt",
        grid=(n_batch, nj),
        in_specs=[
            pl.BlockSpec((2 * tl, D_MODEL), lambda b, j: (b * nj + j, 0)),
            pl.BlockSpec((tl, D_MODEL), lambda b, j: (jnp.minimum(2 * (b * nj + j) + 2, last_tile), 0)),
            pl.BlockSpec(memory_space=pl.ANY),
            _layer_spec((1, D_MODEL), layer),
            _layer_spec((D_MODEL, P_W), layer, pipeline_mode=once),
            _layer_spec((CONV_W, QK_W), layer),
            _layer_spec((1, QK_W), layer),
            _layer_spec((1, LANES), layer),
            _layer_spec((1, W_A), layer),
            _layer_spec((1, W_B), layer),
            _layer_spec((G_B, CHUNK, CHUNK), layer),
            _layer_spec((CHUNK, LANES), layer),
            _layer_spec((D_MODEL, D_MODEL), layer, pipeline_mode=once),
        ],
        out_specs=[
            pl.BlockSpec((2 * tl, D_MODEL), lambda b, j: (b * nj + j, 0)),
            pl.BlockSpec((1, H_A, DH_A, DH_A), lambda b, j: (b, 0, 0, 0)),
            pl.BlockSpec((1, H_A, DH_A), lambda b, j: (b, 0, 0)),
            pl.BlockSpec((1, H_A, LANES), lambda b, j: (b, 0, 0)),
            pl.BlockSpec((1, CONV_W - 1, QK_W), lambda b, j: (b, 0, 0)),
        ],
        out_shape=[
            jax.ShapeDtypeStruct(h_out.shape, F32),
            jax.ShapeDtypeStruct((n_batch, H_A, DH_A, DH_A), F32),
            jax.ShapeDtypeStruct((n_batch, H_A, DH_A), F32),
            jax.ShapeDtypeStruct((n_batch, H_A, LANES), F32),
            jax.ShapeDtypeStruct((n_batch, CONV_W - 1, QK_W), F32),
        ],
        scratch_shapes=[
            pltpu.VMEM((tl, P_W), F32),
            pltpu.VMEM((tl, P_W), F32),
            pltpu.VMEM((SUBLANES + tl, QK_W), F32),
            pltpu.VMEM((tl, D_MODEL), BF16),
            pltpu.VMEM((H_A, DH_A, DH_A), F32),
            pltpu.VMEM((SUBLANES, LANES), F32),
            pltpu.VMEM((SUBLANES, LANES), F32),
        ],
        input_output_aliases={2: 0},
        compiler_params=_cparams("arbitrary", "arbitrary"),
    )(h_in, h_in, h_out, lw["g_mix"], lw["w_in"], lw["conv_w"], lw["conv_b"], lw["gate_b"], lw["hn_g"], lw["vn_g"],
      lw["w_s"], lw["b_s_col"], lw["w_out"])


def _mix_sample_kernel(x_ref, hprev_ref, cprev_ref, gmix_ref, win_ref, cw_ref, cb_ref, gb_ref, hng_ref, vng_ref,
                       w00_ref, bs0_ref, wout_ref, sconv_ref, c_in, n_in, m_in,
                       out_ref, c_out, n_out, m_out, conv_out, vrow_out,
                       proj, hcat, *, bt):
    del hprev_ref, cprev_ref
    i = pl.program_id(0)
    ni = pl.num_programs(0)

    @pl.when(i == 0)
    def _():
        x = x_ref[...]
        xn = _rms(x, gmix_ref[...]).astype(BF16)
        proj[...] = jnp.dot(xn, win_ref[...], preferred_element_type=F32)
        pre = proj[:, P_QK:P_QK + QK_W]
        qk = cb_ref[...] + cw_ref[CONV_W - 1:CONV_W, :] * pre
        for jj in range(CONV_W - 1):
            qk = qk + cw_ref[jj:jj + 1, :] * sconv_ref[jj]
        for jj in range(CONV_W - 2):
            conv_out[jj] = sconv_ref[jj + 1]
        conv_out[CONV_W - 2] = pre
        qk = qk * jax.nn.sigmoid(qk)
        proj[:, 0:W_A] = qk[:, 0:W_A]
        proj[:, W_A:QK_W] = qk[:, W_A:QK_W] * (DH_A ** -0.5)
        for g in range(G_B):
            sl = slice(g * DG_B, (g + 1) * DG_B)
            vb = proj[:, P_VB + g * DG_B:P_VB + (g + 1) * DG_B]
            u = proj[:, P_U + g * DG_B:P_U + (g + 1) * DG_B]
            vn = _rms(jax.nn.gelu(vb), vng_ref[:, sl])
            vrow_out[:, sl] = vn
            z = w00_ref[:, sl] * vn + bs0_ref[:, sl]
            hcat[:, W_A + g * DG_B:W_A + (g + 1) * DG_B] = jax.nn.gelu(u) * z

    rows = pl.ds(pl.multiple_of(i * bt, bt), bt)
    gates = proj[rows, P_G:P_G + LANES] + gb_ref[...]
    lane = lax.broadcasted_iota(jnp.int32, (bt, LANES), 1)
    sub = lax.broadcasted_iota(jnp.int32, (bt, DH_A), 0)
    m_new_all = jnp.zeros((bt, LANES), F32)
    for h in range(H_A):
        ig = gates[:, h:h + 1]
        lf = _log_sigmoid(gates[:, H_A + h:H_A + h + 1])
        m_prev = m_in[rows, h:h + 1]
        inter = lf + m_prev
        m_t = jnp.maximum(inter, ig)
        w_inter = jnp.exp(inter - m_t)
        e_d = jnp.exp(ig - m_t)
        q = proj[rows, h * DH_A:(h + 1) * DH_A]
        k = proj[rows, W_A + h * DH_A:W_A + (h + 1) * DH_A]
        v = proj[rows, P_V + h * DH_A:P_V + (h + 1) * DH_A]
        n_old = n_in[h]
        s = jnp.sum(q * k, axis=-1, keepdims=True) * e_d
        vw = v * e_d
        vw_t = jnp.concatenate([vw, jnp.zeros((DH_A - bt, DH_A), F32)], axis=0).T
        qb = q.astype(BF16)
        qc = jnp.zeros((bt, DH_A), F32)
        for t in range(bt):
            c_old = c_in[t, h]
            r = _dot_nt(qb, c_old.astype(BF16))
            qc = jnp.where(sub == t, r, qc)
            c_out[t, h] = w_inter[t:t + 1, :] * c_old + vw_t[:, t:t + 1] * k[t:t + 1, :]
        num = w_inter * qc + s * v
        nq = w_inter * jnp.sum(q * n_old, axis=-1, keepdims=True) + s
        hh = num / jnp.maximum(jnp.abs(nq), jnp.exp(-m_t))
        n_out[h] = w_inter * n_old + e_d * k
        m_new_all = jnp.where(lane == h, m_t, m_new_all)
        ha = _rms(hh, hng_ref[:, h * DH_A:(h + 1) * DH_A])
        o = proj[rows, P_O + h * DH_A:P_O + (h + 1) * DH_A]
        hcat[rows, h * DH_A:(h + 1) * DH_A] = ha * jax.nn.sigmoid(o)
    m_out[rows, :] = m_new_all

    @pl.when(i == ni - 1)
    def _():
        out_ref[...] = x_ref[...] + jnp.dot(hcat[...].astype(BF16), wout_ref[...],
                                            preferred_element_type=F32)


def _mix_sample(h_in, h_out, c_all, lw, sconv_t, c0, n0_t, m0_pad, n_s, layer):
    bt = BT_SAMPLE
    ni = n_s // bt
    const = lambda i: (0, 0)
    in_blk = h_in.shape[0] // n_s - 1
    out_blk = h_out.shape[0] // n_s - 1
    kern = functools.partial(_mix_sample_kernel, bt=bt)
    return pl.pallas_call(
        kern,
        name="mix_sample",
        grid=(ni,),
        in_specs=[
            pl.BlockSpec((n_s, D_MODEL), lambda i: (in_blk, 0)),
            pl.BlockSpec(memory_space=pl.ANY),
            pl.BlockSpec(memory_space=pl.ANY),
            _layer_spec((1, D_MODEL), layer),
            _layer_spec((D_MODEL, P_W), layer),
            _layer_spec((CONV_W, QK_W), layer),
            _layer_spec((1, QK_W), layer),
            _layer_spec((1, LANES), layer),
            _layer_spec((1, W_A), layer),
            _layer_spec((1, W_B), layer),
            _layer_spec((1, W_B), layer),
            _layer_spec((1, W_B), layer),
            _layer_spec((D_MODEL, D_MODEL), layer),
            pl.BlockSpec((None, CONV_W - 1, n_s, QK_W), lambda i: (layer, 0, 0, 0)),
            pl.BlockSpec((None, bt, H_A, DH_A, DH_A), lambda i: (layer, i, 0, 0, 0)),
            pl.BlockSpec((None, H_A, bt, DH_A), lambda i: (layer, 0, i, 0)),
            pl.BlockSpec((None, n_s, LANES), lambda i: (layer, 0, 0)),
        ],
        out_specs=[
            pl.BlockSpec((n_s, D_MODEL), lambda i: (out_blk, 0)),
            pl.BlockSpec((None, bt, H_A, DH_A, DH_A), lambda i: (layer, i, 0, 0, 0)),
            pl.BlockSpec((H_A, bt, DH_A), lambda i: (0, i, 0)),
            pl.BlockSpec((n_s, LANES), const),
            pl.BlockSpec((CONV_W - 1, n_s, QK_W), lambda i: (0, 0, 0)),
            pl.BlockSpec((n_s, W_B), const),
        ],
        out_shape=[
            jax.ShapeDtypeStruct(h_out.shape, F32),
            jax.ShapeDtypeStruct(c_all.shape, F32),
            jax.ShapeDtypeStruct(n0_t.shape[1:], F32),
            jax.ShapeDtypeStruct((n_s, LANES), F32),
            jax.ShapeDtypeStruct((CONV_W - 1, n_s, QK_W), F32),
            jax.ShapeDtypeStruct((n_s, W_B), F32),
        ],
        scratch_shapes=[
            pltpu.VMEM((n_s, P_W), F32),
            pltpu.VMEM((n_s, D_MODEL), F32),
        ],
        input_output_aliases={1: 0, 2: 1},
        compiler_params=_cparams("arbitrary"),
    )(h_in, h_out, c_all, lw["g_mix"], lw["w_in"], lw["conv_w"], lw["conv_b"], lw["gate_b"], lw["hn_g"], lw["vn_g"],
      lw["w00_row"], lw["bs0_row"], lw["w_out"], sconv_t, c0, n0_t, m0_pad)


def _router_kernel(h_ref, g_ref, wr_ref, b_ref, route_ref, route_t_ref, tcnt_ref, *, tm):
    hn = _rms(h_ref[...], g_ref[...]).astype(BF16)
    logits = jnp.dot(hn, wr_ref[...], preferred_element_type=F32) + b_ref[...]

    lane_i = lax.broadcasted_iota(jnp.int32, (tm, LANES), 1)
    lane = lane_i.astype(F32)
    neg = -jnp.inf
    is_g = lane_i < N_GROUPS
    gl = jnp.where(is_g, logits, neg)
    gmax = jnp.max(gl, axis=-1, keepdims=True)
    g_sel = jnp.min(jnp.where(gl == gmax, lane, float(LANES)), axis=-1, keepdims=True)
    g_w = 1.0 / jnp.sum(jnp.where(is_g, jnp.exp(logits - gmax), 0.0), axis=-1, keepdims=True)

    e_lane = lane_i - N_GROUPS
    lane_grp = (e_lane >> 3).astype(F32)
    in_grp = (e_lane >= 0) & (e_lane < N_EXPERTS) & (lane_grp == g_sel)
    el = jnp.where(in_grp, logits, neg)
    v1 = jnp.max(el, axis=-1, keepdims=True)
    i1 = jnp.min(jnp.where(el == v1, lane, float(LANES)), axis=-1, keepdims=True)
    el2 = jnp.where(lane == i1, neg, el)
    v2 = jnp.max(el2, axis=-1, keepdims=True)
    i2 = jnp.min(jnp.where(el2 == v2, lane, float(LANES)), axis=-1, keepdims=True)
    e21 = jnp.exp(v2 - v1)
    den = 1.0 + e21
    w1 = g_w * (1.0 / den)
    w2 = g_w * (e21 / den)

    onehot = jnp.where((lane == i1) | (lane == i2), 1.0, 0.0)
    r_i = lax.broadcasted_iota(jnp.int32, (tm, tm), 0)
    c_i = lax.broadcasted_iota(jnp.int32, (tm, tm), 1)
    strict = jnp.where(r_i > c_i, 1.0, 0.0).astype(BF16)
    before = jnp.dot(strict, onehot.astype(BF16), preferred_element_type=F32)
    cnt = jnp.sum(onehot, axis=0, keepdims=True)
    cnt8 = jnp.floor((cnt + (SUBLANES - 1.0)) * (1.0 / SUBLANES)) * SUBLANES
    l_r = lax.broadcasted_iota(jnp.int32, (LANES, LANES), 0)
    l_c = lax.broadcasted_iota(jnp.int32, (LANES, LANES), 1)
    lanes_before = jnp.where(l_r < l_c, 1.0, 0.0).astype(BF16)
    start = _split_dot(jnp.broadcast_to(cnt8, (2 * SUBLANES, LANES)), lanes_before)[0:1, :]
    slot = start + before
    pos1 = jnp.sum(jnp.where(lane == i1, slot, 0.0), axis=-1, keepdims=True)
    pos2 = jnp.sum(jnp.where(lane == i2, slot, 0.0), axis=-1, keepdims=True)
    tcnt_ref[...] = jnp.broadcast_to(cnt8, tcnt_ref.shape)

    out = jnp.zeros((tm, LANES), F32)
    for idx, val in enumerate((pos1, pos2, w1, w2)):
        out = jnp.where(lane_i == idx, val, out)
    route_ref[...] = out
    route_t_ref[...] = out.T[0:SUBLANES, :]


def _router(h1, lw, layer):
    t_total = h1.shape[0]
    tm = _token_tile(t_total)
    n_tiles = t_total // tm
    return pl.pallas_call(
        functools.partial(_router_kernel, tm=tm),
        name="router",
        grid=(n_tiles,),
        in_specs=[
            pl.BlockSpec((tm, D_MODEL), lambda i: (i, 0)),
            _layer_spec((1, D_MODEL), layer),
            _layer_spec((D_MODEL, LANES), layer),
            _layer_spec((1, LANES), layer),
        ],
        out_specs=[
            pl.BlockSpec((tm, LANES), lambda i: (i, 0)),
            pl.BlockSpec((SUBLANES, tm), lambda i: (0, i)),
            pl.BlockSpec((None, SUBLANES, LANES), lambda i: (i, 0, 0)),
        ],
        out_shape=[
            jax.ShapeDtypeStruct((t_total, LANES), F32),
            jax.ShapeDtypeStruct((SUBLANES, t_total), F32),
            jax.ShapeDtypeStruct((n_tiles, SUBLANES, LANES), F32),
        ],
        compiler_params=_cparams("arbitrary"),
    )(h1, lw["g_ffn"], lw["wr"], lw["br"])


def _tile_slots(tm):
    raw = tm * TOP_K + N_EXPERTS * (SUBLANES - 1)
    return -(-raw // LANES) * LANES


def _token_tile(t_total):
    tm = TM_TOK
    while t_total % tm:
        tm //= 2
    return tm


def _run_copies(tile, tables, units_per_tile, make_copy):
    psrc_ref, pdst_ref, npair_ref, ssrc_ref, sdst_ref, nsingle_ref, _ = tables
    pair_base = tile * (units_per_tile // 2)
    single_base = tile * N_EXPERTS

    def per_pair(u, carry):
        make_copy(pl.multiple_of(psrc_ref[pair_base + u], SUBLANES),
                  pl.multiple_of(pdst_ref[pair_base + u], SUBLANES), 2 * SUBLANES).start()
        return carry

    def per_single(u, carry):
        make_copy(pl.multiple_of(ssrc_ref[single_base + u], SUBLANES),
                  pl.multiple_of(sdst_ref[single_base + u], SUBLANES), SUBLANES).start()
        return carry

    lax.fori_loop(0, npair_ref[tile], per_pair, 0)
    lax.fori_loop(0, nsingle_ref[tile], per_single, 0)


def _drain_copies(n_units, make_copy):
    def wait_group(u, carry):
        make_copy(0, 0, DRAIN_GROUP * SUBLANES).wait()
        return carry

    def wait_one(u, carry):
        make_copy(0, 0, SUBLANES).wait()
        return carry

    lax.fori_loop(0, lax.div(n_units, DRAIN_GROUP), wait_group, 0)
    lax.fori_loop(0, lax.rem(n_units, DRAIN_GROUP), wait_one, 0)


def _dispatch_kernel(*refs, tm, slots):
    tables = refs[:N_COPY_TABLES]
    tunits_ref = tables[-1]
    h_ref, g_ref, rt_ref, xs_in, xs_out, srt, sem = refs[N_COPY_TABLES:]
    del xs_in
    i = pl.program_id(0)
    n = pl.num_programs(0)
    buf = lax.rem(i, 2)

    def copy_maker(b):
        def make_copy(tile_row, sorted_row, rows):
            return pltpu.make_async_copy(srt.at[b, pl.ds(tile_row, rows)],
                                         xs_out.at[pl.ds(sorted_row, rows), pl.ds(0, HALF_D)], sem.at[b])
        return make_copy

    @pl.when(i >= 2)
    def _():
        _drain_copies(tunits_ref[i - 2], copy_maker(buf))

    xn = _rms(h_ref[...], g_ref[...]).astype(BF16)
    slot_id = lax.broadcasted_iota(jnp.int32, (slots, tm), 0).astype(F32)
    sel = (slot_id == rt_ref[0:1, :]) | (slot_id == rt_ref[1:2, :])
    srt[buf] = _pack_rows(jnp.dot(jnp.where(sel, 1.0, 0.0).astype(BF16), xn, preferred_element_type=F32))
    _run_copies(i, tables, slots // SUBLANES, copy_maker(buf))

    @pl.when(i == n - 1)
    def _():
        @pl.when(i >= 1)
        def _():
            _drain_copies(tunits_ref[i - 1], copy_maker(1 - buf))

        _drain_copies(tunits_ref[i], copy_maker(buf))


def _dispatch(tables, h1, route_t, lw, xs, layer):
    t_total = h1.shape[0]
    tm = _token_tile(t_total)
    slots = _tile_slots(tm)
    return pl.pallas_call(
        functools.partial(_dispatch_kernel, tm=tm, slots=slots),
        name="dispatch",
        grid_spec=pltpu.PrefetchScalarGridSpec(
            num_scalar_prefetch=N_COPY_TABLES,
            grid=(t_total // tm,),
            in_specs=[
                pl.BlockSpec((tm, D_MODEL), lambda i, *_: (i, 0)),
                _layer_spec((1, D_MODEL), layer),
                pl.BlockSpec((SUBLANES, tm), lambda i, *_: (0, i)),
                pl.BlockSpec(memory_space=pl.ANY),
            ],
            out_specs=pl.BlockSpec(memory_space=pl.ANY),
            scratch_shapes=[pltpu.VMEM((2, slots, HALF_D), U32), pltpu.SemaphoreType.DMA((2,))],
        ),
        out_shape=jax.ShapeDtypeStruct(xs.shape, U32),
        input_output_aliases={N_COPY_TABLES + 3: 0},
        compiler_params=_cparams("arbitrary"),
    )(*tables, h1, lw["g_ffn"], route_t, xs)


def _ffn_kernel(bstart_ref, nblk_ref, nused_ref, w1_ref, w3_ref, w2_ref, xs_ref, ys_ref,
                w1b, w3b, w2b, xbuf, ybuf, sem_in, sem_out):
    e = pl.program_id(0)
    g0 = bstart_ref[e]
    nb = nblk_ref[e]
    n_used = nused_ref[0]

    def in_copy(g):
        slot = lax.rem(g, FFN_RING)
        r = pl.multiple_of(g * FFN_BLK, FFN_BLK)
        return pltpu.make_async_copy(xs_ref.at[pl.ds(r, FFN_BLK), pl.ds(0, HALF_D)], xbuf.at[slot],
                                     sem_in.at[slot])

    def out_copy(g):
        slot = lax.rem(g, FFN_RING)
        r = pl.multiple_of(g * FFN_BLK, FFN_BLK)
        return pltpu.make_async_copy(ybuf.at[slot], ys_ref.at[pl.ds(r, FFN_BLK)], sem_out.at[slot])

    @pl.when(e == 0)
    def _():
        for ahead in range(FFN_RING - 1):
            @pl.when(ahead < n_used)
            def _():
                in_copy(jnp.int32(ahead)).start()

    @pl.when(nb > 0)
    def _():
        w1b[...] = w1_ref[...].astype(BF16)
        w3b[...] = w3_ref[...].astype(BF16)
        w2b[...] = w2_ref[...].astype(BF16)

    def block_step(k, carry):
        g = g0 + k
        slot = lax.rem(g, FFN_RING)
        in_copy(g).wait()

        @pl.when(g + (FFN_RING - 1) < n_used)
        def _():
            in_copy(g + (FFN_RING - 1)).start()

        @pl.when(g >= FFN_RING)
        def _():
            out_copy(g - FFN_RING).wait()

        x_lo, x_hi = _unpack_rows(xbuf[slot])
        a = (jnp.dot(x_lo, w1b[0:HALF_D, :], preferred_element_type=F32)
             + jnp.dot(x_hi, w1b[HALF_D:, :], preferred_element_type=F32))
        b = (jnp.dot(x_lo, w3b[0:HALF_D, :], preferred_element_type=F32)
             + jnp.dot(x_hi, w3b[HALF_D:, :], preferred_element_type=F32))
        hmid = (a * jax.nn.sigmoid(a) * b).astype(BF16)
        y = jnp.dot(hmid, w2b[...], preferred_element_type=F32)
        ybuf[slot] = lax.bitcast_convert_type(y, U32)
        out_copy(g).start()
        return carry

    lax.fori_loop(0, nb, block_step, 0)

    @pl.when(e == pl.num_programs(0) - 1)
    def _():
        for back in range(FFN_RING, 0, -1):
            @pl.when(n_used >= back)
            def _():
                out_copy(n_used - back).wait()


def _ffn(ffn_tables, xs, w1, w3, w2, layer):
    def w_map(e, *_):
        return (layer, e, 0, 0)

    return pl.pallas_call(
        _ffn_kernel,
        name="expert_ffn",
        grid_spec=pltpu.PrefetchScalarGridSpec(
            num_scalar_prefetch=3,
            grid=(N_EXPERTS,),
            in_specs=[
                pl.BlockSpec((None, None, D_MODEL, D_FF), w_map),
                pl.BlockSpec((None, None, D_MODEL, D_FF), w_map),
                pl.BlockSpec((None, None, D_FF, D_MODEL), w_map),
                pl.BlockSpec(memory_space=pl.ANY),
            ],
            out_specs=pl.BlockSpec(memory_space=pl.ANY),
            scratch_shapes=[
                pltpu.VMEM((D_MODEL, D_FF), BF16),
                pltpu.VMEM((D_MODEL, D_FF), BF16),
                pltpu.VMEM((D_FF, D_MODEL), BF16),
                pltpu.VMEM((FFN_RING, FFN_BLK, HALF_D), U32),
                pltpu.VMEM((FFN_RING, FFN_BLK, D_MODEL), U32),
                pltpu.SemaphoreType.DMA((FFN_RING,)),
                pltpu.SemaphoreType.DMA((FFN_RING,)),
            ],
        ),
        out_shape=jax.ShapeDtypeStruct(xs.shape, U32),
        input_output_aliases={6: 0},
        compiler_params=_cparams("arbitrary"),
    )(*ffn_tables, w1, w3, w2, xs)


def _ple_kernel(*refs, tm, slots, n_s, final):
    tables = refs[:N_COPY_TABLES]
    tunits_ref = tables[-1]
    h_ref, route_ref, p_ref, ps_ref, wpg_ref, wple_ref, pg_ref, fg_ref, ys_ref = refs[N_COPY_TABLES:N_COPY_TABLES + 9]
    rest = refs[N_COPY_TABLES + 9:]
    if final:
        yp_ref, ysm_ref, ysrt, sem = rest
    else:
        out_ref, ysrt, sem = rest
    i = pl.program_id(0)
    n = pl.num_programs(0)
    buf = lax.rem(i, 2)

    def copy_maker(b):
        def make_copy(tile_row, sorted_row, rows):
            return pltpu.make_async_copy(ys_ref.at[pl.ds(sorted_row, rows)],
                                         ysrt.at[b, pl.ds(tile_row, rows)], sem.at[b])
        return make_copy

    @pl.when(i == 0)
    def _():
        ysrt[...] = jnp.zeros_like(ysrt)
        _run_copies(i, tables, slots // SUBLANES, copy_maker(buf))

    @pl.when(i + 1 < n)
    def _():
        _run_copies(i + 1, tables, slots // SUBLANES, copy_maker(1 - buf))

    p = p_ref[...]
    p_last = jnp.concatenate([p[0:tm - n_s, :], ps_ref[...]], axis=0)
    p = jnp.where(i == n - 1, p_last, p)
    pe = _rms(jnp.dot(p.astype(BF16), wple_ref[...], preferred_element_type=F32), pg_ref[...])
    _drain_copies(tunits_ref[i], copy_maker(buf))

    route = route_ref[...]
    y = lax.bitcast_convert_type(ysrt[buf], F32)
    y_head = y.astype(BF16)
    y_rem = (y - y_head.astype(F32)).astype(BF16)
    slot_id = lax.broadcasted_iota(jnp.int32, (tm, slots), 1).astype(F32)
    moe = None
    weighted_pick = jnp.zeros((tm, slots), F32)
    for kk in range(TOP_K):
        hit = slot_id == route[:, kk:kk + 1]
        w_kk = route[:, TOP_K + kk:TOP_K + kk + 1]
        term = w_kk * jnp.dot(jnp.where(hit, 1.0, 0.0).astype(BF16), y_head, preferred_element_type=F32)
        moe = term if moe is None else moe + term
        weighted_pick = jnp.where(hit, w_kk, weighted_pick)
    moe = moe + jnp.dot(weighted_pick.astype(BF16), y_rem, preferred_element_type=F32)
    h2 = h_ref[...] + moe
    gate = jax.nn.sigmoid(jnp.dot(h2.astype(BF16), wpg_ref[...], preferred_element_type=F32))
    out = h2 + gate * pe
    if final:
        fin = _rms(out, fg_ref[...])
        yp_ref[...] = fin

        @pl.when(i == pl.num_programs(0) - 1)
        def _():
            ysm_ref[...] = fin[tm - n_s:tm, :]
    else:
        out_ref[...] = out


def _ple(tables, h1, route, p_prompt, p_sample, lw, final_g, ys, n_s, layer, final, in_place):
    t_total = h1.shape[0]
    tm = _token_tile(t_total)
    slots = _tile_slots(tm)
    assert n_s <= tm and (t_total - n_s) % tm == tm - n_s
    const = lambda i, *_: (0, 0)
    tok = lambda i, *_: (i, 0)
    if final:
        out_specs = [pl.BlockSpec((tm, D_MODEL), tok), pl.BlockSpec((n_s, D_MODEL), const)]
        out_shape = [jax.ShapeDtypeStruct((t_total - n_s, D_MODEL), F32),
                     jax.ShapeDtypeStruct((n_s, D_MODEL), F32)]
        aliases = {}
    else:
        out_specs = [pl.BlockSpec((tm, D_MODEL), tok)]
        out_shape = [jax.ShapeDtypeStruct((t_total, D_MODEL), F32)]
        aliases = {N_COPY_TABLES: 0} if in_place else {}
    return pl.pallas_call(
        functools.partial(_ple_kernel, tm=tm, slots=slots, n_s=n_s, final=final),
        name="combine_ple",
        grid_spec=pltpu.PrefetchScalarGridSpec(
            num_scalar_prefetch=N_COPY_TABLES,
            grid=(t_total // tm,),
            in_specs=[
                pl.BlockSpec((tm, D_MODEL), tok),
                pl.BlockSpec((tm, LANES), tok),
                pl.BlockSpec((None, tm, PLE_DIM), lambda i, *_: (layer, i, 0)),
                pl.BlockSpec((None, n_s, PLE_DIM), lambda i, *_: (layer, 0, 0)),
                _layer_spec((D_MODEL, D_MODEL), layer),
                _layer_spec((PLE_DIM, D_MODEL), layer),
                _layer_spec((1, D_MODEL), layer),
                pl.BlockSpec((1, D_MODEL), const),
                pl.BlockSpec(memory_space=pl.ANY),
            ],
            out_specs=out_specs,
            scratch_shapes=[pltpu.VMEM((2, slots, D_MODEL), U32), pltpu.SemaphoreType.DMA((2,))],
        ),
        out_shape=out_shape,
        input_output_aliases=aliases,
        compiler_params=_cparams("arbitrary"),
    )(*tables, h1, route, p_prompt, p_sample, lw["w_pg"], lw["w_ple"], lw["ple_g"], final_g, ys)


def _stacked_weights(norm_mix_g, w_in, conv_w, conv_b, b_igate, b_fgate, hnorm_g, vnorm_g, w_s, b_s, w_out,
                     norm_ffn_g, router_g_w, router_g_b, router_e_w, router_e_b, w_ple, ple_norm_g, w_ple_gate):
    a0 = QK_W + 2 * W_A
    lane_pad = lambda x, used: jnp.pad(x, ((0, 0),) * (x.ndim - 1) + ((0, LANES - used),))
    w_packed = jnp.concatenate(
        [w_in[:, :, :a0], w_in[:, :, a0 + 2 * H_A:], lane_pad(w_in[:, :, a0:a0 + 2 * H_A], 2 * H_A)],
        axis=2).astype(BF16)
    gate_b = lane_pad(jnp.concatenate([b_igate, b_fgate], axis=1), 2 * H_A)[:, None, :]
    wr = lane_pad(jnp.concatenate([router_g_w, router_e_w], axis=2), N_GROUPS + N_EXPERTS).astype(BF16)
    br = lane_pad(jnp.concatenate([router_g_b, router_e_b], axis=1), N_GROUPS + N_EXPERTS)[:, None, :]
    return {
        "g_mix": norm_mix_g[:, None, :],
        "w_in": w_packed,
        "conv_w": conv_w,
        "conv_b": conv_b[:, None, :],
        "gate_b": gate_b,
        "hn_g": hnorm_g[:, None, :],
        "vn_g": vnorm_g[:, None, :],
        "w_s": w_s,
        "b_s_col": lane_pad(jnp.transpose(b_s, (0, 2, 1)), G_B),
        "w00_row": jnp.repeat(w_s[:, :, 0, 0], DG_B, axis=1)[:, None, :],
        "bs0_row": jnp.repeat(b_s[:, :, 0], DG_B, axis=1)[:, None, :],
        "w_out": w_out.astype(BF16),
        "g_ffn": norm_ffn_g[:, None, :],
        "wr": wr,
        "br": br,
        "w_pg": w_ple_gate.astype(BF16),
        "w_ple": w_ple.astype(BF16),
        "ple_g": ple_norm_g[:, None, :],
    }


def _routing_tables(tcnt, units_per_tile):
    c8 = tcnt[:, 0, N_GROUPS:N_GROUPS + N_EXPERTS].astype(jnp.int32)
    blocks = (jnp.sum(c8, axis=0) + FFN_BLK - 1) // FFN_BLK
    blk_end = jnp.cumsum(blocks)
    pstart = (blk_end - blocks) * FFN_BLK
    run_dst = pstart[None, :] + jnp.cumsum(c8, axis=0) - c8
    run_src = jnp.cumsum(c8, axis=1) - c8
    units = c8 // SUBLANES
    n_pair = units // 2
    n_single = units - 2 * n_pair
    pair_first = jnp.cumsum(n_pair, axis=1) - n_pair
    single_first = jnp.cumsum(n_single, axis=1) - n_single

    def piece_lists(first, count, run_off, n_list, rows):
        j = jnp.arange(n_list, dtype=jnp.int32)[None, :, None]
        hit = (j >= first[:, None, :]) & (j < (first + count)[:, None, :])
        off = run_off[:, None, :] + (j - first[:, None, :]) * rows
        src = jnp.sum(jnp.where(hit, run_src[:, None, :] + off, 0), axis=2)
        dst = jnp.sum(jnp.where(hit, run_dst[:, None, :] + off, 0), axis=2)
        return src.reshape(-1), dst.reshape(-1)

    psrc, pdst = piece_lists(pair_first, n_pair, jnp.zeros_like(c8), units_per_tile // 2, 2 * SUBLANES)
    ssrc, sdst = piece_lists(single_first, n_single, n_pair * (2 * SUBLANES), N_EXPERTS, SUBLANES)
    tables = (psrc, pdst, jnp.sum(n_pair, axis=1), ssrc, sdst, jnp.sum(n_single, axis=1),
              jnp.sum(units, axis=1))
    assert len(tables) == N_COPY_TABLES
    ffn_tables = (blk_end - blocks, blocks, blk_end[-1:])
    return tables, ffn_tables


def kernel(x_prompt, x_sample, state_C, state_n, state_m, state_conv, p_prompt, p_sample, norm_mix_g, w_in, conv_w, conv_b, b_igate, b_fgate, hnorm_g, vnorm_g, w_s, b_s, w_out, norm_ffn_g, router_g_w, router_g_b, router_e_w, router_e_b, w1, w3, w2, w_ple, ple_norm_g, w_ple_gate, final_norm_g):
    n_batch, seq, _ = x_prompt.shape
    n_s = x_sample.shape[0]
    depth = w_in.shape[0]
    t_p = n_batch * seq
    t_total = t_p + n_s
    assert seq % CHUNK == 0 and t_p % n_s == 0 and n_s % BT_SAMPLE == 0

    n_tiles = t_total // _token_tile(t_total)
    max_rows = t_total * TOP_K + N_EXPERTS * ((SUBLANES - 1) * n_tiles + FFN_BLK - 1)
    n_blocks = max_rows // FFN_BLK
    xs = jnp.zeros((n_blocks * FFN_BLK, D_MODEL), U32)

    p_prompt_t = p_prompt.reshape(depth, t_p, PLE_DIM)
    p_sample_t = p_sample.reshape(depth, n_s, PLE_DIM)
    sconv_t = jnp.transpose(state_conv, (0, 2, 1, 3))
    n0_t = jnp.transpose(state_n, (0, 2, 1, 3))
    m0_pad = jnp.pad(state_m, ((0, 0), (0, 0), (0, LANES - H_A)))
    final_g = final_norm_g[None, :]

    src_p = x_prompt.reshape(t_p, D_MODEL)
    src_s = x_sample.reshape(n_s, D_MODEL)
    spare = jnp.zeros((t_total, D_MODEL), F32)
    c_all = jnp.zeros(state_C.shape, F32)
    outs = {k: [] for k in ("Cp", "np", "mp", "cbp", "ns", "ms", "cbs", "vs")}
    res = None
    lw = _stacked_weights(norm_mix_g, w_in, conv_w, conv_b, b_igate, b_fgate, hnorm_g, vnorm_g, w_s, b_s,
                          w_out, norm_ffn_g, router_g_w, router_g_b, router_e_w, router_e_b, w_ple,
                          ple_norm_g, w_ple_gate)
    for l in range(depth):
        h_mix, c_p, n_p, m_p, cb_p = _mix_prompt(src_p, spare, n_batch, seq, lw, l)
        h_mix, c_all, n_s_t, m_s, cb_s, v_s = _mix_sample(src_s, h_mix, c_all, lw, sconv_t, state_C, n0_t,
                                                          m0_pad, n_s, l)
        route, route_t, tcnt = _router(h_mix, lw, l)
        tables, ffn_tables = _routing_tables(tcnt, _tile_slots(_token_tile(t_total)) // SUBLANES)
        xs = _dispatch(tables, h_mix, route_t, lw, xs, l)
        xs = _ffn(ffn_tables, xs, w1, w3, w2, l)
        final = l == depth - 1
        res = _ple(tables, h_mix, route, p_prompt_t, p_sample_t, lw, final_g, xs, n_s, l, final, in_place=l > 0)
        if l == 0:
            spare = h_mix
        else:
            spare = src_p
        src_p = src_s = res[0]
        outs["Cp"].append(c_p)
        outs["np"].append(n_p)
        outs["mp"].append(m_p[:, :, 0])
        outs["cbp"].append(cb_p)
        outs["ns"].append(jnp.transpose(n_s_t, (1, 0, 2)))
        outs["ms"].append(m_s[:, 0:H_A])
        outs["cbs"].append(jnp.transpose(cb_s, (1, 0, 2)))
        outs["vs"].append(v_s[:, None, :])

    y_prompt = res[0].reshape(n_batch, seq, D_MODEL)
    y_sample = res[1].reshape(n_s, 1, D_MODEL)
    st = lambda k: jnp.stack(outs[k])
    return (y_prompt, y_sample, st("Cp"), st("np"), st("mp"), st("cbp"),
            c_all, st("ns"), st("ms"), st("cbs"), st("vs"))
```
